```python
import jax, jax.numpy as jnp
from jax import lax
import numpy as np

D_MODEL = 1024
BATCH = 4
SEQ = 4096
DEPTH = 1

GRID_W = 64
CTX_LEN = 256
D_MIX = D_MODEL
GDN_WIDTH = D_MIX // 2
GDN_HEADS = 4
GDN_DK = GDN_WIDTH // GDN_HEADS
GDN_DV = GDN_WIDTH // GDN_HEADS
GDN_CHUNK = 64
CONV_W = 4
CONV_PAD_L = 2
CONV_PAD_R = 1
LRU_WIDTH = D_MIX - GDN_WIDTH
LRU_BLOCKS = 8
LRU_BW = LRU_WIDTH // LRU_BLOCKS
LRU_C = 8.0
D_FF = 2816
N_MOD = 9
FFN_RESIDUAL = 0.5
EPS = 1e-6

OFF_Z = 3 * GDN_WIDTH
OFF_BA = 4 * GDN_WIDTH
OFF_RX = OFF_BA + 4 * GDN_HEADS
OFF_RG = OFF_RX + LRU_WIDTH
IN_COLS = OFF_RG + LRU_WIDTH

kernel_name = "hybrid_gdn_rglru_macaron_dit_layer"


def rms_norm(x, g):
    xf = x.astype(jnp.float32)
    y = xf * lax.rsqrt(jnp.mean(xf * xf, axis=-1, keepdims=True) + EPS)
    return (y * g.astype(jnp.float32)).astype(x.dtype)


def modulate(h, shift, scale):
    return h * (1.0 + scale) + shift


def adaln(cvec, w_ada, b_ada):
    m = jax.nn.silu(cvec) @ w_ada + b_ada
    return jnp.split(m[:, None, :], N_MOD, axis=-1)


def ffn_sublayer(h, mods, g, w1, w3, w2):
    shift, scale, gate = mods
    u = modulate(rms_norm(h, g), shift, scale)
    return h + FFN_RESIDUAL * gate * ((jax.nn.silu(u @ w1) * (u @ w3)) @ w2)


def centred_dwconv(x, w, b=None):
    T = x.shape[1]
    xp = jnp.pad(x, ((0, 0), (CONV_PAD_L, CONV_PAD_R), (0, 0)))
    y = xp[:, 0:T] * w[0]
    for k in range(1, CONV_W):
        y = y + xp[:, k:k + T] * w[k]
    return y if b is None else y + b


def l2norm(t):
    return t * lax.rsqrt(jnp.sum(t * t, axis=-1, keepdims=True) + EPS)


def _ident(t):
    return t


def _flip(t):
    return jnp.flip(t, axis=1)


def to_column_major(t):
    B, T, C = t.shape
    rows = T // GRID_W
    return t.reshape(B, rows, GRID_W, C).transpose(0, 2, 1, 3).reshape(B, T, C)


def from_column_major(t):
    B, T, C = t.shape
    rows = T // GRID_W
    return t.reshape(B, GRID_W, rows, C).transpose(0, 2, 1, 3).reshape(B, T, C)


def gdn_chunk_scan(q, k, v, g, beta, s0, with_output):
    B, T, H, DK = q.shape
    DV = v.shape[-1]
    N = T // GDN_CHUNK

    def chunks(t):
        t = t.reshape((B, N, GDN_CHUNK, H) + t.shape[3:])
        return jnp.moveaxis(t, (1, 3), (0, 2))

    qc, kc, vc, gc, bc = chunks(q), chunks(k), chunks(v), chunks(g), chunks(beta)
    gcum = jnp.cumsum(gc, axis=-1)
    idx = jnp.arange(GDN_CHUNK)
    incl = idx[:, None] >= idx[None, :]
    strict = idx[:, None] > idx[None, :]
    diff = gcum[..., :, None] - gcum[..., None, :]
    decay = jnp.exp(jnp.where(incl, diff, -jnp.inf))
    kb = kc * bc[..., None]
    kk = jnp.einsum('nbhid,nbhjd->nbhij', kb, kc)
    a_mat = jnp.where(strict, kk * decay, 0.0) + jnp.eye(GDN_CHUNK, dtype=jnp.float32)
    rhs = jnp.concatenate([vc * bc[..., None], kb * jnp.exp(gcum)[..., None]], axis=-1)
    sol = lax.linalg.triangular_solve(a_mat, rhs, left_side=True, lower=True, unit_diagonal=True)
    u, w = sol[..., :DV], sol[..., DV:]

    if with_output:
        attn = jnp.where(incl, jnp.einsum('nbhid,nbhjd->nbhij', qc, kc) * decay, 0.0)

        def step_out(s, inp):
            q_n, k_n, u_n, w_n, g_n, attn_n = inp
            v_new = u_n - jnp.einsum('bhcd,bhde->bhce', w_n, s)
            g_last = g_n[..., -1:]
            out = (jnp.einsum('bhcd,bhde->bhce', q_n * jnp.exp(g_n)[..., None], s)
                   + jnp.einsum('bhij,bhje->bhie', attn_n, v_new))
            k_dec = k_n * jnp.exp(g_last - g_n)[..., None]
            s = s * jnp.exp(g_last)[..., None] + jnp.einsum('bhcd,bhce->bhde', k_dec, v_new)
            return s, out

        s_fin, out = lax.scan(step_out, s0, (qc, kc, u, w, gcum, attn))
        out = jnp.moveaxis(out, (0, 2), (1, 3)).reshape(B, T, H, DV)
        return out, s_fin

    def step_state(s, inp):
        k_n, u_n, w_n, g_n = inp
        v_new = u_n - jnp.einsum('bhcd,bhde->bhce', w_n, s)
        g_last = g_n[..., -1:]
        k_dec = k_n * jnp.exp(g_last - g_n)[..., None]
        s = s * jnp.exp(g_last)[..., None] + jnp.einsum('bhcd,bhce->bhde', k_dec, v_new)
        return s, None

    s_fin, _ = lax.scan(step_state, s0, (kc, u, w, gcum))
    return None, s_fin


def gdn_prepare(p, conv_w, a_log, dt_bias):
    B, T, _ = p.shape
    qkv = jax.nn.silu(centred_dwconv(p[..., :OFF_Z], conv_w)).astype(jnp.float32)
    q, k, v = jnp.split(qkv, 3, axis=-1)
    q = l2norm(q.reshape(B, T, GDN_HEADS, GDN_DK)) * (GDN_DK ** -0.5)
    k = l2norm(k.reshape(B, T, GDN_HEADS, GDN_DK))
    v = v.reshape(B, T, GDN_HEADS, GDN_DV)
    z = p[..., OFF_Z:OFF_BA]
    ba = p[..., OFF_BA:OFF_RX].astype(jnp.float32).reshape(B, T, 4, GDN_HEADS)
    beta = jax.nn.sigmoid(ba[:, :, 0:2])
    g = -jnp.exp(a_log.astype(jnp.float32)) * jax.nn.softplus(ba[:, :, 2:4] + dt_bias)
    return q, k, v, z, beta, g


def gdn_gated_norm(o, z, norm_w):
    B, T = o.shape[:2]
    zf = z.astype(jnp.float32).reshape(B, T, GDN_HEADS, GDN_DV)
    y = o * lax.rsqrt(jnp.mean(o * o, axis=-1, keepdims=True) + EPS) * norm_w.astype(jnp.float32) * jax.nn.silu(zf)
    return y.reshape(B, T, GDN_WIDTH)


def gdn_group(p_lat, p_ctx, conv_w, a_log, dt_bias, norm_w, with_ctx_out):
    q_l, k_l, v_l, z_l, beta_l, g_l = gdn_prepare(p_lat, conv_w, a_log, dt_bias)
    q_c, k_c, v_c, z_c, beta_c, g_c = gdn_prepare(p_ctx, conv_w, a_log, dt_bias)
    B = p_lat.shape[0]
    o_l, o_c = None, None
    for d in range(2):
        f = _flip if d else _ident
        s0 = jnp.zeros((B, GDN_HEADS, GDN_DK, GDN_DV), jnp.float32)
        oc, s_ctx = gdn_chunk_scan(f(q_c), f(k_c), f(v_c), f(g_c[:, :, d]), f(beta_c[:, :, d]), s0, with_ctx_out)
        ol, _ = gdn_chunk_scan(f(q_l), f(k_l), f(v_l), f(g_l[:, :, d]), f(beta_l[:, :, d]), s_ctx, True)
        o_l = f(ol) if o_l is None else o_l + f(ol)
        if with_ctx_out:
            o_c = f(oc) if o_c is None else o_c + f(oc)
    y_l = gdn_gated_norm(o_l, z_l, norm_w)
    y_c = gdn_gated_norm(o_c, z_c, norm_w) if with_ctx_out else None
    return y_l, y_c


def _lin_combine(e1, e2):
    a1, b1 = e1
    a2, b2 = e2
    return a1 * a2, a2 * b1 + b2


def rglru_scan(xs, w_gate, b_gate, lam, h0, reset_first):
    B, T, C = xs.shape
    xg = xs.reshape(B, T, LRU_BLOCKS, LRU_BW)
    gates = jax.nn.sigmoid(jnp.einsum('btnc,gncd->gbtnd', xg, w_gate.astype(jnp.float32)).reshape(2, B, T, C)
                           + b_gate.astype(jnp.float32)[:, None, None, :])
    r, i = gates[0], gates[1]
    log_a = -LRU_C * jax.nn.softplus(-lam.astype(jnp.float32)) * r
    a = jnp.exp(log_a)
    mult = jnp.sqrt(-jnp.expm1(2.0 * log_a))
    if reset_first:
        mult = mult.at[:, 0].set(1.0)
    b = mult * (i * xs)
    a_cum, h = lax.associative_scan(_lin_combine, (a, b), axis=1)
    return h + a_cum * h0[:, None, :]


def lru_group(p_lat, p_ctx, conv_w, conv_b, w_gate, b_gate, lam, with_ctx_out):
    x_l = centred_dwconv(to_column_major(p_lat[..., OFF_RX:OFF_RG]), conv_w, conv_b).astype(jnp.float32)
    x_c = centred_dwconv(p_ctx[..., OFF_RX:OFF_RG], conv_w, conv_b).astype(jnp.float32)
    B = p_lat.shape[0]
    h_l, h_c = None, None
    for d in range(2):
        f = _flip if d else _ident
        hc = rglru_scan(f(x_c), w_gate[d], b_gate[d], lam[d], jnp.zeros((B, LRU_WIDTH), jnp.float32), True)
        hl = rglru_scan(f(x_l), w_gate[d], b_gate[d], lam[d], hc[:, -1], False)
        h_l = f(hl) if h_l is None else h_l + f(hl)
        if with_ctx_out:
            h_c = f(hc) if h_c is None else h_c + f(hc)
    y_l = from_column_major(h_l) * jax.nn.gelu(p_lat[..., OFF_RG:].astype(jnp.float32))
    y_c = h_c * jax.nn.gelu(p_ctx[..., OFF_RG:].astype(jnp.float32)) if with_ctx_out else None
    return y_l, y_c


def token_mixing(p_lat, p_ctx, w_out, gdn_conv_w, gdn_a_log, gdn_dt_bias, gdn_norm_w,
                 lru_conv_w, lru_conv_b, lru_w_gate, lru_b_gate, lru_lambda, with_ctx_out):
    dt = p_lat.dtype
    a_l, a_c = gdn_group(p_lat, p_ctx, gdn_conv_w, gdn_a_log, gdn_dt_bias, gdn_norm_w, with_ctx_out)
    b_l, b_c = lru_group(p_lat, p_ctx, lru_conv_w, lru_conv_b, lru_w_gate, lru_b_gate, lru_lambda, with_ctx_out)
    y_l = jnp.concatenate([a_l, b_l], axis=-1).astype(dt) @ w_out
    y_c = jnp.concatenate([a_c, b_c], axis=-1).astype(dt) @ w_out if with_ctx_out else None
    return y_l, y_c


def setup_inputs(seed: int = 0) -> dict:
    key = jax.random.key(seed)
    ks = jax.random.split(key, 24)
    nrm = jax.random.normal
    f32 = jnp.float32
    x = nrm(ks[0], (BATCH, SEQ, D_MODEL), f32)
    c = nrm(ks[1], (BATCH, D_MODEL), f32)
    ctx = nrm(ks[2], (BATCH, CTX_LEN, D_MODEL), f32)
    c_ctx = nrm(ks[3], (D_MODEL,), f32)
    w_ada = nrm(ks[4], (DEPTH, D_MODEL, N_MOD * D_MODEL), f32) * (0.5 * D_MODEL ** -0.5)
    b_ada = nrm(ks[5], (DEPTH, N_MOD * D_MODEL), f32) * 0.01
    norm_g = 1.0 + 0.1 * nrm(ks[6], (DEPTH, 3, D_MODEL), f32)
    ffn_w1 = nrm(ks[7], (DEPTH, 2, D_MODEL, D_FF), f32) * D_MODEL ** -0.5
    ffn_w3 = nrm(ks[8], (DEPTH, 2, D_MODEL, D_FF), f32) * D_MODEL ** -0.5
    ffn_w2 = nrm(ks[9], (DEPTH, 2, D_FF, D_MODEL), f32) * D_FF ** -0.5
    w_in = nrm(ks[10], (DEPTH, D_MODEL, IN_COLS), f32) * D_MODEL ** -0.5
    w_out = nrm(ks[11], (DEPTH, D_MIX, D_MODEL), f32) * D_MIX ** -0.5
    gdn_conv_w = nrm(ks[12], (DEPTH, CONV_W, 3 * GDN_WIDTH), f32) * CONV_W ** -0.5
    gdn_a_log = jnp.log(jax.random.uniform(ks[13], (DEPTH, 2, GDN_HEADS), f32, 1.0, 16.0))
    dt = jnp.exp(jax.random.uniform(ks[14], (DEPTH, 2, GDN_HEADS), f32, np.log(1e-3), np.log(1e-1)))
    gdn_dt_bias = dt + jnp.log(-jnp.expm1(-dt))
    gdn_norm_w = 1.0 + 0.1 * nrm(ks[15], (DEPTH, GDN_DV), f32)
    lru_conv_w = nrm(ks[16], (DEPTH, CONV_W, LRU_WIDTH), f32) * CONV_W ** -0.5
    lru_conv_b = nrm(ks[17], (DEPTH, LRU_WIDTH), f32) * 0.01
    lru_w_gate = nrm(ks[18], (DEPTH, 2, 2, LRU_BLOCKS, LRU_BW, LRU_BW), f32) * LRU_BW ** -0.5
    lru_b_gate = nrm(ks[19], (DEPTH, 2, 2, LRU_WIDTH), f32) * 0.1
    a_pow = jax.random.uniform(ks[20], (DEPTH, 2, LRU_WIDTH), f32, 0.9, 0.999) ** (1.0 / LRU_C)
    lru_lambda = jnp.log(a_pow) - jnp.log1p(-a_pow)
    final_norm_g = 1.0 + 0.1 * nrm(ks[21], (D_MODEL,), f32)
    return {"x": x, "c": c, "ctx": ctx, "c_ctx": c_ctx, "w_ada": w_ada, "b_ada": b_ada,
            "norm_g": norm_g, "ffn_w1": ffn_w1, "ffn_w3": ffn_w3, "ffn_w2": ffn_w2,
            "w_in": w_in, "w_out": w_out, "gdn_conv_w": gdn_conv_w, "gdn_a_log": gdn_a_log,
            "gdn_dt_bias": gdn_dt_bias, "gdn_norm_w": gdn_norm_w, "lru_conv_w": lru_conv_w,
            "lru_conv_b": lru_conv_b, "lru_w_gate": lru_w_gate, "lru_b_gate": lru_b_gate,
            "lru_lambda": lru_lambda, "final_norm_g": final_norm_g}


def reference(x, c, ctx, c_ctx, w_ada, b_ada, norm_g, ffn_w1, ffn_w3, ffn_w2, w_in, w_out,
              gdn_conv_w, gdn_a_log, gdn_dt_bias, gdn_norm_w, lru_conv_w, lru_conv_b,
              lru_w_gate, lru_b_gate, lru_lambda, final_norm_g):
    h_lat, h_ctx = x, ctx
    for l in range(DEPTH):
        last = l == DEPTH - 1
        m_lat = adaln(c, w_ada[l], b_ada[l])
        m_ctx = adaln(c_ctx[None, :], w_ada[l], b_ada[l])
        h_lat = ffn_sublayer(h_lat, m_lat[0:3], norm_g[l, 0], ffn_w1[l, 0], ffn_w3[l, 0], ffn_w2[l, 0])
        h_ctx = ffn_sublayer(h_ctx, m_ctx[0:3], norm_g[l, 0], ffn_w1[l, 0], ffn_w3[l, 0], ffn_w2[l, 0])
        p_lat = modulate(rms_norm(h_lat, norm_g[l, 1]), m_lat[3], m_lat[4]) @ w_in[l]
        p_ctx = modulate(rms_norm(h_ctx, norm_g[l, 1]), m_ctx[3], m_ctx[4]) @ w_in[l]
        y_lat, y_ctx = token_mixing(p_lat, p_ctx, w_out[l], gdn_conv_w[l], gdn_a_log[l], gdn_dt_bias[l],
                                    gdn_norm_w[l], lru_conv_w[l], lru_conv_b[l], lru_w_gate[l],
                                    lru_b_gate[l], lru_lambda[l], not last)
        h_lat = h_lat + m_lat[5] * y_lat
        h_lat = ffn_sublayer(h_lat, m_lat[6:9], norm_g[l, 2], ffn_w1[l, 1], ffn_w3[l, 1], ffn_w2[l, 1])
        if not last:
            h_ctx = h_ctx + m_ctx[5] * y_ctx
            h_ctx = ffn_sublayer(h_ctx, m_ctx[6:9], norm_g[l, 2], ffn_w1[l, 1], ffn_w3[l, 1], ffn_w2[l, 1])
    return rms_norm(h_lat, final_norm_g)
```

```python
import functools

import jax
import jax.numpy as jnp
from jax import lax
from jax.experimental import pallas as pl
from jax.experimental.pallas import tpu as pltpu

D_MODEL = 1024
D_FF = 2816
N_MOD = 9
EPS = 1e-6
FFN_RESIDUAL = 0.5

GDN_WIDTH = 512
GDN_HEADS = 4
GDN_DK = 128
GDN_CHUNK = 64
CONV_W = 4
LRU_WIDTH = 512
LRU_BLOCKS = 8
LRU_BW = 64
LRU_C = 8.0
GRID_W = 64

TOKEN_TILE = 256
HALO = 8
BA_PAD = 128
P_COLS = 3 * GDN_WIDTH + GDN_WIDTH + 2 * LRU_WIDTH + BA_PAD
VMEM_LIMIT = 56 * 1024 * 1024

BF16 = jnp.bfloat16
F32 = jnp.float32
HIGHEST = lax.Precision.HIGHEST
NEG_BIG = -1e30


def _mm(a, b):
    return jnp.dot(a.astype(BF16), b.astype(BF16), preferred_element_type=F32)


def _mm_nt(a, b):
    return lax.dot_general(a.astype(BF16), b.astype(BF16), (((1,), (1,)), ((), ())),
                           preferred_element_type=F32)


def _mm_tn(a, b):
    return lax.dot_general(a.astype(BF16), b.astype(BF16), (((0,), (0,)), ((), ())),
                           preferred_element_type=F32)


_mm_inv = _mm
_mm_kk = _mm_nt
_mm_sol = _mm
_mm_state = _mm
_mm_attn = _mm
_mm_upd = _mm_tn


def _sigmoid(x):
    return 1.0 / (1.0 + jnp.exp(-x))


def _silu(x):
    return x * _sigmoid(x)


def _softplus(x):
    return jnp.maximum(x, 0.0) + jnp.log(1.0 + jnp.exp(-jnp.abs(x)))


def _gelu_tanh(x):
    return 0.5 * x * (1.0 + jnp.tanh(0.7978845608028654 * (x + 0.044715 * x * x * x)))


def _rms_mod(h, g, shift, scale):
    y = h * lax.rsqrt(jnp.mean(h * h, axis=-1, keepdims=True) + EPS) * g
    return y * (1.0 + scale) + shift


def _swiglu(u, w1_ref, w3_ref, w2_ref):
    ub = u.astype(BF16)
    a = jnp.dot(ub, w1_ref[...], preferred_element_type=F32)
    b = jnp.dot(ub, w3_ref[...], preferred_element_type=F32)
    g = (_silu(a) * b).astype(BF16)
    return jnp.dot(g, w2_ref[...], preferred_element_type=F32)


def _dwconv(x, prev, nxt, w):
    t = x.shape[0]
    row = lax.broadcasted_iota(jnp.int32, x.shape, 0)
    xm1 = jnp.where(row == 0, prev[HALO - 1:HALO], pltpu.roll(x, 1, 0))
    xm2 = jnp.where(row == 0, prev[HALO - 2:HALO - 1],
                    jnp.where(row == 1, prev[HALO - 1:HALO], pltpu.roll(x, 2, 0)))
    xp1 = jnp.where(row == t - 1, nxt[0:1], pltpu.roll(x, t - 1, 0))
    return xm2 * w[0:1] + xm1 * w[1:2] + x * w[2:3] + xp1 * w[3:4]


def _ada_kernel(c_ref, w_ref, b_ref, o_ref):
    o_ref[...] = _mm(_silu(c_ref[...]), w_ref[...]) + b_ref[...]


def _ada_call(cvec, w_ada, b_ada):
    rows = cvec.shape[0]
    return pl.pallas_call(
        _ada_kernel,
        grid=(N_MOD,),
        in_specs=[pl.BlockSpec((rows, D_MODEL), lambda k: (0, 0)),
                  pl.BlockSpec((D_MODEL, D_MODEL), lambda k: (0, k)),
                  pl.BlockSpec((1, D_MODEL), lambda k: (0, k))],
        out_specs=pl.BlockSpec((rows, D_MODEL), lambda k: (0, k)),
        out_shape=jax.ShapeDtypeStruct((rows, N_MOD * D_MODEL), F32),
        name="ada",
    )(cvec, w_ada, b_ada)


def _ffn1_in_kernel(n_lat_tiles, x_ref, ctx_ref, mod_ref, ng_ref, w1_ref, w3_ref, w2_ref, win_ref,
                    h1_ref, qkv_ref, z_ref, rx_ref, rg_ref, ba_ref):
    j = pl.program_id(1)
    h = jnp.where(j == n_lat_tiles, ctx_ref[0], x_ref[0])
    mod = mod_ref[0]
    u = _rms_mod(h, ng_ref[0:1], mod[0:1], mod[1:2])
    h1 = h + FFN_RESIDUAL * mod[2:3] * _swiglu(u, w1_ref, w3_ref, w2_ref)
    h1_ref[0] = h1
    u2 = _rms_mod(h1, ng_ref[1:2], mod[3:4], mod[4:5])
    p = jnp.dot(u2.astype(BF16), win_ref[...], preferred_element_type=F32)
    o = 3 * GDN_WIDTH
    qkv_ref[0] = p[:, :o]
    z_ref[0] = p[:, o:o + GDN_WIDTH]
    o += GDN_WIDTH
    rx_ref[0] = p[:, o:o + LRU_WIDTH]
    o += LRU_WIDTH
    rg_ref[0] = p[:, o:o + LRU_WIDTH]
    o += LRU_WIDTH
    ba_ref[0] = p[:, o:o + BA_PAD]


def _resident(shape):
    return pl.BlockSpec(shape, lambda *_: (0,) * len(shape), pipeline_mode=pl.Buffered(1))


def _ffn1_in_call(x, ctx, mods, norm_g, w1, w3, w2, w_in_r):
    bsz, seq, _ = x.shape
    n_lat = seq // TOKEN_TILE
    n_all = n_lat + 1
    t_all = n_all * TOKEN_TILE
    tok = lambda w: pl.BlockSpec((1, TOKEN_TILE, w), lambda b, j: (b, j, 0))
    out_shape = [jax.ShapeDtypeStruct((bsz, t_all, w), F32)
                 for w in (D_MODEL, 3 * GDN_WIDTH, GDN_WIDTH, LRU_WIDTH, LRU_WIDTH, BA_PAD)]
    return pl.pallas_call(
        functools.partial(_ffn1_in_kernel, n_lat),
        grid=(bsz, n_all),
        in_specs=[pl.BlockSpec((1, TOKEN_TILE, D_MODEL), lambda b, j: (b, jnp.minimum(j, n_lat - 1), 0)),
                  pl.BlockSpec((1, TOKEN_TILE, D_MODEL), lambda b, j: (b, 0, 0)),
                  pl.BlockSpec((1, N_MOD, D_MODEL), lambda b, j: (jnp.where(j == n_lat, 0, b + 1), 0, 0)),
                  _resident((3, D_MODEL)),
                  _resident((D_MODEL, D_FF)), _resident((D_MODEL, D_FF)), _resident((D_FF, D_MODEL)),
                  _resident((D_MODEL, P_COLS))],
        out_specs=[tok(s.shape[-1]) for s in out_shape],
        out_shape=out_shape,
        compiler_params=pltpu.CompilerParams(dimension_semantics=("arbitrary", "arbitrary"),
                                             vmem_limit_bytes=VMEM_LIMIT),
        name="ffn1_in",
    )(x, ctx, mods, norm_g, w1, w3, w2, w_in_r)


def _gdn_prep_kernel(n_lat_tiles, x_ref, prev_ref, next_ref, cw_ref, q_ref, k_ref, v_ref):
    j = pl.program_id(1)
    has_prev = jnp.logical_and(j != 0, j != n_lat_tiles)
    has_next = j < n_lat_tiles - 1
    prev = jnp.where(has_prev, prev_ref[0], 0.0)
    nxt = jnp.where(has_next, next_ref[0], 0.0)
    y = _silu(_dwconv(x_ref[0], prev, nxt, cw_ref[...]))
    for h in range(GDN_HEADS):
        lo = h * GDN_DK
        q = y[:, lo:lo + GDN_DK]
        k = y[:, GDN_WIDTH + lo:GDN_WIDTH + lo + GDN_DK]
        q_ref[0, :, lo:lo + GDN_DK] = q * (lax.rsqrt(jnp.sum(q * q, axis=-1, keepdims=True) + EPS)
                                           * (GDN_DK ** -0.5))
        k_ref[0, :, lo:lo + GDN_DK] = k * lax.rsqrt(jnp.sum(k * k, axis=-1, keepdims=True) + EPS)
    v_ref[0] = y[:, 2 * GDN_WIDTH:]


def _gdn_prep_call(qkv, conv_w, n_lat):
    bsz, t_all, width = qkv.shape
    n_all = t_all // TOKEN_TILE
    per_tile = TOKEN_TILE // HALO
    last_halo = t_all // HALO - 1
    out = jax.ShapeDtypeStruct((bsz, t_all, GDN_WIDTH), F32)
    tok = pl.BlockSpec((1, TOKEN_TILE, GDN_WIDTH), lambda b, j: (b, j, 0))
    return pl.pallas_call(
        functools.partial(_gdn_prep_kernel, n_lat),
        grid=(bsz, n_all),
        in_specs=[pl.BlockSpec((1, TOKEN_TILE, width), lambda b, j: (b, j, 0)),
                  pl.BlockSpec((1, HALO, width), lambda b, j: (b, jnp.maximum(j * per_tile - 1, 0), 0)),
                  pl.BlockSpec((1, HALO, width),
                               lambda b, j: (b, jnp.minimum((j + 1) * per_tile, last_halo), 0)),
                  pl.BlockSpec((CONV_W, width), lambda b, j: (0, 0))],
        out_specs=[tok, tok, tok],
        out_shape=[out, out, out],
        compiler_params=pltpu.CompilerParams(dimension_semantics=("arbitrary", "arbitrary"),
                                             vmem_limit_bytes=VMEM_LIMIT),
        name="gdn_prep",
    )(qkv, qkv, qkv, conv_w)


def _unit_tri_inverse(a):
    n = a.shape[0]
    row = lax.broadcasted_iota(jnp.int32, (n, n), 0)
    col = lax.broadcasted_iota(jnp.int32, (n, n), 1)

    def coupling(k):
        return jnp.logical_and((row >> k) != (col >> k), (row >> (k + 1)) == (col >> (k + 1)))

    t = (row == col).astype(F32) - jnp.where(coupling(0), a, 0.0)
    k = 1
    while 2 ** k < n:
        l = jnp.where(coupling(k), a, 0.0)
        t = t - _mm_inv(t, _mm_inv(l, t))
        k += 1
    return t


def _gdn_direction(reverse, q_ref, k_ref, v_ref, ba_ref, alog, dtb, o_ref, s_ref):
    c = GDN_CHUNK
    row = lax.broadcasted_iota(jnp.int32, (c, c), 0)
    col = lax.broadcasted_iota(jnp.int32, (c, c), 1)
    incl = (row <= col) if reverse else (row >= col)
    strict = (row < col) if reverse else (row > col)
    last = 0 if reverse else c - 1
    d = 1 if reverse else 0

    ba = ba_ref[0]
    lane = lax.broadcasted_iota(jnp.int32, (1, BA_PAD), 1)
    is_g = jnp.logical_and(lane >= 2 * GDN_HEADS, lane < 4 * GDN_HEADS)
    beta_all = _sigmoid(ba)
    g_all = jnp.where(is_g, -jnp.exp(alog) * _softplus(ba + dtb), 0.0)
    gcum = jnp.dot(incl.astype(F32), g_all, precision=HIGHEST, preferred_element_type=F32)
    lrow = lax.broadcasted_iota(jnp.int32, (BA_PAD, BA_PAD), 0)
    lcol = lax.broadcasted_iota(jnp.int32, (BA_PAD, BA_PAD), 1)
    eye_l = (lrow == lcol).astype(F32)
    gcum_t = lax.dot_general(eye_l, gcum, (((1,), (1,)), ((), ())), precision=HIGHEST,
                             preferred_element_type=F32)

    for h in range(GDN_HEADS):
        lo = h * GDN_DK
        cb = d * GDN_HEADS + h
        cg = 2 * GDN_HEADS + cb
        beta = beta_all[:, cb:cb + 1]
        gc = gcum[:, cg:cg + 1]
        gr = gcum_t[cg:cg + 1, :]
        g_last = gcum[last:last + 1, cg:cg + 1]
        q = q_ref[0, :, lo:lo + GDN_DK]
        k = k_ref[0, :, lo:lo + GDN_DK]
        v = v_ref[0, :, lo:lo + GDN_DK]

        decay = jnp.exp(jnp.where(incl, gc - gr, NEG_BIG))
        kb = k * beta
        prods = _mm_kk(jnp.concatenate([kb, q], axis=0), k)
        a_mat = jnp.where(strict, prods[:c] * decay, 0.0)
        attn = prods[c:] * decay
        t_inv = _unit_tri_inverse(a_mat)
        e_gc = jnp.exp(gc)
        sol = _mm_sol(t_inv, jnp.concatenate([v * beta, kb * e_gc], axis=1))
        u = sol[:, :GDN_DK]
        w = sol[:, GDN_DK:]

        s = s_ref[h]
        both = _mm_state(jnp.concatenate([w, q * e_gc], axis=0), s)
        v_new = u - both[:c]
        o_ref[0, :, lo:lo + GDN_DK] = both[c:] + _mm_attn(attn, v_new)
        k_dec = k * jnp.exp(g_last - gc)
        s_ref[h] = s * jnp.exp(g_last) + _mm_upd(k_dec, v_new)


def _gdn_scan_kernel(qf_ref, kf_ref, vf_ref, baf_ref, qb_ref, kb_ref, vb_ref, bab_ref, alog_ref, dtb_ref,
                     of_ref, ob_ref, sf_ref, sb_ref):
    @pl.when(pl.program_id(1) == 0)
    def _():
        sf_ref[...] = jnp.zeros_like(sf_ref)
        sb_ref[...] = jnp.zeros_like(sb_ref)

    alog = alog_ref[...]
    dtb = dtb_ref[...]
    _gdn_direction(False, qf_ref, kf_ref, vf_ref, baf_ref, alog, dtb, of_ref, sf_ref)
    _gdn_direction(True, qb_ref, kb_ref, vb_ref, bab_ref, alog, dtb, ob_ref, sb_ref)


def _gdn_scan_call(q, k, v, ba, alog_vec, dtb_vec, n_lat_tok):
    bsz, t_all, _ = q.shape
    c = GDN_CHUNK
    n_lat = n_lat_tok // c
    n_all = t_all // c
    n_ctx = n_all - n_lat
    fwd = lambda s: jnp.where(s < n_ctx, n_lat + s, s - n_ctx)
    bwd = lambda s: n_all - 1 - s
    fwd_out = lambda s: jnp.maximum(s - n_ctx, 0)
    bwd_out = lambda s: jnp.minimum(n_all - 1 - s, n_lat - 1)
    spec = lambda width, f: pl.BlockSpec((1, c, width), lambda b, s: (b, f(s), 0))
    vec = pl.BlockSpec((1, BA_PAD), lambda b, s: (0, 0))
    out = jax.ShapeDtypeStruct((bsz, n_lat_tok, GDN_WIDTH), F32)
    state = pltpu.VMEM((GDN_HEADS, GDN_DK, GDN_DK), F32)
    return pl.pallas_call(
        _gdn_scan_kernel,
        grid=(bsz, n_all),
        in_specs=[spec(GDN_WIDTH, fwd)] * 3 + [spec(BA_PAD, fwd)]
                 + [spec(GDN_WIDTH, bwd)] * 3 + [spec(BA_PAD, bwd)] + [vec, vec],
        out_specs=[spec(GDN_WIDTH, fwd_out), spec(GDN_WIDTH, bwd_out)],
        out_shape=[out, out],
        scratch_shapes=[state, state],
        compiler_params=pltpu.CompilerParams(dimension_semantics=("arbitrary", "arbitrary"),
                                             vmem_limit_bytes=VMEM_LIMIT),
        name="gdn_scan",
    )(q, k, v, ba, q, k, v, ba, alog_vec, dtb_vec)


def _lin_scan(a, b, reverse):
    r = a.shape[0]
    row = lax.broadcasted_iota(jnp.int32, a.shape, 0)
    s = 1
    while s < r:
        ok = (row < r - s) if reverse else (row >= s)
        shift = r - s if reverse else s
        a_sh = jnp.where(ok, pltpu.roll(a, shift, 0), 1.0)
        b_sh = jnp.where(ok, pltpu.roll(b, shift, 0), 0.0)
        b = a * b_sh + b
        a = a * a_sh
        s *= 2
    return a, b


def _lru_direction(reverse, reset_first, n_slabs, slab, x_ref, prev_ref, next_ref, cw, cb, wr_ref, wi_ref,
                   br, bi, lam, h_ref, carry_ref):
    bsz, r, width = x_ref.shape
    step = pl.program_id(0)
    has_prev = slab > 0
    has_next = slab < n_slabs - 1
    xs = []
    for b in range(bsz):
        prev = jnp.where(has_prev, prev_ref[b], 0.0)
        nxt = jnp.where(has_next, next_ref[b], 0.0)
        xs.append(_dwconv(x_ref[b], prev, nxt, cw) + cb)
    x = jnp.concatenate(xs, axis=0)
    xb = x.astype(BF16)
    gate_r = _sigmoid(jnp.dot(xb, wr_ref[...], preferred_element_type=F32) + br)
    gate_i = _sigmoid(jnp.dot(xb, wi_ref[...], preferred_element_type=F32) + bi)
    log_a = (-LRU_C * _softplus(-lam)) * gate_r
    a = jnp.exp(log_a)
    mult = jnp.sqrt(1.0 - jnp.exp(2.0 * log_a))
    if reset_first:
        row = lax.broadcasted_iota(jnp.int32, (r, width), 0)
        first = (r - 1) if reverse else 0
        is_first = jnp.logical_and(step == 0, row == first)
    edge = 0 if reverse else r - 1
    for b in range(bsz):
        sl = slice(b * r, (b + 1) * r)
        m = mult[sl]
        if reset_first:
            m = jnp.where(is_first, 1.0, m)
        a_cum, h = _lin_scan(a[sl], m * (gate_i[sl] * x[sl]), reverse)
        h = h + a_cum * carry_ref[b:b + 1]
        h_ref[b] = h
        carry_ref[b:b + 1] = h[edge:edge + 1]


def _lru_kernel(reset_first, n_slabs,
                xf_ref, pf_ref, nf_ref, xb_ref, pb_ref, nb_ref, cw_ref, cb_ref,
                wrf_ref, wif_ref, wrb_ref, wib_ref, bg_ref, lam_ref, h0_ref,
                hf_ref, hb_ref, fin_ref, cf_ref, cbk_ref):
    step = pl.program_id(0)

    @pl.when(step == 0)
    def _():
        cf_ref[...] = h0_ref[0]
        cbk_ref[...] = h0_ref[1]

    cw = cw_ref[...]
    cb = cb_ref[...]
    _lru_direction(False, reset_first, n_slabs, step, xf_ref, pf_ref, nf_ref, cw, cb, wrf_ref, wif_ref,
                   bg_ref[0:1], bg_ref[1:2], lam_ref[0:1], hf_ref, cf_ref)
    _lru_direction(True, reset_first, n_slabs, n_slabs - 1 - step, xb_ref, pb_ref, nb_ref, cw, cb,
                   wrb_ref, wib_ref, bg_ref[2:3], bg_ref[3:4], lam_ref[1:2], hb_ref, cbk_ref)
    fin_ref[0] = cf_ref[...]
    fin_ref[1] = cbk_ref[...]


def _lru_call(rx, n_lat_tok, latents, conv_w, conv_b, w_dense, b_gate, lam, h0):
    bsz, t_all, width = rx.shape
    r = GRID_W
    per = r // HALO
    if latents:
        n_slabs = n_lat_tok // r
        x3 = rx.reshape(bsz, t_all // n_slabs, n_slabs * width)
        main = lambda f: pl.BlockSpec((bsz, r, width), lambda s: (0, 0, f(s)))
        prev = lambda f: pl.BlockSpec((bsz, HALO, width), lambda s: (0, per - 1, jnp.maximum(f(s) - 1, 0)))
        nxt = lambda f: pl.BlockSpec((bsz, HALO, width), lambda s: (0, 0, jnp.minimum(f(s) + 1, n_slabs - 1)))
        out_main = main
        out_shape = (bsz, r, n_slabs * width)
    else:
        n_slabs = (t_all - n_lat_tok) // r
        first = n_lat_tok // r
        x3 = rx
        main = lambda f: pl.BlockSpec((bsz, r, width), lambda s: (0, first + f(s), 0))
        prev = lambda f: pl.BlockSpec((bsz, HALO, width), lambda s: (0, (first + f(s)) * per - 1, 0))
        nxt = lambda f: pl.BlockSpec(
            (bsz, HALO, width), lambda s: (0, jnp.minimum((first + f(s) + 1) * per, t_all // HALO - 1), 0))
        out_main = lambda f: pl.BlockSpec((bsz, r, width), lambda s: (0, f(s), 0))
        out_shape = (bsz, n_slabs * r, width)
    fwd = lambda s: s
    bwd = lambda s: n_slabs - 1 - s
    full = lambda shape: pl.BlockSpec(shape, lambda s: (0,) * len(shape))
    carry = pltpu.VMEM((bsz, width), F32)
    return pl.pallas_call(
        functools.partial(_lru_kernel, not latents, n_slabs),
        grid=(n_slabs,),
        in_specs=[main(fwd), prev(fwd), nxt(fwd), main(bwd), prev(bwd), nxt(bwd),
                  full((CONV_W, width)), full((1, width))]
                 + [full((width, width))] * 4
                 + [full((4, width)), full((2, width)), full((2, bsz, width))],
        out_specs=[out_main(fwd), out_main(bwd), full((2, bsz, width))],
        out_shape=[jax.ShapeDtypeStruct(out_shape, F32), jax.ShapeDtypeStruct(out_shape, F32),
                   jax.ShapeDtypeStruct((2, bsz, width), F32)],
        scratch_shapes=[carry, carry],
        compiler_params=pltpu.CompilerParams(dimension_semantics=("arbitrary",),
                                             vmem_limit_bytes=VMEM_LIMIT),
        name="lru_lat" if latents else "lru_ctx",
    )(x3, x3, x3, x3, x3, x3, conv_w, conv_b, *w_dense, b_gate, lam, h0)


def _out_ffn2_kernel(h1_ref, of_ref, ob_ref, z_ref, hf_ref, hb_ref, rg_ref, mod_ref, ng_ref, gnw_ref,
                     wout_ref, w1_ref, w3_ref, w2_ref, fg_ref, out_ref):
    mod = mod_ref[0]
    o = of_ref[0] + ob_ref[0]
    z = z_ref[0]
    parts = []
    for h in range(GDN_HEADS):
        sl = slice(h * GDN_DK, (h + 1) * GDN_DK)
        oh = o[:, sl]
        parts.append(oh * lax.rsqrt(jnp.mean(oh * oh, axis=-1, keepdims=True) + EPS)
                     * gnw_ref[...] * _silu(z[:, sl]))
    parts.append((hf_ref[0] + hb_ref[0]) * _gelu_tanh(rg_ref[0]))
    y = jnp.dot(jnp.concatenate(parts, axis=1).astype(BF16), wout_ref[...], preferred_element_type=F32)
    h2 = h1_ref[0] + mod[5:6] * y
    u = _rms_mod(h2, ng_ref[2:3], mod[6:7], mod[7:8])
    h3 = h2 + FFN_RESIDUAL * mod[8:9] * _swiglu(u, w1_ref, w3_ref, w2_ref)
    out_ref[0] = h3 * lax.rsqrt(jnp.mean(h3 * h3, axis=-1, keepdims=True) + EPS) * fg_ref[...]


def _out_ffn2_call(h1, o_f, o_b, z, h_f, h_b, rg, mods, norm_g, gdn_norm_w, w_out, w1, w3, w2, final_g):
    bsz, seq, _ = o_f.shape
    n_lat = seq // TOKEN_TILE
    tok = lambda w: pl.BlockSpec((1, TOKEN_TILE, w), lambda b, j: (b, j, 0))
    return pl.pallas_call(
        _out_ffn2_kernel,
        grid=(bsz, n_lat),
        in_specs=[tok(D_MODEL), tok(GDN_WIDTH), tok(GDN_WIDTH), tok(GDN_WIDTH),
                  tok(LRU_WIDTH), tok(LRU_WIDTH), tok(LRU_WIDTH),
                  pl.BlockSpec((1, N_MOD, D_MODEL), lambda b, j: (b + 1, 0, 0)),
                  _resident((3, D_MODEL)), _resident((1, GDN_DK)),
                  _resident((D_MODEL, D_MODEL)),
                  _resident((D_MODEL, D_FF)), _resident((D_MODEL, D_FF)), _resident((D_FF, D_MODEL)),
                  _resident((1, D_MODEL))],
        out_specs=tok(D_MODEL),
        out_shape=jax.ShapeDtypeStruct((bsz, seq, D_MODEL), F32),
        compiler_params=pltpu.CompilerParams(dimension_semantics=("arbitrary", "arbitrary"),
                                             vmem_limit_bytes=VMEM_LIMIT),
        name="out_ffn2",
    )(h1, o_f, o_b, z, h_f, h_b, rg, mods, norm_g, gdn_norm_w, w_out, w1, w3, w2, final_g)


def _block_diag(w):
    nb, bw, _ = w.shape
    eye = jnp.eye(nb, dtype=w.dtype)
    return jnp.einsum('ncd,nm->ncmd', w, eye).reshape(nb * bw, nb * bw)


def kernel(x, c, ctx, c_ctx, w_ada, b_ada, norm_g, ffn_w1, ffn_w3, ffn_w2, w_in, w_out, gdn_conv_w, gdn_a_log,
           gdn_dt_bias, gdn_norm_w, lru_conv_w, lru_conv_b, lru_w_gate, lru_b_gate, lru_lambda, final_norm_g):
    bsz, seq, _ = x.shape
    assert w_ada.shape[0] == 1 and seq == GRID_W * GRID_W and ctx.shape[1] == TOKEN_TILE

    rows = -(-(bsz + 1) // 8) * 8
    cvec = jnp.concatenate([c_ctx[None, :], c, jnp.zeros((rows - bsz - 1, D_MODEL), F32)], axis=0)
    mods = _ada_call(cvec, w_ada[0], b_ada).reshape(rows, N_MOD, D_MODEL)

    wi = w_in[0]
    o_z, o_ba = 3 * GDN_WIDTH, 4 * GDN_WIDTH
    o_rx = o_ba + 4 * GDN_HEADS
    w_in_r = jnp.concatenate([wi[:, :o_ba], wi[:, o_rx:], wi[:, o_ba:o_rx],
                              jnp.zeros((D_MODEL, BA_PAD - 4 * GDN_HEADS), F32)], axis=1).astype(BF16)
    w1 = ffn_w1[0].astype(BF16)
    w3 = ffn_w3[0].astype(BF16)
    w2 = ffn_w2[0].astype(BF16)

    h1, qkv, z, rx, rg, ba = _ffn1_in_call(x, ctx, mods, norm_g[0], w1[0], w3[0], w2[0], w_in_r)

    q, k, v = _gdn_prep_call(qkv, gdn_conv_w[0], seq // TOKEN_TILE)
    pad_g = lambda t: jnp.zeros((1, BA_PAD), F32).at[0, 2 * GDN_HEADS:4 * GDN_HEADS].set(t.reshape(-1))
    o_f, o_b = _gdn_scan_call(q, k, v, ba, pad_g(gdn_a_log[0]), pad_g(gdn_dt_bias[0]), seq)

    wg = lru_w_gate[0]
    w_dense = [_block_diag(wg[d, g]).astype(BF16) for d in range(2) for g in range(2)]
    b_gate = lru_b_gate[0].reshape(4, LRU_WIDTH)
    lam = lru_lambda[0]
    cw, cb = lru_conv_w[0], lru_conv_b
    _, _, h_ctx = _lru_call(rx, seq, False, cw, cb, w_dense, b_gate, lam, jnp.zeros((2, bsz, LRU_WIDTH), F32))
    h_f, h_b, _ = _lru_call(rx, seq, True, cw, cb, w_dense, b_gate, lam, h_ctx)
    h_f = h_f.reshape(bsz, seq, LRU_WIDTH)
    h_b = h_b.reshape(bsz, seq, LRU_WIDTH)

    return _out_ffn2_call(h1, o_f, o_b, z, h_f, h_b, rg, mods, norm_g[0], gdn_norm_w,
                          w_out[0].astype(BF16), w1[1], w3[1], w2[1], final_norm_g[None, :])
```

```python
import functools

import jax
import jax.numpy as jnp
from jax import lax
from jax.experimental import pallas as pl
from jax.experimental.pallas import tpu as pltpu

D_MODEL = 1024
D_FF = 2816
N_MOD = 9
EPS = 1e-6
FFN_RESIDUAL = 0.5

GDN_WIDTH = 512
GDN_HEADS = 4
GDN_DK = 128
GDN_CHUNK = 64
CONV_W = 4
LRU_WIDTH = 512
LRU_BLOCKS = 8
LRU_BW = 64
LRU_C = 8.0
GRID_W = 64

TOKEN_TILE = 256
GDN_TILE_CHUNKS = 2
HALO = 8
BA_PAD = 128
P_COLS = 3 * GDN_WIDTH + GDN_WIDTH + 2 * LRU_WIDTH + BA_PAD
VMEM_LIMIT = 56 * 1024 * 1024

BF16 = jnp.bfloat16
F32 = jnp.float32
HIGHEST = lax.Precision.HIGHEST
NEG_BIG = -1e30


def _mm(a, b):
    return jnp.dot(a.astype(BF16), b.astype(BF16), preferred_element_type=F32)


def _mm_nt(a, b):
    return lax.dot_general(a.astype(BF16), b.astype(BF16), (((1,), (1,)), ((), ())),
                           preferred_element_type=F32)


def _sigmoid(x):
    return 1.0 / (1.0 + jnp.exp(-x))


def _silu(x):
    return x * _sigmoid(x)


def _softplus(x):
    return jnp.maximum(x, 0.0) + jnp.log(1.0 + jnp.exp(-jnp.abs(x)))


def _gelu_tanh(x):
    return 0.5 * x * (1.0 + jnp.tanh(0.7978845608028654 * (x + 0.044715 * x * x * x)))


def _rms_mod(h, g, shift, scale):
    y = h * lax.rsqrt(jnp.mean(h * h, axis=-1, keepdims=True) + EPS) * g
    return y * (1.0 + scale) + shift


def _swiglu(u, w1_ref, w3_ref, w2_ref):
    ub = u.astype(BF16)
    a = jnp.dot(ub, w1_ref[...], preferred_element_type=F32)
    b = jnp.dot(ub, w3_ref[...], preferred_element_type=F32)
    g = (_silu(a) * b).astype(BF16)
    return jnp.dot(g, w2_ref[...], preferred_element_type=F32)


def _dwconv(x, prev, nxt, w):
    t = x.shape[0]
    row = lax.broadcasted_iota(jnp.int32, x.shape, 0)
    xm1 = jnp.where(row == 0, prev[HALO - 1:HALO], pltpu.roll(x, 1, 0))
    xm2 = jnp.where(row == 0, prev[HALO - 2:HALO - 1],
                    jnp.where(row == 1, prev[HALO - 1:HALO], pltpu.roll(x, 2, 0)))
    xp1 = jnp.where(row == t - 1, nxt[0:1], pltpu.roll(x, t - 1, 0))
    return xm2 * w[0:1] + xm1 * w[1:2] + x * w[2:3] + xp1 * w[3:4]


def _eye(n):
    return (lax.broadcasted_iota(jnp.int32, (n, n), 0) == lax.broadcasted_iota(jnp.int32, (n, n), 1)).astype(F32)


def _ada_kernel(c_ref, w_ref, b_ref, o_ref):
    o_ref[...] = _mm(_silu(c_ref[...]), w_ref[...]) + b_ref[...]


def _ada_call(cvec, w_ada, b_ada):
    rows = cvec.shape[0]
    return pl.pallas_call(
        _ada_kernel,
        grid=(N_MOD,),
        in_specs=[pl.BlockSpec((rows, D_MODEL), lambda k: (0, 0)),
                  pl.BlockSpec((D_MODEL, D_MODEL), lambda k: (0, k)),
                  pl.BlockSpec((1, D_MODEL), lambda k: (0, k))],
        out_specs=pl.BlockSpec((rows, D_MODEL), lambda k: (0, k)),
        out_shape=jax.ShapeDtypeStruct((rows, N_MOD * D_MODEL), F32),
        name="ada",
    )(cvec, w_ada, b_ada)


def _ffn1_in_kernel(n_lat_tiles, x_ref, ctx_ref, mod_ref, ng_ref, w1_ref, w3_ref, w2_ref, win_ref,
                    h1_ref, qkv_ref, z_ref, rx_ref, rg_ref, ba_ref):
    j = pl.program_id(1)
    h = jnp.where(j == n_lat_tiles, ctx_ref[0], x_ref[0])
    mod = mod_ref[0]
    u = _rms_mod(h, ng_ref[0:1], mod[0:1], mod[1:2])
    h1 = h + FFN_RESIDUAL * mod[2:3] * _swiglu(u, w1_ref, w3_ref, w2_ref)
    h1_ref[0] = h1
    u2 = _rms_mod(h1, ng_ref[1:2], mod[3:4], mod[4:5])
    p = jnp.dot(u2.astype(BF16), win_ref[...], preferred_element_type=F32)
    o = 3 * GDN_WIDTH
    qkv_ref[0] = p[:, :o]
    z_ref[0] = p[:, o:o + GDN_WIDTH]
    o += GDN_WIDTH
    rx_ref[0] = p[:, o:o + LRU_WIDTH]
    o += LRU_WIDTH
    rg_ref[0] = p[:, o:o + LRU_WIDTH]
    o += LRU_WIDTH
    ba_ref[0] = p[:, o:o + BA_PAD]


def _resident(shape):
    return pl.BlockSpec(shape, lambda *_: (0,) * len(shape), pipeline_mode=pl.Buffered(1))


def _ffn1_in_call(x, ctx, mods, norm_g, w1, w3, w2, w_in_r):
    bsz, seq, _ = x.shape
    n_lat = seq // TOKEN_TILE
    n_all = n_lat + 1
    t_all = n_all * TOKEN_TILE
    tok = lambda w: pl.BlockSpec((1, TOKEN_TILE, w), lambda b, j: (b, j, 0))
    out_shape = [jax.ShapeDtypeStruct((bsz, t_all, w), F32)
                 for w in (D_MODEL, 3 * GDN_WIDTH, GDN_WIDTH, LRU_WIDTH, LRU_WIDTH, BA_PAD)]
    return pl.pallas_call(
        functools.partial(_ffn1_in_kernel, n_lat),
        grid=(bsz, n_all),
        in_specs=[pl.BlockSpec((1, TOKEN_TILE, D_MODEL), lambda b, j: (b, jnp.minimum(j, n_lat - 1), 0)),
                  pl.BlockSpec((1, TOKEN_TILE, D_MODEL), lambda b, j: (b, 0, 0)),
                  pl.BlockSpec((1, N_MOD, D_MODEL), lambda b, j: (jnp.where(j == n_lat, 0, b + 1), 0, 0)),
                  _resident((3, D_MODEL)),
                  _resident((D_MODEL, D_FF)), _resident((D_MODEL, D_FF)), _resident((D_FF, D_MODEL)),
                  _resident((D_MODEL, P_COLS))],
        out_specs=[tok(s.shape[-1]) for s in out_shape],
        out_shape=out_shape,
        compiler_params=pltpu.CompilerParams(dimension_semantics=("arbitrary", "arbitrary"),
                                             vmem_limit_bytes=VMEM_LIMIT),
        name="ffn1_in",
    )(x, ctx, mods, norm_g, w1, w3, w2, w_in_r)


def _unit_tri_inverses(a_mats):
    n = a_mats[0].shape[0]
    row = lax.broadcasted_iota(jnp.int32, (n, n), 0)
    col = lax.broadcasted_iota(jnp.int32, (n, n), 1)

    def coupling(k):
        return jnp.logical_and((row >> k) != (col >> k), (row >> (k + 1)) == (col >> (k + 1)))

    eye = (row == col).astype(F32)
    ts = [eye - jnp.where(coupling(0), a, 0.0) for a in a_mats]
    k = 1
    while 2 ** k < n:
        mask = coupling(k)
        lt = [_mm(jnp.where(mask, a, 0.0), t) for a, t in zip(a_mats, ts)]
        tlt = [_mm(t, y) for t, y in zip(ts, lt)]
        ts = [t - z for t, z in zip(ts, tlt)]
        k += 1
    return ts


def _gdn_local_kernel(n_lat_tiles, n_all_tiles, x_ref, prev_ref, next_ref, ba_ref, cw_ref, alog_ref, dtb_ref,
                      uf_ref, wqf_ref, akf_ref, egf_ref, ub_ref, wqb_ref, akb_ref, egb_ref):
    j = pl.program_id(1)
    c = GDN_CHUNK
    seg_start = jnp.logical_or(j == 0, j == n_lat_tiles)
    seg_end = jnp.logical_or(j == n_lat_tiles - 1, j == n_all_tiles - 1)
    prev = jnp.where(seg_start, 0.0, prev_ref[0])
    nxt = jnp.where(seg_end, 0.0, next_ref[0])
    y = _silu(_dwconv(x_ref[0], prev, nxt, cw_ref[...]))

    ba = ba_ref[0]
    lane = lax.broadcasted_iota(jnp.int32, (1, BA_PAD), 1)
    is_g = jnp.logical_and(lane >= 2 * GDN_HEADS, lane < 4 * GDN_HEADS)
    beta_all = _sigmoid(ba)
    g_all = jnp.where(is_g, -jnp.exp(alog_ref[...]) * _softplus(ba + dtb_ref[...]), 0.0)

    row = lax.broadcasted_iota(jnp.int32, (c, c), 0)
    col = lax.broadcasted_iota(jnp.int32, (c, c), 1)
    eye_l = _eye(BA_PAD)
    eye_k = _eye(GDN_DK)
    out_refs = ((uf_ref, wqf_ref, akf_ref, egf_ref), (ub_ref, wqb_ref, akb_ref, egb_ref))
    egf_ref[...] = jnp.zeros_like(egf_ref)
    egb_ref[...] = jnp.zeros_like(egb_ref)

    units = []
    for ci in range(GDN_TILE_CHUNKS):
        rs = slice(ci * c, (ci + 1) * c)
        heads = []
        for h in range(GDN_HEADS):
            lo = h * GDN_DK
            q = y[rs, lo:lo + GDN_DK]
            k = y[rs, GDN_WIDTH + lo:GDN_WIDTH + lo + GDN_DK]
            v = y[rs, 2 * GDN_WIDTH + lo:2 * GDN_WIDTH + lo + GDN_DK]
            q = q * (lax.rsqrt(jnp.sum(q * q, axis=-1, keepdims=True) + EPS) * (GDN_DK ** -0.5))
            k = k * lax.rsqrt(jnp.sum(k * k, axis=-1, keepdims=True) + EPS)
            heads.append((q, k, v, _mm_nt(eye_k, k)))
        for d in range(2):
            incl = (row <= col) if d else (row >= col)
            strict = (row < col) if d else (row > col)
            last = 0 if d else c - 1
            gcum = jnp.dot(incl.astype(F32), g_all[rs], precision=HIGHEST, preferred_element_type=F32)
            gcum_t = lax.dot_general(eye_l, gcum, (((1,), (1,)), ((), ())), precision=HIGHEST,
                                     preferred_element_type=F32)
            for h in range(GDN_HEADS):
                q, k, v, k_t = heads[h]
                cb = d * GDN_HEADS + h
                cg = 2 * GDN_HEADS + cb
                beta = beta_all[rs, cb:cb + 1]
                gc = gcum[:, cg:cg + 1]
                gr = gcum_t[cg:cg + 1, :]
                g_last = gcum[last:last + 1, cg:cg + 1]
                decay = jnp.exp(jnp.where(incl, gc - gr, NEG_BIG))
                e_gc = jnp.exp(gc)
                kb = k * beta
                units.append(dict(ci=ci, d=d, h=h, strict=strict, decay=decay, q=q, k=k, kb=kb,
                                  rhs=jnp.concatenate([v * beta, kb * e_gc], axis=1),
                                  qe=q * e_gc, kdt=k_t * jnp.exp(g_last - gr),
                                  eg=jnp.broadcast_to(jnp.exp(g_last), (1, BA_PAD))))

    prods = [_mm_nt(jnp.concatenate([u["kb"], u["q"]], axis=0), u["k"]) for u in units]
    a_mats = [jnp.where(u["strict"], p[:c] * u["decay"], 0.0) for u, p in zip(units, prods)]
    t_invs = _unit_tri_inverses(a_mats)
    sols = [_mm(t, u["rhs"]) for t, u in zip(t_invs, units)]
    for u, p, sol in zip(units, prods, sols):
        u_ref, wq_ref, ak_ref, eg_ref = out_refs[u["d"]]
        ci, h = u["ci"], u["h"]
        hs = slice(h * GDN_DK, (h + 1) * GDN_DK)
        cs = slice(h * c, (h + 1) * c)
        u_ref[0, ci * c:(ci + 1) * c, hs] = sol[:, :GDN_DK]
        wq_ref[0, ci, 0:c, hs] = sol[:, GDN_DK:].astype(BF16)
        wq_ref[0, ci, c:2 * c, hs] = u["qe"].astype(BF16)
        ak_ref[0, ci, 0:c, cs] = (p[c:] * u["decay"]).astype(BF16)
        ak_ref[0, ci, c:3 * c, cs] = u["kdt"].astype(BF16)
        eg_ref[0, ci, h:h + 1, :] = u["eg"]


def _gdn_local_call(qkv, ba, conv_w, alog_vec, dtb_vec, n_lat_tok):
    bsz, t_all, width = qkv.shape
    c = GDN_CHUNK
    tile = GDN_TILE_CHUNKS * c
    n_all = t_all // tile
    n_lat = n_lat_tok // tile
    per_tile = tile // HALO
    last_halo = t_all // HALO - 1
    n_chunks = t_all // c
    cblk = lambda rows, w: pl.BlockSpec((1, GDN_TILE_CHUNKS, rows, w), lambda b, j: (b, j, 0, 0))
    vec = pl.BlockSpec((1, BA_PAD), lambda b, j: (0, 0))
    outs = [(jax.ShapeDtypeStruct((bsz, t_all, GDN_WIDTH), F32),
             pl.BlockSpec((1, tile, GDN_WIDTH), lambda b, j: (b, j, 0))),
            (jax.ShapeDtypeStruct((bsz, n_chunks, 2 * c, GDN_WIDTH), BF16), cblk(2 * c, GDN_WIDTH)),
            (jax.ShapeDtypeStruct((bsz, n_chunks, 3 * c, GDN_HEADS * c), BF16), cblk(3 * c, GDN_HEADS * c)),
            (jax.ShapeDtypeStruct((bsz, n_chunks, HALO, BA_PAD), F32), cblk(HALO, BA_PAD))] * 2
    return pl.pallas_call(
        functools.partial(_gdn_local_kernel, n_lat, n_all),
        grid=(bsz, n_all),
        in_specs=[pl.BlockSpec((1, tile, width), lambda b, j: (b, j, 0)),
                  pl.BlockSpec((1, HALO, width), lambda b, j: (b, jnp.maximum(j * per_tile - 1, 0), 0)),
                  pl.BlockSpec((1, HALO, width),
                               lambda b, j: (b, jnp.minimum((j + 1) * per_tile, last_halo), 0)),
                  pl.BlockSpec((1, tile, BA_PAD), lambda b, j: (b, j, 0)),
                  pl.BlockSpec((CONV_W, width), lambda b, j: (0, 0)), vec, vec],
        out_specs=[o[1] for o in outs],
        out_shape=[o[0] for o in outs],
        compiler_params=pltpu.CompilerParams(dimension_semantics=("arbitrary", "arbitrary"),
                                             vmem_limit_bytes=VMEM_LIMIT),
        name="gdn_local",
    )(qkv, qkv, qkv, ba, conv_w, alog_vec, dtb_vec)


def _gdn_scan_kernel(uf_ref, wqf_ref, akf_ref, egf_ref, ub_ref, wqb_ref, akb_ref, egb_ref,
                     of_ref, ob_ref, sf_ref, sb_ref):
    @pl.when(pl.program_id(0) == 0)
    def _():
        sf_ref[...] = jnp.zeros_like(sf_ref)
        sb_ref[...] = jnp.zeros_like(sb_ref)

    c = GDN_CHUNK
    bsz = uf_ref.shape[0]
    chains = [(refs, b, h)
              for refs in ((uf_ref, wqf_ref, akf_ref, egf_ref, of_ref, sf_ref),
                           (ub_ref, wqb_ref, akb_ref, egb_ref, ob_ref, sb_ref))
              for b in range(bsz) for h in range(GDN_HEADS)]
    hs = lambda h: slice(h * GDN_DK, (h + 1) * GDN_DK)
    states = [refs[5][b * GDN_HEADS + h] for refs, b, h in chains]
    with_s = [jnp.dot(refs[1][b, 0, :, hs(h)], s.astype(BF16), preferred_element_type=F32)
              for (refs, b, h), s in zip(chains, states)]
    v_new = [refs[0][b, :, hs(h)] - r[:c] for (refs, b, h), r in zip(chains, with_s)]
    with_v = [jnp.dot(refs[2][b, 0, :, h * c:(h + 1) * c], v.astype(BF16), preferred_element_type=F32)
              for (refs, b, h), v in zip(chains, v_new)]
    for (refs, b, h), s, r, a in zip(chains, states, with_s, with_v):
        refs[4][b, :, hs(h)] = r[c:] + a[:c]
        refs[5][b * GDN_HEADS + h] = s * refs[3][b, 0, h:h + 1, :] + a[c:]


def _gdn_scan_call(local, n_lat_tok):
    u = local[0]
    bsz, t_all, _ = u.shape
    c = GDN_CHUNK
    n_lat = n_lat_tok // c
    n_all = t_all // c
    n_ctx = n_all - n_lat
    fwd = lambda s: jnp.where(s < n_ctx, n_lat + s, s - n_ctx)
    bwd = lambda s: n_all - 1 - s
    fwd_out = lambda s: jnp.maximum(s - n_ctx, 0)
    bwd_out = lambda s: jnp.minimum(n_all - 1 - s, n_lat - 1)
    tok = lambda f: pl.BlockSpec((bsz, c, GDN_WIDTH), lambda s: (0, f(s), 0))
    cblk = lambda rows, w, f: pl.BlockSpec((bsz, 1, rows, w), lambda s: (0, f(s), 0, 0))
    ins = lambda f: [tok(f), cblk(2 * c, GDN_WIDTH, f), cblk(3 * c, GDN_HEADS * c, f), cblk(HALO, BA_PAD, f)]
    out = jax.ShapeDtypeStruct((bsz, n_lat_tok, GDN_WIDTH), F32)
    state = pltpu.VMEM((bsz * GDN_HEADS, GDN_DK, GDN_DK), F32)
    return pl.pallas_call(
        _gdn_scan_kernel,
        grid=(n_all,),
        in_specs=ins(fwd) + ins(bwd),
        out_specs=[tok(fwd_out), tok(bwd_out)],
        out_shape=[out, out],
        scratch_shapes=[state, state],
        compiler_params=pltpu.CompilerParams(dimension_semantics=("arbitrary",),
                                             vmem_limit_bytes=VMEM_LIMIT),
        name="gdn_scan",
    )(*local)


def _lin_scan(a, b, reverse):
    r = a.shape[0]
    row = lax.broadcasted_iota(jnp.int32, a.shape, 0)
    s = 1
    while s < r:
        ok = (row < r - s) if reverse else (row >= s)
        shift = r - s if reverse else s
        a_sh = jnp.where(ok, pltpu.roll(a, shift, 0), 1.0)
        b_sh = jnp.where(ok, pltpu.roll(b, shift, 0), 0.0)
        b = a * b_sh + b
        a = a * a_sh
        s *= 2
    return a, b


def _lru_direction(reverse, reset_first, n_slabs, slab, x_ref, prev_ref, next_ref, cw, cb, wr_ref, wi_ref,
                   br, bi, lam, h_ref, carry_ref):
    bsz, r, width = x_ref.shape
    step = pl.program_id(0)
    has_prev = slab > 0
    has_next = slab < n_slabs - 1
    xs = []
    for b in range(bsz):
        prev = jnp.where(has_prev, prev_ref[b], 0.0)
        nxt = jnp.where(has_next, next_ref[b], 0.0)
        xs.append(_dwconv(x_ref[b], prev, nxt, cw) + cb)
    x = jnp.concatenate(xs, axis=0)
    xb = x.astype(BF16)
    gate_r = _sigmoid(jnp.dot(xb, wr_ref[...], preferred_element_type=F32) + br)
    gate_i = _sigmoid(jnp.dot(xb, wi_ref[...], preferred_element_type=F32) + bi)
    log_a = (-LRU_C * _softplus(-lam)) * gate_r
    a = jnp.exp(log_a)
    mult = jnp.sqrt(1.0 - jnp.exp(2.0 * log_a))
    if reset_first:
        row = lax.broadcasted_iota(jnp.int32, (r, width), 0)
        first = (r - 1) if reverse else 0
        is_first = jnp.logical_and(step == 0, row == first)
    edge = 0 if reverse else r - 1
    for b in range(bsz):
        sl = slice(b * r, (b + 1) * r)
        m = mult[sl]
        if reset_first:
            m = jnp.where(is_first, 1.0, m)
        a_cum, h = _lin_scan(a[sl], m * (gate_i[sl] * x[sl]), reverse)
        h = h + a_cum * carry_ref[b:b + 1]
        h_ref[b] = h
        carry_ref[b:b + 1] = h[edge:edge + 1]


def _lru_kernel(reset_first, n_slabs,
                xf_ref, pf_ref, nf_ref, xb_ref, pb_ref, nb_ref, cw_ref, cb_ref,
                wrf_ref, wif_ref, wrb_ref, wib_ref, bg_ref, lam_ref, h0_ref,
                hf_ref, hb_ref, fin_ref, cf_ref, cbk_ref):
    step = pl.program_id(0)

    @pl.when(step == 0)
    def _():
        cf_ref[...] = h0_ref[0]
        cbk_ref[...] = h0_ref[1]

    cw = cw_ref[...]
    cb = cb_ref[...]
    _lru_direction(False, reset_first, n_slabs, step, xf_ref, pf_ref, nf_ref, cw, cb, wrf_ref, wif_ref,
                   bg_ref[0:1], bg_ref[1:2], lam_ref[0:1], hf_ref, cf_ref)
    _lru_direction(True, reset_first, n_slabs, n_slabs - 1 - step, xb_ref, pb_ref, nb_ref, cw, cb,
                   wrb_ref, wib_ref, bg_ref[2:3], bg_ref[3:4], lam_ref[1:2], hb_ref, cbk_ref)
    fin_ref[0] = cf_ref[...]
    fin_ref[1] = cbk_ref[...]


def _lru_call(rx, n_lat_tok, latents, conv_w, conv_b, w_dense, b_gate, lam, h0):
    bsz, t_all, width = rx.shape
    r = GRID_W
    per = r // HALO
    if latents:
        n_slabs = n_lat_tok // r
        x3 = rx.reshape(bsz, t_all // n_slabs, n_slabs * width)
        main = lambda f: pl.BlockSpec((bsz, r, width), lambda s: (0, 0, f(s)))
        prev = lambda f: pl.BlockSpec((bsz, HALO, width), lambda s: (0, per - 1, jnp.maximum(f(s) - 1, 0)))
        nxt = lambda f: pl.BlockSpec((bsz, HALO, width), lambda s: (0, 0, jnp.minimum(f(s) + 1, n_slabs - 1)))
        out_main = main
        out_shape = (bsz, r, n_slabs * width)
    else:
        n_slabs = (t_all - n_lat_tok) // r
        first = n_lat_tok // r
        x3 = rx
        main = lambda f: pl.BlockSpec((bsz, r, width), lambda s: (0, first + f(s), 0))
        prev = lambda f: pl.BlockSpec((bsz, HALO, width), lambda s: (0, (first + f(s)) * per - 1, 0))
        nxt = lambda f: pl.BlockSpec(
            (bsz, HALO, width), lambda s: (0, jnp.minimum((first + f(s) + 1) * per, t_all // HALO - 1), 0))
        out_main = lambda f: pl.BlockSpec((bsz, r, width), lambda s: (0, f(s), 0))
        out_shape = (bsz, n_slabs * r, width)
    fwd = lambda s: s
    bwd = lambda s: n_slabs - 1 - s
    full = lambda shape: pl.BlockSpec(shape, lambda s: (0,) * len(shape))
    carry = pltpu.VMEM((bsz, width), F32)
    return pl.pallas_call(
        functools.partial(_lru_kernel, not latents, n_slabs),
        grid=(n_slabs,),
        in_specs=[main(fwd), prev(fwd), nxt(fwd), main(bwd), prev(bwd), nxt(bwd),
                  full((CONV_W, width)), full((1, width))]
                 + [full((width, width))] * 4
                 + [full((4, width)), full((2, width)), full((2, bsz, width))],
        out_specs=[out_main(fwd), out_main(bwd), full((2, bsz, width))],
        out_shape=[jax.ShapeDtypeStruct(out_shape, F32), jax.ShapeDtypeStruct(out_shape, F32),
                   jax.ShapeDtypeStruct((2, bsz, width), F32)],
        scratch_shapes=[carry, carry],
        compiler_params=pltpu.CompilerParams(dimension_semantics=("arbitrary",),
                                             vmem_limit_bytes=VMEM_LIMIT),
        name="lru_lat" if latents else "lru_ctx",
    )(x3, x3, x3, x3, x3, x3, conv_w, conv_b, *w_dense, b_gate, lam, h0)


def _out_ffn2_kernel(h1_ref, of_ref, ob_ref, z_ref, hf_ref, hb_ref, rg_ref, mod_ref, ng_ref, gnw_ref,
                     wout_ref, w1_ref, w3_ref, w2_ref, fg_ref, out_ref):
    mod = mod_ref[0]
    o = of_ref[0] + ob_ref[0]
    z = z_ref[0]
    parts = []
    for h in range(GDN_HEADS):
        sl = slice(h * GDN_DK, (h + 1) * GDN_DK)
        oh = o[:, sl]
        parts.append(oh * lax.rsqrt(jnp.mean(oh * oh, axis=-1, keepdims=True) + EPS)
                     * gnw_ref[...] * _silu(z[:, sl]))
    parts.append((hf_ref[0] + hb_ref[0]) * _gelu_tanh(rg_ref[0]))
    y = jnp.dot(jnp.concatenate(parts, axis=1).astype(BF16), wout_ref[...], preferred_element_type=F32)
    h2 = h1_ref[0] + mod[5:6] * y
    u = _rms_mod(h2, ng_ref[2:3], mod[6:7], mod[7:8])
    h3 = h2 + FFN_RESIDUAL * mod[8:9] * _swiglu(u, w1_ref, w3_ref, w2_ref)
    out_ref[0] = h3 * lax.rsqrt(jnp.mean(h3 * h3, axis=-1, keepdims=True) + EPS) * fg_ref[...]


def _out_ffn2_call(h1, o_f, o_b, z, h_f, h_b, rg, mods, norm_g, gdn_norm_w, w_out, w1, w3, w2, final_g):
    bsz, seq, _ = o_f.shape
    n_lat = seq // TOKEN_TILE
    tok = lambda w: pl.BlockSpec((1, TOKEN_TILE, w), lambda b, j: (b, j, 0))
    return pl.pallas_call(
        _out_ffn2_kernel,
        grid=(bsz, n_lat),
        in_specs=[tok(D_MODEL), tok(GDN_WIDTH), tok(GDN_WIDTH), tok(GDN_WIDTH),
                  tok(LRU_WIDTH), tok(LRU_WIDTH), tok(LRU_WIDTH),
                  pl.BlockSpec((1, N_MOD, D_MODEL), lambda b, j: (b + 1, 0, 0)),
                  _resident((3, D_MODEL)), _resident((1, GDN_DK)),
                  _resident((D_MODEL, D_MODEL)),
                  _resident((D_MODEL, D_FF)), _resident((D_MODEL, D_FF)), _resident((D_FF, D_MODEL)),
                  _resident((1, D_MODEL))],
        out_specs=tok(D_MODEL),
        out_shape=jax.ShapeDtypeStruct((bsz, seq, D_MODEL), F32),
        compiler_params=pltpu.CompilerParams(dimension_semantics=("arbitrary", "arbitrary"),
                                             vmem_limit_bytes=VMEM_LIMIT),
        name="out_ffn2",
    )(h1, o_f, o_b, z, h_f, h_b, rg, mods, norm_g, gdn_norm_w, w_out, w1, w3, w2, final_g)


def _block_diag(w):
    nb, bw, _ = w.shape
    eye = jnp.eye(nb, dtype=w.dtype)
    return jnp.einsum('ncd,nm->ncmd', w, eye).reshape(nb * bw, nb * bw)


def kernel(x, c, ctx, c_ctx, w_ada, b_ada, norm_g, ffn_w1, ffn_w3, ffn_w2, w_in, w_out, gdn_conv_w, gdn_a_log,
           gdn_dt_bias, gdn_norm_w, lru_conv_w, lru_conv_b, lru_w_gate, lru_b_gate, lru_lambda, final_norm_g):
    bsz, seq, _ = x.shape
    assert w_ada.shape[0] == 1 and seq == GRID_W * GRID_W and ctx.shape[1] == TOKEN_TILE

    rows = -(-(bsz + 1) // 8) * 8
    cvec = jnp.concatenate([c_ctx[None, :], c, jnp.zeros((rows - bsz - 1, D_MODEL), F32)], axis=0)
    mods = _ada_call(cvec, w_ada[0], b_ada).reshape(rows, N_MOD, D_MODEL)

    wi = w_in[0]
    o_z, o_ba = 3 * GDN_WIDTH, 4 * GDN_WIDTH
    o_rx = o_ba + 4 * GDN_HEADS
    w_in_r = jnp.concatenate([wi[:, :o_ba], wi[:, o_rx:], wi[:, o_ba:o_rx],
                              jnp.zeros((D_MODEL, BA_PAD - 4 * GDN_HEADS), F32)], axis=1).astype(BF16)
    w1 = ffn_w1[0].astype(BF16)
    w3 = ffn_w3[0].astype(BF16)
    w2 = ffn_w2[0].astype(BF16)

    h1, qkv, z, rx, rg, ba = _ffn1_in_call(x, ctx, mods, norm_g[0], w1[0], w3[0], w2[0], w_in_r)

    pad_g = lambda t: jnp.zeros((1, BA_PAD), F32).at[0, 2 * GDN_HEADS:4 * GDN_HEADS].set(t.reshape(-1))
    local = _gdn_local_call(qkv, ba, gdn_conv_w[0], pad_g(gdn_a_log[0]), pad_g(gdn_dt_bias[0]), seq)
    o_f, o_b = _gdn_scan_call(local, seq)

    wg = lru_w_gate[0]
    w_dense = [_block_diag(wg[d, g]).astype(BF16) for d in range(2) for g in range(2)]
    b_gate = lru_b_gate[0].reshape(4, LRU_WIDTH)
    lam = lru_lambda[0]
    cw, cb = lru_conv_w[0], lru_conv_b
    _, _, h_ctx = _lru_call(rx, seq, False, cw, cb, w_dense, b_gate, lam, jnp.zeros((2, bsz, LRU_WIDTH), F32))
    h_f, h_b, _ = _lru_call(rx, seq, True, cw, cb, w_dense, b_gate, lam, h_ctx)
    h_f = h_f.reshape(bsz, seq, LRU_WIDTH)
    h_b = h_b.reshape(bsz, seq, LRU_WIDTH)

    return _out_ffn2_call(h1, o_f, o_b, z, h_f, h_b, rg, mods, norm_g[0], gdn_norm_w,
                          w_out[0].astype(BF16), w1[1], w3[1], w2[1], final_norm_g[None, :])
```

```python
import functools

import jax
import jax.numpy as jnp
from jax import lax
from jax.experimental import pallas as pl
from jax.experimental.pallas import tpu as pltpu

D_MODEL = 1024
D_FF = 2816
N_MOD = 9
EPS = 1e-6
FFN_RESIDUAL = 0.5

GDN_WIDTH = 512
GDN_HEADS = 4
GDN_DK = 128
GDN_CHUNK = 64
CONV_W = 4
LRU_WIDTH = 512
LRU_BLOCKS = 8
LRU_BW = 64
LRU_C = 8.0
GRID_W = 64

TOKEN_TILE = 256
GDN_TILE_CHUNKS = 2
HALO = 8
BA_PAD = 128
P_COLS = 3 * GDN_WIDTH + GDN_WIDTH + 2 * LRU_WIDTH + BA_PAD
VMEM_LIMIT = 56 * 1024 * 1024

BF16 = jnp.bfloat16
F32 = jnp.float32
HIGHEST = lax.Precision.HIGHEST
NEG_BIG = -1e30


def _mm(a, b):
    return jnp.dot(a.astype(BF16), b.astype(BF16), preferred_element_type=F32)


def _mm_nt(a, b):
    return lax.dot_general(a.astype(BF16), b.astype(BF16), (((1,), (1,)), ((), ())),
                           preferred_element_type=F32)


def _sigmoid(x):
    return 1.0 / (1.0 + jnp.exp(-x))


def _silu(x):
    return x * _sigmoid(x)


def _softplus(x):
    return jnp.maximum(x, 0.0) + jnp.log(1.0 + jnp.exp(-jnp.abs(x)))


def _gelu_tanh(x):
    return 0.5 * x * (1.0 + jnp.tanh(0.7978845608028654 * (x + 0.044715 * x * x * x)))


def _rms_mod(h, g, shift, scale):
    y = h * lax.rsqrt(jnp.mean(h * h, axis=-1, keepdims=True) + EPS) * g
    return y * (1.0 + scale) + shift


def _swiglu(u, w1_ref, w3_ref, w2_ref):
    ub = u.astype(BF16)
    a = jnp.dot(ub, w1_ref[...], preferred_element_type=F32)
    b = jnp.dot(ub, w3_ref[...], preferred_element_type=F32)
    g = (_silu(a) * b).astype(BF16)
    return jnp.dot(g, w2_ref[...], preferred_element_type=F32)


def _dwconv(x, prev, nxt, w):
    t = x.shape[0]
    row = lax.broadcasted_iota(jnp.int32, x.shape, 0)
    xm1 = jnp.where(row == 0, prev[HALO - 1:HALO], pltpu.roll(x, 1, 0))
    xm2 = jnp.where(row == 0, prev[HALO - 2:HALO - 1],
                    jnp.where(row == 1, prev[HALO - 1:HALO], pltpu.roll(x, 2, 0)))
    xp1 = jnp.where(row == t - 1, nxt[0:1], pltpu.roll(x, t - 1, 0))
    return xm2 * w[0:1] + xm1 * w[1:2] + x * w[2:3] + xp1 * w[3:4]


def _eye(n):
    return (lax.broadcasted_iota(jnp.int32, (n, n), 0) == lax.broadcasted_iota(jnp.int32, (n, n), 1)).astype(F32)


def _ada_kernel(c_ref, w_ref, b_ref, o_ref):
    o_ref[...] = _mm(_silu(c_ref[...]), w_ref[...]) + b_ref[...]


def _ada_call(cvec, w_ada, b_ada):
    rows = cvec.shape[0]
    return pl.pallas_call(
        _ada_kernel,
        grid=(N_MOD,),
        in_specs=[pl.BlockSpec((rows, D_MODEL), lambda k: (0, 0)),
                  pl.BlockSpec((D_MODEL, D_MODEL), lambda k: (0, k)),
                  pl.BlockSpec((1, D_MODEL), lambda k: (0, k))],
        out_specs=pl.BlockSpec((rows, D_MODEL), lambda k: (0, k)),
        out_shape=jax.ShapeDtypeStruct((rows, N_MOD * D_MODEL), F32),
        name="ada",
    )(cvec, w_ada, b_ada)


def _ffn1_in_kernel(n_lat_tiles, x_ref, ctx_ref, mod_ref, ng_ref, w1_ref, w3_ref, w2_ref, win_ref,
                    h1_ref, qkv_ref, z_ref, rx_ref, rg_ref, ba_ref):
    j = pl.program_id(1)
    h = jnp.where(j == n_lat_tiles, ctx_ref[0], x_ref[0])
    mod = mod_ref[0]
    u = _rms_mod(h, ng_ref[0:1], mod[0:1], mod[1:2])
    h1 = h + FFN_RESIDUAL * mod[2:3] * _swiglu(u, w1_ref, w3_ref, w2_ref)
    h1_ref[0] = h1
    u2 = _rms_mod(h1, ng_ref[1:2], mod[3:4], mod[4:5])
    p = jnp.dot(u2.astype(BF16), win_ref[...], preferred_element_type=F32)
    o = 3 * GDN_WIDTH
    qkv_ref[0] = p[:, :o]
    z_ref[0] = p[:, o:o + GDN_WIDTH]
    o += GDN_WIDTH
    rx_ref[0] = p[:, o:o + LRU_WIDTH]
    o += LRU_WIDTH
    rg_ref[0] = p[:, o:o + LRU_WIDTH]
    o += LRU_WIDTH
    ba_ref[0] = p[:, o:o + BA_PAD]


def _resident(shape):
    return pl.BlockSpec(shape, lambda *_: (0,) * len(shape), pipeline_mode=pl.Buffered(1))


def _ffn1_in_call(x, ctx, mods, norm_g, w1, w3, w2, w_in_r):
    bsz, seq, _ = x.shape
    n_lat = seq // TOKEN_TILE
    n_all = n_lat + 1
    t_all = n_all * TOKEN_TILE
    tok = lambda w: pl.BlockSpec((1, TOKEN_TILE, w), lambda b, j: (b, j, 0))
    out_shape = [jax.ShapeDtypeStruct((bsz, t_all, w), F32)
                 for w in (D_MODEL, 3 * GDN_WIDTH, GDN_WIDTH, LRU_WIDTH, LRU_WIDTH, BA_PAD)]
    return pl.pallas_call(
        functools.partial(_ffn1_in_kernel, n_lat),
        grid=(bsz, n_all),
        in_specs=[pl.BlockSpec((1, TOKEN_TILE, D_MODEL), lambda b, j: (b, jnp.minimum(j, n_lat - 1), 0)),
                  pl.BlockSpec((1, TOKEN_TILE, D_MODEL), lambda b, j: (b, 0, 0)),
                  pl.BlockSpec((1, N_MOD, D_MODEL), lambda b, j: (jnp.where(j == n_lat, 0, b + 1), 0, 0)),
                  _resident((3, D_MODEL)),
                  _resident((D_MODEL, D_FF)), _resident((D_MODEL, D_FF)), _resident((D_FF, D_MODEL)),
                  _resident((D_MODEL, P_COLS))],
        out_specs=[tok(s.shape[-1]) for s in out_shape],
        out_shape=out_shape,
        compiler_params=pltpu.CompilerParams(dimension_semantics=("arbitrary", "arbitrary"),
                                             vmem_limit_bytes=VMEM_LIMIT),
        name="ffn1_in",
    )(x, ctx, mods, norm_g, w1, w3, w2, w_in_r)


def _unit_tri_inverses(a_mats):
    n = a_mats[0].shape[0]
    row = lax.broadcasted_iota(jnp.int32, (n, n), 0)
    col = lax.broadcasted_iota(jnp.int32, (n, n), 1)

    def coupling(k):
        return jnp.logical_and((row >> k) != (col >> k), (row >> (k + 1)) == (col >> (k + 1)))

    eye = (row == col).astype(F32)
    ts = [eye - jnp.where(coupling(0), a, 0.0) for a in a_mats]
    k = 1
    while 2 ** k < n:
        mask = coupling(k)
        lt = [_mm(jnp.where(mask, a, 0.0), t) for a, t in zip(a_mats, ts)]
        tlt = [_mm(t, y) for t, y in zip(ts, lt)]
        ts = [t - z for t, z in zip(ts, tlt)]
        k += 1
    return ts


def _gdn_local_kernel(n_lat_tiles, n_all_tiles, x_ref, prev_ref, next_ref, ba_ref, cw_ref, alog_ref, dtb_ref,
                      uf_ref, wqf_ref, akf_ref, egf_ref, ub_ref, wqb_ref, akb_ref, egb_ref):
    j = pl.program_id(1)
    c = GDN_CHUNK
    seg_start = jnp.logical_or(j == 0, j == n_lat_tiles)
    seg_end = jnp.logical_or(j == n_lat_tiles - 1, j == n_all_tiles - 1)
    prev = jnp.where(seg_start, 0.0, prev_ref[0])
    nxt = jnp.where(seg_end, 0.0, next_ref[0])
    y = _silu(_dwconv(x_ref[0], prev, nxt, cw_ref[...]))

    ba = ba_ref[0]
    lane = lax.broadcasted_iota(jnp.int32, (1, BA_PAD), 1)
    is_g = jnp.logical_and(lane >= 2 * GDN_HEADS, lane < 4 * GDN_HEADS)
    beta_all = _sigmoid(ba)
    g_all = jnp.where(is_g, -jnp.exp(alog_ref[...]) * _softplus(ba + dtb_ref[...]), 0.0)

    row = lax.broadcasted_iota(jnp.int32, (c, c), 0)
    col = lax.broadcasted_iota(jnp.int32, (c, c), 1)
    eye_l = _eye(BA_PAD)
    eye_k = _eye(GDN_DK)
    out_refs = ((uf_ref, wqf_ref, akf_ref, egf_ref), (ub_ref, wqb_ref, akb_ref, egb_ref))
    egf_ref[...] = jnp.zeros_like(egf_ref)
    egb_ref[...] = jnp.zeros_like(egb_ref)

    units = []
    for ci in range(GDN_TILE_CHUNKS):
        rs = slice(ci * c, (ci + 1) * c)
        heads = []
        for h in range(GDN_HEADS):
            lo = h * GDN_DK
            q = y[rs, lo:lo + GDN_DK]
            k = y[rs, GDN_WIDTH + lo:GDN_WIDTH + lo + GDN_DK]
            v = y[rs, 2 * GDN_WIDTH + lo:2 * GDN_WIDTH + lo + GDN_DK]
            q = q * (lax.rsqrt(jnp.sum(q * q, axis=-1, keepdims=True) + EPS) * (GDN_DK ** -0.5))
            k = k * lax.rsqrt(jnp.sum(k * k, axis=-1, keepdims=True) + EPS)
            heads.append((q, k, v, _mm_nt(eye_k, k)))
        for d in range(2):
            incl = (row <= col) if d else (row >= col)
            strict = (row < col) if d else (row > col)
            last = 0 if d else c - 1
            gcum = jnp.dot(incl.astype(F32), g_all[rs], precision=HIGHEST, preferred_element_type=F32)
            gcum_t = lax.dot_general(eye_l, gcum, (((1,), (1,)), ((), ())), precision=HIGHEST,
                                     preferred_element_type=F32)
            for h in range(GDN_HEADS):
                q, k, v, k_t = heads[h]
                cb = d * GDN_HEADS + h
                cg = 2 * GDN_HEADS + cb
                beta = beta_all[rs, cb:cb + 1]
                gc = gcum[:, cg:cg + 1]
                gr = gcum_t[cg:cg + 1, :]
                g_last = gcum[last:last + 1, cg:cg + 1]
                decay = jnp.exp(jnp.where(incl, gc - gr, NEG_BIG))
                e_gc = jnp.exp(gc)
                kb = k * beta
                units.append(dict(ci=ci, d=d, h=h, strict=strict, decay=decay, q=q, k=k, kb=kb,
                                  rhs=jnp.concatenate([v * beta, kb * e_gc], axis=1),
                                  qe=q * e_gc, kdt=k_t * jnp.exp(g_last - gr),
                                  eg=jnp.broadcast_to(jnp.exp(g_last), (1, BA_PAD))))

    prods = [_mm_nt(jnp.concatenate([u["kb"], u["q"]], axis=0), u["k"]) for u in units]
    a_mats = [jnp.where(u["strict"], p[:c] * u["decay"], 0.0) for u, p in zip(units, prods)]
    t_invs = _unit_tri_inverses(a_mats)
    sols = [_mm(t, u["rhs"]) for t, u in zip(t_invs, units)]
    for u, p, sol in zip(units, prods, sols):
        u_ref, wq_ref, ak_ref, eg_ref = out_refs[u["d"]]
        ci, h = u["ci"], u["h"]
        hs = slice(h * GDN_DK, (h + 1) * GDN_DK)
        cs = slice(h * c, (h + 1) * c)
        u_ref[0, ci * c:(ci + 1) * c, hs] = sol[:, :GDN_DK]
        wq_ref[0, ci, 0:c, hs] = sol[:, GDN_DK:].astype(BF16)
        wq_ref[0, ci, c:2 * c, hs] = u["qe"].astype(BF16)
        ak_ref[0, ci, 0:c, cs] = (p[c:] * u["decay"]).astype(BF16)
        ak_ref[0, ci, c:3 * c, cs] = u["kdt"].astype(BF16)
        eg_ref[0, ci, h:h + 1, :] = u["eg"]


def _gdn_local_call(qkv, ba, conv_w, alog_vec, dtb_vec, n_lat_tok):
    bsz, t_all, width = qkv.shape
    c = GDN_CHUNK
    tile = GDN_TILE_CHUNKS * c
    n_all = t_all // tile
    n_lat = n_lat_tok // tile
    per_tile = tile // HALO
    last_halo = t_all // HALO - 1
    n_chunks = t_all // c
    cblk = lambda rows, w: pl.BlockSpec((1, GDN_TILE_CHUNKS, rows, w), lambda b, j: (b, j, 0, 0))
    vec = pl.BlockSpec((1, BA_PAD), lambda b, j: (0, 0))
    outs = [(jax.ShapeDtypeStruct((bsz, t_all, GDN_WIDTH), F32),
             pl.BlockSpec((1, tile, GDN_WIDTH), lambda b, j: (b, j, 0))),
            (jax.ShapeDtypeStruct((bsz, n_chunks, 2 * c, GDN_WIDTH), BF16), cblk(2 * c, GDN_WIDTH)),
            (jax.ShapeDtypeStruct((bsz, n_chunks, 3 * c, GDN_HEADS * c), BF16), cblk(3 * c, GDN_HEADS * c)),
            (jax.ShapeDtypeStruct((bsz, n_chunks, HALO, BA_PAD), F32), cblk(HALO, BA_PAD))] * 2
    return pl.pallas_call(
        functools.partial(_gdn_local_kernel, n_lat, n_all),
        grid=(bsz, n_all),
        in_specs=[pl.BlockSpec((1, tile, width), lambda b, j: (b, j, 0)),
                  pl.BlockSpec((1, HALO, width), lambda b, j: (b, jnp.maximum(j * per_tile - 1, 0), 0)),
                  pl.BlockSpec((1, HALO, width),
                               lambda b, j: (b, jnp.minimum((j + 1) * per_tile, last_halo), 0)),
                  pl.BlockSpec((1, tile, BA_PAD), lambda b, j: (b, j, 0)),
                  pl.BlockSpec((CONV_W, width), lambda b, j: (0, 0)), vec, vec],
        out_specs=[o[1] for o in outs],
        out_shape=[o[0] for o in outs],
        compiler_params=pltpu.CompilerParams(dimension_semantics=("arbitrary", "arbitrary"),
                                             vmem_limit_bytes=VMEM_LIMIT),
        name="gdn_local",
    )(qkv, qkv, qkv, ba, conv_w, alog_vec, dtb_vec)


def _gdn_scan_kernel(uf_ref, wqf_ref, akf_ref, egf_ref, ub_ref, wqb_ref, akb_ref, egb_ref,
                     of_ref, ob_ref, sf_ref, sb_ref):
    @pl.when(pl.program_id(0) == 0)
    def _():
        sf_ref[...] = jnp.zeros_like(sf_ref)
        sb_ref[...] = jnp.zeros_like(sb_ref)

    c = GDN_CHUNK
    bsz = uf_ref.shape[0]
    chains = [(refs, b, h)
              for refs in ((uf_ref, wqf_ref, akf_ref, egf_ref, of_ref, sf_ref),
                           (ub_ref, wqb_ref, akb_ref, egb_ref, ob_ref, sb_ref))
              for b in range(bsz) for h in range(GDN_HEADS)]
    hs = lambda h: slice(h * GDN_DK, (h + 1) * GDN_DK)
    states = [refs[5][b * GDN_HEADS + h] for refs, b, h in chains]
    with_s = [jnp.dot(refs[1][b, 0, :, hs(h)], s.astype(BF16), preferred_element_type=F32)
              for (refs, b, h), s in zip(chains, states)]
    v_new = [refs[0][b, :, hs(h)] - r[:c] for (refs, b, h), r in zip(chains, with_s)]
    with_v = [jnp.dot(refs[2][b, 0, :, h * c:(h + 1) * c], v.astype(BF16), preferred_element_type=F32)
              for (refs, b, h), v in zip(chains, v_new)]
    for (refs, b, h), s, r, a in zip(chains, states, with_s, with_v):
        refs[4][b, :, hs(h)] = r[c:] + a[:c]
        refs[5][b * GDN_HEADS + h] = s * refs[3][b, 0, h:h + 1, :] + a[c:]


def _gdn_scan_call(local, n_lat_tok):
    u = local[0]
    bsz, t_all, _ = u.shape
    c = GDN_CHUNK
    n_lat = n_lat_tok // c
    n_all = t_all // c
    n_ctx = n_all - n_lat
    fwd = lambda s: jnp.where(s < n_ctx, n_lat + s, s - n_ctx)
    bwd = lambda s: n_all - 1 - s
    fwd_out = lambda s: jnp.maximum(s - n_ctx, 0)
    bwd_out = lambda s: jnp.minimum(n_all - 1 - s, n_lat - 1)
    tok = lambda f: pl.BlockSpec((bsz, c, GDN_WIDTH), lambda s: (0, f(s), 0))
    cblk = lambda rows, w, f: pl.BlockSpec((bsz, 1, rows, w), lambda s: (0, f(s), 0, 0))
    ins = lambda f: [tok(f), cblk(2 * c, GDN_WIDTH, f), cblk(3 * c, GDN_HEADS * c, f), cblk(HALO, BA_PAD, f)]
    out = jax.ShapeDtypeStruct((bsz, n_lat_tok, GDN_WIDTH), F32)
    state = pltpu.VMEM((bsz * GDN_HEADS, GDN_DK, GDN_DK), F32)
    return pl.pallas_call(
        _gdn_scan_kernel,
        grid=(n_all,),
        in_specs=ins(fwd) + ins(bwd),
        out_specs=[tok(fwd_out), tok(bwd_out)],
        out_shape=[out, out],
        scratch_shapes=[state, state],
        compiler_params=pltpu.CompilerParams(dimension_semantics=("arbitrary",),
                                             vmem_limit_bytes=VMEM_LIMIT),
        name="gdn_scan",
    )(*local)


def _lin_scan(a, b, reverse):
    r = a.shape[0]
    row = lax.broadcasted_iota(jnp.int32, a.shape, 0)
    s = 1
    while s < r:
        ok = (row < r - s) if reverse else (row >= s)
        shift = r - s if reverse else s
        a_sh = jnp.where(ok, pltpu.roll(a, shift, 0), 1.0)
        b_sh = jnp.where(ok, pltpu.roll(b, shift, 0), 0.0)
        b = a * b_sh + b
        a = a * a_sh
        s *= 2
    return a, b


def _lru_direction(reverse, reset_first, n_slabs, slab, x_ref, prev_ref, next_ref, cw, cb, wr_ref, wi_ref,
                   br, bi, lam, h_ref, carry_ref):
    bsz, r, width = x_ref.shape
    step = pl.program_id(0)
    has_prev = slab > 0
    has_next = slab < n_slabs - 1
    xs = []
    for b in range(bsz):
        prev = jnp.where(has_prev, prev_ref[b], 0.0)
        nxt = jnp.where(has_next, next_ref[b], 0.0)
        xs.append(_dwconv(x_ref[b], prev, nxt, cw) + cb)
    x = jnp.concatenate(xs, axis=0)
    xb = x.astype(BF16)
    gate_r = _sigmoid(jnp.dot(xb, wr_ref[...], preferred_element_type=F32) + br)
    gate_i = _sigmoid(jnp.dot(xb, wi_ref[...], preferred_element_type=F32) + bi)
    log_a = (-LRU_C * _softplus(-lam)) * gate_r
    a = jnp.exp(log_a)
    mult = jnp.sqrt(1.0 - jnp.exp(2.0 * log_a))
    if reset_first:
        row = lax.broadcasted_iota(jnp.int32, (r, width), 0)
        first = (r - 1) if reverse else 0
        is_first = jnp.logical_and(step == 0, row == first)
    edge = 0 if reverse else r - 1
    for b in range(bsz):
        sl = slice(b * r, (b + 1) * r)
        m = mult[sl]
        if reset_first:
            m = jnp.where(is_first, 1.0, m)
        a_cum, h = _lin_scan(a[sl], m * (gate_i[sl] * x[sl]), reverse)
        h = h + a_cum * carry_ref[b:b + 1]
        h_ref[b] = h
        carry_ref[b:b + 1] = h[edge:edge + 1]


def _lru_kernel(reset_first, n_slabs,
                xf_ref, pf_ref, nf_ref, xb_ref, pb_ref, nb_ref, cw_ref, cb_ref,
                wrf_ref, wif_ref, wrb_ref, wib_ref, bg_ref, lam_ref, h0_ref,
                hf_ref, hb_ref, fin_ref, cf_ref, cbk_ref):
    step = pl.program_id(0)

    @pl.when(step == 0)
    def _():
        cf_ref[...] = h0_ref[0]
        cbk_ref[...] = h0_ref[1]

    cw = cw_ref[...]
    cb = cb_ref[...]
    _lru_direction(False, reset_first, n_slabs, step, xf_ref, pf_ref, nf_ref, cw, cb, wrf_ref, wif_ref,
                   bg_ref[0:1], bg_ref[1:2], lam_ref[0:1], hf_ref, cf_ref)
    _lru_direction(True, reset_first, n_slabs, n_slabs - 1 - step, xb_ref, pb_ref, nb_ref, cw, cb,
                   wrb_ref, wib_ref, bg_ref[2:3], bg_ref[3:4], lam_ref[1:2], hb_ref, cbk_ref)
    fin_ref[0] = cf_ref[...]
    fin_ref[1] = cbk_ref[...]


def _lru_ctx_call(rx, n_lat_tok, conv_w, conv_b, w_dense, b_gate, lam):
    bsz, t_all, width = rx.shape
    r = GRID_W
    per = r // HALO
    n_slabs = (t_all - n_lat_tok) // r
    first = n_lat_tok // r
    main = lambda f: pl.BlockSpec((bsz, r, width), lambda s: (0, first + f(s), 0))
    prev = lambda f: pl.BlockSpec((bsz, HALO, width), lambda s: (0, (first + f(s)) * per - 1, 0))
    nxt = lambda f: pl.BlockSpec(
        (bsz, HALO, width), lambda s: (0, jnp.minimum((first + f(s) + 1) * per, t_all // HALO - 1), 0))
    out_main = lambda f: pl.BlockSpec((bsz, r, width), lambda s: (0, f(s), 0))
    out_shape = (bsz, n_slabs * r, width)
    fwd = lambda s: s
    bwd = lambda s: n_slabs - 1 - s
    full = lambda shape: pl.BlockSpec(shape, lambda s: (0,) * len(shape))
    carry = pltpu.VMEM((bsz, width), F32)
    return pl.pallas_call(
        functools.partial(_lru_kernel, True, n_slabs),
        grid=(n_slabs,),
        in_specs=[main(fwd), prev(fwd), nxt(fwd), main(bwd), prev(bwd), nxt(bwd),
                  full((CONV_W, width)), full((1, width))]
                 + [full((width, width))] * 4
                 + [full((4, width)), full((2, width)), full((2, bsz, width))],
        out_specs=[out_main(fwd), out_main(bwd), full((2, bsz, width))],
        out_shape=[jax.ShapeDtypeStruct(out_shape, F32), jax.ShapeDtypeStruct(out_shape, F32),
                   jax.ShapeDtypeStruct((2, bsz, width), F32)],
        scratch_shapes=[carry, carry],
        compiler_params=pltpu.CompilerParams(dimension_semantics=("arbitrary",),
                                             vmem_limit_bytes=VMEM_LIMIT),
        name="lru_ctx",
    )(rx, rx, rx, rx, rx, rx, conv_w, conv_b, *w_dense, b_gate, lam,
      jnp.zeros((2, bsz, width), F32))[2]


LRU_LANES = 128
LRU_ROWS_PER_PASS = 16


def _lru_grid_kernel(x_ref, h0_ref, cw_ref, cb_ref, w_ref, bg_ref, lam_ref, out_ref,
                     af_ref, bf_ref, ab_ref, bb_ref):
    n = GRID_W
    rp = LRU_ROWS_PER_PASS
    lanes = x_ref.shape[-1]
    crow = lax.broadcasted_iota(jnp.int32, (n, lanes), 0)
    cw = cw_ref[...]
    cb = cb_ref[...]

    def grid_rows(lo, hi):
        parts = []
        for r in range(lo, min(hi, 0)):
            parts.append(jnp.where(crow == 0, 0.0, pltpu.roll(x_ref[0, (n + r) * n:(n + r + 1) * n, :], 1, 0)))
        if max(lo, 0) < min(hi, n):
            parts.append(x_ref[0, max(lo, 0) * n:min(hi, n) * n, :])
        for r in range(max(lo, n), hi):
            parts.append(jnp.where(crow == n - 1, 0.0,
                                   pltpu.roll(x_ref[0, (r - n) * n:(r - n + 1) * n, :], n - 1, 0)))
        return parts[0] if len(parts) == 1 else jnp.concatenate(parts, axis=0)

    for r0 in range(0, n, rp):
        sl = slice(r0 * n, (r0 + rp) * n)
        xc = (grid_rows(r0 - 2, r0 + rp - 2) * cw[0:1] + grid_rows(r0 - 1, r0 + rp - 1) * cw[1:2]
              + grid_rows(r0, r0 + rp) * cw[2:3] + grid_rows(r0 + 1, r0 + rp + 1) * cw[3:4] + cb)
        xb = xc.astype(BF16)
        for d, (a_ref, b_ref) in enumerate(((af_ref, bf_ref), (ab_ref, bb_ref))):
            gate_r = _sigmoid(jnp.dot(xb, w_ref[2 * d, 0], preferred_element_type=F32) + bg_ref[2 * d:2 * d + 1])
            gate_i = _sigmoid(jnp.dot(xb, w_ref[2 * d + 1, 0], preferred_element_type=F32)
                              + bg_ref[2 * d + 1:2 * d + 2])
            log_a = (-LRU_C * _softplus(-lam_ref[d:d + 1])) * gate_r
            a_ref[sl] = jnp.exp(log_a)
            b_ref[sl] = jnp.sqrt(1.0 - jnp.exp(2.0 * log_a)) * (gate_i * xc)

    hf = jnp.zeros((n, lanes), F32)
    hb = jnp.zeros((n, lanes), F32)
    pf = jnp.ones((n, lanes), F32)
    pb = jnp.ones((n, lanes), F32)
    for i in range(n):
        sf = slice(i * n, (i + 1) * n)
        sb = slice((n - 1 - i) * n, (n - i) * n)
        a = af_ref[sf]
        hf = a * hf + bf_ref[sf]
        pf = a * pf
        bf_ref[sf] = hf
        af_ref[sf] = pf
        a = ab_ref[sb]
        hb = a * hb + bb_ref[sb]
        pb = a * pb
        bb_ref[sb] = hb
        ab_ref[sb] = pb

    h0f = h0_ref[0, 0]
    h0b = h0_ref[1, 0]
    acc_a, acc_h = _lin_scan(pf, hf, False)
    in_f = jnp.where(crow == 0, h0f, pltpu.roll(acc_h + acc_a * h0f, 1, 0))
    acc_a, acc_h = _lin_scan(pb, hb, True)
    in_b = jnp.where(crow == n - 1, h0b, pltpu.roll(acc_h + acc_a * h0b, n - 1, 0))

    for r in range(n):
        sl = slice(r * n, (r + 1) * n)
        out_ref[0, sl, :] = (bf_ref[sl] + af_ref[sl] * in_f) + (bb_ref[sl] + ab_ref[sl] * in_b)


def _lru_grid_call(rx, n_lat_tok, h0, conv_w, conv_b, w_tiles, b_gate, lam):
    bsz, _, width = rx.shape
    nt = width // LRU_LANES
    lane = lambda rows: pl.BlockSpec((rows, LRU_LANES), lambda b, j: (0, j))
    scratch = pltpu.VMEM((n_lat_tok, LRU_LANES), F32)
    return pl.pallas_call(
        _lru_grid_kernel,
        grid=(bsz, nt),
        in_specs=[pl.BlockSpec((1, n_lat_tok, LRU_LANES), lambda b, j: (b, 0, j)),
                  pl.BlockSpec((2, 1, 1, LRU_LANES), lambda b, j: (0, b, 0, j)),
                  lane(CONV_W), lane(1),
                  pl.BlockSpec((4, 1, LRU_LANES, LRU_LANES), lambda b, j: (0, j, 0, 0)),
                  lane(4), lane(2)],
        out_specs=pl.BlockSpec((1, n_lat_tok, LRU_LANES), lambda b, j: (b, 0, j)),
        out_shape=jax.ShapeDtypeStruct((bsz, n_lat_tok, width), F32),
        scratch_shapes=[scratch] * 4,
        compiler_params=pltpu.CompilerParams(dimension_semantics=("arbitrary", "arbitrary"),
                                             vmem_limit_bytes=VMEM_LIMIT),
        name="lru_grid",
    )(rx, h0.reshape(2, bsz, 1, width), conv_w, conv_b, w_tiles, b_gate, lam)


def _out_ffn2_kernel(h1_ref, of_ref, ob_ref, z_ref, hs_ref, rg_ref, mod_ref, ng_ref, gnw_ref,
                     wout_ref, w1_ref, w3_ref, w2_ref, fg_ref, out_ref):
    mod = mod_ref[0]
    o = of_ref[0] + ob_ref[0]
    z = z_ref[0]
    parts = []
    for h in range(GDN_HEADS):
        sl = slice(h * GDN_DK, (h + 1) * GDN_DK)
        oh = o[:, sl]
        parts.append(oh * lax.rsqrt(jnp.mean(oh * oh, axis=-1, keepdims=True) + EPS)
                     * gnw_ref[...] * _silu(z[:, sl]))
    parts.append(hs_ref[0] * _gelu_tanh(rg_ref[0]))
    y = jnp.dot(jnp.concatenate(parts, axis=1).astype(BF16), wout_ref[...], preferred_element_type=F32)
    h2 = h1_ref[0] + mod[5:6] * y
    u = _rms_mod(h2, ng_ref[2:3], mod[6:7], mod[7:8])
    h3 = h2 + FFN_RESIDUAL * mod[8:9] * _swiglu(u, w1_ref, w3_ref, w2_ref)
    out_ref[0] = h3 * lax.rsqrt(jnp.mean(h3 * h3, axis=-1, keepdims=True) + EPS) * fg_ref[...]


def _out_ffn2_call(h1, o_f, o_b, z, h_sum, rg, mods, norm_g, gdn_norm_w, w_out, w1, w3, w2, final_g):
    bsz, seq, _ = o_f.shape
    n_lat = seq // TOKEN_TILE
    tok = lambda w: pl.BlockSpec((1, TOKEN_TILE, w), lambda b, j: (b, j, 0))
    return pl.pallas_call(
        _out_ffn2_kernel,
        grid=(bsz, n_lat),
        in_specs=[tok(D_MODEL), tok(GDN_WIDTH), tok(GDN_WIDTH), tok(GDN_WIDTH),
                  tok(LRU_WIDTH), tok(LRU_WIDTH),
                  pl.BlockSpec((1, N_MOD, D_MODEL), lambda b, j: (b + 1, 0, 0)),
                  _resident((3, D_MODEL)), _resident((1, GDN_DK)),
                  _resident((D_MODEL, D_MODEL)),
                  _resident((D_MODEL, D_FF)), _resident((D_MODEL, D_FF)), _resident((D_FF, D_MODEL)),
                  _resident((1, D_MODEL))],
        out_specs=tok(D_MODEL),
        out_shape=jax.ShapeDtypeStruct((bsz, seq, D_MODEL), F32),
        compiler_params=pltpu.CompilerParams(dimension_semantics=("arbitrary", "arbitrary"),
                                             vmem_limit_bytes=VMEM_LIMIT),
        name="out_ffn2",
    )(h1, o_f, o_b, z, h_sum, rg, mods, norm_g, gdn_norm_w, w_out, w1, w3, w2, final_g)


def _block_diag(w):
    nb, bw, _ = w.shape
    eye = jnp.eye(nb, dtype=w.dtype)
    return jnp.einsum('ncd,nm->ncmd', w, eye).reshape(nb * bw, nb * bw)


def kernel(x, c, ctx, c_ctx, w_ada, b_ada, norm_g, ffn_w1, ffn_w3, ffn_w2, w_in, w_out, gdn_conv_w, gdn_a_log,
           gdn_dt_bias, gdn_norm_w, lru_conv_w, lru_conv_b, lru_w_gate, lru_b_gate, lru_lambda, final_norm_g):
    bsz, seq, _ = x.shape
    assert w_ada.shape[0] == 1 and seq == GRID_W * GRID_W and ctx.shape[1] == TOKEN_TILE

    rows = -(-(bsz + 1) // 8) * 8
    cvec = jnp.concatenate([c_ctx[None, :], c, jnp.zeros((rows - bsz - 1, D_MODEL), F32)], axis=0)
    mods = _ada_call(cvec, w_ada[0], b_ada).reshape(rows, N_MOD, D_MODEL)

    wi = w_in[0]
    o_z, o_ba = 3 * GDN_WIDTH, 4 * GDN_WIDTH
    o_rx = o_ba + 4 * GDN_HEADS
    w_in_r = jnp.concatenate([wi[:, :o_ba], wi[:, o_rx:], wi[:, o_ba:o_rx],
                              jnp.zeros((D_MODEL, BA_PAD - 4 * GDN_HEADS), F32)], axis=1).astype(BF16)
    w1 = ffn_w1[0].astype(BF16)
    w3 = ffn_w3[0].astype(BF16)
    w2 = ffn_w2[0].astype(BF16)

    h1, qkv, z, rx, rg, ba = _ffn1_in_call(x, ctx, mods, norm_g[0], w1[0], w3[0], w2[0], w_in_r)

    pad_g = lambda t: jnp.zeros((1, BA_PAD), F32).at[0, 2 * GDN_HEADS:4 * GDN_HEADS].set(t.reshape(-1))
    local = _gdn_local_call(qkv, ba, gdn_conv_w[0], pad_g(gdn_a_log[0]), pad_g(gdn_dt_bias[0]), seq)
    o_f, o_b = _gdn_scan_call(local, seq)

    wg = lru_w_gate[0]
    w_dense = [_block_diag(wg[d, g]).astype(BF16) for d in range(2) for g in range(2)]
    b_gate = lru_b_gate[0].reshape(4, LRU_WIDTH)
    lam = lru_lambda[0]
    cw, cb = lru_conv_w[0], lru_conv_b
    h_ctx = _lru_ctx_call(rx, seq, cw, cb, w_dense, b_gate, lam)
    nt = LRU_WIDTH // LRU_LANES
    w_tiles = jnp.stack([jnp.stack([w[t * LRU_LANES:(t + 1) * LRU_LANES, t * LRU_LANES:(t + 1) * LRU_LANES]
                                    for t in range(nt)]) for w in w_dense])
    h_sum = _lru_grid_call(rx, seq, h_ctx, cw, cb, w_tiles, b_gate, lam)

    return _out_ffn2_call(h1, o_f, o_b, z, h_sum, rg, mods, norm_g[0], gdn_norm_w,
                          w_out[0].astype(BF16), w1[1], w3[1], w2[1], final_norm_g[None, :])
```

```python
import functools

import jax
import jax.numpy as jnp
from jax import lax
from jax.experimental import pallas as pl
from jax.experimental.pallas import tpu as pltpu

D_MODEL = 1024
D_FF = 2816
N_MOD = 9
EPS = 1e-6
FFN_RESIDUAL = 0.5

GDN_WIDTH = 512
GDN_HEADS = 4
GDN_DK = 128
GDN_CHUNK = 64
CONV_W = 4
LRU_WIDTH = 512
LRU_BLOCKS = 8
LRU_BW = 64
LRU_C = 8.0
GRID_W = 64

TOKEN_TILE = 256
GDN_TILE_CHUNKS = 4
HALO = 8
BA_PAD = 128
P_COLS = 3 * GDN_WIDTH + GDN_WIDTH + 2 * LRU_WIDTH + BA_PAD
VMEM_LIMIT = 56 * 1024 * 1024

BF16 = jnp.bfloat16
F32 = jnp.float32
HIGHEST = lax.Precision.HIGHEST
NEG_BIG = -1e30


def _mm(a, b):
    return jnp.dot(a.astype(BF16), b.astype(BF16), preferred_element_type=F32)


def _mm_nt(a, b):
    return lax.dot_general(a.astype(BF16), b.astype(BF16), (((1,), (1,)), ((), ())),
                           preferred_element_type=F32)


def _sigmoid(x):
    return 1.0 / (1.0 + jnp.exp(-x))


def _silu(x):
    return x * _sigmoid(x)


def _softplus(x):
    return jnp.maximum(x, 0.0) + jnp.log(1.0 + jnp.exp(-jnp.abs(x)))


def _gelu_tanh(x):
    return 0.5 * x * (1.0 + jnp.tanh(0.7978845608028654 * (x + 0.044715 * x * x * x)))


def _rms_mod(h, g, shift, scale):
    y = h * lax.rsqrt(jnp.mean(h * h, axis=-1, keepdims=True) + EPS) * g
    return y * (1.0 + scale) + shift


def _swiglu(u, w1_ref, w3_ref, w2_ref):
    ub = u.astype(BF16)
    a = jnp.dot(ub, w1_ref[...], preferred_element_type=F32)
    b = jnp.dot(ub, w3_ref[...], preferred_element_type=F32)
    g = (_silu(a) * b).astype(BF16)
    return jnp.dot(g, w2_ref[...], preferred_element_type=F32)


def _dwconv(x, prev, nxt, w):
    t = x.shape[0]
    row = lax.broadcasted_iota(jnp.int32, x.shape, 0)
    xm1 = jnp.where(row == 0, prev[HALO - 1:HALO], pltpu.roll(x, 1, 0))
    xm2 = jnp.where(row == 0, prev[HALO - 2:HALO - 1],
                    jnp.where(row == 1, prev[HALO - 1:HALO], pltpu.roll(x, 2, 0)))
    xp1 = jnp.where(row == t - 1, nxt[0:1], pltpu.roll(x, t - 1, 0))
    return xm2 * w[0:1] + xm1 * w[1:2] + x * w[2:3] + xp1 * w[3:4]


def _eye(n):
    return (lax.broadcasted_iota(jnp.int32, (n, n), 0) == lax.broadcasted_iota(jnp.int32, (n, n), 1)).astype(F32)


def _ada_kernel(c_ref, w_ref, b_ref, o_ref):
    o_ref[...] = _mm(_silu(c_ref[...]), w_ref[...]) + b_ref[...]


def _ada_call(cvec, w_ada, b_ada):
    rows = cvec.shape[0]
    return pl.pallas_call(
        _ada_kernel,
        grid=(N_MOD,),
        in_specs=[pl.BlockSpec((rows, D_MODEL), lambda k: (0, 0)),
                  pl.BlockSpec((D_MODEL, D_MODEL), lambda k: (0, k)),
                  pl.BlockSpec((1, D_MODEL), lambda k: (0, k))],
        out_specs=pl.BlockSpec((rows, D_MODEL), lambda k: (0, k)),
        out_shape=jax.ShapeDtypeStruct((rows, N_MOD * D_MODEL), F32),
        name="ada",
    )(cvec, w_ada, b_ada)


def _ffn1_in_kernel(n_lat_tiles, x_ref, ctx_ref, mod_ref, ng_ref, w1_ref, w3_ref, w2_ref, win_ref,
                    h1_ref, qkv_ref, z_ref, rx_ref, rg_ref, ba_ref):
    j = pl.program_id(1)
    h = jnp.where(j == n_lat_tiles, ctx_ref[0], x_ref[0])
    mod = mod_ref[0]
    u = _rms_mod(h, ng_ref[0:1], mod[0:1], mod[1:2])
    h1 = h + FFN_RESIDUAL * mod[2:3] * _swiglu(u, w1_ref, w3_ref, w2_ref)
    h1_ref[0] = h1
    u2 = _rms_mod(h1, ng_ref[1:2], mod[3:4], mod[4:5])
    p = jnp.dot(u2.astype(BF16), win_ref[...], preferred_element_type=F32)
    o = 3 * GDN_WIDTH
    qkv_ref[0] = p[:, :o]
    z_ref[0] = p[:, o:o + GDN_WIDTH]
    o += GDN_WIDTH
    rx_ref[0] = p[:, o:o + LRU_WIDTH]
    o += LRU_WIDTH
    rg_ref[0] = p[:, o:o + LRU_WIDTH]
    o += LRU_WIDTH
    ba_ref[0] = p[:, o:o + BA_PAD]


def _resident(shape):
    return pl.BlockSpec(shape, lambda *_: (0,) * len(shape), pipeline_mode=pl.Buffered(1))


def _ffn1_in_call(x, ctx, mods, norm_g, w1, w3, w2, w_in_r):
    bsz, seq, _ = x.shape
    n_lat = seq // TOKEN_TILE
    n_all = n_lat + 1
    t_all = n_all * TOKEN_TILE
    tok = lambda w: pl.BlockSpec((1, TOKEN_TILE, w), lambda b, j: (b, j, 0))
    out_shape = [jax.ShapeDtypeStruct((bsz, t_all, w), F32)
                 for w in (D_MODEL, 3 * GDN_WIDTH, GDN_WIDTH, LRU_WIDTH, LRU_WIDTH, BA_PAD)]
    return pl.pallas_call(
        functools.partial(_ffn1_in_kernel, n_lat),
        grid=(bsz, n_all),
        in_specs=[pl.BlockSpec((1, TOKEN_TILE, D_MODEL), lambda b, j: (b, jnp.minimum(j, n_lat - 1), 0)),
                  pl.BlockSpec((1, TOKEN_TILE, D_MODEL), lambda b, j: (b, 0, 0)),
                  pl.BlockSpec((1, N_MOD, D_MODEL), lambda b, j: (jnp.where(j == n_lat, 0, b + 1), 0, 0)),
                  _resident((3, D_MODEL)),
                  _resident((D_MODEL, D_FF)), _resident((D_MODEL, D_FF)), _resident((D_FF, D_MODEL)),
                  _resident((D_MODEL, P_COLS))],
        out_specs=[tok(s.shape[-1]) for s in out_shape],
        out_shape=out_shape,
        compiler_params=pltpu.CompilerParams(dimension_semantics=("arbitrary", "arbitrary"),
                                             vmem_limit_bytes=VMEM_LIMIT),
        name="ffn1_in",
    )(x, ctx, mods, norm_g, w1, w3, w2, w_in_r)


def _split3(x):
    hi = x.astype(BF16)
    rest = x - hi.astype(F32)
    mid = rest.astype(BF16)
    return hi, mid, (rest - mid.astype(F32)).astype(BF16)


def _pair_block_diag(x, first):
    return jnp.concatenate([jnp.where(first, x, 0.0), jnp.where(first, 0.0, x)], axis=0)


def _unit_tri_inverses(a_pairs):
    c = a_pairs[0].shape[0]
    row = lax.broadcasted_iota(jnp.int32, (c, 2 * c), 0)
    lane = lax.broadcasted_iota(jnp.int32, (c, 2 * c), 1)
    col = lane & (c - 1)
    first = lane < c

    def coupling(k):
        return jnp.logical_and((row >> k) != (col >> k), (row >> (k + 1)) == (col >> (k + 1)))

    eye = (row == col).astype(F32)
    ts = [eye - jnp.where(coupling(0), a, 0.0) for a in a_pairs]
    k = 1
    while 2 ** k < c:
        mask = coupling(k)
        lt = [_mm(jnp.where(mask, a, 0.0), _pair_block_diag(t, first)) for a, t in zip(a_pairs, ts)]
        tlt = [_mm(t, _pair_block_diag(y, first)) for t, y in zip(ts, lt)]
        ts = [t - z for t, z in zip(ts, tlt)]
        k += 1
    return ts


def _gdn_local_kernel(n_lat_tiles, n_all_tiles, x_ref, prev_ref, next_ref, ba_ref, cw_ref, alog_ref, dtb_ref,
                      uf_ref, wqf_ref, akf_ref, egf_ref, ub_ref, wqb_ref, akb_ref, egb_ref):
    j = pl.program_id(1)
    c = GDN_CHUNK
    seg_start = jnp.logical_or(j == 0, j == n_lat_tiles)
    seg_end = jnp.logical_or(j == n_lat_tiles - 1, j == n_all_tiles - 1)
    prev = jnp.where(seg_start, 0.0, prev_ref[0])
    nxt = jnp.where(seg_end, 0.0, next_ref[0])
    y = _silu(_dwconv(x_ref[0], prev, nxt, cw_ref[...]))

    ba = ba_ref[0]
    lane = lax.broadcasted_iota(jnp.int32, (1, BA_PAD), 1)
    is_g = jnp.logical_and(lane >= 2 * GDN_HEADS, lane < 4 * GDN_HEADS)
    beta_all = _sigmoid(ba)
    g_all = jnp.where(is_g, -jnp.exp(alog_ref[...]) * _softplus(ba + dtb_ref[...]), 0.0)

    row = lax.broadcasted_iota(jnp.int32, (c, 2 * c), 0)
    lane2 = lax.broadcasted_iota(jnp.int32, (c, 2 * c), 1)
    col = lane2 & (c - 1)
    first = lane2 < c
    first_row = first[0:1]
    tri = (lax.broadcasted_iota(jnp.int32, (c, c), 0) >= lax.broadcasted_iota(jnp.int32, (c, c), 1))
    eye_bf = _eye(GDN_DK).astype(BF16)
    zeros_k = jnp.zeros((c, GDN_DK), F32)
    out_refs = ((uf_ref, wqf_ref, akf_ref, egf_ref), (ub_ref, wqb_ref, akb_ref, egb_ref))
    egf_ref[...] = jnp.zeros_like(egf_ref)
    egb_ref[...] = jnp.zeros_like(egb_ref)

    def transpose_exact(x):
        t = lax.dot_general(eye_bf, jnp.concatenate(_split3(x), axis=0), (((1,), (1,)), ((), ())),
                            preferred_element_type=F32)
        return t[:, :2 * c] + t[:, 2 * c:4 * c] + t[:, 4 * c:]

    units = []
    for ci in range(GDN_TILE_CHUNKS):
        rs = slice(ci * c, (ci + 1) * c)
        heads = []
        for h in range(GDN_HEADS):
            lo = h * GDN_DK
            q = y[rs, lo:lo + GDN_DK]
            k = y[rs, GDN_WIDTH + lo:GDN_WIDTH + lo + GDN_DK]
            v = y[rs, 2 * GDN_WIDTH + lo:2 * GDN_WIDTH + lo + GDN_DK]
            q = q * (lax.rsqrt(jnp.sum(q * q, axis=-1, keepdims=True) + EPS) * (GDN_DK ** -0.5))
            k = k * lax.rsqrt(jnp.sum(k * k, axis=-1, keepdims=True) + EPS)
            heads.append((q, k, v))
        k_ts = [lax.dot_general(eye_bf, jnp.concatenate([heads[2 * p][1], heads[2 * p + 1][1]], axis=0).astype(BF16),
                                (((1,), (1,)), ((), ())), preferred_element_type=F32)
                for p in range(GDN_HEADS // 2)]
        g = g_all[rs]
        parts = jnp.dot(tri.astype(BF16), jnp.concatenate(_split3(g), axis=1), preferred_element_type=F32)
        gcum_f = parts[:, :BA_PAD] + parts[:, BA_PAD:2 * BA_PAD] + parts[:, 2 * BA_PAD:]
        gcum_b = gcum_f[c - 1:c] - gcum_f + g
        gcum_t = transpose_exact(jnp.concatenate([gcum_f, gcum_b], axis=0))
        for d in range(2):
            gcum = gcum_b if d else gcum_f
            incl = (row <= col) if d else (row >= col)
            strict = (row < col) if d else (row > col)
            last = 0 if d else c - 1
            for p in range(GDN_HEADS // 2):
                (q0, k0, v0), (q1, k1, v1) = heads[2 * p], heads[2 * p + 1]
                cb = d * GDN_HEADS + 2 * p
                cg = 2 * GDN_HEADS + cb
                beta0, beta1 = beta_all[rs, cb:cb + 1], beta_all[rs, cb + 1:cb + 2]
                gc0, gc1 = gcum[:, cg:cg + 1], gcum[:, cg + 1:cg + 2]
                gt0, gt1 = gcum_t[cg:cg + 1], gcum_t[cg + 1:cg + 2]
                gr = (jnp.where(first_row, pltpu.roll(gt0, c, 1), gt1) if d
                      else jnp.where(first_row, gt0, pltpu.roll(gt1, c, 1)))
                gc = jnp.where(first, gc0, gc1)
                decay = jnp.exp(jnp.where(incl, gc - gr, NEG_BIG))
                e0, e1 = jnp.exp(gc0), jnp.exp(gc1)
                kb0, kb1 = k0 * beta0, k1 * beta1
                units.append(dict(
                    ci=ci, d=d, p=p, strict=strict, decay=decay,
                    lhs=jnp.concatenate([jnp.concatenate([kb0, kb1], axis=1),
                                         jnp.concatenate([q0, q1], axis=1)], axis=0),
                    k_bd=jnp.concatenate([jnp.concatenate([k0, zeros_k], axis=1),
                                          jnp.concatenate([zeros_k, k1], axis=1)], axis=0),
                    rhs=jnp.concatenate([jnp.concatenate([v0 * beta0, kb0 * e0], axis=1),
                                         jnp.concatenate([v1 * beta1, kb1 * e1], axis=1)], axis=0),
                    qe=(q0 * e0, q1 * e1),
                    kdt=k_ts[p] * jnp.exp(gc[last:last + 1] - gr),
                    eg=(jnp.broadcast_to(jnp.exp(gc0[last:last + 1]), (1, BA_PAD)),
                        jnp.broadcast_to(jnp.exp(gc1[last:last + 1]), (1, BA_PAD)))))

    prods = [_mm_nt(u["lhs"], u["k_bd"]) for u in units]
    a_pairs = [jnp.where(u["strict"], pr[:c] * u["decay"], 0.0) for u, pr in zip(units, prods)]
    t_invs = _unit_tri_inverses(a_pairs)
    sols = [_mm(_pair_block_diag(t, first), u["rhs"]) for t, u in zip(t_invs, units)]
    for u, pr, sol in zip(units, prods, sols):
        u_ref, wq_ref, ak_ref, eg_ref = out_refs[u["d"]]
        ci, p = u["ci"], u["p"]
        for i in range(2):
            h = 2 * p + i
            hs = slice(h * GDN_DK, (h + 1) * GDN_DK)
            u_ref[0, ci * c:(ci + 1) * c, hs] = sol[i * c:(i + 1) * c, :GDN_DK]
            wq_ref[0, ci, 0:c, hs] = sol[i * c:(i + 1) * c, GDN_DK:].astype(BF16)
            wq_ref[0, ci, c:2 * c, hs] = u["qe"][i].astype(BF16)
            eg_ref[0, ci, h:h + 1, :] = u["eg"][i]
        ps = slice(p * 2 * c, (p + 1) * 2 * c)
        ak_ref[0, ci, 0:c, ps] = (pr[c:] * u["decay"]).astype(BF16)
        ak_ref[0, ci, c:3 * c, ps] = u["kdt"].astype(BF16)


def _gdn_local_call(qkv, ba, conv_w, alog_vec, dtb_vec, n_lat_tok):
    bsz, t_all, width = qkv.shape
    c = GDN_CHUNK
    tile = GDN_TILE_CHUNKS * c
    n_all = t_all // tile
    n_lat = n_lat_tok // tile
    per_tile = tile // HALO
    last_halo = t_all // HALO - 1
    n_chunks = t_all // c
    cblk = lambda rows, w: pl.BlockSpec((1, GDN_TILE_CHUNKS, rows, w), lambda b, j: (b, j, 0, 0))
    vec = pl.BlockSpec((1, BA_PAD), lambda b, j: (0, 0))
    outs = [(jax.ShapeDtypeStruct((bsz, t_all, GDN_WIDTH), F32),
             pl.BlockSpec((1, tile, GDN_WIDTH), lambda b, j: (b, j, 0))),
            (jax.ShapeDtypeStruct((bsz, n_chunks, 2 * c, GDN_WIDTH), BF16), cblk(2 * c, GDN_WIDTH)),
            (jax.ShapeDtypeStruct((bsz, n_chunks, 3 * c, GDN_HEADS * c), BF16), cblk(3 * c, GDN_HEADS * c)),
            (jax.ShapeDtypeStruct((bsz, n_chunks, HALO, BA_PAD), F32), cblk(HALO, BA_PAD))] * 2
    return pl.pallas_call(
        functools.partial(_gdn_local_kernel, n_lat, n_all),
        grid=(bsz, n_all),
        in_specs=[pl.BlockSpec((1, tile, width), lambda b, j: (b, j, 0)),
                  pl.BlockSpec((1, HALO, width), lambda b, j: (b, jnp.maximum(j * per_tile - 1, 0), 0)),
                  pl.BlockSpec((1, HALO, width),
                               lambda b, j: (b, jnp.minimum((j + 1) * per_tile, last_halo), 0)),
                  pl.BlockSpec((1, tile, BA_PAD), lambda b, j: (b, j, 0)),
                  pl.BlockSpec((CONV_W, width), lambda b, j: (0, 0)), vec, vec],
        out_specs=[o[1] for o in outs],
        out_shape=[o[0] for o in outs],
        compiler_params=pltpu.CompilerParams(dimension_semantics=("arbitrary", "arbitrary"),
                                             vmem_limit_bytes=VMEM_LIMIT),
        name="gdn_local",
    )(qkv, qkv, qkv, ba, conv_w, alog_vec, dtb_vec)


def _gdn_scan_kernel(uf_ref, wqf_ref, akf_ref, egf_ref, ub_ref, wqb_ref, akb_ref, egb_ref,
                     of_ref, ob_ref, sf_ref, sb_ref):
    @pl.when(pl.program_id(0) == 0)
    def _():
        sf_ref[...] = jnp.zeros_like(sf_ref)
        sb_ref[...] = jnp.zeros_like(sb_ref)

    c = GDN_CHUNK
    bsz = uf_ref.shape[0]
    chains = [(refs, b, h)
              for refs in ((uf_ref, wqf_ref, akf_ref, egf_ref, of_ref, sf_ref),
                           (ub_ref, wqb_ref, akb_ref, egb_ref, ob_ref, sb_ref))
              for b in range(bsz) for h in range(GDN_HEADS)]
    hs = lambda h: slice(h * GDN_DK, (h + 1) * GDN_DK)
    states = [refs[5][b * GDN_HEADS + h] for refs, b, h in chains]
    with_s = [jnp.dot(refs[1][b, 0, :, hs(h)], s.astype(BF16), preferred_element_type=F32)
              for (refs, b, h), s in zip(chains, states)]
    v_new = [refs[0][b, :, hs(h)] - r[:c] for (refs, b, h), r in zip(chains, with_s)]
    with_v = [jnp.dot(refs[2][b, 0, :, h * c:(h + 1) * c], v.astype(BF16), preferred_element_type=F32)
              for (refs, b, h), v in zip(chains, v_new)]
    for (refs, b, h), s, r, a in zip(chains, states, with_s, with_v):
        refs[4][b, :, hs(h)] = r[c:] + a[:c]
        refs[5][b * GDN_HEADS + h] = s * refs[3][b, 0, h:h + 1, :] + a[c:]


def _gdn_scan_call(local, n_lat_tok):
    u = local[0]
    bsz, t_all, _ = u.shape
    c = GDN_CHUNK
    n_lat = n_lat_tok // c
    n_all = t_all // c
    n_ctx = n_all - n_lat
    fwd = lambda s: jnp.where(s < n_ctx, n_lat + s, s - n_ctx)
    bwd = lambda s: n_all - 1 - s
    fwd_out = lambda s: jnp.maximum(s - n_ctx, 0)
    bwd_out = lambda s: jnp.minimum(n_all - 1 - s, n_lat - 1)
    tok = lambda f: pl.BlockSpec((bsz, c, GDN_WIDTH), lambda s: (0, f(s), 0))
    cblk = lambda rows, w, f: pl.BlockSpec((bsz, 1, rows, w), lambda s: (0, f(s), 0, 0))
    ins = lambda f: [tok(f), cblk(2 * c, GDN_WIDTH, f), cblk(3 * c, GDN_HEADS * c, f), cblk(HALO, BA_PAD, f)]
    out = jax.ShapeDtypeStruct((bsz, n_lat_tok, GDN_WIDTH), F32)
    state = pltpu.VMEM((bsz * GDN_HEADS, GDN_DK, GDN_DK), F32)
    return pl.pallas_call(
        _gdn_scan_kernel,
        grid=(n_all,),
        in_specs=ins(fwd) + ins(bwd),
        out_specs=[tok(fwd_out), tok(bwd_out)],
        out_shape=[out, out],
        scratch_shapes=[state, state],
        compiler_params=pltpu.CompilerParams(dimension_semantics=("arbitrary",),
                                             vmem_limit_bytes=VMEM_LIMIT),
        name="gdn_scan",
    )(*local)


def _lin_scan(a, b, reverse):
    r = a.shape[0]
    row = lax.broadcasted_iota(jnp.int32, a.shape, 0)
    s = 1
    while s < r:
        ok = (row < r - s) if reverse else (row >= s)
        shift = r - s if reverse else s
        a_sh = jnp.where(ok, pltpu.roll(a, shift, 0), 1.0)
        b_sh = jnp.where(ok, pltpu.roll(b, shift, 0), 0.0)
        b = a * b_sh + b
        a = a * a_sh
        s *= 2
    return a, b


def _lru_direction(reverse, reset_first, n_slabs, slab, x_ref, prev_ref, next_ref, cw, cb, wr_ref, wi_ref,
                   br, bi, lam, h_ref, carry_ref):
    bsz, r, width = x_ref.shape
    step = pl.program_id(0)
    has_prev = slab > 0
    has_next = slab < n_slabs - 1
    xs = []
    for b in range(bsz):
        prev = jnp.where(has_prev, prev_ref[b], 0.0)
        nxt = jnp.where(has_next, next_ref[b], 0.0)
        xs.append(_dwconv(x_ref[b], prev, nxt, cw) + cb)
    x = jnp.concatenate(xs, axis=0)
    xb = x.astype(BF16)
    gate_r = _sigmoid(jnp.dot(xb, wr_ref[...], preferred_element_type=F32) + br)
    gate_i = _sigmoid(jnp.dot(xb, wi_ref[...], preferred_element_type=F32) + bi)
    log_a = (-LRU_C * _softplus(-lam)) * gate_r
    a = jnp.exp(log_a)
    mult = jnp.sqrt(1.0 - jnp.exp(2.0 * log_a))
    if reset_first:
        row = lax.broadcasted_iota(jnp.int32, (r, width), 0)
        first = (r - 1) if reverse else 0
        is_first = jnp.logical_and(step == 0, row == first)
    edge = 0 if reverse else r - 1
    for b in range(bsz):
        sl = slice(b * r, (b + 1) * r)
        m = mult[sl]
        if reset_first:
            m = jnp.where(is_first, 1.0, m)
        a_cum, h = _lin_scan(a[sl], m * (gate_i[sl] * x[sl]), reverse)
        h = h + a_cum * carry_ref[b:b + 1]
        h_ref[b] = h
        carry_ref[b:b + 1] = h[edge:edge + 1]


def _lru_kernel(reset_first, n_slabs,
                xf_ref, pf_ref, nf_ref, xb_ref, pb_ref, nb_ref, cw_ref, cb_ref,
                wrf_ref, wif_ref, wrb_ref, wib_ref, bg_ref, lam_ref, h0_ref,
                hf_ref, hb_ref, fin_ref, cf_ref, cbk_ref):
    step = pl.program_id(0)

    @pl.when(step == 0)
    def _():
        cf_ref[...] = h0_ref[0]
        cbk_ref[...] = h0_ref[1]

    cw = cw_ref[...]
    cb = cb_ref[...]
    _lru_direction(False, reset_first, n_slabs, step, xf_ref, pf_ref, nf_ref, cw, cb, wrf_ref, wif_ref,
                   bg_ref[0:1], bg_ref[1:2], lam_ref[0:1], hf_ref, cf_ref)
    _lru_direction(True, reset_first, n_slabs, n_slabs - 1 - step, xb_ref, pb_ref, nb_ref, cw, cb,
                   wrb_ref, wib_ref, bg_ref[2:3], bg_ref[3:4], lam_ref[1:2], hb_ref, cbk_ref)
    fin_ref[0] = cf_ref[...]
    fin_ref[1] = cbk_ref[...]


def _lru_ctx_call(rx, n_lat_tok, conv_w, conv_b, w_dense, b_gate, lam):
    bsz, t_all, width = rx.shape
    r = GRID_W
    per = r // HALO
    n_slabs = (t_all - n_lat_tok) // r
    first = n_lat_tok // r
    main = lambda f: pl.BlockSpec((bsz, r, width), lambda s: (0, first + f(s), 0))
    prev = lambda f: pl.BlockSpec((bsz, HALO, width), lambda s: (0, (first + f(s)) * per - 1, 0))
    nxt = lambda f: pl.BlockSpec(
        (bsz, HALO, width), lambda s: (0, jnp.minimum((first + f(s) + 1) * per, t_all // HALO - 1), 0))
    out_main = lambda f: pl.BlockSpec((bsz, r, width), lambda s: (0, f(s), 0))
    out_shape = (bsz, n_slabs * r, width)
    fwd = lambda s: s
    bwd = lambda s: n_slabs - 1 - s
    full = lambda shape: pl.BlockSpec(shape, lambda s: (0,) * len(shape))
    carry = pltpu.VMEM((bsz, width), F32)
    return pl.pallas_call(
        functools.partial(_lru_kernel, True, n_slabs),
        grid=(n_slabs,),
        in_specs=[main(fwd), prev(fwd), nxt(fwd), main(bwd), prev(bwd), nxt(bwd),
                  full((CONV_W, width)), full((1, width))]
                 + [full((width, width))] * 4
                 + [full((4, width)), full((2, width)), full((2, bsz, width))],
        out_specs=[out_main(fwd), out_main(bwd), full((2, bsz, width))],
        out_shape=[jax.ShapeDtypeStruct(out_shape, F32), jax.ShapeDtypeStruct(out_shape, F32),
                   jax.ShapeDtypeStruct((2, bsz, width), F32)],
        scratch_shapes=[carry, carry],
        compiler_params=pltpu.CompilerParams(dimension_semantics=("arbitrary",),
                                             vmem_limit_bytes=VMEM_LIMIT),
        name="lru_ctx",
    )(rx, rx, rx, rx, rx, rx, conv_w, conv_b, *w_dense, b_gate, lam,
      jnp.zeros((2, bsz, width), F32))[2]


LRU_LANES = 128
LRU_ROWS_PER_PASS = 16


def _lru_grid_kernel(x_ref, h0_ref, cw_ref, cb_ref, w_ref, bg_ref, lam_ref, out_ref,
                     af_ref, bf_ref, ab_ref, bb_ref):
    n = GRID_W
    rp = LRU_ROWS_PER_PASS
    lanes = x_ref.shape[-1]
    crow = lax.broadcasted_iota(jnp.int32, (n, lanes), 0)
    cw = cw_ref[...]
    cb = cb_ref[...]

    def grid_rows(lo, hi):
        parts = []
        for r in range(lo, min(hi, 0)):
            parts.append(jnp.where(crow == 0, 0.0, pltpu.roll(x_ref[0, (n + r) * n:(n + r + 1) * n, :], 1, 0)))
        if max(lo, 0) < min(hi, n):
            parts.append(x_ref[0, max(lo, 0) * n:min(hi, n) * n, :])
        for r in range(max(lo, n), hi):
            parts.append(jnp.where(crow == n - 1, 0.0,
                                   pltpu.roll(x_ref[0, (r - n) * n:(r - n + 1) * n, :], n - 1, 0)))
        return parts[0] if len(parts) == 1 else jnp.concatenate(parts, axis=0)

    for r0 in range(0, n, rp):
        sl = slice(r0 * n, (r0 + rp) * n)
        xc = (grid_rows(r0 - 2, r0 + rp - 2) * cw[0:1] + grid_rows(r0 - 1, r0 + rp - 1) * cw[1:2]
              + grid_rows(r0, r0 + rp) * cw[2:3] + grid_rows(r0 + 1, r0 + rp + 1) * cw[3:4] + cb)
        xb = xc.astype(BF16)
        for d, (a_ref, b_ref) in enumerate(((af_ref, bf_ref), (ab_ref, bb_ref))):
            gate_r = _sigmoid(jnp.dot(xb, w_ref[2 * d, 0], preferred_element_type=F32) + bg_ref[2 * d:2 * d + 1])
            gate_i = _sigmoid(jnp.dot(xb, w_ref[2 * d + 1, 0], preferred_element_type=F32)
                              + bg_ref[2 * d + 1:2 * d + 2])
            log_a = (-LRU_C * _softplus(-lam_ref[d:d + 1])) * gate_r
            a_ref[sl] = jnp.exp(log_a)
            b_ref[sl] = jnp.sqrt(1.0 - jnp.exp(2.0 * log_a)) * (gate_i * xc)

    hf = jnp.zeros((n, lanes), F32)
    hb = jnp.zeros((n, lanes), F32)
    pf = jnp.ones((n, lanes), F32)
    pb = jnp.ones((n, lanes), F32)
    for i in range(n):
        sf = slice(i * n, (i + 1) * n)
        sb = slice((n - 1 - i) * n, (n - i) * n)
        a = af_ref[sf]
        hf = a * hf + bf_ref[sf]
        pf = a * pf
        bf_ref[sf] = hf
        af_ref[sf] = pf
        a = ab_ref[sb]
        hb = a * hb + bb_ref[sb]
        pb = a * pb
        bb_ref[sb] = hb
        ab_ref[sb] = pb

    h0f = h0_ref[0, 0]
    h0b = h0_ref[1, 0]
    acc_a, acc_h = _lin_scan(pf, hf, False)
    in_f = jnp.where(crow == 0, h0f, pltpu.roll(acc_h + acc_a * h0f, 1, 0))
    acc_a, acc_h = _lin_scan(pb, hb, True)
    in_b = jnp.where(crow == n - 1, h0b, pltpu.roll(acc_h + acc_a * h0b, n - 1, 0))

    for r in range(n):
        sl = slice(r * n, (r + 1) * n)
        out_ref[0, sl, :] = (bf_ref[sl] + af_ref[sl] * in_f) + (bb_ref[sl] + ab_ref[sl] * in_b)


def _lru_grid_call(rx, n_lat_tok, h0, conv_w, conv_b, w_tiles, b_gate, lam):
    bsz, _, width = rx.shape
    nt = width // LRU_LANES
    lane = lambda rows: pl.BlockSpec((rows, LRU_LANES), lambda b, j: (0, j))
    scratch = pltpu.VMEM((n_lat_tok, LRU_LANES), F32)
    return pl.pallas_call(
        _lru_grid_kernel,
        grid=(bsz, nt),
        in_specs=[pl.BlockSpec((1, n_lat_tok, LRU_LANES), lambda b, j: (b, 0, j)),
                  pl.BlockSpec((2, 1, 1, LRU_LANES), lambda b, j: (0, b, 0, j)),
                  lane(CONV_W), lane(1),
                  pl.BlockSpec((4, 1, LRU_LANES, LRU_LANES), lambda b, j: (0, j, 0, 0)),
                  lane(4), lane(2)],
        out_specs=pl.BlockSpec((1, n_lat_tok, LRU_LANES), lambda b, j: (b, 0, j)),
        out_shape=jax.ShapeDtypeStruct((bsz, n_lat_tok, width), F32),
        scratch_shapes=[scratch] * 4,
        compiler_params=pltpu.CompilerParams(dimension_semantics=("arbitrary", "arbitrary"),
                                             vmem_limit_bytes=VMEM_LIMIT),
        name="lru_grid",
    )(rx, h0.reshape(2, bsz, 1, width), conv_w, conv_b, w_tiles, b_gate, lam)


def _out_ffn2_kernel(h1_ref, of_ref, ob_ref, z_ref, hs_ref, rg_ref, mod_ref, ng_ref, gnw_ref,
                     wout_ref, w1_ref, w3_ref, w2_ref, fg_ref, out_ref):
    mod = mod_ref[0]
    o = of_ref[0] + ob_ref[0]
    z = z_ref[0]
    parts = []
    for h in range(GDN_HEADS):
        sl = slice(h * GDN_DK, (h + 1) * GDN_DK)
        oh = o[:, sl]
        parts.append(oh * lax.rsqrt(jnp.mean(oh * oh, axis=-1, keepdims=True) + EPS)
                     * gnw_ref[...] * _silu(z[:, sl]))
    parts.append(hs_ref[0] * _gelu_tanh(rg_ref[0]))
    y = jnp.dot(jnp.concatenate(parts, axis=1).astype(BF16), wout_ref[...], preferred_element_type=F32)
    h2 = h1_ref[0] + mod[5:6] * y
    u = _rms_mod(h2, ng_ref[2:3], mod[6:7], mod[7:8])
    h3 = h2 + FFN_RESIDUAL * mod[8:9] * _swiglu(u, w1_ref, w3_ref, w2_ref)
    out_ref[0] = h3 * lax.rsqrt(jnp.mean(h3 * h3, axis=-1, keepdims=True) + EPS) * fg_ref[...]


def _out_ffn2_call(h1, o_f, o_b, z, h_sum, rg, mods, norm_g, gdn_norm_w, w_out, w1, w3, w2, final_g):
    bsz, seq, _ = o_f.shape
    n_lat = seq // TOKEN_TILE
    tok = lambda w: pl.BlockSpec((1, TOKEN_TILE, w), lambda b, j: (b, j, 0))
    return pl.pallas_call(
        _out_ffn2_kernel,
        grid=(bsz, n_lat),
        in_specs=[tok(D_MODEL), tok(GDN_WIDTH), tok(GDN_WIDTH), tok(GDN_WIDTH),
                  tok(LRU_WIDTH), tok(LRU_WIDTH),
                  pl.BlockSpec((1, N_MOD, D_MODEL), lambda b, j: (b + 1, 0, 0)),
                  _resident((3, D_MODEL)), _resident((1, GDN_DK)),
                  _resident((D_MODEL, D_MODEL)),
                  _resident((D_MODEL, D_FF)), _resident((D_MODEL, D_FF)), _resident((D_FF, D_MODEL)),
                  _resident((1, D_MODEL))],
        out_specs=tok(D_MODEL),
        out_shape=jax.ShapeDtypeStruct((bsz, seq, D_MODEL), F32),
        compiler_params=pltpu.CompilerParams(dimension_semantics=("arbitrary", "arbitrary"),
                                             vmem_limit_bytes=VMEM_LIMIT),
        name="out_ffn2",
    )(h1, o_f, o_b, z, h_sum, rg, mods, norm_g, gdn_norm_w, w_out, w1, w3, w2, final_g)


def _block_diag(w):
    nb, bw, _ = w.shape
    eye = jnp.eye(nb, dtype=w.dtype)
    return jnp.einsum('ncd,nm->ncmd', w, eye).reshape(nb * bw, nb * bw)


def kernel(x, c, ctx, c_ctx, w_ada, b_ada, norm_g, ffn_w1, ffn_w3, ffn_w2, w_in, w_out, gdn_conv_w, gdn_a_log,
           gdn_dt_bias, gdn_norm_w, lru_conv_w, lru_conv_b, lru_w_gate, lru_b_gate, lru_lambda, final_norm_g):
    bsz, seq, _ = x.shape
    assert w_ada.shape[0] == 1 and seq == GRID_W * GRID_W and ctx.shape[1] == TOKEN_TILE

    rows = -(-(bsz + 1) // 8) * 8
    cvec = jnp.concatenate([c_ctx[None, :], c, jnp.zeros((rows - bsz - 1, D_MODEL), F32)], axis=0)
    mods = _ada_call(cvec, w_ada[0], b_ada).reshape(rows, N_MOD, D_MODEL)

    wi = w_in[0]
    o_z, o_ba = 3 * GDN_WIDTH, 4 * GDN_WIDTH
    o_rx = o_ba + 4 * GDN_HEADS
    w_in_r = jnp.concatenate([wi[:, :o_ba], wi[:, o_rx:], wi[:, o_ba:o_rx],
                              jnp.zeros((D_MODEL, BA_PAD - 4 * GDN_HEADS), F32)], axis=1).astype(BF16)
    w1 = ffn_w1[0].astype(BF16)
    w3 = ffn_w3[0].astype(BF16)
    w2 = ffn_w2[0].astype(BF16)

    h1, qkv, z, rx, rg, ba = _ffn1_in_call(x, ctx, mods, norm_g[0], w1[0], w3[0], w2[0], w_in_r)

    pad_g = lambda t: jnp.zeros((1, BA_PAD), F32).at[0, 2 * GDN_HEADS:4 * GDN_HEADS].set(t.reshape(-1))
    local = _gdn_local_call(qkv, ba, gdn_conv_w[0], pad_g(gdn_a_log[0]), pad_g(gdn_dt_bias[0]), seq)
    o_f, o_b = _gdn_scan_call(local, seq)

    wg = lru_w_gate[0]
    w_dense = [_block_diag(wg[d, g]).astype(BF16) for d in range(2) for g in range(2)]
    b_gate = lru_b_gate[0].reshape(4, LRU_WIDTH)
    lam = lru_lambda[0]
    cw, cb = lru_conv_w[0], lru_conv_b
    h_ctx = _lru_ctx_call(rx, seq, cw, cb, w_dense, b_gate, lam)
    nt = LRU_WIDTH // LRU_LANES
    w_tiles = jnp.stack([jnp.stack([w[t * LRU_LANES:(t + 1) * LRU_LANES, t * LRU_LANES:(t + 1) * LRU_LANES]
                                    for t in range(nt)]) for w in w_dense])
    h_sum = _lru_grid_call(rx, seq, h_ctx, cw, cb, w_tiles, b_gate, lam)

    return _out_ffn2_call(h1, o_f, o_b, z, h_sum, rg, mods, norm_g[0], gdn_norm_w,
                          w_out[0].astype(BF16), w1[1], w3[1], w2[1], final_norm_g[None, :])
```

```python
import functools

import jax
import jax.numpy as jnp
from jax import lax
from jax.experimental import pallas as pl
from jax.experimental.pallas import tpu as pltpu

D_MODEL = 1024
D_FF = 2816
N_MOD = 9
EPS = 1e-6
FFN_RESIDUAL = 0.5

GDN_WIDTH = 512
GDN_HEADS = 4
GDN_DK = 128
GDN_CHUNK = 64
CONV_W = 4
LRU_WIDTH = 512
LRU_BLOCKS = 8
LRU_BW = 64
LRU_C = 8.0
GRID_W = 64

TOKEN_TILE = 256
GDN_TILE_CHUNKS = 4
GDN_STAGE1_SLOT = 6
GDN_CHUNK_SKEW = 0
HALO = 8
BA_PAD = 128
P_COLS = 3 * GDN_WIDTH + GDN_WIDTH + 2 * LRU_WIDTH + BA_PAD
VMEM_LIMIT = 56 * 1024 * 1024

BF16 = jnp.bfloat16
F32 = jnp.float32
HIGHEST = lax.Precision.HIGHEST
NEG_BIG = -1e30


def _mm(a, b):
    return jnp.dot(a.astype(BF16), b.astype(BF16), preferred_element_type=F32)


def _mm_nt(a, b):
    return lax.dot_general(a.astype(BF16), b.astype(BF16), (((1,), (1,)), ((), ())),
                           preferred_element_type=F32)


def _sigmoid(x):
    return 1.0 / (1.0 + jnp.exp(-x))


def _silu(x):
    return x * _sigmoid(x)


def _softplus(x):
    return jnp.maximum(x, 0.0) + jnp.log(1.0 + jnp.exp(-jnp.abs(x)))


def _gelu_tanh(x):
    return 0.5 * x * (1.0 + jnp.tanh(0.7978845608028654 * (x + 0.044715 * x * x * x)))


def _rms_mod(h, g, shift, scale):
    y = h * lax.rsqrt(jnp.mean(h * h, axis=-1, keepdims=True) + EPS) * g
    return y * (1.0 + scale) + shift


def _swiglu(u, w1_ref, w3_ref, w2_ref):
    ub = u.astype(BF16)
    a = jnp.dot(ub, w1_ref[...], preferred_element_type=F32)
    b = jnp.dot(ub, w3_ref[...], preferred_element_type=F32)
    g = (_silu(a) * b).astype(BF16)
    return jnp.dot(g, w2_ref[...], preferred_element_type=F32)


def _dwconv(x, prev, nxt, w):
    t = x.shape[0]
    row = lax.broadcasted_iota(jnp.int32, x.shape, 0)
    xm1 = jnp.where(row == 0, prev[HALO - 1:HALO], pltpu.roll(x, 1, 0))
    xm2 = jnp.where(row == 0, prev[HALO - 2:HALO - 1],
                    jnp.where(row == 1, prev[HALO - 1:HALO], pltpu.roll(x, 2, 0)))
    xp1 = jnp.where(row == t - 1, nxt[0:1], pltpu.roll(x, t - 1, 0))
    return xm2 * w[0:1] + xm1 * w[1:2] + x * w[2:3] + xp1 * w[3:4]


def _eye(n):
    return (lax.broadcasted_iota(jnp.int32, (n, n), 0) == lax.broadcasted_iota(jnp.int32, (n, n), 1)).astype(F32)


def _ada_kernel(c_ref, w_ref, b_ref, o_ref):
    o_ref[...] = _mm(_silu(c_ref[...]), w_ref[...]) + b_ref[...]


def _ada_call(cvec, w_ada, b_ada):
    rows = cvec.shape[0]
    return pl.pallas_call(
        _ada_kernel,
        grid=(N_MOD,),
        in_specs=[pl.BlockSpec((rows, D_MODEL), lambda k: (0, 0)),
                  pl.BlockSpec((D_MODEL, D_MODEL), lambda k: (0, k)),
                  pl.BlockSpec((1, D_MODEL), lambda k: (0, k))],
        out_specs=pl.BlockSpec((rows, D_MODEL), lambda k: (0, k)),
        out_shape=jax.ShapeDtypeStruct((rows, N_MOD * D_MODEL), F32),
        name="ada",
    )(cvec, w_ada, b_ada)


def _ffn1_in_kernel(n_lat_tiles, x_ref, ctx_ref, mod_ref, ng_ref, w1_ref, w3_ref, w2_ref, win_ref,
                    h1_ref, qkv_ref, z_ref, rx_ref, rg_ref, ba_ref):
    j = pl.program_id(1)
    h = jnp.where(j == n_lat_tiles, ctx_ref[0], x_ref[0])
    mod = mod_ref[0]
    u = _rms_mod(h, ng_ref[0:1], mod[0:1], mod[1:2])
    h1 = h + FFN_RESIDUAL * mod[2:3] * _swiglu(u, w1_ref, w3_ref, w2_ref)
    h1_ref[0] = h1
    u2 = _rms_mod(h1, ng_ref[1:2], mod[3:4], mod[4:5])
    p = jnp.dot(u2.astype(BF16), win_ref[...], preferred_element_type=F32)
    o = 3 * GDN_WIDTH
    qkv_ref[0] = p[:, :o]
    z_ref[0] = p[:, o:o + GDN_WIDTH]
    o += GDN_WIDTH
    rx_ref[0] = p[:, o:o + LRU_WIDTH]
    o += LRU_WIDTH
    rg_ref[0] = p[:, o:o + LRU_WIDTH]
    o += LRU_WIDTH
    ba_ref[0] = p[:, o:o + BA_PAD]


def _resident(shape):
    return pl.BlockSpec(shape, lambda *_: (0,) * len(shape), pipeline_mode=pl.Buffered(1))


def _ffn1_in_call(x, ctx, mods, norm_g, w1, w3, w2, w_in_r):
    bsz, seq, _ = x.shape
    n_lat = seq // TOKEN_TILE
    n_all = n_lat + 1
    t_all = n_all * TOKEN_TILE
    tok = lambda w: pl.BlockSpec((1, TOKEN_TILE, w), lambda b, j: (b, j, 0))
    out_shape = [jax.ShapeDtypeStruct((bsz, t_all, w), F32)
                 for w in (D_MODEL, 3 * GDN_WIDTH, GDN_WIDTH, LRU_WIDTH, LRU_WIDTH, BA_PAD)]
    return pl.pallas_call(
        functools.partial(_ffn1_in_kernel, n_lat),
        grid=(bsz, n_all),
        in_specs=[pl.BlockSpec((1, TOKEN_TILE, D_MODEL), lambda b, j: (b, jnp.minimum(j, n_lat - 1), 0)),
                  pl.BlockSpec((1, TOKEN_TILE, D_MODEL), lambda b, j: (b, 0, 0)),
                  pl.BlockSpec((1, N_MOD, D_MODEL), lambda b, j: (jnp.where(j == n_lat, 0, b + 1), 0, 0)),
                  _resident((3, D_MODEL)),
                  _resident((D_MODEL, D_FF)), _resident((D_MODEL, D_FF)), _resident((D_FF, D_MODEL)),
                  _resident((D_MODEL, P_COLS))],
        out_specs=[tok(s.shape[-1]) for s in out_shape],
        out_shape=out_shape,
        compiler_params=pltpu.CompilerParams(dimension_semantics=("arbitrary", "arbitrary"),
                                             vmem_limit_bytes=VMEM_LIMIT),
        name="ffn1_in",
    )(x, ctx, mods, norm_g, w1, w3, w2, w_in_r)


def _split3(x):
    hi = x.astype(BF16)
    rest = x - hi.astype(F32)
    mid = rest.astype(BF16)
    return hi, mid, (rest - mid.astype(F32)).astype(BF16)


def _pair_block_diag(x, first):
    return jnp.concatenate([jnp.where(first, x, 0.0), jnp.where(first, 0.0, x)], axis=0)


def _run_interleaved(starters, skew):
    waiting = [(skew(i) if callable(skew) else skew * i, gen) for i, gen in enumerate(starters)]
    active = []
    slot = 0
    while waiting or active:
        while waiting and slot >= waiting[0][0]:
            gen = waiting.pop(0)[1]
            active.append([gen, next(gen)])
        still = []
        for item in active:
            results = [dot() for dot in item[1]]
            try:
                item[1] = item[0].send(results)
                still.append(item)
            except StopIteration as stop:
                still.extend([child, next(child)] for child in (stop.value or []))
        active = still
        slot += 1


def _gdn_local_kernel(n_lat_tiles, n_all_tiles, x_ref, prev_ref, next_ref, ba_ref, cw_ref, alog_ref, dtb_ref,
                      uf_ref, wqf_ref, akf_ref, egf_ref, ub_ref, wqb_ref, akb_ref, egb_ref):
    j = pl.program_id(1)
    c = GDN_CHUNK
    seg_start = jnp.logical_or(j == 0, j == n_lat_tiles)
    seg_end = jnp.logical_or(j == n_lat_tiles - 1, j == n_all_tiles - 1)
    prev = jnp.where(seg_start, 0.0, prev_ref[0])
    nxt = jnp.where(seg_end, 0.0, next_ref[0])
    y = _silu(_dwconv(x_ref[0], prev, nxt, cw_ref[...]))

    ba = ba_ref[0]
    lane = lax.broadcasted_iota(jnp.int32, (1, BA_PAD), 1)
    is_g = jnp.logical_and(lane >= 2 * GDN_HEADS, lane < 4 * GDN_HEADS)
    beta_all = _sigmoid(ba)
    g_all = jnp.where(is_g, -jnp.exp(alog_ref[...]) * _softplus(ba + dtb_ref[...]), 0.0)

    row = lax.broadcasted_iota(jnp.int32, (c, 2 * c), 0)
    lane2 = lax.broadcasted_iota(jnp.int32, (c, 2 * c), 1)
    col = lane2 & (c - 1)
    first = lane2 < c
    first_row = first[0:1]
    tri_bf = (lax.broadcasted_iota(jnp.int32, (c, c), 0)
              >= lax.broadcasted_iota(jnp.int32, (c, c), 1)).astype(BF16)
    eye2 = (row == col).astype(F32)
    eye_bf = _eye(GDN_DK).astype(BF16)
    zeros_k = jnp.zeros((c, GDN_DK), F32)
    out_refs = ((uf_ref, wqf_ref, akf_ref, egf_ref), (ub_ref, wqb_ref, akb_ref, egb_ref))
    egf_ref[...] = jnp.zeros_like(egf_ref)
    egb_ref[...] = jnp.zeros_like(egb_ref)

    def coupling(k):
        return jnp.logical_and((row >> k) != (col >> k), (row >> (k + 1)) == (col >> (k + 1)))

    nt_dims = (((1,), (1,)), ((), ()))

    def pair_unit(ci, d, p, heads, k_t, beta_all_c, gcum, gcum_t):
        u_ref, wq_ref, ak_ref, eg_ref = out_refs[d]
        (q0, k0, v0), (q1, k1, v1) = heads[2 * p], heads[2 * p + 1]
        incl = (row <= col) if d else (row >= col)
        strict = (row < col) if d else (row > col)
        last = 0 if d else c - 1
        cb = d * GDN_HEADS + 2 * p
        cg = 2 * GDN_HEADS + cb
        beta0, beta1 = beta_all_c[:, cb:cb + 1], beta_all_c[:, cb + 1:cb + 2]
        kb0, kb1 = k0 * beta0, k1 * beta1
        lhs = jnp.concatenate([jnp.concatenate([kb0, kb1], axis=1),
                               jnp.concatenate([q0, q1], axis=1)], axis=0)
        k_bd = jnp.concatenate([jnp.concatenate([k0, zeros_k], axis=1),
                                jnp.concatenate([zeros_k, k1], axis=1)], axis=0)
        (prods,) = yield [lambda: _mm_nt(lhs, k_bd)]

        gc0, gc1 = gcum[:, cg:cg + 1], gcum[:, cg + 1:cg + 2]
        gt0, gt1 = gcum_t[cg:cg + 1], gcum_t[cg + 1:cg + 2]
        gr = (jnp.where(first_row, pltpu.roll(gt0, c, 1), gt1) if d
              else jnp.where(first_row, gt0, pltpu.roll(gt1, c, 1)))
        gc = jnp.where(first, gc0, gc1)
        decay = jnp.exp(jnp.where(incl, gc - gr, NEG_BIG))
        a = jnp.where(strict, prods[:c] * decay, 0.0)
        ak_ref[0, ci, 0:c, p * 2 * c:(p + 1) * 2 * c] = (prods[c:] * decay).astype(BF16)
        ak_ref[0, ci, c:3 * c, p * 2 * c:(p + 1) * 2 * c] = (k_t * jnp.exp(gc[last:last + 1] - gr)).astype(BF16)

        t = eye2 - jnp.where(coupling(0), a, 0.0)
        k = 1
        while 2 ** k < c:
            l = jnp.where(coupling(k), a, 0.0)
            t_bd = _pair_block_diag(t, first)
            (lt,) = yield [lambda: _mm(l, t_bd)]
            lt_bd = _pair_block_diag(lt, first)
            (tlt,) = yield [lambda: _mm(t, lt_bd)]
            t = t - tlt
            k += 1

        e0, e1 = jnp.exp(gc0), jnp.exp(gc1)
        rhs = jnp.concatenate([jnp.concatenate([v0 * beta0, kb0 * e0], axis=1),
                               jnp.concatenate([v1 * beta1, kb1 * e1], axis=1)], axis=0)
        t_bd = _pair_block_diag(t, first)
        (sol,) = yield [lambda: _mm(t_bd, rhs)]
        for i, (qe, g_last) in enumerate(((q0 * e0, gc0[last:last + 1]), (q1 * e1, gc1[last:last + 1]))):
            h = 2 * p + i
            hs = slice(h * GDN_DK, (h + 1) * GDN_DK)
            u_ref[0, ci * c:(ci + 1) * c, hs] = sol[i * c:(i + 1) * c, :GDN_DK]
            wq_ref[0, ci, 0:c, hs] = sol[i * c:(i + 1) * c, GDN_DK:].astype(BF16)
            wq_ref[0, ci, c:2 * c, hs] = qe.astype(BF16)
            eg_ref[0, ci, h:h + 1, :] = jnp.broadcast_to(jnp.exp(g_last), (1, BA_PAD))

    def chunk(ci):
        rs = slice(ci * c, (ci + 1) * c)
        heads = []
        for h in range(GDN_HEADS):
            lo = h * GDN_DK
            q = y[rs, lo:lo + GDN_DK]
            k = y[rs, GDN_WIDTH + lo:GDN_WIDTH + lo + GDN_DK]
            v = y[rs, 2 * GDN_WIDTH + lo:2 * GDN_WIDTH + lo + GDN_DK]
            q = q * (lax.rsqrt(jnp.sum(q * q, axis=-1, keepdims=True) + EPS) * (GDN_DK ** -0.5))
            k = k * lax.rsqrt(jnp.sum(k * k, axis=-1, keepdims=True) + EPS)
            heads.append((q, k, v))
        g = g_all[rs]
        g3 = jnp.concatenate(_split3(g), axis=1)
        k_rows = [jnp.concatenate([heads[2 * p][1], heads[2 * p + 1][1]], axis=0).astype(BF16)
                  for p in range(GDN_HEADS // 2)]
        parts, kt0, kt1 = yield [
            lambda: jnp.dot(tri_bf, g3, preferred_element_type=F32),
            lambda: lax.dot_general(eye_bf, k_rows[0], nt_dims, preferred_element_type=F32),
            lambda: lax.dot_general(eye_bf, k_rows[1], nt_dims, preferred_element_type=F32)]
        gcum_f = parts[:, :BA_PAD] + parts[:, BA_PAD:2 * BA_PAD] + parts[:, 2 * BA_PAD:]
        gcum_b = gcum_f[c - 1:c] - gcum_f + g
        both3 = jnp.concatenate(_split3(jnp.concatenate([gcum_f, gcum_b], axis=0)), axis=0)
        (tr,) = yield [lambda: lax.dot_general(eye_bf, both3, nt_dims, preferred_element_type=F32)]
        gcum_t = tr[:, :2 * c] + tr[:, 2 * c:4 * c] + tr[:, 4 * c:]
        return [pair_unit(ci, d, p, heads, (kt0, kt1)[p], beta_all[rs], (gcum_f, gcum_b)[d], gcum_t)
                for d in range(2) for p in range(GDN_HEADS // 2)]

    _run_interleaved([chunk(ci) for ci in range(GDN_TILE_CHUNKS)], GDN_CHUNK_SKEW)


def _gdn_local_call(qkv, ba, conv_w, alog_vec, dtb_vec, n_lat_tok):
    bsz, t_all, width = qkv.shape
    c = GDN_CHUNK
    tile = GDN_TILE_CHUNKS * c
    n_all = t_all // tile
    n_lat = n_lat_tok // tile
    per_tile = tile // HALO
    last_halo = t_all // HALO - 1
    n_chunks = t_all // c
    cblk = lambda rows, w: pl.BlockSpec((1, GDN_TILE_CHUNKS, rows, w), lambda b, j: (b, j, 0, 0))
    vec = pl.BlockSpec((1, BA_PAD), lambda b, j: (0, 0))
    outs = [(jax.ShapeDtypeStruct((bsz, t_all, GDN_WIDTH), F32),
             pl.BlockSpec((1, tile, GDN_WIDTH), lambda b, j: (b, j, 0))),
            (jax.ShapeDtypeStruct((bsz, n_chunks, 2 * c, GDN_WIDTH), BF16), cblk(2 * c, GDN_WIDTH)),
            (jax.ShapeDtypeStruct((bsz, n_chunks, 3 * c, GDN_HEADS * c), BF16), cblk(3 * c, GDN_HEADS * c)),
            (jax.ShapeDtypeStruct((bsz, n_chunks, HALO, BA_PAD), F32), cblk(HALO, BA_PAD))] * 2
    return pl.pallas_call(
        functools.partial(_gdn_local_kernel, n_lat, n_all),
        grid=(bsz, n_all),
        in_specs=[pl.BlockSpec((1, tile, width), lambda b, j: (b, j, 0)),
                  pl.BlockSpec((1, HALO, width), lambda b, j: (b, jnp.maximum(j * per_tile - 1, 0), 0)),
                  pl.BlockSpec((1, HALO, width),
                               lambda b, j: (b, jnp.minimum((j + 1) * per_tile, last_halo), 0)),
                  pl.BlockSpec((1, tile, BA_PAD), lambda b, j: (b, j, 0)),
                  pl.BlockSpec((CONV_W, width), lambda b, j: (0, 0)), vec, vec],
        out_specs=[o[1] for o in outs],
        out_shape=[o[0] for o in outs],
        compiler_params=pltpu.CompilerParams(dimension_semantics=("arbitrary", "arbitrary"),
                                             vmem_limit_bytes=VMEM_LIMIT),
        name="gdn_local",
    )(qkv, qkv, qkv, ba, conv_w, alog_vec, dtb_vec)


def _gdn_pipe_body(seg_start, seg_end, x_ref, prev_ref, next_ref, ba_ref, cw_ref, alog_ref, dtb_ref,
                   out_refs, stage_in, stage_out):
    c = GDN_CHUNK
    pairs = GDN_HEADS // 2
    row = lax.broadcasted_iota(jnp.int32, (c, 2 * c), 0)
    lane2 = lax.broadcasted_iota(jnp.int32, (c, 2 * c), 1)
    col = lane2 & (c - 1)
    first = lane2 < c
    first_row = first[0:1]
    tri_bf = (lax.broadcasted_iota(jnp.int32, (c, c), 0)
              >= lax.broadcasted_iota(jnp.int32, (c, c), 1)).astype(BF16)
    eye2 = (row == col).astype(F32)
    eye_bf = _eye(GDN_DK).astype(BF16)
    zeros_k = jnp.zeros((c, GDN_DK), BF16)
    nt_dims = (((1,), (1,)), ((), ()))
    unit_index = lambda ci, d, p: (ci * 2 + d) * pairs + p

    def coupling(k):
        return jnp.logical_and((row >> k) != (col >> k), (row >> (k + 1)) == (col >> (k + 1)))

    lhs_in, kr_in, rhs_in, dec_in, kdt_in, qe_in, eg_in = stage_in

    def pair_unit(ci, d, p):
        u_ref, wq_ref, ak_ref, eg_ref = out_refs[d]
        un = unit_index(ci, d, p)
        strict = (row < col) if d else (row > col)
        k_rows = kr_in[ci * pairs + p]
        k_bd = jnp.concatenate([jnp.concatenate([k_rows[:c], zeros_k], axis=1),
                                jnp.concatenate([zeros_k, k_rows[c:]], axis=1)], axis=0)
        lhs = lhs_in[un]
        (prods,) = yield [lambda: lax.dot_general(lhs, k_bd, nt_dims, preferred_element_type=F32)]
        decay = dec_in[un]
        a = jnp.where(strict, prods[:c] * decay, 0.0)
        ps = slice(p * 2 * c, (p + 1) * 2 * c)
        ak_ref[0, ci, 0:c, ps] = (prods[c:] * decay).astype(BF16)
        ak_ref[0, ci, c:3 * c, ps] = kdt_in[un]
        t = eye2 - jnp.where(coupling(0), a, 0.0)
        k = 1
        while 2 ** k < c:
            l = jnp.where(coupling(k), a, 0.0)
            t_bd = _pair_block_diag(t, first)
            (lt,) = yield [lambda: _mm(l, t_bd)]
            lt_bd = _pair_block_diag(lt, first)
            (tlt,) = yield [lambda: _mm(t, lt_bd)]
            t = t - tlt
            k += 1
        t_bd = _pair_block_diag(t, first).astype(BF16)
        rhs = rhs_in[un]
        (sol,) = yield [lambda: jnp.dot(t_bd, rhs, preferred_element_type=F32)]
        for i in range(2):
            h = 2 * p + i
            hs = slice(h * GDN_DK, (h + 1) * GDN_DK)
            u_ref[0, ci * c:(ci + 1) * c, hs] = sol[i * c:(i + 1) * c, :GDN_DK]
            wq_ref[0, ci, 0:c, hs] = sol[i * c:(i + 1) * c, GDN_DK:].astype(BF16)
            wq_ref[0, ci, c:2 * c, hs] = qe_in[un, i * c:(i + 1) * c]
            eg_ref[0, ci, h:h + 1, :] = eg_in[un, i:i + 1]

    lhs_out, kr_out, rhs_out, dec_out, kdt_out, qe_out, eg_out = stage_out
    prev = jnp.where(seg_start, 0.0, prev_ref[0])
    nxt = jnp.where(seg_end, 0.0, next_ref[0])
    y = _silu(_dwconv(x_ref[0], prev, nxt, cw_ref[...]))
    ba = ba_ref[0]
    lane = lax.broadcasted_iota(jnp.int32, (1, BA_PAD), 1)
    is_g = jnp.logical_and(lane >= 2 * GDN_HEADS, lane < 4 * GDN_HEADS)
    beta_all = _sigmoid(ba)
    g_all = jnp.where(is_g, -jnp.exp(alog_ref[...]) * _softplus(ba + dtb_ref[...]), 0.0)

    def chunk(ci):
        rs = slice(ci * c, (ci + 1) * c)
        heads = []
        for h in range(GDN_HEADS):
            lo = h * GDN_DK
            q = y[rs, lo:lo + GDN_DK]
            k = y[rs, GDN_WIDTH + lo:GDN_WIDTH + lo + GDN_DK]
            v = y[rs, 2 * GDN_WIDTH + lo:2 * GDN_WIDTH + lo + GDN_DK]
            q = q * (lax.rsqrt(jnp.sum(q * q, axis=-1, keepdims=True) + EPS) * (GDN_DK ** -0.5))
            k = k * lax.rsqrt(jnp.sum(k * k, axis=-1, keepdims=True) + EPS)
            heads.append((q, k, v))
        g = g_all[rs]
        k_rows = [jnp.concatenate([heads[2 * p][1], heads[2 * p + 1][1]], axis=0) for p in range(pairs)]
        kt0, kt1 = (kr.T for kr in k_rows)
        grow = lax.broadcasted_iota(jnp.int32, g.shape, 0)
        gcum_f = g
        step = 1
        while step < c:
            gcum_f = gcum_f + jnp.where(grow >= step, pltpu.roll(gcum_f, step, 0), 0.0)
            step *= 2
        gcum_b = gcum_f[c - 1:c] - gcum_f + g
        gcum_t = jnp.concatenate([gcum_f, gcum_b], axis=0).T
        beta_c = beta_all[rs]
        for p in range(pairs):
            kr_out[ci * pairs + p] = k_rows[p].astype(BF16)
        for d in range(2):
            gcum = gcum_b if d else gcum_f
            incl = (row <= col) if d else (row >= col)
            last = 0 if d else c - 1
            for p in range(pairs):
                un = unit_index(ci, d, p)
                (q0, k0, v0), (q1, k1, v1) = heads[2 * p], heads[2 * p + 1]
                cb = d * GDN_HEADS + 2 * p
                cg = 2 * GDN_HEADS + cb
                beta0, beta1 = beta_c[:, cb:cb + 1], beta_c[:, cb + 1:cb + 2]
                gc0, gc1 = gcum[:, cg:cg + 1], gcum[:, cg + 1:cg + 2]
                gt0, gt1 = gcum_t[cg:cg + 1], gcum_t[cg + 1:cg + 2]
                gr = (jnp.where(first_row, pltpu.roll(gt0, c, 1), gt1) if d
                      else jnp.where(first_row, gt0, pltpu.roll(gt1, c, 1)))
                gc = jnp.where(first, gc0, gc1)
                e0, e1 = jnp.exp(gc0), jnp.exp(gc1)
                kb0, kb1 = k0 * beta0, k1 * beta1
                dec_out[un] = jnp.exp(jnp.where(incl, gc - gr, NEG_BIG))
                lhs_out[un] = jnp.concatenate([jnp.concatenate([kb0, kb1], axis=1),
                                               jnp.concatenate([q0, q1], axis=1)], axis=0).astype(BF16)
                rhs_out[un] = jnp.concatenate([jnp.concatenate([v0 * beta0, kb0 * e0], axis=1),
                                               jnp.concatenate([v1 * beta1, kb1 * e1], axis=1)],
                                              axis=0).astype(BF16)
                kdt_out[un] = ((kt0, kt1)[p] * jnp.exp(gc[last:last + 1] - gr)).astype(BF16)
                qe_out[un] = jnp.concatenate([q0 * e0, q1 * e1], axis=0).astype(BF16)
                eg_out[un] = jnp.concatenate(
                    [jnp.broadcast_to(jnp.exp(gc0[last:last + 1]), (1, BA_PAD)),
                     jnp.broadcast_to(jnp.exp(gc1[last:last + 1]), (1, BA_PAD)),
                     jnp.zeros((HALO - 2, BA_PAD), F32)], axis=0)

    for refs in out_refs:
        refs[3][...] = jnp.zeros_like(refs[3])
    _run_interleaved([pair_unit(ci, d, p) for ci in range(GDN_TILE_CHUNKS) for d in range(2)
                      for p in range(pairs)], 0)
    for ci in range(GDN_TILE_CHUNKS):
        chunk(ci)


def _gdn_local_pipe_kernel(n_lat_tiles, n_all_tiles, n_steps, x_ref, prev_ref, next_ref, ba_ref, cw_ref,
                           alog_ref, dtb_ref, uf_ref, wqf_ref, akf_ref, egf_ref, ub_ref, wqb_ref, akb_ref,
                           egb_ref, *stage_refs):
    s = pl.program_id(0)
    j = jnp.minimum(s, n_steps - 2) % n_all_tiles
    seg_start = jnp.logical_or(j == 0, j == n_lat_tiles)
    seg_end = jnp.logical_or(j == n_lat_tiles - 1, j == n_all_tiles - 1)
    out_refs = ((uf_ref, wqf_ref, akf_ref, egf_ref), (ub_ref, wqb_ref, akb_ref, egb_ref))
    half = len(stage_refs) // 2
    sets = (stage_refs[:half], stage_refs[half:])

    @pl.when(s == 0)
    def _():
        for r in sets[1]:
            r[...] = jnp.zeros_like(r)

    for parity in range(2):
        @pl.when(s % 2 == parity)
        def _():
            _gdn_pipe_body(seg_start, seg_end, x_ref, prev_ref, next_ref, ba_ref, cw_ref, alog_ref, dtb_ref,
                           out_refs, sets[1 - parity], sets[parity])


def _gdn_local_pipe_call(qkv, ba, conv_w, alog_vec, dtb_vec, n_lat_tok):
    bsz, t_all, width = qkv.shape
    c = GDN_CHUNK
    tile = GDN_TILE_CHUNKS * c
    n_all = t_all // tile
    n_lat = n_lat_tok // tile
    per_tile = tile // HALO
    last_halo = t_all // HALO - 1
    n_chunks = t_all // c
    n_tiles = bsz * n_all
    n_steps = n_tiles + 1
    pairs = GDN_HEADS // 2
    units = GDN_TILE_CHUNKS * 2 * pairs
    t_in = lambda s: jnp.minimum(s, n_tiles - 1)
    t_out = lambda s: jnp.maximum(s - 1, 0)
    in_tok = lambda rows, w: pl.BlockSpec((1, rows, w), lambda s: (t_in(s) // n_all, t_in(s) % n_all, 0))
    cblk = lambda rows, w: pl.BlockSpec((1, GDN_TILE_CHUNKS, rows, w),
                                        lambda s: (t_out(s) // n_all, t_out(s) % n_all, 0, 0))
    vec = pl.BlockSpec((1, BA_PAD), lambda s: (0, 0))
    outs = [(jax.ShapeDtypeStruct((bsz, t_all, GDN_WIDTH), F32),
             pl.BlockSpec((1, tile, GDN_WIDTH), lambda s: (t_out(s) // n_all, t_out(s) % n_all, 0))),
            (jax.ShapeDtypeStruct((bsz, n_chunks, 2 * c, GDN_WIDTH), BF16), cblk(2 * c, GDN_WIDTH)),
            (jax.ShapeDtypeStruct((bsz, n_chunks, 3 * c, GDN_HEADS * c), BF16), cblk(3 * c, GDN_HEADS * c)),
            (jax.ShapeDtypeStruct((bsz, n_chunks, HALO, BA_PAD), F32), cblk(HALO, BA_PAD))] * 2
    stage = [pltpu.VMEM((units, 2 * c, 2 * GDN_DK), BF16),
             pltpu.VMEM((GDN_TILE_CHUNKS * pairs, 2 * c, GDN_DK), BF16),
             pltpu.VMEM((units, 2 * c, 2 * GDN_DK), BF16),
             pltpu.VMEM((units, c, 2 * c), F32),
             pltpu.VMEM((units, GDN_DK, 2 * c), BF16),
             pltpu.VMEM((units, 2 * c, GDN_DK), BF16),
             pltpu.VMEM((units, HALO, BA_PAD), F32)]
    return pl.pallas_call(
        functools.partial(_gdn_local_pipe_kernel, n_lat, n_all, n_steps),
        grid=(n_steps,),
        in_specs=[in_tok(tile, width),
                  pl.BlockSpec((1, HALO, width),
                               lambda s: (t_in(s) // n_all, jnp.maximum((t_in(s) % n_all) * per_tile - 1, 0), 0)),
                  pl.BlockSpec((1, HALO, width),
                               lambda s: (t_in(s) // n_all,
                                          jnp.minimum((t_in(s) % n_all + 1) * per_tile, last_halo), 0)),
                  in_tok(tile, BA_PAD),
                  pl.BlockSpec((CONV_W, width), lambda s: (0, 0)), vec, vec],
        out_specs=[o[1] for o in outs],
        out_shape=[o[0] for o in outs],
        scratch_shapes=stage + stage,
        compiler_params=pltpu.CompilerParams(dimension_semantics=("arbitrary",),
                                             vmem_limit_bytes=VMEM_LIMIT),
        name="gdn_local",
    )(qkv, qkv, qkv, ba, conv_w, alog_vec, dtb_vec)


def _gdn_scan_kernel(uf_ref, wqf_ref, akf_ref, egf_ref, ub_ref, wqb_ref, akb_ref, egb_ref,
                     of_ref, ob_ref, sf_ref, sb_ref):
    @pl.when(pl.program_id(0) == 0)
    def _():
        sf_ref[...] = jnp.zeros_like(sf_ref)
        sb_ref[...] = jnp.zeros_like(sb_ref)

    c = GDN_CHUNK
    bsz = uf_ref.shape[0]
    chains = [(refs, b, h)
              for refs in ((uf_ref, wqf_ref, akf_ref, egf_ref, of_ref, sf_ref),
                           (ub_ref, wqb_ref, akb_ref, egb_ref, ob_ref, sb_ref))
              for b in range(bsz) for h in range(GDN_HEADS)]
    hs = lambda h: slice(h * GDN_DK, (h + 1) * GDN_DK)
    states = [refs[5][b * GDN_HEADS + h] for refs, b, h in chains]
    with_s = [jnp.dot(refs[1][b, 0, :, hs(h)], s.astype(BF16), preferred_element_type=F32)
              for (refs, b, h), s in zip(chains, states)]
    v_new = [refs[0][b, :, hs(h)] - r[:c] for (refs, b, h), r in zip(chains, with_s)]
    with_v = [jnp.dot(refs[2][b, 0, :, h * c:(h + 1) * c], v.astype(BF16), preferred_element_type=F32)
              for (refs, b, h), v in zip(chains, v_new)]
    for (refs, b, h), s, r, a in zip(chains, states, with_s, with_v):
        refs[4][b, :, hs(h)] = r[c:] + a[:c]
        refs[5][b * GDN_HEADS + h] = s * refs[3][b, 0, h:h + 1, :] + a[c:]


def _gdn_scan_call(local, n_lat_tok):
    u = local[0]
    bsz, t_all, _ = u.shape
    c = GDN_CHUNK
    n_lat = n_lat_tok // c
    n_all = t_all // c
    n_ctx = n_all - n_lat
    fwd = lambda s: jnp.where(s < n_ctx, n_lat + s, s - n_ctx)
    bwd = lambda s: n_all - 1 - s
    fwd_out = lambda s: jnp.maximum(s - n_ctx, 0)
    bwd_out = lambda s: jnp.minimum(n_all - 1 - s, n_lat - 1)
    tok = lambda f: pl.BlockSpec((bsz, c, GDN_WIDTH), lambda s: (0, f(s), 0))
    cblk = lambda rows, w, f: pl.BlockSpec((bsz, 1, rows, w), lambda s: (0, f(s), 0, 0))
    ins = lambda f: [tok(f), cblk(2 * c, GDN_WIDTH, f), cblk(3 * c, GDN_HEADS * c, f), cblk(HALO, BA_PAD, f)]
    out = jax.ShapeDtypeStruct((bsz, n_lat_tok, GDN_WIDTH), F32)
    state = pltpu.VMEM((bsz * GDN_HEADS, GDN_DK, GDN_DK), F32)
    return pl.pallas_call(
        _gdn_scan_kernel,
        grid=(n_all,),
        in_specs=ins(fwd) + ins(bwd),
        out_specs=[tok(fwd_out), tok(bwd_out)],
        out_shape=[out, out],
        scratch_shapes=[state, state],
        compiler_params=pltpu.CompilerParams(dimension_semantics=("arbitrary",),
                                             vmem_limit_bytes=VMEM_LIMIT),
        name="gdn_scan",
    )(*local)


def _lin_scan(a, b, reverse):
    r = a.shape[0]
    row = lax.broadcasted_iota(jnp.int32, a.shape, 0)
    s = 1
    while s < r:
        ok = (row < r - s) if reverse else (row >= s)
        shift = r - s if reverse else s
        a_sh = jnp.where(ok, pltpu.roll(a, shift, 0), 1.0)
        b_sh = jnp.where(ok, pltpu.roll(b, shift, 0), 0.0)
        b = a * b_sh + b
        a = a * a_sh
        s *= 2
    return a, b


def _lru_direction(reverse, reset_first, n_slabs, slab, x_ref, prev_ref, next_ref, cw, cb, wr_ref, wi_ref,
                   br, bi, lam, h_ref, carry_ref):
    bsz, r, width = x_ref.shape
    step = pl.program_id(0)
    has_prev = slab > 0
    has_next = slab < n_slabs - 1
    xs = []
    for b in range(bsz):
        prev = jnp.where(has_prev, prev_ref[b], 0.0)
        nxt = jnp.where(has_next, next_ref[b], 0.0)
        xs.append(_dwconv(x_ref[b], prev, nxt, cw) + cb)
    x = jnp.concatenate(xs, axis=0)
    xb = x.astype(BF16)
    gate_r = _sigmoid(jnp.dot(xb, wr_ref[...], preferred_element_type=F32) + br)
    gate_i = _sigmoid(jnp.dot(xb, wi_ref[...], preferred_element_type=F32) + bi)
    log_a = (-LRU_C * _softplus(-lam)) * gate_r
    a = jnp.exp(log_a)
    mult = jnp.sqrt(1.0 - jnp.exp(2.0 * log_a))
    if reset_first:
        row = lax.broadcasted_iota(jnp.int32, (r, width), 0)
        first = (r - 1) if reverse else 0
        is_first = jnp.logical_and(step == 0, row == first)
    edge = 0 if reverse else r - 1
    for b in range(bsz):
        sl = slice(b * r, (b + 1) * r)
        m = mult[sl]
        if reset_first:
            m = jnp.where(is_first, 1.0, m)
        a_cum, h = _lin_scan(a[sl], m * (gate_i[sl] * x[sl]), reverse)
        h = h + a_cum * carry_ref[b:b + 1]
        h_ref[b] = h
        carry_ref[b:b + 1] = h[edge:edge + 1]


def _lru_kernel(reset_first, n_slabs,
                xf_ref, pf_ref, nf_ref, xb_ref, pb_ref, nb_ref, cw_ref, cb_ref,
                wrf_ref, wif_ref, wrb_ref, wib_ref, bg_ref, lam_ref, h0_ref,
                hf_ref, hb_ref, fin_ref, cf_ref, cbk_ref):
    step = pl.program_id(0)

    @pl.when(step == 0)
    def _():
        cf_ref[...] = h0_ref[0]
        cbk_ref[...] = h0_ref[1]

    cw = cw_ref[...]
    cb = cb_ref[...]
    _lru_direction(False, reset_first, n_slabs, step, xf_ref, pf_ref, nf_ref, cw, cb, wrf_ref, wif_ref,
                   bg_ref[0:1], bg_ref[1:2], lam_ref[0:1], hf_ref, cf_ref)
    _lru_direction(True, reset_first, n_slabs, n_slabs - 1 - step, xb_ref, pb_ref, nb_ref, cw, cb,
                   wrb_ref, wib_ref, bg_ref[2:3], bg_ref[3:4], lam_ref[1:2], hb_ref, cbk_ref)
    fin_ref[0] = cf_ref[...]
    fin_ref[1] = cbk_ref[...]


def _lru_ctx_call(rx, n_lat_tok, conv_w, conv_b, w_dense, b_gate, lam):
    bsz, t_all, width = rx.shape
    r = GRID_W
    per = r // HALO
    n_slabs = (t_all - n_lat_tok) // r
    first = n_lat_tok // r
    main = lambda f: pl.BlockSpec((bsz, r, width), lambda s: (0, first + f(s), 0))
    prev = lambda f: pl.BlockSpec((bsz, HALO, width), lambda s: (0, (first + f(s)) * per - 1, 0))
    nxt = lambda f: pl.BlockSpec(
        (bsz, HALO, width), lambda s: (0, jnp.minimum((first + f(s) + 1) * per, t_all // HALO - 1), 0))
    out_main = lambda f: pl.BlockSpec((bsz, r, width), lambda s: (0, f(s), 0))
    out_shape = (bsz, n_slabs * r, width)
    fwd = lambda s: s
    bwd = lambda s: n_slabs - 1 - s
    full = lambda shape: pl.BlockSpec(shape, lambda s: (0,) * len(shape))
    carry = pltpu.VMEM((bsz, width), F32)
    return pl.pallas_call(
        functools.partial(_lru_kernel, True, n_slabs),
        grid=(n_slabs,),
        in_specs=[main(fwd), prev(fwd), nxt(fwd), main(bwd), prev(bwd), nxt(bwd),
                  full((CONV_W, width)), full((1, width))]
                 + [full((width, width))] * 4
                 + [full((4, width)), full((2, width)), full((2, bsz, width))],
        out_specs=[out_main(fwd), out_main(bwd), full((2, bsz, width))],
        out_shape=[jax.ShapeDtypeStruct(out_shape, F32), jax.ShapeDtypeStruct(out_shape, F32),
                   jax.ShapeDtypeStruct((2, bsz, width), F32)],
        scratch_shapes=[carry, carry],
        compiler_params=pltpu.CompilerParams(dimension_semantics=("arbitrary",),
                                             vmem_limit_bytes=VMEM_LIMIT),
        name="lru_ctx",
    )(rx, rx, rx, rx, rx, rx, conv_w, conv_b, *w_dense, b_gate, lam,
      jnp.zeros((2, bsz, width), F32))[2]


LRU_LANES = 128
LRU_ROWS_PER_PASS = 16


def _lru_grid_kernel(x_ref, h0_ref, cw_ref, cb_ref, w_ref, bg_ref, lam_ref, out_ref,
                     af_ref, bf_ref, ab_ref, bb_ref):
    n = GRID_W
    rp = LRU_ROWS_PER_PASS
    lanes = x_ref.shape[-1]
    crow = lax.broadcasted_iota(jnp.int32, (n, lanes), 0)
    cw = cw_ref[...]
    cb = cb_ref[...]

    def grid_rows(lo, hi):
        parts = []
        for r in range(lo, min(hi, 0)):
            parts.append(jnp.where(crow == 0, 0.0, pltpu.roll(x_ref[0, (n + r) * n:(n + r + 1) * n, :], 1, 0)))
        if max(lo, 0) < min(hi, n):
            parts.append(x_ref[0, max(lo, 0) * n:min(hi, n) * n, :])
        for r in range(max(lo, n), hi):
            parts.append(jnp.where(crow == n - 1, 0.0,
                                   pltpu.roll(x_ref[0, (r - n) * n:(r - n + 1) * n, :], n - 1, 0)))
        return parts[0] if len(parts) == 1 else jnp.concatenate(parts, axis=0)

    for r0 in range(0, n, rp):
        sl = slice(r0 * n, (r0 + rp) * n)
        xc = (grid_rows(r0 - 2, r0 + rp - 2) * cw[0:1] + grid_rows(r0 - 1, r0 + rp - 1) * cw[1:2]
              + grid_rows(r0, r0 + rp) * cw[2:3] + grid_rows(r0 + 1, r0 + rp + 1) * cw[3:4] + cb)
        xb = xc.astype(BF16)
        for d, (a_ref, b_ref) in enumerate(((af_ref, bf_ref), (ab_ref, bb_ref))):
            gate_r = _sigmoid(jnp.dot(xb, w_ref[2 * d, 0], preferred_element_type=F32) + bg_ref[2 * d:2 * d + 1])
            gate_i = _sigmoid(jnp.dot(xb, w_ref[2 * d + 1, 0], preferred_element_type=F32)
                              + bg_ref[2 * d + 1:2 * d + 2])
            log_a = (-LRU_C * _softplus(-lam_ref[d:d + 1])) * gate_r
            a_ref[sl] = jnp.exp(log_a)
            b_ref[sl] = jnp.sqrt(1.0 - jnp.exp(2.0 * log_a)) * (gate_i * xc)

    hf = jnp.zeros((n, lanes), F32)
    hb = jnp.zeros((n, lanes), F32)
    pf = jnp.ones((n, lanes), F32)
    pb = jnp.ones((n, lanes), F32)
    for i in range(n):
        sf = slice(i * n, (i + 1) * n)
        sb = slice((n - 1 - i) * n, (n - i) * n)
        a = af_ref[sf]
        hf = a * hf + bf_ref[sf]
        pf = a * pf
        bf_ref[sf] = hf
        af_ref[sf] = pf
        a = ab_ref[sb]
        hb = a * hb + bb_ref[sb]
        pb = a * pb
        bb_ref[sb] = hb
        ab_ref[sb] = pb

    h0f = h0_ref[0, 0]
    h0b = h0_ref[1, 0]
    acc_a, acc_h = _lin_scan(pf, hf, False)
    in_f = jnp.where(crow == 0, h0f, pltpu.roll(acc_h + acc_a * h0f, 1, 0))
    acc_a, acc_h = _lin_scan(pb, hb, True)
    in_b = jnp.where(crow == n - 1, h0b, pltpu.roll(acc_h + acc_a * h0b, n - 1, 0))

    for r in range(n):
        sl = slice(r * n, (r + 1) * n)
        out_ref[0, sl, :] = (bf_ref[sl] + af_ref[sl] * in_f) + (bb_ref[sl] + ab_ref[sl] * in_b)


def _lru_grid_call(rx, n_lat_tok, h0, conv_w, conv_b, w_tiles, b_gate, lam):
    bsz, _, width = rx.shape
    nt = width // LRU_LANES
    lane = lambda rows: pl.BlockSpec((rows, LRU_LANES), lambda b, j: (0, j))
    scratch = pltpu.VMEM((n_lat_tok, LRU_LANES), F32)
    return pl.pallas_call(
        _lru_grid_kernel,
        grid=(bsz, nt),
        in_specs=[pl.BlockSpec((1, n_lat_tok, LRU_LANES), lambda b, j: (b, 0, j)),
                  pl.BlockSpec((2, 1, 1, LRU_LANES), lambda b, j: (0, b, 0, j)),
                  lane(CONV_W), lane(1),
                  pl.BlockSpec((4, 1, LRU_LANES, LRU_LANES), lambda b, j: (0, j, 0, 0)),
                  lane(4), lane(2)],
        out_specs=pl.BlockSpec((1, n_lat_tok, LRU_LANES), lambda b, j: (b, 0, j)),
        out_shape=jax.ShapeDtypeStruct((bsz, n_lat_tok, width), F32),
        scratch_shapes=[scratch] * 4,
        compiler_params=pltpu.CompilerParams(dimension_semantics=("arbitrary", "arbitrary"),
                                             vmem_limit_bytes=VMEM_LIMIT),
        name="lru_grid",
    )(rx, h0.reshape(2, bsz, 1, width), conv_w, conv_b, w_tiles, b_gate, lam)


def _out_ffn2_kernel(h1_ref, of_ref, ob_ref, z_ref, hs_ref, rg_ref, mod_ref, ng_ref, gnw_ref,
                     wout_ref, w1_ref, w3_ref, w2_ref, fg_ref, out_ref):
    mod = mod_ref[0]
    o = of_ref[0] + ob_ref[0]
    z = z_ref[0]
    parts = []
    for h in range(GDN_HEADS):
        sl = slice(h * GDN_DK, (h + 1) * GDN_DK)
        oh = o[:, sl]
        parts.append(oh * lax.rsqrt(jnp.mean(oh * oh, axis=-1, keepdims=True) + EPS)
                     * gnw_ref[...] * _silu(z[:, sl]))
    parts.append(hs_ref[0] * _gelu_tanh(rg_ref[0]))
    y = jnp.dot(jnp.concatenate(parts, axis=1).astype(BF16), wout_ref[...], preferred_element_type=F32)
    h2 = h1_ref[0] + mod[5:6] * y
    u = _rms_mod(h2, ng_ref[2:3], mod[6:7], mod[7:8])
    h3 = h2 + FFN_RESIDUAL * mod[8:9] * _swiglu(u, w1_ref, w3_ref, w2_ref)
    out_ref[0] = h3 * lax.rsqrt(jnp.mean(h3 * h3, axis=-1, keepdims=True) + EPS) * fg_ref[...]


def _out_ffn2_call(h1, o_f, o_b, z, h_sum, rg, mods, norm_g, gdn_norm_w, w_out, w1, w3, w2, final_g):
    bsz, seq, _ = o_f.shape
    n_lat = seq // TOKEN_TILE
    tok = lambda w: pl.BlockSpec((1, TOKEN_TILE, w), lambda b, j: (b, j, 0))
    return pl.pallas_call(
        _out_ffn2_kernel,
        grid=(bsz, n_lat),
        in_specs=[tok(D_MODEL), tok(GDN_WIDTH), tok(GDN_WIDTH), tok(GDN_WIDTH),
                  tok(LRU_WIDTH), tok(LRU_WIDTH),
                  pl.BlockSpec((1, N_MOD, D_MODEL), lambda b, j: (b + 1, 0, 0)),
                  _resident((3, D_MODEL)), _resident((1, GDN_DK)),
                  _resident((D_MODEL, D_MODEL)),
                  _resident((D_MODEL, D_FF)), _resident((D_MODEL, D_FF)), _resident((D_FF, D_MODEL)),
                  _resident((1, D_MODEL))],
        out_specs=tok(D_MODEL),
        out_shape=jax.ShapeDtypeStruct((bsz, seq, D_MODEL), F32),
        compiler_params=pltpu.CompilerParams(dimension_semantics=("arbitrary", "arbitrary"),
                                             vmem_limit_bytes=VMEM_LIMIT),
        name="out_ffn2",
    )(h1, o_f, o_b, z, h_sum, rg, mods, norm_g, gdn_norm_w, w_out, w1, w3, w2, final_g)


def _block_diag(w):
    nb, bw, _ = w.shape
    eye = jnp.eye(nb, dtype=w.dtype)
    return jnp.einsum('ncd,nm->ncmd', w, eye).reshape(nb * bw, nb * bw)


def kernel(x, c, ctx, c_ctx, w_ada, b_ada, norm_g, ffn_w1, ffn_w3, ffn_w2, w_in, w_out, gdn_conv_w, gdn_a_log,
           gdn_dt_bias, gdn_norm_w, lru_conv_w, lru_conv_b, lru_w_gate, lru_b_gate, lru_lambda, final_norm_g):
    bsz, seq, _ = x.shape
    assert w_ada.shape[0] == 1 and seq == GRID_W * GRID_W and ctx.shape[1] == TOKEN_TILE

    rows = -(-(bsz + 1) // 8) * 8
    cvec = jnp.concatenate([c_ctx[None, :], c, jnp.zeros((rows - bsz - 1, D_MODEL), F32)], axis=0)
    mods = _ada_call(cvec, w_ada[0], b_ada).reshape(rows, N_MOD, D_MODEL)

    wi = w_in[0]
    o_z, o_ba = 3 * GDN_WIDTH, 4 * GDN_WIDTH
    o_rx = o_ba + 4 * GDN_HEADS
    w_in_r = jnp.concatenate([wi[:, :o_ba].astype(BF16), wi[:, o_rx:].astype(BF16), wi[:, o_ba:o_rx].astype(BF16),
                              jnp.zeros((D_MODEL, BA_PAD - 4 * GDN_HEADS), BF16)], axis=1)
    w1 = [ffn_w1[0, i].astype(BF16) for i in range(2)]
    w3 = [ffn_w3[0, i].astype(BF16) for i in range(2)]
    w2 = [ffn_w2[0, i].astype(BF16) for i in range(2)]

    h1, qkv, z, rx, rg, ba = _ffn1_in_call(x, ctx, mods, norm_g[0], w1[0], w3[0], w2[0], w_in_r)

    pad_g = lambda t: jnp.zeros((1, BA_PAD), F32).at[0, 2 * GDN_HEADS:4 * GDN_HEADS].set(t.reshape(-1))
    local = _gdn_local_pipe_call(qkv, ba, gdn_conv_w[0], pad_g(gdn_a_log[0]), pad_g(gdn_dt_bias[0]), seq)
    o_f, o_b = _gdn_scan_call(local, seq)

    wg = lru_w_gate[0]
    w_dense = [_block_diag(wg[d, g]).astype(BF16) for d in range(2) for g in range(2)]
    b_gate = lru_b_gate[0].reshape(4, LRU_WIDTH)
    lam = lru_lambda[0]
    cw, cb = lru_conv_w[0], lru_conv_b
    h_ctx = _lru_ctx_call(rx, seq, cw, cb, w_dense, b_gate, lam)
    nt = LRU_WIDTH // LRU_LANES
    w_tiles = jnp.stack([jnp.stack([w[t * LRU_LANES:(t + 1) * LRU_LANES, t * LRU_LANES:(t + 1) * LRU_LANES]
                                    for t in range(nt)]) for w in w_dense])
    h_sum = _lru_grid_call(rx, seq, h_ctx, cw, cb, w_tiles, b_gate, lam)

    return _out_ffn2_call(h1, o_f, o_b, z, h_sum, rg, mods, norm_g[0], gdn_norm_w,
                          w_out[0].astype(BF16), w1[1], w3[1], w2[1], final_norm_g[None, :])
```

```python
import functools

import jax
import jax.numpy as jnp
from jax import lax
from jax.experimental import pallas as pl
from jax.experimental.pallas import tpu as pltpu

D_MODEL = 1024
D_FF = 2816
N_MOD = 9
EPS = 1e-6
FFN_RESIDUAL = 0.5

GDN_WIDTH = 512
GDN_HEADS = 4
GDN_DK = 128
GDN_CHUNK = 64
CONV_W = 4
LRU_WIDTH = 512
LRU_C = 8.0
GRID_W = 64

TOKEN_TILE = 256
GDN_TILE_CHUNKS = TOKEN_TILE // GDN_CHUNK
GDN_SCAN_CHUNKS = 2
HALO = 8
BA_PAD = 128
P_HEAD_COLS = 4 * GDN_WIDTH
P_TAIL_COLS = 2 * LRU_WIDTH + BA_PAD
LRU_LANES = 128
LRU_ROWS_PER_PASS = 16
VMEM_LIMIT = 56 * 1024 * 1024

BF16 = jnp.bfloat16
F32 = jnp.float32
NEG_BIG = -1e30
NT_DIMS = (((1,), (1,)), ((), ()))


def _mm(a, b):
    return jnp.dot(a.astype(BF16), b.astype(BF16), preferred_element_type=F32)


def _sigmoid(x):
    return 1.0 / (1.0 + jnp.exp(-x))


def _silu(x):
    return x * _sigmoid(x)


def _softplus(x):
    return jnp.maximum(x, 0.0) + jnp.log(1.0 + jnp.exp(-jnp.abs(x)))


def _gelu_tanh(x):
    return 0.5 * x * (1.0 + jnp.tanh(0.7978845608028654 * (x + 0.044715 * x * x * x)))


def _rms_mod(h, g, shift, scale):
    y = h * lax.rsqrt(jnp.mean(h * h, axis=-1, keepdims=True) + EPS) * g
    return y * (1.0 + scale) + shift


def _swiglu(u, w1_ref, w3_ref, w2_ref):
    ub = u.astype(BF16)
    a = jnp.dot(ub, w1_ref[...], preferred_element_type=F32)
    b = jnp.dot(ub, w3_ref[...], preferred_element_type=F32)
    g = (_silu(a) * b).astype(BF16)
    return jnp.dot(g, w2_ref[...], preferred_element_type=F32)


def _dwconv(x, prev, nxt, w):
    t = x.shape[0]
    row = lax.broadcasted_iota(jnp.int32, x.shape, 0)
    xm1 = jnp.where(row == 0, prev[HALO - 1:HALO], pltpu.roll(x, 1, 0))
    xm2 = jnp.where(row == 0, prev[HALO - 2:HALO - 1],
                    jnp.where(row == 1, prev[HALO - 1:HALO], pltpu.roll(x, 2, 0)))
    xp1 = jnp.where(row == t - 1, nxt[0:1], pltpu.roll(x, t - 1, 0))
    return xm2 * w[0:1] + xm1 * w[1:2] + x * w[2:3] + xp1 * w[3:4]


def _resident(shape, index=None):
    index = index or (0,) * len(shape)
    return pl.BlockSpec(shape, lambda *_: index, pipeline_mode=pl.Buffered(1))


def _ada_kernel(c_ref, w_ref, b_ref, o_ref):
    o_ref[...] = _mm(_silu(c_ref[...]), w_ref[...]) + b_ref[...]


def _ada_call(cvec, w_ada, b_ada):
    rows = cvec.shape[0]
    return pl.pallas_call(
        _ada_kernel,
        grid=(N_MOD,),
        in_specs=[pl.BlockSpec((rows, D_MODEL), lambda k: (0, 0)),
                  pl.BlockSpec((D_MODEL, D_MODEL), lambda k: (0, k)),
                  pl.BlockSpec((1, D_MODEL), lambda k: (0, k))],
        out_specs=pl.BlockSpec((rows, D_MODEL), lambda k: (0, k)),
        out_shape=jax.ShapeDtypeStruct((rows, N_MOD * D_MODEL), F32),
        name="ada",
    )(cvec, w_ada, b_ada)


def _segment_edges(j, n_lat_tiles, n_all_tiles):
    return (jnp.logical_or(j == 0, j == n_lat_tiles),
            jnp.logical_or(j == n_lat_tiles - 1, j == n_all_tiles - 1))


def _ffn1_in_kernel(n_lat_tiles, x_ref, ctx_ref, mod_ref, ng_ref, w1_ref, w3_ref, w2_ref, win_ref, wtail_ref,
                    h1_ref, qkv_ref, z_ref, rx_ref, rg_ref, ba_ref):
    j = pl.program_id(1)
    h = jnp.where(j == n_lat_tiles, ctx_ref[0], x_ref[0])
    mod = mod_ref[0]
    u = _rms_mod(h, ng_ref[0:1], mod[0:1], mod[1:2])
    h1 = h + FFN_RESIDUAL * mod[2:3] * _swiglu(u, w1_ref, w3_ref, w2_ref)
    h1_ref[0] = h1
    ub = _rms_mod(h1, ng_ref[1:2], mod[3:4], mod[4:5]).astype(BF16)
    head = jnp.dot(ub, win_ref[...], preferred_element_type=F32)
    tail = jnp.dot(ub, wtail_ref[...], preferred_element_type=F32)
    qkv_ref[0] = head[:, :3 * GDN_WIDTH]
    z_ref[0] = head[:, 3 * GDN_WIDTH:]
    rx_ref[0] = tail[:, :LRU_WIDTH]
    rg_ref[0] = tail[:, LRU_WIDTH:2 * LRU_WIDTH]
    ba_ref[0] = tail[:, 2 * LRU_WIDTH:]


def _ffn1_in_call(x, ctx, mods, norm_g, w1, w3, w2, w_in_bf, w_tail):
    bsz, seq, _ = x.shape
    n_lat = seq // TOKEN_TILE
    n_all = n_lat + 1
    t_all = n_all * TOKEN_TILE
    tok = lambda w: pl.BlockSpec((1, TOKEN_TILE, w), lambda b, j: (b, j, 0))
    widths = (D_MODEL, 3 * GDN_WIDTH, GDN_WIDTH, LRU_WIDTH, LRU_WIDTH, BA_PAD)
    return pl.pallas_call(
        functools.partial(_ffn1_in_kernel, n_lat),
        grid=(bsz, n_all),
        in_specs=[pl.BlockSpec((1, TOKEN_TILE, D_MODEL), lambda b, j: (b, jnp.minimum(j, n_lat - 1), 0)),
                  pl.BlockSpec((1, TOKEN_TILE, D_MODEL), lambda b, j: (b, 0, 0)),
                  pl.BlockSpec((1, N_MOD, D_MODEL), lambda b, j: (jnp.where(j == n_lat, 0, b + 1), 0, 0)),
                  _resident((3, D_MODEL)),
                  _resident((None, D_MODEL, D_FF), (0, 0, 0)), _resident((None, D_MODEL, D_FF), (0, 0, 0)),
                  _resident((None, D_FF, D_MODEL), (0, 0, 0)),
                  _resident((D_MODEL, P_HEAD_COLS)), _resident((D_MODEL, P_TAIL_COLS))],
        out_specs=[tok(w) for w in widths],
        out_shape=[jax.ShapeDtypeStruct((bsz, t_all, w), F32) for w in widths],
        compiler_params=pltpu.CompilerParams(dimension_semantics=("arbitrary", "arbitrary"),
                                             vmem_limit_bytes=VMEM_LIMIT),
        name="ffn1_in",
    )(x, ctx, mods, norm_g, w1, w3, w2, w_in_bf, w_tail)


def _pair_block_diag(x, first):
    return jnp.concatenate([jnp.where(first, x, 0.0), jnp.where(first, 0.0, x)], axis=0)


def _run_interleaved(coroutines):
    active = [[gen, next(gen)] for gen in coroutines]
    while active:
        still = []
        for item in active:
            results = [dot() for dot in item[1]]
            try:
                item[1] = item[0].send(results)
                still.append(item)
            except StopIteration:
                pass
        active = still


def _gdn_local_body(seg_start, seg_end, x_ref, prev_ref, next_ref, ba_ref, cw_ref, alog_ref, dtb_ref,
                    out_refs, stage_in, stage_out):
    c = GDN_CHUNK
    pairs = GDN_HEADS // 2
    row = lax.broadcasted_iota(jnp.int32, (c, 2 * c), 0)
    lane2 = lax.broadcasted_iota(jnp.int32, (c, 2 * c), 1)
    col = lane2 & (c - 1)
    first = lane2 < c
    first_row = first[0:1]
    eye2 = (row == col).astype(F32)
    zeros_k = jnp.zeros((c, GDN_DK), BF16)
    unit_index = lambda ci, d, p: (ci * 2 + d) * pairs + p

    def coupling(k):
        return jnp.logical_and((row >> k) != (col >> k), (row >> (k + 1)) == (col >> (k + 1)))

    lhs_in, kr_in, rhs_in, dec_in, kdt_in, qe_in, eg_in = stage_in

    def pair_unit(ci, d, p):
        u_ref, wq_ref, ak_ref, eg_ref = out_refs[d]
        un = unit_index(ci, d, p)
        strict = (row < col) if d else (row > col)
        k_rows = kr_in[ci * pairs + p]
        k_bd = jnp.concatenate([jnp.concatenate([k_rows[:c], zeros_k], axis=1),
                                jnp.concatenate([zeros_k, k_rows[c:]], axis=1)], axis=0)
        lhs = lhs_in[un]
        (prods,) = yield [lambda: lax.dot_general(lhs, k_bd, NT_DIMS, preferred_element_type=F32)]
        decay = dec_in[un]
        a = jnp.where(strict, prods[:c] * decay, 0.0)
        ps = slice(p * 2 * c, (p + 1) * 2 * c)
        ak_ref[ci, 0, 0:c, ps] = (prods[c:] * decay).astype(BF16)
        ak_ref[ci, 0, c:3 * c, ps] = kdt_in[un]
        t = eye2 - jnp.where(coupling(0), a, 0.0)
        k = 1
        while 2 ** k < c:
            l = jnp.where(coupling(k), a, 0.0)
            t_bd = _pair_block_diag(t, first)
            (lt,) = yield [lambda: _mm(l, t_bd)]
            lt_bd = _pair_block_diag(lt, first)
            (tlt,) = yield [lambda: _mm(t, lt_bd)]
            t = t - tlt
            k += 1
        t_bd = _pair_block_diag(t, first).astype(BF16)
        rhs = rhs_in[un]
        (sol,) = yield [lambda: jnp.dot(t_bd, rhs, preferred_element_type=F32)]
        for i in range(2):
            h = 2 * p + i
            hs = slice(h * GDN_DK, (h + 1) * GDN_DK)
            u_ref[ci, 0, :, hs] = sol[i * c:(i + 1) * c, :GDN_DK]
            wq_ref[ci, 0, 0:c, hs] = sol[i * c:(i + 1) * c, GDN_DK:].astype(BF16)
            wq_ref[ci, 0, c:2 * c, hs] = qe_in[un, i * c:(i + 1) * c]
            eg_ref[ci, 0, h:h + 1, :] = eg_in[un, i:i + 1]

    lhs_out, kr_out, rhs_out, dec_out, kdt_out, qe_out, eg_out = stage_out
    y = _silu(_dwconv(x_ref[0], jnp.where(seg_start, 0.0, prev_ref[0]), jnp.where(seg_end, 0.0, next_ref[0]),
                      cw_ref[...]))
    ba = ba_ref[0]
    lane = lax.broadcasted_iota(jnp.int32, (1, BA_PAD), 1)
    is_g = jnp.logical_and(lane >= 2 * GDN_HEADS, lane < 4 * GDN_HEADS)
    beta_all = _sigmoid(ba)
    g_all = jnp.where(is_g, -jnp.exp(alog_ref[...]) * _softplus(ba + dtb_ref[...]), 0.0)

    def chunk(ci):
        rs = slice(ci * c, (ci + 1) * c)
        heads = []
        for h in range(GDN_HEADS):
            lo = h * GDN_DK
            q = y[rs, lo:lo + GDN_DK]
            k = y[rs, GDN_WIDTH + lo:GDN_WIDTH + lo + GDN_DK]
            v = y[rs, 2 * GDN_WIDTH + lo:2 * GDN_WIDTH + lo + GDN_DK]
            q = q * (lax.rsqrt(jnp.sum(q * q, axis=-1, keepdims=True) + EPS) * (GDN_DK ** -0.5))
            k = k * lax.rsqrt(jnp.sum(k * k, axis=-1, keepdims=True) + EPS)
            heads.append((q, k, v))
        g = g_all[rs]
        k_rows = [jnp.concatenate([heads[2 * p][1], heads[2 * p + 1][1]], axis=0) for p in range(pairs)]
        k_ts = [kr.T for kr in k_rows]
        grow = lax.broadcasted_iota(jnp.int32, g.shape, 0)
        gcum_f = g
        step = 1
        while step < c:
            gcum_f = gcum_f + jnp.where(grow >= step, pltpu.roll(gcum_f, step, 0), 0.0)
            step *= 2
        gcum_b = gcum_f[c - 1:c] - gcum_f + g
        gcum_t = jnp.concatenate([gcum_f, gcum_b], axis=0).T
        beta_c = beta_all[rs]
        for p in range(pairs):
            kr_out[ci * pairs + p] = k_rows[p].astype(BF16)
        for d in range(2):
            gcum = gcum_b if d else gcum_f
            incl = (row <= col) if d else (row >= col)
            last = 0 if d else c - 1
            for p in range(pairs):
                un = unit_index(ci, d, p)
                (q0, k0, v0), (q1, k1, v1) = heads[2 * p], heads[2 * p + 1]
                cb = d * GDN_HEADS + 2 * p
                cg = 2 * GDN_HEADS + cb
                beta0, beta1 = beta_c[:, cb:cb + 1], beta_c[:, cb + 1:cb + 2]
                gc0, gc1 = gcum[:, cg:cg + 1], gcum[:, cg + 1:cg + 2]
                gt0, gt1 = gcum_t[cg:cg + 1], gcum_t[cg + 1:cg + 2]
                gr = (jnp.where(first_row, pltpu.roll(gt0, c, 1), gt1) if d
                      else jnp.where(first_row, gt0, pltpu.roll(gt1, c, 1)))
                gc = jnp.where(first, gc0, gc1)
                e0, e1 = jnp.exp(gc0), jnp.exp(gc1)
                kb0, kb1 = k0 * beta0, k1 * beta1
                dec_out[un] = jnp.exp(jnp.where(incl, gc - gr, NEG_BIG))
                lhs_out[un] = jnp.concatenate([jnp.concatenate([kb0, kb1], axis=1),
                                               jnp.concatenate([q0, q1], axis=1)], axis=0).astype(BF16)
                rhs_out[un] = jnp.concatenate([jnp.concatenate([v0 * beta0, kb0 * e0], axis=1),
                                               jnp.concatenate([v1 * beta1, kb1 * e1], axis=1)],
                                              axis=0).astype(BF16)
                kdt_out[un] = (k_ts[p] * jnp.exp(gc[last:last + 1] - gr)).astype(BF16)
                qe_out[un] = jnp.concatenate([q0 * e0, q1 * e1], axis=0).astype(BF16)
                eg_out[un] = jnp.concatenate(
                    [jnp.broadcast_to(jnp.exp(gc0[last:last + 1]), (1, BA_PAD)),
                     jnp.broadcast_to(jnp.exp(gc1[last:last + 1]), (1, BA_PAD)),
                     jnp.zeros((HALO - 2, BA_PAD), F32)], axis=0)

    for refs in out_refs:
        refs[3][...] = jnp.zeros_like(refs[3])
    _run_interleaved([pair_unit(ci, d, p) for ci in range(GDN_TILE_CHUNKS) for d in range(2)
                      for p in range(pairs)])
    for ci in range(GDN_TILE_CHUNKS):
        chunk(ci)


def _gdn_local_kernel(n_lat_tiles, n_all_tiles, n_tiles, x_ref, prev_ref, next_ref, ba_ref, cw_ref, alog_ref,
                      dtb_ref, uf_ref, wqf_ref, akf_ref, egf_ref, ub_ref, wqb_ref, akb_ref, egb_ref, *stage_refs):
    s = pl.program_id(0)
    seg_start, seg_end = _segment_edges(jnp.minimum(s, n_tiles - 1) % n_all_tiles, n_lat_tiles, n_all_tiles)
    out_refs = ((uf_ref, wqf_ref, akf_ref, egf_ref), (ub_ref, wqb_ref, akb_ref, egb_ref))
    half = len(stage_refs) // 2
    sets = (stage_refs[:half], stage_refs[half:])

    @pl.when(s == 0)
    def _():
        for r in sets[1]:
            r[...] = jnp.zeros_like(r)

    for parity in range(2):
        @pl.when(s % 2 == parity)
        def _():
            _gdn_local_body(seg_start, seg_end, x_ref, prev_ref, next_ref, ba_ref, cw_ref, alog_ref, dtb_ref,
                            out_refs, sets[1 - parity], sets[parity])


def _gdn_local_call(qkv, ba, conv_w, alog_vec, dtb_vec, n_lat_tok):
    bsz, t_all, width = qkv.shape
    c = GDN_CHUNK
    tile = GDN_TILE_CHUNKS * c
    n_all = t_all // tile
    n_lat = n_lat_tok // tile
    per_tile = tile // HALO
    last_halo = t_all // HALO - 1
    n_chunks = t_all // c
    n_tiles = bsz * n_all
    pairs = GDN_HEADS // 2
    units = GDN_TILE_CHUNKS * 2 * pairs
    t_in = lambda s: jnp.minimum(s, n_tiles - 1)
    t_out = lambda s: jnp.maximum(s - 1, 0)
    in_tok = lambda w: pl.BlockSpec((1, tile, w), lambda s: (t_in(s) // n_all, t_in(s) % n_all, 0))
    cblk = lambda rows, w: pl.BlockSpec((GDN_TILE_CHUNKS, 1, rows, w),
                                        lambda s: (t_out(s) % n_all, t_out(s) // n_all, 0, 0))
    vec = pl.BlockSpec((1, BA_PAD), lambda s: (0, 0))
    outs = [(jax.ShapeDtypeStruct((n_chunks, bsz, c, GDN_WIDTH), F32), cblk(c, GDN_WIDTH)),
            (jax.ShapeDtypeStruct((n_chunks, bsz, 2 * c, GDN_WIDTH), BF16), cblk(2 * c, GDN_WIDTH)),
            (jax.ShapeDtypeStruct((n_chunks, bsz, 3 * c, GDN_HEADS * c), BF16), cblk(3 * c, GDN_HEADS * c)),
            (jax.ShapeDtypeStruct((n_chunks, bsz, HALO, BA_PAD), F32), cblk(HALO, BA_PAD))] * 2
    stage = [pltpu.VMEM((units, 2 * c, 2 * GDN_DK), BF16),
             pltpu.VMEM((GDN_TILE_CHUNKS * pairs, 2 * c, GDN_DK), BF16),
             pltpu.VMEM((units, 2 * c, 2 * GDN_DK), BF16),
             pltpu.VMEM((units, c, 2 * c), F32),
             pltpu.VMEM((units, GDN_DK, 2 * c), BF16),
             pltpu.VMEM((units, 2 * c, GDN_DK), BF16),
             pltpu.VMEM((units, HALO, BA_PAD), F32)]
    return pl.pallas_call(
        functools.partial(_gdn_local_kernel, n_lat, n_all, n_tiles),
        grid=(n_tiles + 1,),
        in_specs=[in_tok(width),
                  pl.BlockSpec((1, HALO, width),
                               lambda s: (t_in(s) // n_all, jnp.maximum((t_in(s) % n_all) * per_tile - 1, 0), 0)),
                  pl.BlockSpec((1, HALO, width),
                               lambda s: (t_in(s) // n_all,
                                          jnp.minimum((t_in(s) % n_all + 1) * per_tile, last_halo), 0)),
                  in_tok(BA_PAD), pl.BlockSpec((CONV_W, width), lambda s: (0, 0)), vec, vec],
        out_specs=[o[1] for o in outs],
        out_shape=[o[0] for o in outs],
        scratch_shapes=stage + stage,
        compiler_params=pltpu.CompilerParams(dimension_semantics=("arbitrary",),
                                             vmem_limit_bytes=VMEM_LIMIT),
        name="gdn_local",
    )(qkv, qkv, qkv, ba, conv_w, alog_vec, dtb_vec)


def _gdn_scan_kernel(uf_ref, wqf_ref, akf_ref, egf_ref, ub_ref, wqb_ref, akb_ref, egb_ref,
                     of_ref, ob_ref, sf_ref, sb_ref):
    @pl.when(pl.program_id(0) == 0)
    def _():
        sf_ref[...] = jnp.zeros_like(sf_ref)
        sb_ref[...] = jnp.zeros_like(sb_ref)

    c = GDN_CHUNK
    bsz = uf_ref.shape[1]
    hs = lambda h: slice(h * GDN_DK, (h + 1) * GDN_DK)
    for sub in range(GDN_SCAN_CHUNKS):
        chains = [(refs, ci, b, h)
                  for refs, ci in (((uf_ref, wqf_ref, akf_ref, egf_ref, of_ref, sf_ref), sub),
                                   ((ub_ref, wqb_ref, akb_ref, egb_ref, ob_ref, sb_ref),
                                    GDN_SCAN_CHUNKS - 1 - sub))
                  for b in range(bsz) for h in range(GDN_HEADS)]
        states = [refs[5][b * GDN_HEADS + h] for refs, ci, b, h in chains]
        with_s = [jnp.dot(refs[1][ci, b, :, hs(h)], s.astype(BF16), preferred_element_type=F32)
                  for (refs, ci, b, h), s in zip(chains, states)]
        v_new = [refs[0][ci, b, :, hs(h)] - r[:c] for (refs, ci, b, h), r in zip(chains, with_s)]
        with_v = [jnp.dot(refs[2][ci, b, :, h * c:(h + 1) * c], v.astype(BF16), preferred_element_type=F32)
                  for (refs, ci, b, h), v in zip(chains, v_new)]
        for (refs, ci, b, h), s, r, a in zip(chains, states, with_s, with_v):
            refs[4][b, ci * c:(ci + 1) * c, hs(h)] = r[c:] + a[:c]
            refs[5][b * GDN_HEADS + h] = s * refs[3][ci, b, h:h + 1, :] + a[c:]


def _gdn_scan_call(local, n_lat_tok):
    n_chunks, bsz = local[0].shape[:2]
    c = GDN_CHUNK
    g = GDN_SCAN_CHUNKS
    n_all = n_chunks // g
    n_lat = n_lat_tok // (c * g)
    n_ctx = n_all - n_lat
    fwd = lambda s: jnp.where(s < n_ctx, n_lat + s, s - n_ctx)
    bwd = lambda s: n_all - 1 - s
    fwd_out = lambda s: jnp.maximum(s - n_ctx, 0)
    bwd_out = lambda s: jnp.minimum(n_all - 1 - s, n_lat - 1)
    cblk = lambda rows, w, f: pl.BlockSpec((g, bsz, rows, w), lambda s: (f(s), 0, 0, 0))
    ins = lambda f: [cblk(c, GDN_WIDTH, f), cblk(2 * c, GDN_WIDTH, f), cblk(3 * c, GDN_HEADS * c, f),
                     cblk(HALO, BA_PAD, f)]
    tok = lambda f: pl.BlockSpec((bsz, g * c, GDN_WIDTH), lambda s: (0, f(s), 0))
    out = jax.ShapeDtypeStruct((bsz, n_lat_tok, GDN_WIDTH), F32)
    state = pltpu.VMEM((bsz * GDN_HEADS, GDN_DK, GDN_DK), F32)
    return pl.pallas_call(
        _gdn_scan_kernel,
        grid=(n_all,),
        in_specs=ins(fwd) + ins(bwd),
        out_specs=[tok(fwd_out), tok(bwd_out)],
        out_shape=[out, out],
        scratch_shapes=[state, state],
        compiler_params=pltpu.CompilerParams(dimension_semantics=("arbitrary",),
                                             vmem_limit_bytes=VMEM_LIMIT),
        name="gdn_scan",
    )(*local)


def _lin_scan(a, b, reverse):
    r = a.shape[0]
    row = lax.broadcasted_iota(jnp.int32, a.shape, 0)
    s = 1
    while s < r:
        ok = (row < r - s) if reverse else (row >= s)
        shift = r - s if reverse else s
        a_sh = jnp.where(ok, pltpu.roll(a, shift, 0), 1.0)
        b_sh = jnp.where(ok, pltpu.roll(b, shift, 0), 0.0)
        b = a * b_sh + b
        a = a * a_sh
        s *= 2
    return a, b


def _lru_ctx_direction(reverse, n_slabs, slab, x_ref, prev_ref, next_ref, cw, cb, wr_ref, wi_ref,
                       br, bi, lam, carry_ref):
    bsz, r, width = x_ref.shape
    step = pl.program_id(0)
    has_prev = slab > 0
    has_next = slab < n_slabs - 1
    xs = []
    for b in range(bsz):
        prev = jnp.where(has_prev, prev_ref[b], 0.0)
        nxt = jnp.where(has_next, next_ref[b], 0.0)
        xs.append(_dwconv(x_ref[b], prev, nxt, cw) + cb)
    x = jnp.concatenate(xs, axis=0)
    xb = x.astype(BF16)
    gate_r = _sigmoid(jnp.dot(xb, wr_ref[...], preferred_element_type=F32) + br)
    gate_i = _sigmoid(jnp.dot(xb, wi_ref[...], preferred_element_type=F32) + bi)
    log_a = (-LRU_C * _softplus(-lam)) * gate_r
    a = jnp.exp(log_a)
    mult = jnp.sqrt(1.0 - jnp.exp(2.0 * log_a))
    row = lax.broadcasted_iota(jnp.int32, (r, width), 0)
    is_first = jnp.logical_and(step == 0, row == ((r - 1) if reverse else 0))
    edge = 0 if reverse else r - 1
    for b in range(bsz):
        sl = slice(b * r, (b + 1) * r)
        m = jnp.where(is_first, 1.0, mult[sl])
        a_cum, h = _lin_scan(a[sl], m * (gate_i[sl] * x[sl]), reverse)
        h = h + a_cum * carry_ref[b:b + 1]
        carry_ref[b:b + 1] = h[edge:edge + 1]


def _lru_ctx_kernel(n_slabs, xf_ref, pf_ref, nf_ref, xb_ref, pb_ref, nb_ref, cw_ref, cb_ref,
                    wrf_ref, wif_ref, wrb_ref, wib_ref, bg_ref, lam_ref, fin_ref, cf_ref, cbk_ref):
    step = pl.program_id(0)

    @pl.when(step == 0)
    def _():
        cf_ref[...] = jnp.zeros_like(cf_ref)
        cbk_ref[...] = jnp.zeros_like(cbk_ref)

    cw = cw_ref[...]
    cb = cb_ref[...]
    _lru_ctx_direction(False, n_slabs, step, xf_ref, pf_ref, nf_ref, cw, cb, wrf_ref, wif_ref,
                       bg_ref[0:1], bg_ref[1:2], lam_ref[0:1], cf_ref)
    _lru_ctx_direction(True, n_slabs, n_slabs - 1 - step, xb_ref, pb_ref, nb_ref, cw, cb, wrb_ref, wib_ref,
                       bg_ref[2:3], bg_ref[3:4], lam_ref[1:2], cbk_ref)
    fin_ref[0] = cf_ref[...]
    fin_ref[1] = cbk_ref[...]


def _lru_ctx_call(rx, n_lat_tok, conv_w, conv_b, w_dense, b_gate, lam):
    bsz, t_all, width = rx.shape
    r = GRID_W
    per = r // HALO
    n_slabs = (t_all - n_lat_tok) // r
    first = n_lat_tok // r
    main = lambda f: pl.BlockSpec((bsz, r, width), lambda s: (0, first + f(s), 0))
    prev = lambda f: pl.BlockSpec((bsz, HALO, width), lambda s: (0, (first + f(s)) * per - 1, 0))
    nxt = lambda f: pl.BlockSpec(
        (bsz, HALO, width), lambda s: (0, jnp.minimum((first + f(s) + 1) * per, t_all // HALO - 1), 0))
    fwd = lambda s: s
    bwd = lambda s: n_slabs - 1 - s
    full = lambda shape: pl.BlockSpec(shape, lambda s: (0,) * len(shape))
    carry = pltpu.VMEM((bsz, width), F32)
    return pl.pallas_call(
        functools.partial(_lru_ctx_kernel, n_slabs),
        grid=(n_slabs,),
        in_specs=[main(fwd), prev(fwd), nxt(fwd), main(bwd), prev(bwd), nxt(bwd),
                  full((CONV_W, width)), full((1, width))]
                 + [full((width, width))] * 4
                 + [full((4, width)), full((2, width))],
        out_specs=full((2, bsz, width)),
        out_shape=jax.ShapeDtypeStruct((2, bsz, width), F32),
        scratch_shapes=[carry, carry],
        compiler_params=pltpu.CompilerParams(dimension_semantics=("arbitrary",),
                                             vmem_limit_bytes=VMEM_LIMIT),
        name="lru_ctx",
    )(rx, rx, rx, rx, rx, rx, conv_w, conv_b, *w_dense, b_gate, lam)


def _lru_grid_kernel(x_ref, h0_ref, cw_ref, cb_ref, w_ref, bg_ref, lam_ref, out_ref,
                     af_ref, bf_ref, ab_ref, bb_ref):
    n = GRID_W
    rp = LRU_ROWS_PER_PASS
    lanes = x_ref.shape[-1]
    crow = lax.broadcasted_iota(jnp.int32, (n, lanes), 0)
    cw = cw_ref[...]
    cb = cb_ref[...]

    def grid_rows(lo, hi):
        parts = []
        for r in range(lo, min(hi, 0)):
            parts.append(jnp.where(crow == 0, 0.0, pltpu.roll(x_ref[0, (n + r) * n:(n + r + 1) * n, :], 1, 0)))
        if max(lo, 0) < min(hi, n):
            parts.append(x_ref[0, max(lo, 0) * n:min(hi, n) * n, :])
        for r in range(max(lo, n), hi):
            parts.append(jnp.where(crow == n - 1, 0.0,
                                   pltpu.roll(x_ref[0, (r - n) * n:(r - n + 1) * n, :], n - 1, 0)))
        return parts[0] if len(parts) == 1 else jnp.concatenate(parts, axis=0)

    for r0 in range(0, n, rp):
        sl = slice(r0 * n, (r0 + rp) * n)
        xc = (grid_rows(r0 - 2, r0 + rp - 2) * cw[0:1] + grid_rows(r0 - 1, r0 + rp - 1) * cw[1:2]
              + grid_rows(r0, r0 + rp) * cw[2:3] + grid_rows(r0 + 1, r0 + rp + 1) * cw[3:4] + cb)
        xb = xc.astype(BF16)
        for d, (a_ref, b_ref) in enumerate(((af_ref, bf_ref), (ab_ref, bb_ref))):
            gate_r = _sigmoid(jnp.dot(xb, w_ref[2 * d, 0], preferred_element_type=F32) + bg_ref[2 * d:2 * d + 1])
            gate_i = _sigmoid(jnp.dot(xb, w_ref[2 * d + 1, 0], preferred_element_type=F32)
                              + bg_ref[2 * d + 1:2 * d + 2])
            log_a = (-LRU_C * _softplus(-lam_ref[d:d + 1])) * gate_r
            a_ref[sl] = jnp.exp(log_a)
            b_ref[sl] = jnp.sqrt(1.0 - jnp.exp(2.0 * log_a)) * (gate_i * xc)

    hf = jnp.zeros((n, lanes), F32)
    hb = jnp.zeros((n, lanes), F32)
    pf = jnp.ones((n, lanes), F32)
    pb = jnp.ones((n, lanes), F32)
    for i in range(n):
        sf = slice(i * n, (i + 1) * n)
        sb = slice((n - 1 - i) * n, (n - i) * n)
        a = af_ref[sf]
        hf = a * hf + bf_ref[sf]
        pf = a * pf
        bf_ref[sf] = hf
        af_ref[sf] = pf
        a = ab_ref[sb]
        hb = a * hb + bb_ref[sb]
        pb = a * pb
        bb_ref[sb] = hb
        ab_ref[sb] = pb

    h0f = h0_ref[0, 0]
    h0b = h0_ref[1, 0]
    acc_a, acc_h = _lin_scan(pf, hf, False)
    in_f = jnp.where(crow == 0, h0f, pltpu.roll(acc_h + acc_a * h0f, 1, 0))
    acc_a, acc_h = _lin_scan(pb, hb, True)
    in_b = jnp.where(crow == n - 1, h0b, pltpu.roll(acc_h + acc_a * h0b, n - 1, 0))

    for r in range(n):
        sl = slice(r * n, (r + 1) * n)
        out_ref[0, sl, :] = (bf_ref[sl] + af_ref[sl] * in_f) + (bb_ref[sl] + ab_ref[sl] * in_b)


def _lru_grid_call(rx, n_lat_tok, h0, conv_w, conv_b, w_tiles, b_gate, lam):
    bsz, _, width = rx.shape
    nt = width // LRU_LANES
    lane = lambda rows: pl.BlockSpec((rows, LRU_LANES), lambda b, j: (0, j))
    scratch = pltpu.VMEM((n_lat_tok, LRU_LANES), F32)
    return pl.pallas_call(
        _lru_grid_kernel,
        grid=(bsz, nt),
        in_specs=[pl.BlockSpec((1, n_lat_tok, LRU_LANES), lambda b, j: (b, 0, j)),
                  pl.BlockSpec((2, 1, 1, LRU_LANES), lambda b, j: (0, b, 0, j)),
                  lane(CONV_W), lane(1),
                  pl.BlockSpec((4, 1, LRU_LANES, LRU_LANES), lambda b, j: (0, j, 0, 0)),
                  lane(4), lane(2)],
        out_specs=pl.BlockSpec((1, n_lat_tok, LRU_LANES), lambda b, j: (b, 0, j)),
        out_shape=jax.ShapeDtypeStruct((bsz, n_lat_tok, width), F32),
        scratch_shapes=[scratch] * 4,
        compiler_params=pltpu.CompilerParams(dimension_semantics=("arbitrary", "arbitrary"),
                                             vmem_limit_bytes=VMEM_LIMIT),
        name="lru_grid",
    )(rx, h0.reshape(2, bsz, 1, width), conv_w, conv_b, w_tiles, b_gate, lam)


def _out_ffn2_kernel(h1_ref, of_ref, ob_ref, z_ref, hs_ref, rg_ref, mod_ref, ng_ref, gnw_ref,
                     wout_ref, w1_ref, w3_ref, w2_ref, fg_ref, out_ref):
    mod = mod_ref[0]
    o = of_ref[0] + ob_ref[0]
    z = z_ref[0]
    parts = []
    for h in range(GDN_HEADS):
        sl = slice(h * GDN_DK, (h + 1) * GDN_DK)
        oh = o[:, sl]
        parts.append(oh * lax.rsqrt(jnp.mean(oh * oh, axis=-1, keepdims=True) + EPS)
                     * gnw_ref[...] * _silu(z[:, sl]))
    parts.append(hs_ref[0] * _gelu_tanh(rg_ref[0]))
    y = jnp.dot(jnp.concatenate(parts, axis=1).astype(BF16), wout_ref[...], preferred_element_type=F32)
    h2 = h1_ref[0] + mod[5:6] * y
    u = _rms_mod(h2, ng_ref[2:3], mod[6:7], mod[7:8])
    h3 = h2 + FFN_RESIDUAL * mod[8:9] * _swiglu(u, w1_ref, w3_ref, w2_ref)
    out_ref[0] = h3 * lax.rsqrt(jnp.mean(h3 * h3, axis=-1, keepdims=True) + EPS) * fg_ref[...]


def _out_ffn2_call(h1, o_f, o_b, z, h_sum, rg, mods, norm_g, gdn_norm_w, w_out, w1, w3, w2, final_g):
    bsz, seq, _ = o_f.shape
    n_lat = seq // TOKEN_TILE
    tok = lambda w: pl.BlockSpec((1, TOKEN_TILE, w), lambda b, j: (b, j, 0))
    return pl.pallas_call(
        _out_ffn2_kernel,
        grid=(bsz, n_lat),
        in_specs=[tok(D_MODEL), tok(GDN_WIDTH), tok(GDN_WIDTH), tok(GDN_WIDTH),
                  tok(LRU_WIDTH), tok(LRU_WIDTH),
                  pl.BlockSpec((1, N_MOD, D_MODEL), lambda b, j: (b + 1, 0, 0)),
                  _resident((3, D_MODEL)), _resident((1, GDN_DK)),
                  _resident((D_MODEL, D_MODEL)),
                  _resident((None, D_MODEL, D_FF), (1, 0, 0)), _resident((None, D_MODEL, D_FF), (1, 0, 0)),
                  _resident((None, D_FF, D_MODEL), (1, 0, 0)),
                  _resident((1, D_MODEL))],
        out_specs=tok(D_MODEL),
        out_shape=jax.ShapeDtypeStruct((bsz, seq, D_MODEL), F32),
        compiler_params=pltpu.CompilerParams(dimension_semantics=("arbitrary", "arbitrary"),
                                             vmem_limit_bytes=VMEM_LIMIT),
        name="out_ffn2",
    )(h1, o_f, o_b, z, h_sum, rg, mods, norm_g, gdn_norm_w, w_out, w1, w3, w2, final_g)


def _block_diag(w):
    nb, bw, _ = w.shape
    eye = jnp.eye(nb, dtype=w.dtype)
    return jnp.einsum('ncd,nm->ncmd', w, eye).reshape(nb * bw, nb * bw)


def kernel(x, c, ctx, c_ctx, w_ada, b_ada, norm_g, ffn_w1, ffn_w3, ffn_w2, w_in, w_out, gdn_conv_w, gdn_a_log,
           gdn_dt_bias, gdn_norm_w, lru_conv_w, lru_conv_b, lru_w_gate, lru_b_gate, lru_lambda, final_norm_g):
    bsz, seq, _ = x.shape
    assert w_ada.shape[0] == 1 and seq == GRID_W * GRID_W and ctx.shape[1] == TOKEN_TILE

    rows = -(-(bsz + 1) // 8) * 8
    cvec = jnp.concatenate([c_ctx[None, :], c, jnp.zeros((rows - bsz - 1, D_MODEL), F32)], axis=0)
    mods = _ada_call(cvec, w_ada[0], b_ada).reshape(rows, N_MOD, D_MODEL)

    w_in_bf = w_in[0].astype(BF16)
    o_ba = P_HEAD_COLS
    o_rx = o_ba + 4 * GDN_HEADS
    w_tail = jnp.concatenate([w_in_bf[:, o_rx:], w_in_bf[:, o_ba:o_rx],
                              jnp.zeros((D_MODEL, BA_PAD - 4 * GDN_HEADS), BF16)], axis=1)
    w1 = ffn_w1[0].astype(BF16)
    w3 = ffn_w3[0].astype(BF16)
    w2 = ffn_w2[0].astype(BF16)

    h1, qkv, z, rx, rg, ba = _ffn1_in_call(x, ctx, mods, norm_g[0], w1, w3, w2, w_in_bf, w_tail)

    pad_g = lambda t: jnp.zeros((1, BA_PAD), F32).at[0, 2 * GDN_HEADS:4 * GDN_HEADS].set(t.reshape(-1))
    local = _gdn_local_call(qkv, ba, gdn_conv_w[0], pad_g(gdn_a_log[0]), pad_g(gdn_dt_bias[0]), seq)
    o_f, o_b = _gdn_scan_call(local, seq)

    wg = lru_w_gate[0]
    w_dense = [_block_diag(wg[d, g]).astype(BF16) for d in range(2) for g in range(2)]
    b_gate = lru_b_gate[0].reshape(4, LRU_WIDTH)
    lam = lru_lambda[0]
    cw, cb = lru_conv_w[0], lru_conv_b
    h_ctx = _lru_ctx_call(rx, seq, cw, cb, w_dense, b_gate, lam)
    nt = LRU_WIDTH // LRU_LANES
    w_tiles = jnp.stack([jnp.stack([w[t * LRU_LANES:(t + 1) * LRU_LANES, t * LRU_LANES:(t + 1) * LRU_LANES]
                                    for t in range(nt)]) for w in w_dense])
    h_sum = _lru_grid_call(rx, seq, h_ctx, cw, cb, w_tiles, b_gate, lam)

    return _out_ffn2_call(h1, o_f, o_b, z, h_sum, rg, mods, norm_g[0], gdn_norm_w,
                          w_out[0].astype(BF16), w1, w3, w2, final_norm_g[None, :])
```

```python
import functools

import jax
import jax.numpy as jnp
from jax import lax
from jax.experimental import pallas as pl
from jax.experimental.pallas import tpu as pltpu

D_MODEL = 1024
D_FF = 2816
N_MOD = 9
EPS = 1e-6
FFN_RESIDUAL = 0.5

GDN_WIDTH = 512
GDN_HEADS = 4
GDN_DK = 128
GDN_CHUNK = 64
CONV_W = 4
LRU_WIDTH = 512
LRU_C = 8.0
GRID_W = 64

TOKEN_TILE = 256
GDN_TILE_CHUNKS = TOKEN_TILE // GDN_CHUNK
GDN_SCAN_CHUNKS = 2
OUT_SUB_TILES = 2
HALO = 8
BA_PAD = 128
P_HEAD_COLS = 4 * GDN_WIDTH
P_TAIL_COLS = 2 * LRU_WIDTH + BA_PAD
LRU_LANES = 128
LRU_ROWS_PER_PASS = 16
VMEM_LIMIT = 56 * 1024 * 1024

BF16 = jnp.bfloat16
F32 = jnp.float32
NEG_BIG = -1e30
NT_DIMS = (((1,), (1,)), ((), ()))


def _mm(a, b):
    return jnp.dot(a.astype(BF16), b.astype(BF16), preferred_element_type=F32)


def _sigmoid(x):
    return 1.0 / (1.0 + jnp.exp(-x))


def _silu(x):
    return x * _sigmoid(x)


def _softplus(x):
    return jnp.maximum(x, 0.0) + jnp.log(1.0 + jnp.exp(-jnp.abs(x)))


def _gelu_tanh(x):
    return 0.5 * x * (1.0 + jnp.tanh(0.7978845608028654 * (x + 0.044715 * x * x * x)))


def _rms_mod(h, g, shift, scale):
    y = h * lax.rsqrt(jnp.mean(h * h, axis=-1, keepdims=True) + EPS) * g
    return y * (1.0 + scale) + shift


def _swiglu(u, w1_ref, w3_ref, w2_ref):
    ub = u.astype(BF16)
    a = jnp.dot(ub, w1_ref[...], preferred_element_type=F32)
    b = jnp.dot(ub, w3_ref[...], preferred_element_type=F32)
    g = (_silu(a) * b).astype(BF16)
    return jnp.dot(g, w2_ref[...], preferred_element_type=F32)


def _dwconv(x, prev, nxt, w):
    t = x.shape[0]
    row = lax.broadcasted_iota(jnp.int32, x.shape, 0)
    xm1 = jnp.where(row == 0, prev[HALO - 1:HALO], pltpu.roll(x, 1, 0))
    xm2 = jnp.where(row == 0, prev[HALO - 2:HALO - 1],
                    jnp.where(row == 1, prev[HALO - 1:HALO], pltpu.roll(x, 2, 0)))
    xp1 = jnp.where(row == t - 1, nxt[0:1], pltpu.roll(x, t - 1, 0))
    return xm2 * w[0:1] + xm1 * w[1:2] + x * w[2:3] + xp1 * w[3:4]


def _resident(shape, index=None):
    index = index or (0,) * len(shape)
    return pl.BlockSpec(shape, lambda *_: index, pipeline_mode=pl.Buffered(1))


def _ada_kernel(c_ref, w_ref, b_ref, o_ref):
    o_ref[...] = _mm(_silu(c_ref[...]), w_ref[...]) + b_ref[...]


def _ada_call(cvec, w_ada, b_ada):
    rows = cvec.shape[0]
    return pl.pallas_call(
        _ada_kernel,
        grid=(N_MOD,),
        in_specs=[pl.BlockSpec((rows, D_MODEL), lambda k: (0, 0)),
                  pl.BlockSpec((D_MODEL, D_MODEL), lambda k: (0, k)),
                  pl.BlockSpec((1, D_MODEL), lambda k: (0, k))],
        out_specs=pl.BlockSpec((rows, D_MODEL), lambda k: (0, k)),
        out_shape=jax.ShapeDtypeStruct((rows, N_MOD * D_MODEL), F32),
        name="ada",
    )(cvec, w_ada, b_ada)


def _segment_edges(j, n_lat_tiles, n_all_tiles):
    return (jnp.logical_or(j == 0, j == n_lat_tiles),
            jnp.logical_or(j == n_lat_tiles - 1, j == n_all_tiles - 1))


def _ffn1_in_kernel(n_lat_tiles, x_ref, ctx_ref, mod_ref, ng_ref, w1_ref, w3_ref, w2_ref, win_ref, wtail_ref,
                    h1_ref, qkv_ref, z_ref, rx_ref, rg_ref, ba_ref):
    j = pl.program_id(1)
    h = jnp.where(j == n_lat_tiles, ctx_ref[0], x_ref[0])
    mod = mod_ref[0]
    u = _rms_mod(h, ng_ref[0:1], mod[0:1], mod[1:2])
    h1 = h + FFN_RESIDUAL * mod[2:3] * _swiglu(u, w1_ref, w3_ref, w2_ref)
    h1_ref[0] = h1
    ub = _rms_mod(h1, ng_ref[1:2], mod[3:4], mod[4:5]).astype(BF16)
    head = jnp.dot(ub, win_ref[...], preferred_element_type=F32)
    tail = jnp.dot(ub, wtail_ref[...], preferred_element_type=F32)
    qkv_ref[0] = head[:, :3 * GDN_WIDTH]
    z_ref[0] = head[:, 3 * GDN_WIDTH:]
    rx_ref[0] = tail[:, :LRU_WIDTH]
    rg_ref[0] = tail[:, LRU_WIDTH:2 * LRU_WIDTH]
    ba_ref[0] = tail[:, 2 * LRU_WIDTH:]


def _ffn1_in_call(x, ctx, mods, norm_g, w1, w3, w2, w_in_bf, w_tail):
    bsz, seq, _ = x.shape
    n_lat = seq // TOKEN_TILE
    n_all = n_lat + 1
    t_all = n_all * TOKEN_TILE
    tok = lambda w: pl.BlockSpec((1, TOKEN_TILE, w), lambda b, j: (b, j, 0))
    widths = (D_MODEL, 3 * GDN_WIDTH, GDN_WIDTH, LRU_WIDTH, LRU_WIDTH, BA_PAD)
    return pl.pallas_call(
        functools.partial(_ffn1_in_kernel, n_lat),
        grid=(bsz, n_all),
        in_specs=[pl.BlockSpec((1, TOKEN_TILE, D_MODEL), lambda b, j: (b, jnp.minimum(j, n_lat - 1), 0)),
                  pl.BlockSpec((1, TOKEN_TILE, D_MODEL), lambda b, j: (b, 0, 0)),
                  pl.BlockSpec((1, N_MOD, D_MODEL), lambda b, j: (jnp.where(j == n_lat, 0, b + 1), 0, 0)),
                  _resident((3, D_MODEL)),
                  _resident((None, D_MODEL, D_FF), (0, 0, 0)), _resident((None, D_MODEL, D_FF), (0, 0, 0)),
                  _resident((None, D_FF, D_MODEL), (0, 0, 0)),
                  _resident((D_MODEL, P_HEAD_COLS)), _resident((D_MODEL, P_TAIL_COLS))],
        out_specs=[tok(w) for w in widths],
        out_shape=[jax.ShapeDtypeStruct((bsz, t_all, w), F32) for w in widths],
        compiler_params=pltpu.CompilerParams(dimension_semantics=("arbitrary", "arbitrary"),
                                             vmem_limit_bytes=VMEM_LIMIT),
        name="ffn1_in",
    )(x, ctx, mods, norm_g, w1, w3, w2, w_in_bf, w_tail)


def _pair_block_diag(x, first):
    return jnp.concatenate([jnp.where(first, x, 0.0), jnp.where(first, 0.0, x)], axis=0)


def _run_interleaved(coroutines):
    active = [[gen, next(gen)] for gen in coroutines]
    while active:
        still = []
        for item in active:
            results = [dot() for dot in item[1]]
            try:
                item[1] = item[0].send(results)
                still.append(item)
            except StopIteration:
                pass
        active = still


def _gdn_local_body(seg_start, seg_end, x_ref, prev_ref, next_ref, ba_ref, cw_ref, alog_ref, dtb_ref,
                    out_refs, stage_in, stage_out):
    c = GDN_CHUNK
    pairs = GDN_HEADS // 2
    row = lax.broadcasted_iota(jnp.int32, (c, 2 * c), 0)
    lane2 = lax.broadcasted_iota(jnp.int32, (c, 2 * c), 1)
    col = lane2 & (c - 1)
    first = lane2 < c
    first_row = first[0:1]
    eye2 = (row == col).astype(F32)
    zeros_k = jnp.zeros((c, GDN_DK), BF16)
    unit_index = lambda ci, d, p: (ci * 2 + d) * pairs + p

    def coupling(k):
        return jnp.logical_and((row >> k) != (col >> k), (row >> (k + 1)) == (col >> (k + 1)))

    lhs_in, kr_in, rhs_in, dec_in, kdt_in, qe_in, eg_in = stage_in

    def pair_unit(ci, d, p):
        u_ref, wq_ref, ak_ref, eg_ref = out_refs[d]
        un = unit_index(ci, d, p)
        strict = (row < col) if d else (row > col)
        k_rows = kr_in[ci * pairs + p]
        k_bd = jnp.concatenate([jnp.concatenate([k_rows[:c], zeros_k], axis=1),
                                jnp.concatenate([zeros_k, k_rows[c:]], axis=1)], axis=0)
        lhs = lhs_in[un]
        (prods,) = yield [lambda: lax.dot_general(lhs, k_bd, NT_DIMS, preferred_element_type=F32)]
        decay = dec_in[un]
        a = jnp.where(strict, prods[:c] * decay, 0.0)
        ps = slice(p * 2 * c, (p + 1) * 2 * c)
        ak_ref[ci, 0, 0:c, ps] = (prods[c:] * decay).astype(BF16)
        ak_ref[ci, 0, c:3 * c, ps] = kdt_in[un]
        t = eye2 - jnp.where(coupling(0), a, 0.0)
        k = 1
        while 2 ** k < c:
            l = jnp.where(coupling(k), a, 0.0)
            t_bd = _pair_block_diag(t, first)
            (lt,) = yield [lambda: _mm(l, t_bd)]
            lt_bd = _pair_block_diag(lt, first)
            (tlt,) = yield [lambda: _mm(t, lt_bd)]
            t = t - tlt
            k += 1
        t_bd = _pair_block_diag(t, first).astype(BF16)
        rhs = rhs_in[un]
        (sol,) = yield [lambda: jnp.dot(t_bd, rhs, preferred_element_type=F32)]
        for i in range(2):
            h = 2 * p + i
            hs = slice(h * GDN_DK, (h + 1) * GDN_DK)
            u_ref[ci, 0, :, hs] = sol[i * c:(i + 1) * c, :GDN_DK]
            wq_ref[ci, 0, 0:c, hs] = sol[i * c:(i + 1) * c, GDN_DK:].astype(BF16)
            wq_ref[ci, 0, c:2 * c, hs] = qe_in[un, i * c:(i + 1) * c]
            eg_ref[ci, 0, h:h + 1, :] = eg_in[un, i:i + 1]

    lhs_out, kr_out, rhs_out, dec_out, kdt_out, qe_out, eg_out = stage_out
    y = _silu(_dwconv(x_ref[0], jnp.where(seg_start, 0.0, prev_ref[0]), jnp.where(seg_end, 0.0, next_ref[0]),
                      cw_ref[...]))
    ba = ba_ref[0]
    lane = lax.broadcasted_iota(jnp.int32, (1, BA_PAD), 1)
    is_g = jnp.logical_and(lane >= 2 * GDN_HEADS, lane < 4 * GDN_HEADS)
    beta_all = _sigmoid(ba)
    g_all = jnp.where(is_g, -jnp.exp(alog_ref[...]) * _softplus(ba + dtb_ref[...]), 0.0)

    def chunk(ci):
        rs = slice(ci * c, (ci + 1) * c)
        heads = []
        for h in range(GDN_HEADS):
            lo = h * GDN_DK
            q = y[rs, lo:lo + GDN_DK]
            k = y[rs, GDN_WIDTH + lo:GDN_WIDTH + lo + GDN_DK]
            v = y[rs, 2 * GDN_WIDTH + lo:2 * GDN_WIDTH + lo + GDN_DK]
            q = q * (lax.rsqrt(jnp.sum(q * q, axis=-1, keepdims=True) + EPS) * (GDN_DK ** -0.5))
            k = k * lax.rsqrt(jnp.sum(k * k, axis=-1, keepdims=True) + EPS)
            heads.append((q, k, v))
        g = g_all[rs]
        k_rows = [jnp.concatenate([heads[2 * p][1], heads[2 * p + 1][1]], axis=0) for p in range(pairs)]
        k_ts = [kr.T for kr in k_rows]
        grow = lax.broadcasted_iota(jnp.int32, g.shape, 0)
        gcum_f = g
        step = 1
        while step < c:
            gcum_f = gcum_f + jnp.where(grow >= step, pltpu.roll(gcum_f, step, 0), 0.0)
            step *= 2
        gcum_b = gcum_f[c - 1:c] - gcum_f + g
        gcum_t = jnp.concatenate([gcum_f, gcum_b], axis=0).T
        beta_c = beta_all[rs]
        for p in range(pairs):
            kr_out[ci * pairs + p] = k_rows[p].astype(BF16)
        for d in range(2):
            gcum = gcum_b if d else gcum_f
            incl = (row <= col) if d else (row >= col)
            last = 0 if d else c - 1
            for p in range(pairs):
                un = unit_index(ci, d, p)
                (q0, k0, v0), (q1, k1, v1) = heads[2 * p], heads[2 * p + 1]
                cb = d * GDN_HEADS + 2 * p
                cg = 2 * GDN_HEADS + cb
                beta0, beta1 = beta_c[:, cb:cb + 1], beta_c[:, cb + 1:cb + 2]
                gc0, gc1 = gcum[:, cg:cg + 1], gcum[:, cg + 1:cg + 2]
                gt0, gt1 = gcum_t[cg:cg + 1], gcum_t[cg + 1:cg + 2]
                gr = (jnp.where(first_row, pltpu.roll(gt0, c, 1), gt1) if d
                      else jnp.where(first_row, gt0, pltpu.roll(gt1, c, 1)))
                gc = jnp.where(first, gc0, gc1)
                e0, e1 = jnp.exp(gc0), jnp.exp(gc1)
                kb0, kb1 = k0 * beta0, k1 * beta1
                dec_out[un] = jnp.exp(jnp.where(incl, gc - gr, NEG_BIG))
                lhs_out[un] = jnp.concatenate([jnp.concatenate([kb0, kb1], axis=1),
                                               jnp.concatenate([q0, q1], axis=1)], axis=0).astype(BF16)
                rhs_out[un] = jnp.concatenate([jnp.concatenate([v0 * beta0, kb0 * e0], axis=1),
                                               jnp.concatenate([v1 * beta1, kb1 * e1], axis=1)],
                                              axis=0).astype(BF16)
                kdt_out[un] = (k_ts[p] * jnp.exp(gc[last:last + 1] - gr)).astype(BF16)
                qe_out[un] = jnp.concatenate([q0 * e0, q1 * e1], axis=0).astype(BF16)
                eg_out[un] = jnp.concatenate(
                    [jnp.broadcast_to(jnp.exp(gc0[last:last + 1]), (1, BA_PAD)),
                     jnp.broadcast_to(jnp.exp(gc1[last:last + 1]), (1, BA_PAD)),
                     jnp.zeros((HALO - 2, BA_PAD), F32)], axis=0)

    for refs in out_refs:
        refs[3][...] = jnp.zeros_like(refs[3])
    _run_interleaved([pair_unit(ci, d, p) for ci in range(GDN_TILE_CHUNKS) for d in range(2)
                      for p in range(pairs)])
    for ci in range(GDN_TILE_CHUNKS):
        chunk(ci)


def _gdn_local_kernel(n_lat_tiles, n_all_tiles, n_tiles, x_ref, prev_ref, next_ref, ba_ref, cw_ref, alog_ref,
                      dtb_ref, uf_ref, wqf_ref, akf_ref, egf_ref, ub_ref, wqb_ref, akb_ref, egb_ref, *stage_refs):
    s = pl.program_id(0)
    seg_start, seg_end = _segment_edges(jnp.minimum(s, n_tiles - 1) % n_all_tiles, n_lat_tiles, n_all_tiles)
    out_refs = ((uf_ref, wqf_ref, akf_ref, egf_ref), (ub_ref, wqb_ref, akb_ref, egb_ref))
    half = len(stage_refs) // 2
    sets = (stage_refs[:half], stage_refs[half:])

    @pl.when(s == 0)
    def _():
        for r in sets[1]:
            r[...] = jnp.zeros_like(r)

    for parity in range(2):
        @pl.when(s % 2 == parity)
        def _():
            _gdn_local_body(seg_start, seg_end, x_ref, prev_ref, next_ref, ba_ref, cw_ref, alog_ref, dtb_ref,
                            out_refs, sets[1 - parity], sets[parity])


def _gdn_local_call(qkv, ba, conv_w, alog_vec, dtb_vec, n_lat_tok):
    bsz, t_all, width = qkv.shape
    c = GDN_CHUNK
    tile = GDN_TILE_CHUNKS * c
    n_all = t_all // tile
    n_lat = n_lat_tok // tile
    per_tile = tile // HALO
    last_halo = t_all // HALO - 1
    n_chunks = t_all // c
    n_tiles = bsz * n_all
    pairs = GDN_HEADS // 2
    units = GDN_TILE_CHUNKS * 2 * pairs
    t_in = lambda s: jnp.minimum(s, n_tiles - 1)
    t_out = lambda s: jnp.maximum(s - 1, 0)
    in_tok = lambda w: pl.BlockSpec((1, tile, w), lambda s: (t_in(s) // n_all, t_in(s) % n_all, 0))
    cblk = lambda rows, w: pl.BlockSpec((GDN_TILE_CHUNKS, 1, rows, w),
                                        lambda s: (t_out(s) % n_all, t_out(s) // n_all, 0, 0))
    vec = pl.BlockSpec((1, BA_PAD), lambda s: (0, 0))
    outs = [(jax.ShapeDtypeStruct((n_chunks, bsz, c, GDN_WIDTH), F32), cblk(c, GDN_WIDTH)),
            (jax.ShapeDtypeStruct((n_chunks, bsz, 2 * c, GDN_WIDTH), BF16), cblk(2 * c, GDN_WIDTH)),
            (jax.ShapeDtypeStruct((n_chunks, bsz, 3 * c, GDN_HEADS * c), BF16), cblk(3 * c, GDN_HEADS * c)),
            (jax.ShapeDtypeStruct((n_chunks, bsz, HALO, BA_PAD), F32), cblk(HALO, BA_PAD))] * 2
    stage = [pltpu.VMEM((units, 2 * c, 2 * GDN_DK), BF16),
             pltpu.VMEM((GDN_TILE_CHUNKS * pairs, 2 * c, GDN_DK), BF16),
             pltpu.VMEM((units, 2 * c, 2 * GDN_DK), BF16),
             pltpu.VMEM((units, c, 2 * c), F32),
             pltpu.VMEM((units, GDN_DK, 2 * c), BF16),
             pltpu.VMEM((units, 2 * c, GDN_DK), BF16),
             pltpu.VMEM((units, HALO, BA_PAD), F32)]
    return pl.pallas_call(
        functools.partial(_gdn_local_kernel, n_lat, n_all, n_tiles),
        grid=(n_tiles + 1,),
        in_specs=[in_tok(width),
                  pl.BlockSpec((1, HALO, width),
                               lambda s: (t_in(s) // n_all, jnp.maximum((t_in(s) % n_all) * per_tile - 1, 0), 0)),
                  pl.BlockSpec((1, HALO, width),
                               lambda s: (t_in(s) // n_all,
                                          jnp.minimum((t_in(s) % n_all + 1) * per_tile, last_halo), 0)),
                  in_tok(BA_PAD), pl.BlockSpec((CONV_W, width), lambda s: (0, 0)), vec, vec],
        out_specs=[o[1] for o in outs],
        out_shape=[o[0] for o in outs],
        scratch_shapes=stage + stage,
        compiler_params=pltpu.CompilerParams(dimension_semantics=("arbitrary",),
                                             vmem_limit_bytes=VMEM_LIMIT),
        name="gdn_local",
    )(qkv, qkv, qkv, ba, conv_w, alog_vec, dtb_vec)


def _gdn_scan_kernel(uf_ref, wqf_ref, akf_ref, egf_ref, ub_ref, wqb_ref, akb_ref, egb_ref,
                     of_ref, ob_ref, sf_ref, sb_ref):
    @pl.when(pl.program_id(0) == 0)
    def _():
        sf_ref[...] = jnp.zeros_like(sf_ref)
        sb_ref[...] = jnp.zeros_like(sb_ref)

    c = GDN_CHUNK
    bsz = uf_ref.shape[1]
    hs = lambda h: slice(h * GDN_DK, (h + 1) * GDN_DK)
    for sub in range(GDN_SCAN_CHUNKS):
        chains = [(refs, ci, b, h)
                  for refs, ci in (((uf_ref, wqf_ref, akf_ref, egf_ref, of_ref, sf_ref), sub),
                                   ((ub_ref, wqb_ref, akb_ref, egb_ref, ob_ref, sb_ref),
                                    GDN_SCAN_CHUNKS - 1 - sub))
                  for b in range(bsz) for h in range(GDN_HEADS)]
        states = [refs[5][b * GDN_HEADS + h] for refs, ci, b, h in chains]
        with_s = [jnp.dot(refs[1][ci, b, :, hs(h)], s.astype(BF16), preferred_element_type=F32)
                  for (refs, ci, b, h), s in zip(chains, states)]
        v_new = [refs[0][ci, b, :, hs(h)] - r[:c] for (refs, ci, b, h), r in zip(chains, with_s)]
        with_v = [jnp.dot(refs[2][ci, b, :, h * c:(h + 1) * c], v.astype(BF16), preferred_element_type=F32)
                  for (refs, ci, b, h), v in zip(chains, v_new)]
        for (refs, ci, b, h), s, r, a in zip(chains, states, with_s, with_v):
            refs[4][b, ci * c:(ci + 1) * c, hs(h)] = r[c:] + a[:c]
            refs[5][b * GDN_HEADS + h] = s * refs[3][ci, b, h:h + 1, :] + a[c:]


def _gdn_scan_call(local, n_lat_tok):
    n_chunks, bsz = local[0].shape[:2]
    c = GDN_CHUNK
    g = GDN_SCAN_CHUNKS
    n_all = n_chunks // g
    n_lat = n_lat_tok // (c * g)
    n_ctx = n_all - n_lat
    fwd = lambda s: jnp.where(s < n_ctx, n_lat + s, s - n_ctx)
    bwd = lambda s: n_all - 1 - s
    fwd_out = lambda s: jnp.maximum(s - n_ctx, 0)
    bwd_out = lambda s: jnp.minimum(n_all - 1 - s, n_lat - 1)
    cblk = lambda rows, w, f: pl.BlockSpec((g, bsz, rows, w), lambda s: (f(s), 0, 0, 0))
    ins = lambda f: [cblk(c, GDN_WIDTH, f), cblk(2 * c, GDN_WIDTH, f), cblk(3 * c, GDN_HEADS * c, f),
                     cblk(HALO, BA_PAD, f)]
    tok = lambda f: pl.BlockSpec((bsz, g * c, GDN_WIDTH), lambda s: (0, f(s), 0))
    out = jax.ShapeDtypeStruct((bsz, n_lat_tok, GDN_WIDTH), F32)
    state = pltpu.VMEM((bsz * GDN_HEADS, GDN_DK, GDN_DK), F32)
    return pl.pallas_call(
        _gdn_scan_kernel,
        grid=(n_all,),
        in_specs=ins(fwd) + ins(bwd),
        out_specs=[tok(fwd_out), tok(bwd_out)],
        out_shape=[out, out],
        scratch_shapes=[state, state],
        compiler_params=pltpu.CompilerParams(dimension_semantics=("arbitrary",),
                                             vmem_limit_bytes=VMEM_LIMIT),
        name="gdn_scan",
    )(*local)


def _lin_scan(a, b, reverse):
    r = a.shape[0]
    row = lax.broadcasted_iota(jnp.int32, a.shape, 0)
    s = 1
    while s < r:
        ok = (row < r - s) if reverse else (row >= s)
        shift = r - s if reverse else s
        a_sh = jnp.where(ok, pltpu.roll(a, shift, 0), 1.0)
        b_sh = jnp.where(ok, pltpu.roll(b, shift, 0), 0.0)
        b = a * b_sh + b
        a = a * a_sh
        s *= 2
    return a, b


def _lru_ctx_direction(reverse, n_slabs, slab, x_ref, prev_ref, next_ref, cw, cb, wr_ref, wi_ref,
                       br, bi, lam, carry_ref):
    bsz, r, width = x_ref.shape
    step = pl.program_id(0)
    has_prev = slab > 0
    has_next = slab < n_slabs - 1
    xs = []
    for b in range(bsz):
        prev = jnp.where(has_prev, prev_ref[b], 0.0)
        nxt = jnp.where(has_next, next_ref[b], 0.0)
        xs.append(_dwconv(x_ref[b], prev, nxt, cw) + cb)
    x = jnp.concatenate(xs, axis=0)
    xb = x.astype(BF16)
    gate_r = _sigmoid(jnp.dot(xb, wr_ref[...], preferred_element_type=F32) + br)
    gate_i = _sigmoid(jnp.dot(xb, wi_ref[...], preferred_element_type=F32) + bi)
    log_a = (-LRU_C * _softplus(-lam)) * gate_r
    a = jnp.exp(log_a)
    mult = jnp.sqrt(1.0 - jnp.exp(2.0 * log_a))
    row = lax.broadcasted_iota(jnp.int32, (r, width), 0)
    is_first = jnp.logical_and(step == 0, row == ((r - 1) if reverse else 0))
    edge = 0 if reverse else r - 1
    for b in range(bsz):
        sl = slice(b * r, (b + 1) * r)
        m = jnp.where(is_first, 1.0, mult[sl])
        a_cum, h = _lin_scan(a[sl], m * (gate_i[sl] * x[sl]), reverse)
        h = h + a_cum * carry_ref[b:b + 1]
        carry_ref[b:b + 1] = h[edge:edge + 1]


def _lru_ctx_kernel(n_slabs, xf_ref, pf_ref, nf_ref, xb_ref, pb_ref, nb_ref, cw_ref, cb_ref,
                    wrf_ref, wif_ref, wrb_ref, wib_ref, bg_ref, lam_ref, fin_ref, cf_ref, cbk_ref):
    step = pl.program_id(0)

    @pl.when(step == 0)
    def _():
        cf_ref[...] = jnp.zeros_like(cf_ref)
        cbk_ref[...] = jnp.zeros_like(cbk_ref)

    cw = cw_ref[...]
    cb = cb_ref[...]
    _lru_ctx_direction(False, n_slabs, step, xf_ref, pf_ref, nf_ref, cw, cb, wrf_ref, wif_ref,
                       bg_ref[0:1], bg_ref[1:2], lam_ref[0:1], cf_ref)
    _lru_ctx_direction(True, n_slabs, n_slabs - 1 - step, xb_ref, pb_ref, nb_ref, cw, cb, wrb_ref, wib_ref,
                       bg_ref[2:3], bg_ref[3:4], lam_ref[1:2], cbk_ref)
    fin_ref[0] = cf_ref[...]
    fin_ref[1] = cbk_ref[...]


def _lru_ctx_call(rx, n_lat_tok, conv_w, conv_b, w_dense, b_gate, lam):
    bsz, t_all, width = rx.shape
    r = GRID_W
    per = r // HALO
    n_slabs = (t_all - n_lat_tok) // r
    first = n_lat_tok // r
    main = lambda f: pl.BlockSpec((bsz, r, width), lambda s: (0, first + f(s), 0))
    prev = lambda f: pl.BlockSpec((bsz, HALO, width), lambda s: (0, (first + f(s)) * per - 1, 0))
    nxt = lambda f: pl.BlockSpec(
        (bsz, HALO, width), lambda s: (0, jnp.minimum((first + f(s) + 1) * per, t_all // HALO - 1), 0))
    fwd = lambda s: s
    bwd = lambda s: n_slabs - 1 - s
    full = lambda shape: pl.BlockSpec(shape, lambda s: (0,) * len(shape))
    carry = pltpu.VMEM((bsz, width), F32)
    return pl.pallas_call(
        functools.partial(_lru_ctx_kernel, n_slabs),
        grid=(n_slabs,),
        in_specs=[main(fwd), prev(fwd), nxt(fwd), main(bwd), prev(bwd), nxt(bwd),
                  full((CONV_W, width)), full((1, width))]
                 + [full((width, width))] * 4
                 + [full((4, width)), full((2, width))],
        out_specs=full((2, bsz, width)),
        out_shape=jax.ShapeDtypeStruct((2, bsz, width), F32),
        scratch_shapes=[carry, carry],
        compiler_params=pltpu.CompilerParams(dimension_semantics=("arbitrary",),
                                             vmem_limit_bytes=VMEM_LIMIT),
        name="lru_ctx",
    )(rx, rx, rx, rx, rx, rx, conv_w, conv_b, *w_dense, b_gate, lam)


def _lru_grid_kernel(x_ref, h0_ref, cw_ref, cb_ref, w_ref, bg_ref, lam_ref, out_ref,
                     af_ref, bf_ref, ab_ref, bb_ref):
    n = GRID_W
    rp = LRU_ROWS_PER_PASS
    lanes = x_ref.shape[-1]
    crow = lax.broadcasted_iota(jnp.int32, (n, lanes), 0)
    cw = cw_ref[...]
    cb = cb_ref[...]

    def grid_rows(lo, hi):
        parts = []
        for r in range(lo, min(hi, 0)):
            parts.append(jnp.where(crow == 0, 0.0, pltpu.roll(x_ref[0, (n + r) * n:(n + r + 1) * n, :], 1, 0)))
        if max(lo, 0) < min(hi, n):
            parts.append(x_ref[0, max(lo, 0) * n:min(hi, n) * n, :])
        for r in range(max(lo, n), hi):
            parts.append(jnp.where(crow == n - 1, 0.0,
                                   pltpu.roll(x_ref[0, (r - n) * n:(r - n + 1) * n, :], n - 1, 0)))
        return parts[0] if len(parts) == 1 else jnp.concatenate(parts, axis=0)

    for r0 in range(0, n, rp):
        sl = slice(r0 * n, (r0 + rp) * n)
        xc = (grid_rows(r0 - 2, r0 + rp - 2) * cw[0:1] + grid_rows(r0 - 1, r0 + rp - 1) * cw[1:2]
              + grid_rows(r0, r0 + rp) * cw[2:3] + grid_rows(r0 + 1, r0 + rp + 1) * cw[3:4] + cb)
        xb = xc.astype(BF16)
        for d, (a_ref, b_ref) in enumerate(((af_ref, bf_ref), (ab_ref, bb_ref))):
            gate_r = _sigmoid(jnp.dot(xb, w_ref[2 * d, 0], preferred_element_type=F32) + bg_ref[2 * d:2 * d + 1])
            gate_i = _sigmoid(jnp.dot(xb, w_ref[2 * d + 1, 0], preferred_element_type=F32)
                              + bg_ref[2 * d + 1:2 * d + 2])
            log_a = (-LRU_C * _softplus(-lam_ref[d:d + 1])) * gate_r
            a_ref[sl] = jnp.exp(log_a)
            b_ref[sl] = jnp.sqrt(1.0 - jnp.exp(2.0 * log_a)) * (gate_i * xc)

    hf = jnp.zeros((n, lanes), F32)
    hb = jnp.zeros((n, lanes), F32)
    pf = jnp.ones((n, lanes), F32)
    pb = jnp.ones((n, lanes), F32)
    for i in range(n):
        sf = slice(i * n, (i + 1) * n)
        sb = slice((n - 1 - i) * n, (n - i) * n)
        a = af_ref[sf]
        hf = a * hf + bf_ref[sf]
        pf = a * pf
        bf_ref[sf] = hf
        af_ref[sf] = pf
        a = ab_ref[sb]
        hb = a * hb + bb_ref[sb]
        pb = a * pb
        bb_ref[sb] = hb
        ab_ref[sb] = pb

    h0f = h0_ref[0, 0]
    h0b = h0_ref[1, 0]
    acc_a, acc_h = _lin_scan(pf, hf, False)
    in_f = jnp.where(crow == 0, h0f, pltpu.roll(acc_h + acc_a * h0f, 1, 0))
    acc_a, acc_h = _lin_scan(pb, hb, True)
    in_b = jnp.where(crow == n - 1, h0b, pltpu.roll(acc_h + acc_a * h0b, n - 1, 0))

    for r in range(n):
        sl = slice(r * n, (r + 1) * n)
        out_ref[0, sl, :] = (bf_ref[sl] + af_ref[sl] * in_f) + (bb_ref[sl] + ab_ref[sl] * in_b)


def _lru_grid_call(rx, n_lat_tok, h0, conv_w, conv_b, w_tiles, b_gate, lam):
    bsz, _, width = rx.shape
    nt = width // LRU_LANES
    lane = lambda rows: pl.BlockSpec((rows, LRU_LANES), lambda b, j: (0, j))
    scratch = pltpu.VMEM((n_lat_tok, LRU_LANES), F32)
    return pl.pallas_call(
        _lru_grid_kernel,
        grid=(bsz, nt),
        in_specs=[pl.BlockSpec((1, n_lat_tok, LRU_LANES), lambda b, j: (b, 0, j)),
                  pl.BlockSpec((2, 1, 1, LRU_LANES), lambda b, j: (0, b, 0, j)),
                  lane(CONV_W), lane(1),
                  pl.BlockSpec((4, 1, LRU_LANES, LRU_LANES), lambda b, j: (0, j, 0, 0)),
                  lane(4), lane(2)],
        out_specs=pl.BlockSpec((1, n_lat_tok, LRU_LANES), lambda b, j: (b, 0, j)),
        out_shape=jax.ShapeDtypeStruct((bsz, n_lat_tok, width), F32),
        scratch_shapes=[scratch] * 4,
        compiler_params=pltpu.CompilerParams(dimension_semantics=("arbitrary", "arbitrary"),
                                             vmem_limit_bytes=VMEM_LIMIT),
        name="lru_grid",
    )(rx, h0.reshape(2, bsz, 1, width), conv_w, conv_b, w_tiles, b_gate, lam)


def _out_ffn2_kernel(h1_ref, of_ref, ob_ref, z_ref, hs_ref, rg_ref, mod_ref, ng_ref, gnw_ref,
                     wout_ref, w1_ref, w3_ref, w2_ref, fg_ref, out_ref):
    mod = mod_ref[0]

    def sub_tile(rows):
        o = of_ref[0, rows] + ob_ref[0, rows]
        z = z_ref[0, rows]
        parts = []
        for h in range(GDN_HEADS):
            sl = slice(h * GDN_DK, (h + 1) * GDN_DK)
            oh = o[:, sl]
            parts.append(oh * lax.rsqrt(jnp.mean(oh * oh, axis=-1, keepdims=True) + EPS)
                         * gnw_ref[...] * _silu(z[:, sl]))
        parts.append(hs_ref[0, rows] * _gelu_tanh(rg_ref[0, rows]))
        mixed = jnp.concatenate(parts, axis=1).astype(BF16)
        (y,) = yield [lambda: jnp.dot(mixed, wout_ref[...], preferred_element_type=F32)]
        h2 = h1_ref[0, rows] + mod[5:6] * y
        ub = _rms_mod(h2, ng_ref[2:3], mod[6:7], mod[7:8]).astype(BF16)
        a, b = yield [lambda: jnp.dot(ub, w1_ref[...], preferred_element_type=F32),
                      lambda: jnp.dot(ub, w3_ref[...], preferred_element_type=F32)]
        g = (_silu(a) * b).astype(BF16)
        (f,) = yield [lambda: jnp.dot(g, w2_ref[...], preferred_element_type=F32)]
        h3 = h2 + FFN_RESIDUAL * mod[8:9] * f
        out_ref[0, rows] = h3 * lax.rsqrt(jnp.mean(h3 * h3, axis=-1, keepdims=True) + EPS) * fg_ref[...]

    _run_interleaved([sub_tile(slice(i * TOKEN_TILE, (i + 1) * TOKEN_TILE))
                      for i in range(out_ref.shape[1] // TOKEN_TILE)])


def _out_ffn2_call(h1, o_f, o_b, z, h_sum, rg, mods, norm_g, gdn_norm_w, w_out, w1, w3, w2, final_g):
    bsz, seq, _ = o_f.shape
    rows = OUT_SUB_TILES * TOKEN_TILE
    tok = lambda w: pl.BlockSpec((1, rows, w), lambda b, j: (b, j, 0))
    return pl.pallas_call(
        _out_ffn2_kernel,
        grid=(bsz, seq // rows),
        in_specs=[tok(D_MODEL), tok(GDN_WIDTH), tok(GDN_WIDTH), tok(GDN_WIDTH),
                  tok(LRU_WIDTH), tok(LRU_WIDTH),
                  pl.BlockSpec((1, N_MOD, D_MODEL), lambda b, j: (b + 1, 0, 0)),
                  _resident((3, D_MODEL)), _resident((1, GDN_DK)),
                  _resident((D_MODEL, D_MODEL)),
                  _resident((None, D_MODEL, D_FF), (1, 0, 0)), _resident((None, D_MODEL, D_FF), (1, 0, 0)),
                  _resident((None, D_FF, D_MODEL), (1, 0, 0)),
                  _resident((1, D_MODEL))],
        out_specs=tok(D_MODEL),
        out_shape=jax.ShapeDtypeStruct((bsz, seq, D_MODEL), F32),
        compiler_params=pltpu.CompilerParams(dimension_semantics=("arbitrary", "arbitrary"),
                                             vmem_limit_bytes=VMEM_LIMIT),
        name="out_ffn2",
    )(h1, o_f, o_b, z, h_sum, rg, mods, norm_g, gdn_norm_w, w_out, w1, w3, w2, final_g)


def _block_diag(w):
    nb, bw, _ = w.shape
    eye = jnp.eye(nb, dtype=w.dtype)
    return jnp.einsum('ncd,nm->ncmd', w, eye).reshape(nb * bw, nb * bw)


def kernel(x, c, ctx, c_ctx, w_ada, b_ada, norm_g, ffn_w1, ffn_w3, ffn_w2, w_in, w_out, gdn_conv_w, gdn_a_log,
           gdn_dt_bias, gdn_norm_w, lru_conv_w, lru_conv_b, lru_w_gate, lru_b_gate, lru_lambda, final_norm_g):
    bsz, seq, _ = x.shape
    assert w_ada.shape[0] == 1 and seq == GRID_W * GRID_W and ctx.shape[1] == TOKEN_TILE

    rows = -(-(bsz + 1) // 8) * 8
    cvec = jnp.concatenate([c_ctx[None, :], c, jnp.zeros((rows - bsz - 1, D_MODEL), F32)], axis=0)
    mods = _ada_call(cvec, w_ada[0], b_ada).reshape(rows, N_MOD, D_MODEL)

    w_in_bf = w_in[0].astype(BF16)
    o_ba = P_HEAD_COLS
    o_rx = o_ba + 4 * GDN_HEADS
    w_tail = jnp.concatenate([w_in_bf[:, o_rx:], w_in_bf[:, o_ba:o_rx],
                              jnp.zeros((D_MODEL, BA_PAD - 4 * GDN_HEADS), BF16)], axis=1)
    w1 = ffn_w1[0].astype(BF16)
    w3 = ffn_w3[0].astype(BF16)
    w2 = ffn_w2[0].astype(BF16)

    h1, qkv, z, rx, rg, ba = _ffn1_in_call(x, ctx, mods, norm_g[0], w1, w3, w2, w_in_bf, w_tail)

    pad_g = lambda t: jnp.zeros((1, BA_PAD), F32).at[0, 2 * GDN_HEADS:4 * GDN_HEADS].set(t.reshape(-1))
    local = _gdn_local_call(qkv, ba, gdn_conv_w[0], pad_g(gdn_a_log[0]), pad_g(gdn_dt_bias[0]), seq)
    o_f, o_b = _gdn_scan_call(local, seq)

    wg = lru_w_gate[0]
    w_dense = [_block_diag(wg[d, g]).astype(BF16) for d in range(2) for g in range(2)]
    b_gate = lru_b_gate[0].reshape(4, LRU_WIDTH)
    lam = lru_lambda[0]
    cw, cb = lru_conv_w[0], lru_conv_b
    h_ctx = _lru_ctx_call(rx, seq, cw, cb, w_dense, b_gate, lam)
    nt = LRU_WIDTH // LRU_LANES
    w_tiles = jnp.stack([jnp.stack([w[t * LRU_LANES:(t + 1) * LRU_LANES, t * LRU_LANES:(t + 1) * LRU_LANES]
                                    for t in range(nt)]) for w in w_dense])
    h_sum = _lru_grid_call(rx, seq, h_ctx, cw, cb, w_tiles, b_gate, lam)

    return _out_ffn2_call(h1, o_f, o_b, z, h_sum, rg, mods, norm_g[0], gdn_norm_w,
                          w_out[0].astype(BF16), w1, w3, w2, final_norm_g[None, :])
```

```python
import functools

import jax
import jax.numpy as jnp
from jax import lax
from jax.experimental import pallas as pl
from jax.experimental.pallas import tpu as pltpu

D_MODEL = 1024
D_FF = 2816
N_MOD = 9
EPS = 1e-6
FFN_RESIDUAL = 0.5

GDN_WIDTH = 512
GDN_HEADS = 4
GDN_DK = 128
GDN_CHUNK = 64
CONV_W = 4
LRU_WIDTH = 512
LRU_C = 8.0
GRID_W = 64

TOKEN_TILE = 256
GDN_TILE_CHUNKS = TOKEN_TILE // GDN_CHUNK
GDN_SCAN_CHUNKS = 2
OUT_SUB_TILES = 2
FFN1_SUB_TILES = 2
HALO = 8
BA_PAD = 128
P_HEAD_COLS = 4 * GDN_WIDTH
P_TAIL_COLS = 2 * LRU_WIDTH + BA_PAD
LRU_LANES = 128
LRU_ROWS_PER_PASS = 16
VMEM_LIMIT = 56 * 1024 * 1024

BF16 = jnp.bfloat16
F32 = jnp.float32
NEG_BIG = -1e30
NT_DIMS = (((1,), (1,)), ((), ()))


def _mm(a, b):
    return jnp.dot(a.astype(BF16), b.astype(BF16), preferred_element_type=F32)


def _sigmoid(x):
    return 1.0 / (1.0 + jnp.exp(-x))


def _silu(x):
    return x * _sigmoid(x)


def _softplus(x):
    return jnp.maximum(x, 0.0) + jnp.log(1.0 + jnp.exp(-jnp.abs(x)))


def _gelu_tanh(x):
    return 0.5 * x * (1.0 + jnp.tanh(0.7978845608028654 * (x + 0.044715 * x * x * x)))


def _rms_mod(h, g, shift, scale):
    y = h * lax.rsqrt(jnp.mean(h * h, axis=-1, keepdims=True) + EPS) * g
    return y * (1.0 + scale) + shift


def _swiglu(u, w1_ref, w3_ref, w2_ref):
    ub = u.astype(BF16)
    a = jnp.dot(ub, w1_ref[...], preferred_element_type=F32)
    b = jnp.dot(ub, w3_ref[...], preferred_element_type=F32)
    g = (_silu(a) * b).astype(BF16)
    return jnp.dot(g, w2_ref[...], preferred_element_type=F32)


def _dwconv(x, prev, nxt, w):
    t = x.shape[0]
    row = lax.broadcasted_iota(jnp.int32, x.shape, 0)
    xm1 = jnp.where(row == 0, prev[HALO - 1:HALO], pltpu.roll(x, 1, 0))
    xm2 = jnp.where(row == 0, prev[HALO - 2:HALO - 1],
                    jnp.where(row == 1, prev[HALO - 1:HALO], pltpu.roll(x, 2, 0)))
    xp1 = jnp.where(row == t - 1, nxt[0:1], pltpu.roll(x, t - 1, 0))
    return xm2 * w[0:1] + xm1 * w[1:2] + x * w[2:3] + xp1 * w[3:4]


def _resident(shape, index=None):
    index = index or (0,) * len(shape)
    return pl.BlockSpec(shape, lambda *_: index, pipeline_mode=pl.Buffered(1))


def _ada_kernel(c_ref, w_ref, b_ref, o_ref):
    o_ref[...] = _mm(_silu(c_ref[...]), w_ref[...]) + b_ref[...]


def _ada_call(cvec, w_ada, b_ada):
    rows = cvec.shape[0]
    return pl.pallas_call(
        _ada_kernel,
        grid=(N_MOD,),
        in_specs=[pl.BlockSpec((rows, D_MODEL), lambda k: (0, 0)),
                  pl.BlockSpec((D_MODEL, D_MODEL), lambda k: (0, k)),
                  pl.BlockSpec((1, D_MODEL), lambda k: (0, k))],
        out_specs=pl.BlockSpec((rows, D_MODEL), lambda k: (0, k)),
        out_shape=jax.ShapeDtypeStruct((rows, N_MOD * D_MODEL), F32),
        name="ada",
    )(cvec, w_ada, b_ada)


def _segment_edges(j, n_lat_tiles, n_all_tiles):
    return (jnp.logical_or(j == 0, j == n_lat_tiles),
            jnp.logical_or(j == n_lat_tiles - 1, j == n_all_tiles - 1))


def _ffn1_in_kernel(n_lat_tiles, n_all_tiles, *refs):
    n_sub = FFN1_SUB_TILES
    ins, refs = refs[:3 * n_sub], refs[3 * n_sub:]
    ng_ref, w1_ref, w3_ref, w2_ref, win_ref, wtail_ref, h1_ref, qkv_ref, z_ref, rx_ref, rg_ref, ba_ref = refs
    s = pl.program_id(0)

    def sub_tile(i):
        x_ref, ctx_ref, mod_ref = ins[3 * i:3 * i + 3]
        rows = slice(i * TOKEN_TILE, (i + 1) * TOKEN_TILE)
        h = jnp.where((s * n_sub + i) % n_all_tiles == n_lat_tiles, ctx_ref[0], x_ref[0])
        mod = mod_ref[0]
        ub = _rms_mod(h, ng_ref[0:1], mod[0:1], mod[1:2]).astype(BF16)
        a, b = yield [lambda: jnp.dot(ub, w1_ref[...], preferred_element_type=F32),
                      lambda: jnp.dot(ub, w3_ref[...], preferred_element_type=F32)]
        g = (_silu(a) * b).astype(BF16)
        (f,) = yield [lambda: jnp.dot(g, w2_ref[...], preferred_element_type=F32)]
        h1 = h + FFN_RESIDUAL * mod[2:3] * f
        h1_ref[rows] = h1
        ub2 = _rms_mod(h1, ng_ref[1:2], mod[3:4], mod[4:5]).astype(BF16)
        head, tail = yield [lambda: jnp.dot(ub2, win_ref[...], preferred_element_type=F32),
                            lambda: jnp.dot(ub2, wtail_ref[...], preferred_element_type=F32)]
        qkv_ref[rows] = head[:, :3 * GDN_WIDTH]
        z_ref[rows] = head[:, 3 * GDN_WIDTH:]
        rx_ref[rows] = tail[:, :LRU_WIDTH]
        rg_ref[rows] = tail[:, LRU_WIDTH:2 * LRU_WIDTH]
        ba_ref[rows] = tail[:, 2 * LRU_WIDTH:]

    _run_interleaved([sub_tile(i) for i in range(n_sub)])


def _ffn1_in_call(x, ctx, mods, norm_g, w1, w3, w2, w_in_bf, w_tail):
    bsz, seq, _ = x.shape
    n_lat = seq // TOKEN_TILE
    n_all = n_lat + 1
    t_all = n_all * TOKEN_TILE
    n_sub = FFN1_SUB_TILES
    assert (bsz * n_all) % n_sub == 0
    per_tile = []
    for i in range(n_sub):
        tile = lambda s, i=i: s * n_sub + i
        per_tile += [
            pl.BlockSpec((1, TOKEN_TILE, D_MODEL),
                         lambda s, t=tile: (t(s) // n_all, jnp.minimum(t(s) % n_all, n_lat - 1), 0)),
            pl.BlockSpec((1, TOKEN_TILE, D_MODEL), lambda s, t=tile: (t(s) // n_all, 0, 0)),
            pl.BlockSpec((1, N_MOD, D_MODEL),
                         lambda s, t=tile: (jnp.where(t(s) % n_all == n_lat, 0, t(s) // n_all + 1), 0, 0))]
    widths = (D_MODEL, 3 * GDN_WIDTH, GDN_WIDTH, LRU_WIDTH, LRU_WIDTH, BA_PAD)
    outs = pl.pallas_call(
        functools.partial(_ffn1_in_kernel, n_lat, n_all),
        grid=(bsz * n_all // n_sub,),
        in_specs=per_tile + [
            _resident((3, D_MODEL)),
            _resident((None, D_MODEL, D_FF), (0, 0, 0)), _resident((None, D_MODEL, D_FF), (0, 0, 0)),
            _resident((None, D_FF, D_MODEL), (0, 0, 0)),
            _resident((D_MODEL, P_HEAD_COLS)), _resident((D_MODEL, P_TAIL_COLS))],
        out_specs=[pl.BlockSpec((n_sub * TOKEN_TILE, w), lambda s: (s, 0)) for w in widths],
        out_shape=[jax.ShapeDtypeStruct((bsz * t_all, w), F32) for w in widths],
        compiler_params=pltpu.CompilerParams(dimension_semantics=("arbitrary",),
                                             vmem_limit_bytes=VMEM_LIMIT),
        name="ffn1_in",
    )(*([x, ctx, mods] * n_sub), norm_g, w1, w3, w2, w_in_bf, w_tail)
    return [o.reshape(bsz, t_all, o.shape[-1]) for o in outs]


def _pair_block_diag(x, first):
    return jnp.concatenate([jnp.where(first, x, 0.0), jnp.where(first, 0.0, x)], axis=0)


def _run_interleaved(coroutines):
    active = [[gen, next(gen)] for gen in coroutines]
    while active:
        still = []
        for item in active:
            results = [dot() for dot in item[1]]
            try:
                item[1] = item[0].send(results)
                still.append(item)
            except StopIteration:
                pass
        active = still


def _gdn_local_body(seg_start, seg_end, x_ref, prev_ref, next_ref, ba_ref, cw_ref, alog_ref, dtb_ref,
                    out_refs, stage_in, stage_out):
    c = GDN_CHUNK
    pairs = GDN_HEADS // 2
    row = lax.broadcasted_iota(jnp.int32, (c, 2 * c), 0)
    lane2 = lax.broadcasted_iota(jnp.int32, (c, 2 * c), 1)
    col = lane2 & (c - 1)
    first = lane2 < c
    first_row = first[0:1]
    eye2 = (row == col).astype(F32)
    zeros_k = jnp.zeros((c, GDN_DK), BF16)
    unit_index = lambda ci, d, p: (ci * 2 + d) * pairs + p

    def coupling(k):
        return jnp.logical_and((row >> k) != (col >> k), (row >> (k + 1)) == (col >> (k + 1)))

    lhs_in, kr_in, rhs_in, dec_in, kdt_in, qe_in, eg_in = stage_in

    def pair_unit(ci, d, p):
        u_ref, wq_ref, ak_ref, eg_ref = out_refs[d]
        un = unit_index(ci, d, p)
        strict = (row < col) if d else (row > col)
        k_rows = kr_in[ci * pairs + p]
        k_bd = jnp.concatenate([jnp.concatenate([k_rows[:c], zeros_k], axis=1),
                                jnp.concatenate([zeros_k, k_rows[c:]], axis=1)], axis=0)
        lhs = lhs_in[un]
        (prods,) = yield [lambda: lax.dot_general(lhs, k_bd, NT_DIMS, preferred_element_type=F32)]
        decay = dec_in[un]
        a = jnp.where(strict, prods[:c] * decay, 0.0)
        ps = slice(p * 2 * c, (p + 1) * 2 * c)
        ak_ref[ci, 0, 0:c, ps] = (prods[c:] * decay).astype(BF16)
        ak_ref[ci, 0, c:3 * c, ps] = kdt_in[un]
        t = eye2 - jnp.where(coupling(0), a, 0.0)
        k = 1
        while 2 ** k < c:
            l = jnp.where(coupling(k), a, 0.0)
            t_bd = _pair_block_diag(t, first)
            (lt,) = yield [lambda: _mm(l, t_bd)]
            lt_bd = _pair_block_diag(lt, first)
            (tlt,) = yield [lambda: _mm(t, lt_bd)]
            t = t - tlt
            k += 1
        t_bd = _pair_block_diag(t, first).astype(BF16)
        rhs = rhs_in[un]
        (sol,) = yield [lambda: jnp.dot(t_bd, rhs, preferred_element_type=F32)]
        for i in range(2):
            h = 2 * p + i
            hs = slice(h * GDN_DK, (h + 1) * GDN_DK)
            u_ref[ci, 0, :, hs] = sol[i * c:(i + 1) * c, :GDN_DK]
            wq_ref[ci, 0, 0:c, hs] = sol[i * c:(i + 1) * c, GDN_DK:].astype(BF16)
            wq_ref[ci, 0, c:2 * c, hs] = qe_in[un, i * c:(i + 1) * c]
            eg_ref[ci, 0, h:h + 1, :] = eg_in[un, i:i + 1]

    lhs_out, kr_out, rhs_out, dec_out, kdt_out, qe_out, eg_out = stage_out
    y = _silu(_dwconv(x_ref[0], jnp.where(seg_start, 0.0, prev_ref[0]), jnp.where(seg_end, 0.0, next_ref[0]),
                      cw_ref[...]))
    ba = ba_ref[0]
    lane = lax.broadcasted_iota(jnp.int32, (1, BA_PAD), 1)
    is_g = jnp.logical_and(lane >= 2 * GDN_HEADS, lane < 4 * GDN_HEADS)
    beta_all = _sigmoid(ba)
    g_all = jnp.where(is_g, -jnp.exp(alog_ref[...]) * _softplus(ba + dtb_ref[...]), 0.0)

    def chunk(ci):
        rs = slice(ci * c, (ci + 1) * c)
        heads = []
        for h in range(GDN_HEADS):
            lo = h * GDN_DK
            q = y[rs, lo:lo + GDN_DK]
            k = y[rs, GDN_WIDTH + lo:GDN_WIDTH + lo + GDN_DK]
            v = y[rs, 2 * GDN_WIDTH + lo:2 * GDN_WIDTH + lo + GDN_DK]
            q = q * (lax.rsqrt(jnp.sum(q * q, axis=-1, keepdims=True) + EPS) * (GDN_DK ** -0.5))
            k = k * lax.rsqrt(jnp.sum(k * k, axis=-1, keepdims=True) + EPS)
            heads.append((q, k, v))
        g = g_all[rs]
        k_rows = [jnp.concatenate([heads[2 * p][1], heads[2 * p + 1][1]], axis=0) for p in range(pairs)]
        k_ts = [kr.T for kr in k_rows]
        grow = lax.broadcasted_iota(jnp.int32, g.shape, 0)
        gcum_f = g
        step = 1
        while step < c:
            gcum_f = gcum_f + jnp.where(grow >= step, pltpu.roll(gcum_f, step, 0), 0.0)
            step *= 2
        gcum_b = gcum_f[c - 1:c] - gcum_f + g
        gcum_t = jnp.concatenate([gcum_f, gcum_b], axis=0).T
        beta_c = beta_all[rs]
        for p in range(pairs):
            kr_out[ci * pairs + p] = k_rows[p].astype(BF16)
        for d in range(2):
            gcum = gcum_b if d else gcum_f
            incl = (row <= col) if d else (row >= col)
            last = 0 if d else c - 1
            for p in range(pairs):
                un = unit_index(ci, d, p)
                (q0, k0, v0), (q1, k1, v1) = heads[2 * p], heads[2 * p + 1]
                cb = d * GDN_HEADS + 2 * p
                cg = 2 * GDN_HEADS + cb
                beta0, beta1 = beta_c[:, cb:cb + 1], beta_c[:, cb + 1:cb + 2]
                gc0, gc1 = gcum[:, cg:cg + 1], gcum[:, cg + 1:cg + 2]
                gt0, gt1 = gcum_t[cg:cg + 1], gcum_t[cg + 1:cg + 2]
                gr = (jnp.where(first_row, pltpu.roll(gt0, c, 1), gt1) if d
                      else jnp.where(first_row, gt0, pltpu.roll(gt1, c, 1)))
                gc = jnp.where(first, gc0, gc1)
                e0, e1 = jnp.exp(gc0), jnp.exp(gc1)
                kb0, kb1 = k0 * beta0, k1 * beta1
                dec_out[un] = jnp.exp(jnp.where(incl, gc - gr, NEG_BIG))
                lhs_out[un] = jnp.concatenate([jnp.concatenate([kb0, kb1], axis=1),
                                               jnp.concatenate([q0, q1], axis=1)], axis=0).astype(BF16)
                rhs_out[un] = jnp.concatenate([jnp.concatenate([v0 * beta0, kb0 * e0], axis=1),
                                               jnp.concatenate([v1 * beta1, kb1 * e1], axis=1)],
                                              axis=0).astype(BF16)
                kdt_out[un] = (k_ts[p] * jnp.exp(gc[last:last + 1] - gr)).astype(BF16)
                qe_out[un] = jnp.concatenate([q0 * e0, q1 * e1], axis=0).astype(BF16)
                eg_out[un] = jnp.concatenate(
                    [jnp.broadcast_to(jnp.exp(gc0[last:last + 1]), (1, BA_PAD)),
                     jnp.broadcast_to(jnp.exp(gc1[last:last + 1]), (1, BA_PAD)),
                     jnp.zeros((HALO - 2, BA_PAD), F32)], axis=0)

    for refs in out_refs:
        refs[3][...] = jnp.zeros_like(refs[3])
    _run_interleaved([pair_unit(ci, d, p) for ci in range(GDN_TILE_CHUNKS) for d in range(2)
                      for p in range(pairs)])
    for ci in range(GDN_TILE_CHUNKS):
        chunk(ci)


def _gdn_local_kernel(n_lat_tiles, n_all_tiles, n_tiles, x_ref, prev_ref, next_ref, ba_ref, cw_ref, alog_ref,
                      dtb_ref, uf_ref, wqf_ref, akf_ref, egf_ref, ub_ref, wqb_ref, akb_ref, egb_ref, *stage_refs):
    s = pl.program_id(0)
    seg_start, seg_end = _segment_edges(jnp.minimum(s, n_tiles - 1) % n_all_tiles, n_lat_tiles, n_all_tiles)
    out_refs = ((uf_ref, wqf_ref, akf_ref, egf_ref), (ub_ref, wqb_ref, akb_ref, egb_ref))
    half = len(stage_refs) // 2
    sets = (stage_refs[:half], stage_refs[half:])

    @pl.when(s == 0)
    def _():
        for r in sets[1]:
            r[...] = jnp.zeros_like(r)

    for parity in range(2):
        @pl.when(s % 2 == parity)
        def _():
            _gdn_local_body(seg_start, seg_end, x_ref, prev_ref, next_ref, ba_ref, cw_ref, alog_ref, dtb_ref,
                            out_refs, sets[1 - parity], sets[parity])


def _gdn_local_call(qkv, ba, conv_w, alog_vec, dtb_vec, n_lat_tok):
    bsz, t_all, width = qkv.shape
    c = GDN_CHUNK
    tile = GDN_TILE_CHUNKS * c
    n_all = t_all // tile
    n_lat = n_lat_tok // tile
    per_tile = tile // HALO
    last_halo = t_all // HALO - 1
    n_chunks = t_all // c
    n_tiles = bsz * n_all
    pairs = GDN_HEADS // 2
    units = GDN_TILE_CHUNKS * 2 * pairs
    t_in = lambda s: jnp.minimum(s, n_tiles - 1)
    t_out = lambda s: jnp.maximum(s - 1, 0)
    in_tok = lambda w: pl.BlockSpec((1, tile, w), lambda s: (t_in(s) // n_all, t_in(s) % n_all, 0))
    cblk = lambda rows, w: pl.BlockSpec((GDN_TILE_CHUNKS, 1, rows, w),
                                        lambda s: (t_out(s) % n_all, t_out(s) // n_all, 0, 0))
    vec = pl.BlockSpec((1, BA_PAD), lambda s: (0, 0))
    outs = [(jax.ShapeDtypeStruct((n_chunks, bsz, c, GDN_WIDTH), F32), cblk(c, GDN_WIDTH)),
            (jax.ShapeDtypeStruct((n_chunks, bsz, 2 * c, GDN_WIDTH), BF16), cblk(2 * c, GDN_WIDTH)),
            (jax.ShapeDtypeStruct((n_chunks, bsz, 3 * c, GDN_HEADS * c), BF16), cblk(3 * c, GDN_HEADS * c)),
            (jax.ShapeDtypeStruct((n_chunks, bsz, HALO, BA_PAD), F32), cblk(HALO, BA_PAD))] * 2
    stage = [pltpu.VMEM((units, 2 * c, 2 * GDN_DK), BF16),
             pltpu.VMEM((GDN_TILE_CHUNKS * pairs, 2 * c, GDN_DK), BF16),
             pltpu.VMEM((units, 2 * c, 2 * GDN_DK), BF16),
             pltpu.VMEM((units, c, 2 * c), F32),
             pltpu.VMEM((units, GDN_DK, 2 * c), BF16),
             pltpu.VMEM((units, 2 * c, GDN_DK), BF16),
             pltpu.VMEM((units, HALO, BA_PAD), F32)]
    return pl.pallas_call(
        functools.partial(_gdn_local_kernel, n_lat, n_all, n_tiles),
        grid=(n_tiles + 1,),
        in_specs=[in_tok(width),
                  pl.BlockSpec((1, HALO, width),
                               lambda s: (t_in(s) // n_all, jnp.maximum((t_in(s) % n_all) * per_tile - 1, 0), 0)),
                  pl.BlockSpec((1, HALO, width),
                               lambda s: (t_in(s) // n_all,
                                          jnp.minimum((t_in(s) % n_all + 1) * per_tile, last_halo), 0)),
                  in_tok(BA_PAD), pl.BlockSpec((CONV_W, width), lambda s: (0, 0)), vec, vec],
        out_specs=[o[1] for o in outs],
        out_shape=[o[0] for o in outs],
        scratch_shapes=stage + stage,
        compiler_params=pltpu.CompilerParams(dimension_semantics=("arbitrary",),
                                             vmem_limit_bytes=VMEM_LIMIT),
        name="gdn_local",
    )(qkv, qkv, qkv, ba, conv_w, alog_vec, dtb_vec)


def _gdn_scan_kernel(uf_ref, wqf_ref, akf_ref, egf_ref, ub_ref, wqb_ref, akb_ref, egb_ref,
                     of_ref, ob_ref, sf_ref, sb_ref):
    @pl.when(pl.program_id(0) == 0)
    def _():
        sf_ref[...] = jnp.zeros_like(sf_ref)
        sb_ref[...] = jnp.zeros_like(sb_ref)

    c = GDN_CHUNK
    bsz = uf_ref.shape[1]
    hs = lambda h: slice(h * GDN_DK, (h + 1) * GDN_DK)
    for sub in range(GDN_SCAN_CHUNKS):
        chains = [(refs, ci, b, h)
                  for refs, ci in (((uf_ref, wqf_ref, akf_ref, egf_ref, of_ref, sf_ref), sub),
                                   ((ub_ref, wqb_ref, akb_ref, egb_ref, ob_ref, sb_ref),
                                    GDN_SCAN_CHUNKS - 1 - sub))
                  for b in range(bsz) for h in range(GDN_HEADS)]
        states = [refs[5][b * GDN_HEADS + h] for refs, ci, b, h in chains]
        with_s = [jnp.dot(refs[1][ci, b, :, hs(h)], s.astype(BF16), preferred_element_type=F32)
                  for (refs, ci, b, h), s in zip(chains, states)]
        v_new = [refs[0][ci, b, :, hs(h)] - r[:c] for (refs, ci, b, h), r in zip(chains, with_s)]
        with_v = [jnp.dot(refs[2][ci, b, :, h * c:(h + 1) * c], v.astype(BF16), preferred_element_type=F32)
                  for (refs, ci, b, h), v in zip(chains, v_new)]
        for (refs, ci, b, h), s, r, a in zip(chains, states, with_s, with_v):
            refs[4][b, ci * c:(ci + 1) * c, hs(h)] = r[c:] + a[:c]
            refs[5][b * GDN_HEADS + h] = s * refs[3][ci, b, h:h + 1, :] + a[c:]


def _gdn_scan_call(local, n_lat_tok):
    n_chunks, bsz = local[0].shape[:2]
    c = GDN_CHUNK
    g = GDN_SCAN_CHUNKS
    n_all = n_chunks // g
    n_lat = n_lat_tok // (c * g)
    n_ctx = n_all - n_lat
    fwd = lambda s: jnp.where(s < n_ctx, n_lat + s, s - n_ctx)
    bwd = lambda s: n_all - 1 - s
    fwd_out = lambda s: jnp.maximum(s - n_ctx, 0)
    bwd_out = lambda s: jnp.minimum(n_all - 1 - s, n_lat - 1)
    cblk = lambda rows, w, f: pl.BlockSpec((g, bsz, rows, w), lambda s: (f(s), 0, 0, 0))
    ins = lambda f: [cblk(c, GDN_WIDTH, f), cblk(2 * c, GDN_WIDTH, f), cblk(3 * c, GDN_HEADS * c, f),
                     cblk(HALO, BA_PAD, f)]
    tok = lambda f: pl.BlockSpec((bsz, g * c, GDN_WIDTH), lambda s: (0, f(s), 0))
    out = jax.ShapeDtypeStruct((bsz, n_lat_tok, GDN_WIDTH), F32)
    state = pltpu.VMEM((bsz * GDN_HEADS, GDN_DK, GDN_DK), F32)
    return pl.pallas_call(
        _gdn_scan_kernel,
        grid=(n_all,),
        in_specs=ins(fwd) + ins(bwd),
        out_specs=[tok(fwd_out), tok(bwd_out)],
        out_shape=[out, out],
        scratch_shapes=[state, state],
        compiler_params=pltpu.CompilerParams(dimension_semantics=("arbitrary",),
                                             vmem_limit_bytes=VMEM_LIMIT),
        name="gdn_scan",
    )(*local)


def _lin_scan(a, b, reverse):
    r = a.shape[0]
    row = lax.broadcasted_iota(jnp.int32, a.shape, 0)
    s = 1
    while s < r:
        ok = (row < r - s) if reverse else (row >= s)
        shift = r - s if reverse else s
        a_sh = jnp.where(ok, pltpu.roll(a, shift, 0), 1.0)
        b_sh = jnp.where(ok, pltpu.roll(b, shift, 0), 0.0)
        b = a * b_sh + b
        a = a * a_sh
        s *= 2
    return a, b


def _lru_ctx_direction(reverse, n_slabs, slab, x_ref, prev_ref, next_ref, cw, cb, wr_ref, wi_ref,
                       br, bi, lam, carry_ref):
    bsz, r, width = x_ref.shape
    step = pl.program_id(0)
    has_prev = slab > 0
    has_next = slab < n_slabs - 1
    xs = []
    for b in range(bsz):
        prev = jnp.where(has_prev, prev_ref[b], 0.0)
        nxt = jnp.where(has_next, next_ref[b], 0.0)
        xs.append(_dwconv(x_ref[b], prev, nxt, cw) + cb)
    x = jnp.concatenate(xs, axis=0)
    xb = x.astype(BF16)
    gate_r = _sigmoid(jnp.dot(xb, wr_ref[...], preferred_element_type=F32) + br)
    gate_i = _sigmoid(jnp.dot(xb, wi_ref[...], preferred_element_type=F32) + bi)
    log_a = (-LRU_C * _softplus(-lam)) * gate_r
    a = jnp.exp(log_a)
    mult = jnp.sqrt(1.0 - jnp.exp(2.0 * log_a))
    row = lax.broadcasted_iota(jnp.int32, (r, width), 0)
    is_first = jnp.logical_and(step == 0, row == ((r - 1) if reverse else 0))
    edge = 0 if reverse else r - 1
    for b in range(bsz):
        sl = slice(b * r, (b + 1) * r)
        m = jnp.where(is_first, 1.0, mult[sl])
        a_cum, h = _lin_scan(a[sl], m * (gate_i[sl] * x[sl]), reverse)
        h = h + a_cum * carry_ref[b:b + 1]
        carry_ref[b:b + 1] = h[edge:edge + 1]


def _lru_ctx_kernel(n_slabs, xf_ref, pf_ref, nf_ref, xb_ref, pb_ref, nb_ref, cw_ref, cb_ref,
                    wrf_ref, wif_ref, wrb_ref, wib_ref, bg_ref, lam_ref, fin_ref, cf_ref, cbk_ref):
    step = pl.program_id(0)

    @pl.when(step == 0)
    def _():
        cf_ref[...] = jnp.zeros_like(cf_ref)
        cbk_ref[...] = jnp.zeros_like(cbk_ref)

    cw = cw_ref[...]
    cb = cb_ref[...]
    _lru_ctx_direction(False, n_slabs, step, xf_ref, pf_ref, nf_ref, cw, cb, wrf_ref, wif_ref,
                       bg_ref[0:1], bg_ref[1:2], lam_ref[0:1], cf_ref)
    _lru_ctx_direction(True, n_slabs, n_slabs - 1 - step, xb_ref, pb_ref, nb_ref, cw, cb, wrb_ref, wib_ref,
                       bg_ref[2:3], bg_ref[3:4], lam_ref[1:2], cbk_ref)
    fin_ref[0] = cf_ref[...]
    fin_ref[1] = cbk_ref[...]


def _lru_ctx_call(rx, n_lat_tok, conv_w, conv_b, w_dense, b_gate, lam):
    bsz, t_all, width = rx.shape
    r = GRID_W
    per = r // HALO
    n_slabs = (t_all - n_lat_tok) // r
    first = n_lat_tok // r
    main = lambda f: pl.BlockSpec((bsz, r, width), lambda s: (0, first + f(s), 0))
    prev = lambda f: pl.BlockSpec((bsz, HALO, width), lambda s: (0, (first + f(s)) * per - 1, 0))
    nxt = lambda f: pl.BlockSpec(
        (bsz, HALO, width), lambda s: (0, jnp.minimum((first + f(s) + 1) * per, t_all // HALO - 1), 0))
    fwd = lambda s: s
    bwd = lambda s: n_slabs - 1 - s
    full = lambda shape: pl.BlockSpec(shape, lambda s: (0,) * len(shape))
    carry = pltpu.VMEM((bsz, width), F32)
    return pl.pallas_call(
        functools.partial(_lru_ctx_kernel, n_slabs),
        grid=(n_slabs,),
        in_specs=[main(fwd), prev(fwd), nxt(fwd), main(bwd), prev(bwd), nxt(bwd),
                  full((CONV_W, width)), full((1, width))]
                 + [full((width, width))] * 4
                 + [full((4, width)), full((2, width))],
        out_specs=full((2, bsz, width)),
        out_shape=jax.ShapeDtypeStruct((2, bsz, width), F32),
        scratch_shapes=[carry, carry],
        compiler_params=pltpu.CompilerParams(dimension_semantics=("arbitrary",),
                                             vmem_limit_bytes=VMEM_LIMIT),
        name="lru_ctx",
    )(rx, rx, rx, rx, rx, rx, conv_w, conv_b, *w_dense, b_gate, lam)


def _lru_grid_kernel(x_ref, h0_ref, cw_ref, cb_ref, w_ref, bg_ref, lam_ref, out_ref,
                     af_ref, bf_ref, ab_ref, bb_ref):
    n = GRID_W
    rp = LRU_ROWS_PER_PASS
    lanes = x_ref.shape[-1]
    crow = lax.broadcasted_iota(jnp.int32, (n, lanes), 0)
    cw = cw_ref[...]
    cb = cb_ref[...]

    def grid_rows(lo, hi):
        parts = []
        for r in range(lo, min(hi, 0)):
            parts.append(jnp.where(crow == 0, 0.0, pltpu.roll(x_ref[0, (n + r) * n:(n + r + 1) * n, :], 1, 0)))
        if max(lo, 0) < min(hi, n):
            parts.append(x_ref[0, max(lo, 0) * n:min(hi, n) * n, :])
        for r in range(max(lo, n), hi):
            parts.append(jnp.where(crow == n - 1, 0.0,
                                   pltpu.roll(x_ref[0, (r - n) * n:(r - n + 1) * n, :], n - 1, 0)))
        return parts[0] if len(parts) == 1 else jnp.concatenate(parts, axis=0)

    for r0 in range(0, n, rp):
        sl = slice(r0 * n, (r0 + rp) * n)
        xc = (grid_rows(r0 - 2, r0 + rp - 2) * cw[0:1] + grid_rows(r0 - 1, r0 + rp - 1) * cw[1:2]
              + grid_rows(r0, r0 + rp) * cw[2:3] + grid_rows(r0 + 1, r0 + rp + 1) * cw[3:4] + cb)
        xb = xc.astype(BF16)
        for d, (a_ref, b_ref) in enumerate(((af_ref, bf_ref), (ab_ref, bb_ref))):
            gate_r = _sigmoid(jnp.dot(xb, w_ref[2 * d, 0], preferred_element_type=F32) + bg_ref[2 * d:2 * d + 1])
            gate_i = _sigmoid(jnp.dot(xb, w_ref[2 * d + 1, 0], preferred_element_type=F32)
                              + bg_ref[2 * d + 1:2 * d + 2])
            log_a = (-LRU_C * _softplus(-lam_ref[d:d + 1])) * gate_r
            a_ref[sl] = jnp.exp(log_a)
            b_ref[sl] = jnp.sqrt(1.0 - jnp.exp(2.0 * log_a)) * (gate_i * xc)

    hf = jnp.zeros((n, lanes), F32)
    hb = jnp.zeros((n, lanes), F32)
    pf = jnp.ones((n, lanes), F32)
    pb = jnp.ones((n, lanes), F32)
    for i in range(n):
        sf = slice(i * n, (i + 1) * n)
        sb = slice((n - 1 - i) * n, (n - i) * n)
        a = af_ref[sf]
        hf = a * hf + bf_ref[sf]
        pf = a * pf
        bf_ref[sf] = hf
        af_ref[sf] = pf
        a = ab_ref[sb]
        hb = a * hb + bb_ref[sb]
        pb = a * pb
        bb_ref[sb] = hb
        ab_ref[sb] = pb

    h0f = h0_ref[0, 0]
    h0b = h0_ref[1, 0]
    acc_a, acc_h = _lin_scan(pf, hf, False)
    in_f = jnp.where(crow == 0, h0f, pltpu.roll(acc_h + acc_a * h0f, 1, 0))
    acc_a, acc_h = _lin_scan(pb, hb, True)
    in_b = jnp.where(crow == n - 1, h0b, pltpu.roll(acc_h + acc_a * h0b, n - 1, 0))

    for r in range(n):
        sl = slice(r * n, (r + 1) * n)
        out_ref[0, sl, :] = (bf_ref[sl] + af_ref[sl] * in_f) + (bb_ref[sl] + ab_ref[sl] * in_b)


def _lru_grid_call(rx, n_lat_tok, h0, conv_w, conv_b, w_tiles, b_gate, lam):
    bsz, _, width = rx.shape
    nt = width // LRU_LANES
    lane = lambda rows: pl.BlockSpec((rows, LRU_LANES), lambda b, j: (0, j))
    scratch = pltpu.VMEM((n_lat_tok, LRU_LANES), F32)
    return pl.pallas_call(
        _lru_grid_kernel,
        grid=(bsz, nt),
        in_specs=[pl.BlockSpec((1, n_lat_tok, LRU_LANES), lambda b, j: (b, 0, j)),
                  pl.BlockSpec((2, 1, 1, LRU_LANES), lambda b, j: (0, b, 0, j)),
                  lane(CONV_W), lane(1),
                  pl.BlockSpec((4, 1, LRU_LANES, LRU_LANES), lambda b, j: (0, j, 0, 0)),
                  lane(4), lane(2)],
        out_specs=pl.BlockSpec((1, n_lat_tok, LRU_LANES), lambda b, j: (b, 0, j)),
        out_shape=jax.ShapeDtypeStruct((bsz, n_lat_tok, width), F32),
        scratch_shapes=[scratch] * 4,
        compiler_params=pltpu.CompilerParams(dimension_semantics=("arbitrary", "arbitrary"),
                                             vmem_limit_bytes=VMEM_LIMIT),
        name="lru_grid",
    )(rx, h0.reshape(2, bsz, 1, width), conv_w, conv_b, w_tiles, b_gate, lam)


def _out_ffn2_kernel(h1_ref, of_ref, ob_ref, z_ref, hs_ref, rg_ref, mod_ref, ng_ref, gnw_ref,
                     wout_ref, w1_ref, w3_ref, w2_ref, fg_ref, out_ref):
    mod = mod_ref[0]

    def sub_tile(rows):
        o = of_ref[0, rows] + ob_ref[0, rows]
        z = z_ref[0, rows]
        parts = []
        for h in range(GDN_HEADS):
            sl = slice(h * GDN_DK, (h + 1) * GDN_DK)
            oh = o[:, sl]
            parts.append(oh * lax.rsqrt(jnp.mean(oh * oh, axis=-1, keepdims=True) + EPS)
                         * gnw_ref[...] * _silu(z[:, sl]))
        parts.append(hs_ref[0, rows] * _gelu_tanh(rg_ref[0, rows]))
        mixed = jnp.concatenate(parts, axis=1).astype(BF16)
        (y,) = yield [lambda: jnp.dot(mixed, wout_ref[...], preferred_element_type=F32)]
        h2 = h1_ref[0, rows] + mod[5:6] * y
        ub = _rms_mod(h2, ng_ref[2:3], mod[6:7], mod[7:8]).astype(BF16)
        a, b = yield [lambda: jnp.dot(ub, w1_ref[...], preferred_element_type=F32),
                      lambda: jnp.dot(ub, w3_ref[...], preferred_element_type=F32)]
        g = (_silu(a) * b).astype(BF16)
        (f,) = yield [lambda: jnp.dot(g, w2_ref[...], preferred_element_type=F32)]
        h3 = h2 + FFN_RESIDUAL * mod[8:9] * f
        out_ref[0, rows] = h3 * lax.rsqrt(jnp.mean(h3 * h3, axis=-1, keepdims=True) + EPS) * fg_ref[...]

    _run_interleaved([sub_tile(slice(i * TOKEN_TILE, (i + 1) * TOKEN_TILE))
                      for i in range(out_ref.shape[1] // TOKEN_TILE)])


def _out_ffn2_call(h1, o_f, o_b, z, h_sum, rg, mods, norm_g, gdn_norm_w, w_out, w1, w3, w2, final_g):
    bsz, seq, _ = o_f.shape
    rows = OUT_SUB_TILES * TOKEN_TILE
    tok = lambda w: pl.BlockSpec((1, rows, w), lambda b, j: (b, j, 0))
    return pl.pallas_call(
        _out_ffn2_kernel,
        grid=(bsz, seq // rows),
        in_specs=[tok(D_MODEL), tok(GDN_WIDTH), tok(GDN_WIDTH), tok(GDN_WIDTH),
                  tok(LRU_WIDTH), tok(LRU_WIDTH),
                  pl.BlockSpec((1, N_MOD, D_MODEL), lambda b, j: (b + 1, 0, 0)),
                  _resident((3, D_MODEL)), _resident((1, GDN_DK)),
                  _resident((D_MODEL, D_MODEL)),
                  _resident((None, D_MODEL, D_FF), (1, 0, 0)), _resident((None, D_MODEL, D_FF), (1, 0, 0)),
                  _resident((None, D_FF, D_MODEL), (1, 0, 0)),
                  _resident((1, D_MODEL))],
        out_specs=tok(D_MODEL),
        out_shape=jax.ShapeDtypeStruct((bsz, seq, D_MODEL), F32),
        compiler_params=pltpu.CompilerParams(dimension_semantics=("arbitrary", "arbitrary"),
                                             vmem_limit_bytes=VMEM_LIMIT),
        name="out_ffn2",
    )(h1, o_f, o_b, z, h_sum, rg, mods, norm_g, gdn_norm_w, w_out, w1, w3, w2, final_g)


def _block_diag(w):
    nb, bw, _ = w.shape
    eye = jnp.eye(nb, dtype=w.dtype)
    return jnp.einsum('ncd,nm->ncmd', w, eye).reshape(nb * bw, nb * bw)


def kernel(x, c, ctx, c_ctx, w_ada, b_ada, norm_g, ffn_w1, ffn_w3, ffn_w2, w_in, w_out, gdn_conv_w, gdn_a_log,
           gdn_dt_bias, gdn_norm_w, lru_conv_w, lru_conv_b, lru_w_gate, lru_b_gate, lru_lambda, final_norm_g):
    bsz, seq, _ = x.shape
    assert w_ada.shape[0] == 1 and seq == GRID_W * GRID_W and ctx.shape[1] == TOKEN_TILE

    rows = -(-(bsz + 1) // 8) * 8
    cvec = jnp.concatenate([c_ctx[None, :], c, jnp.zeros((rows - bsz - 1, D_MODEL), F32)], axis=0)
    mods = _ada_call(cvec, w_ada[0], b_ada).reshape(rows, N_MOD, D_MODEL)

    w_in_bf = w_in[0].astype(BF16)
    o_ba = P_HEAD_COLS
    o_rx = o_ba + 4 * GDN_HEADS
    w_tail = jnp.concatenate([w_in_bf[:, o_rx:], w_in_bf[:, o_ba:o_rx],
                              jnp.zeros((D_MODEL, BA_PAD - 4 * GDN_HEADS), BF16)], axis=1)
    w1 = ffn_w1[0].astype(BF16)
    w3 = ffn_w3[0].astype(BF16)
    w2 = ffn_w2[0].astype(BF16)

    h1, qkv, z, rx, rg, ba = _ffn1_in_call(x, ctx, mods, norm_g[0], w1, w3, w2, w_in_bf, w_tail)

    pad_g = lambda t: jnp.zeros((1, BA_PAD), F32).at[0, 2 * GDN_HEADS:4 * GDN_HEADS].set(t.reshape(-1))
    local = _gdn_local_call(qkv, ba, gdn_conv_w[0], pad_g(gdn_a_log[0]), pad_g(gdn_dt_bias[0]), seq)
    o_f, o_b = _gdn_scan_call(local, seq)

    wg = lru_w_gate[0]
    w_dense = [_block_diag(wg[d, g]).astype(BF16) for d in range(2) for g in range(2)]
    b_gate = lru_b_gate[0].reshape(4, LRU_WIDTH)
    lam = lru_lambda[0]
    cw, cb = lru_conv_w[0], lru_conv_b
    h_ctx = _lru_ctx_call(rx, seq, cw, cb, w_dense, b_gate, lam)
    nt = LRU_WIDTH // LRU_LANES
    w_tiles = jnp.stack([jnp.stack([w[t * LRU_LANES:(t + 1) * LRU_LANES, t * LRU_LANES:(t + 1) * LRU_LANES]
                                    for t in range(nt)]) for w in w_dense])
    h_sum = _lru_grid_call(rx, seq, h_ctx, cw, cb, w_tiles, b_gate, lam)

    return _out_ffn2_call(h1, o_f, o_b, z, h_sum, rg, mods, norm_g[0], gdn_norm_w,
                          w_out[0].astype(BF16), w1, w3, w2, final_norm_g[None, :])
```

```python
import functools

import jax
import jax.numpy as jnp
from jax import lax
from jax.experimental import pallas as pl
from jax.experimental.pallas import tpu as pltpu

D_MODEL = 1024
D_FF = 2816
N_MOD = 9
EPS = 1e-6
FFN_RESIDUAL = 0.5

GDN_WIDTH = 512
GDN_HEADS = 4
GDN_DK = 128
GDN_CHUNK = 64
CONV_W = 4
LRU_WIDTH = 512
LRU_C = 8.0
GRID_W = 64

TOKEN_TILE = 256
GDN_TILE_CHUNKS = TOKEN_TILE // GDN_CHUNK
GDN_SCAN_CHUNKS = 4
OUT_SUB_TILES = 2
FFN1_SUB_TILES = 2
HALO = 8
BA_PAD = 128
P_HEAD_COLS = 4 * GDN_WIDTH
P_TAIL_COLS = 2 * LRU_WIDTH + BA_PAD
LRU_LANES = 128
LRU_ROWS_PER_PASS = 8
VMEM_LIMIT = 56 * 1024 * 1024

BF16 = jnp.bfloat16
F32 = jnp.float32
NEG_BIG = -1e30
NT_DIMS = (((1,), (1,)), ((), ()))


def _mm(a, b):
    return jnp.dot(a.astype(BF16), b.astype(BF16), preferred_element_type=F32)


def _sigmoid(x):
    return 1.0 / (1.0 + jnp.exp(-x))


def _silu(x):
    return x * _sigmoid(x)


def _softplus(x):
    return jnp.maximum(x, 0.0) + jnp.log(1.0 + jnp.exp(-jnp.abs(x)))


def _gelu_tanh(x):
    return 0.5 * x * (1.0 + jnp.tanh(0.7978845608028654 * (x + 0.044715 * x * x * x)))


def _rms_mod(h, g, shift, scale):
    y = h * lax.rsqrt(jnp.mean(h * h, axis=-1, keepdims=True) + EPS) * g
    return y * (1.0 + scale) + shift


def _swiglu(u, w1_ref, w3_ref, w2_ref):
    ub = u.astype(BF16)
    a = jnp.dot(ub, w1_ref[...], preferred_element_type=F32)
    b = jnp.dot(ub, w3_ref[...], preferred_element_type=F32)
    g = (_silu(a) * b).astype(BF16)
    return jnp.dot(g, w2_ref[...], preferred_element_type=F32)


def _dwconv(x, prev, nxt, w):
    t = x.shape[0]
    row = lax.broadcasted_iota(jnp.int32, x.shape, 0)
    xm1 = jnp.where(row == 0, prev[HALO - 1:HALO], pltpu.roll(x, 1, 0))
    xm2 = jnp.where(row == 0, prev[HALO - 2:HALO - 1],
                    jnp.where(row == 1, prev[HALO - 1:HALO], pltpu.roll(x, 2, 0)))
    xp1 = jnp.where(row == t - 1, nxt[0:1], pltpu.roll(x, t - 1, 0))
    return xm2 * w[0:1] + xm1 * w[1:2] + x * w[2:3] + xp1 * w[3:4]


def _resident(shape, index=None):
    index = index or (0,) * len(shape)
    return pl.BlockSpec(shape, lambda *_: index, pipeline_mode=pl.Buffered(1))


def _ada_kernel(c_ref, w_ref, b_ref, o_ref):
    o_ref[...] = _mm(_silu(c_ref[...]), w_ref[...]) + b_ref[...]


def _ada_call(cvec, w_ada, b_ada):
    rows = cvec.shape[0]
    return pl.pallas_call(
        _ada_kernel,
        grid=(N_MOD,),
        in_specs=[pl.BlockSpec((rows, D_MODEL), lambda k: (0, 0)),
                  pl.BlockSpec((D_MODEL, D_MODEL), lambda k: (0, k)),
                  pl.BlockSpec((1, D_MODEL), lambda k: (0, k))],
        out_specs=pl.BlockSpec((rows, D_MODEL), lambda k: (0, k)),
        out_shape=jax.ShapeDtypeStruct((rows, N_MOD * D_MODEL), F32),
        name="ada",
    )(cvec, w_ada, b_ada)


def _segment_edges(j, n_lat_tiles, n_all_tiles):
    return (jnp.logical_or(j == 0, j == n_lat_tiles),
            jnp.logical_or(j == n_lat_tiles - 1, j == n_all_tiles - 1))


def _ffn1_in_kernel(n_lat_tiles, n_all_tiles, *refs):
    n_sub = FFN1_SUB_TILES
    ins, refs = refs[:3 * n_sub], refs[3 * n_sub:]
    ng_ref, w1_ref, w3_ref, w2_ref, win_ref, wtail_ref, h1_ref, qkv_ref, z_ref, rx_ref, rg_ref, ba_ref = refs
    s = pl.program_id(0)

    def sub_tile(i):
        x_ref, ctx_ref, mod_ref = ins[3 * i:3 * i + 3]
        rows = slice(i * TOKEN_TILE, (i + 1) * TOKEN_TILE)
        h = jnp.where((s * n_sub + i) % n_all_tiles == n_lat_tiles, ctx_ref[0], x_ref[0])
        mod = mod_ref[0]
        ub = _rms_mod(h, ng_ref[0:1], mod[0:1], mod[1:2]).astype(BF16)
        a, b = yield [lambda: jnp.dot(ub, w1_ref[...], preferred_element_type=F32),
                      lambda: jnp.dot(ub, w3_ref[...], preferred_element_type=F32)]
        g = (_silu(a) * b).astype(BF16)
        (f,) = yield [lambda: jnp.dot(g, w2_ref[...], preferred_element_type=F32)]
        h1 = h + FFN_RESIDUAL * mod[2:3] * f
        h1_ref[rows] = h1
        ub2 = _rms_mod(h1, ng_ref[1:2], mod[3:4], mod[4:5]).astype(BF16)
        head, tail = yield [lambda: jnp.dot(ub2, win_ref[...], preferred_element_type=F32),
                            lambda: jnp.dot(ub2, wtail_ref[...], preferred_element_type=F32)]
        qkv_ref[rows] = head[:, :3 * GDN_WIDTH]
        z_ref[rows] = head[:, 3 * GDN_WIDTH:]
        rx_ref[rows] = tail[:, :LRU_WIDTH]
        rg_ref[rows] = tail[:, LRU_WIDTH:2 * LRU_WIDTH]
        ba_ref[rows] = tail[:, 2 * LRU_WIDTH:]

    _run_interleaved([sub_tile(i) for i in range(n_sub)])


def _ffn1_in_call(x, ctx, mods, norm_g, w1, w3, w2, w_in_bf, w_tail):
    bsz, seq, _ = x.shape
    n_lat = seq // TOKEN_TILE
    n_all = n_lat + 1
    t_all = n_all * TOKEN_TILE
    n_sub = FFN1_SUB_TILES
    assert (bsz * n_all) % n_sub == 0
    per_tile = []
    for i in range(n_sub):
        tile = lambda s, i=i: s * n_sub + i
        per_tile += [
            pl.BlockSpec((1, TOKEN_TILE, D_MODEL),
                         lambda s, t=tile: (t(s) // n_all, jnp.minimum(t(s) % n_all, n_lat - 1), 0)),
            pl.BlockSpec((1, TOKEN_TILE, D_MODEL), lambda s, t=tile: (t(s) // n_all, 0, 0)),
            pl.BlockSpec((1, N_MOD, D_MODEL),
                         lambda s, t=tile: (jnp.where(t(s) % n_all == n_lat, 0, t(s) // n_all + 1), 0, 0))]
    widths = (D_MODEL, 3 * GDN_WIDTH, GDN_WIDTH, LRU_WIDTH, LRU_WIDTH, BA_PAD)
    outs = pl.pallas_call(
        functools.partial(_ffn1_in_kernel, n_lat, n_all),
        grid=(bsz * n_all // n_sub,),
        in_specs=per_tile + [
            _resident((3, D_MODEL)),
            _resident((None, D_MODEL, D_FF), (0, 0, 0)), _resident((None, D_MODEL, D_FF), (0, 0, 0)),
            _resident((None, D_FF, D_MODEL), (0, 0, 0)),
            _resident((D_MODEL, P_HEAD_COLS)), _resident((D_MODEL, P_TAIL_COLS))],
        out_specs=[pl.BlockSpec((n_sub * TOKEN_TILE, w), lambda s: (s, 0)) for w in widths],
        out_shape=[jax.ShapeDtypeStruct((bsz * t_all, w), F32) for w in widths],
        compiler_params=pltpu.CompilerParams(dimension_semantics=("arbitrary",),
                                             vmem_limit_bytes=VMEM_LIMIT),
        name="ffn1_in",
    )(*([x, ctx, mods] * n_sub), norm_g, w1, w3, w2, w_in_bf, w_tail)
    return [o.reshape(bsz, t_all, o.shape[-1]) for o in outs]


def _pair_block_diag(x, first):
    return jnp.concatenate([jnp.where(first, x, 0.0), jnp.where(first, 0.0, x)], axis=0)


def _run_interleaved(coroutines):
    active = [[gen, next(gen)] for gen in coroutines]
    while active:
        still = []
        for item in active:
            results = [dot() for dot in item[1]]
            try:
                item[1] = item[0].send(results)
                still.append(item)
            except StopIteration:
                pass
        active = still


def _gdn_local_body(seg_start, seg_end, x_ref, prev_ref, next_ref, ba_ref, cw_ref, alog_ref, dtb_ref,
                    out_refs, stage_in, stage_out):
    c = GDN_CHUNK
    pairs = GDN_HEADS // 2
    row = lax.broadcasted_iota(jnp.int32, (c, 2 * c), 0)
    lane2 = lax.broadcasted_iota(jnp.int32, (c, 2 * c), 1)
    col = lane2 & (c - 1)
    first = lane2 < c
    first_row = first[0:1]
    eye2 = (row == col).astype(F32)
    zeros_k = jnp.zeros((c, GDN_DK), BF16)
    unit_index = lambda ci, d, p: (ci * 2 + d) * pairs + p

    def coupling(k):
        return jnp.logical_and((row >> k) != (col >> k), (row >> (k + 1)) == (col >> (k + 1)))

    lhs_in, kr_in, rhs_in, dec_in, kdt_in, qe_in, eg_in = stage_in

    def pair_unit(ci, d, p):
        u_ref, wq_ref, ak_ref, eg_ref = out_refs[d]
        un = unit_index(ci, d, p)
        strict = (row < col) if d else (row > col)
        k_rows = kr_in[ci * pairs + p]
        k_bd = jnp.concatenate([jnp.concatenate([k_rows[:c], zeros_k], axis=1),
                                jnp.concatenate([zeros_k, k_rows[c:]], axis=1)], axis=0)
        lhs = lhs_in[un]
        (prods,) = yield [lambda: lax.dot_general(lhs, k_bd, NT_DIMS, preferred_element_type=F32)]
        decay = dec_in[un]
        a = jnp.where(strict, prods[:c] * decay, 0.0)
        ps = slice(p * 2 * c, (p + 1) * 2 * c)
        ak_ref[ci, 0, 0:c, ps] = (prods[c:] * decay).astype(BF16)
        ak_ref[ci, 0, c:3 * c, ps] = kdt_in[un]
        t = eye2 - jnp.where(coupling(0), a, 0.0)
        k = 1
        while 2 ** k < c:
            l = jnp.where(coupling(k), a, 0.0)
            t_bd = _pair_block_diag(t, first)
            (lt,) = yield [lambda: _mm(l, t_bd)]
            lt_bd = _pair_block_diag(lt, first)
            (tlt,) = yield [lambda: _mm(t, lt_bd)]
            t = t - tlt
            k += 1
        t_bd = _pair_block_diag(t, first).astype(BF16)
        rhs = rhs_in[un]
        (sol,) = yield [lambda: jnp.dot(t_bd, rhs, preferred_element_type=F32)]
        for i in range(2):
            h = 2 * p + i
            hs = slice(h * GDN_DK, (h + 1) * GDN_DK)
            u_ref[ci, 0, :, hs] = sol[i * c:(i + 1) * c, :GDN_DK]
            wq_ref[ci, 0, 0:c, hs] = sol[i * c:(i + 1) * c, GDN_DK:].astype(BF16)
            wq_ref[ci, 0, c:2 * c, hs] = qe_in[un, i * c:(i + 1) * c]
            eg_ref[ci, 0, h:h + 1, :] = eg_in[un, i:i + 1]

    lhs_out, kr_out, rhs_out, dec_out, kdt_out, qe_out, eg_out = stage_out
    y = _silu(_dwconv(x_ref[0], jnp.where(seg_start, 0.0, prev_ref[0]), jnp.where(seg_end, 0.0, next_ref[0]),
                      cw_ref[...]))
    ba = ba_ref[0]
    lane = lax.broadcasted_iota(jnp.int32, (1, BA_PAD), 1)
    is_g = jnp.logical_and(lane >= 2 * GDN_HEADS, lane < 4 * GDN_HEADS)
    beta_all = _sigmoid(ba)
    g_all = jnp.where(is_g, -jnp.exp(alog_ref[...]) * _softplus(ba + dtb_ref[...]), 0.0)

    def chunk(ci):
        rs = slice(ci * c, (ci + 1) * c)
        heads = []
        for h in range(GDN_HEADS):
            lo = h * GDN_DK
            q = y[rs, lo:lo + GDN_DK]
            k = y[rs, GDN_WIDTH + lo:GDN_WIDTH + lo + GDN_DK]
            v = y[rs, 2 * GDN_WIDTH + lo:2 * GDN_WIDTH + lo + GDN_DK]
            q = q * (lax.rsqrt(jnp.sum(q * q, axis=-1, keepdims=True) + EPS) * (GDN_DK ** -0.5))
            k = k * lax.rsqrt(jnp.sum(k * k, axis=-1, keepdims=True) + EPS)
            heads.append((q, k, v))
        g = g_all[rs]
        k_rows = [jnp.concatenate([heads[2 * p][1], heads[2 * p + 1][1]], axis=0) for p in range(pairs)]
        k_ts = [kr.T for kr in k_rows]
        grow = lax.broadcasted_iota(jnp.int32, g.shape, 0)
        gcum_f = g
        step = 1
        while step < c:
            gcum_f = gcum_f + jnp.where(grow >= step, pltpu.roll(gcum_f, step, 0), 0.0)
            step *= 2
        gcum_b = gcum_f[c - 1:c] - gcum_f + g
        gcum_t = jnp.concatenate([gcum_f, gcum_b], axis=0).T
        beta_c = beta_all[rs]
        for p in range(pairs):
            kr_out[ci * pairs + p] = k_rows[p].astype(BF16)
        for d in range(2):
            gcum = gcum_b if d else gcum_f
            incl = (row <= col) if d else (row >= col)
            last = 0 if d else c - 1
            for p in range(pairs):
                un = unit_index(ci, d, p)
                (q0, k0, v0), (q1, k1, v1) = heads[2 * p], heads[2 * p + 1]
                cb = d * GDN_HEADS + 2 * p
                cg = 2 * GDN_HEADS + cb
                beta0, beta1 = beta_c[:, cb:cb + 1], beta_c[:, cb + 1:cb + 2]
                gc0, gc1 = gcum[:, cg:cg + 1], gcum[:, cg + 1:cg + 2]
                gt0, gt1 = gcum_t[cg:cg + 1], gcum_t[cg + 1:cg + 2]
                gr = (jnp.where(first_row, pltpu.roll(gt0, c, 1), gt1) if d
                      else jnp.where(first_row, gt0, pltpu.roll(gt1, c, 1)))
                gc = jnp.where(first, gc0, gc1)
                e0, e1 = jnp.exp(gc0), jnp.exp(gc1)
                kb0, kb1 = k0 * beta0, k1 * beta1
                dec_out[un] = jnp.exp(jnp.where(incl, gc - gr, NEG_BIG))
                lhs_out[un] = jnp.concatenate([jnp.concatenate([kb0, kb1], axis=1),
                                               jnp.concatenate([q0, q1], axis=1)], axis=0).astype(BF16)
                rhs_out[un] = jnp.concatenate([jnp.concatenate([v0 * beta0, kb0 * e0], axis=1),
                                               jnp.concatenate([v1 * beta1, kb1 * e1], axis=1)],
                                              axis=0).astype(BF16)
                kdt_out[un] = (k_ts[p] * jnp.exp(gc[last:last + 1] - gr)).astype(BF16)
                qe_out[un] = jnp.concatenate([q0 * e0, q1 * e1], axis=0).astype(BF16)
                eg_out[un] = jnp.concatenate(
                    [jnp.broadcast_to(jnp.exp(gc0[last:last + 1]), (1, BA_PAD)),
                     jnp.broadcast_to(jnp.exp(gc1[last:last + 1]), (1, BA_PAD)),
                     jnp.zeros((HALO - 2, BA_PAD), F32)], axis=0)

    for refs in out_refs:
        refs[3][...] = jnp.zeros_like(refs[3])
    _run_interleaved([pair_unit(ci, d, p) for ci in range(GDN_TILE_CHUNKS) for d in range(2)
                      for p in range(pairs)])
    for ci in range(GDN_TILE_CHUNKS):
        chunk(ci)


def _gdn_local_kernel(n_lat_tiles, n_all_tiles, n_tiles, x_ref, prev_ref, next_ref, ba_ref, cw_ref, alog_ref,
                      dtb_ref, uf_ref, wqf_ref, akf_ref, egf_ref, ub_ref, wqb_ref, akb_ref, egb_ref, *stage_refs):
    s = pl.program_id(0)
    seg_start, seg_end = _segment_edges(jnp.minimum(s, n_tiles - 1) % n_all_tiles, n_lat_tiles, n_all_tiles)
    out_refs = ((uf_ref, wqf_ref, akf_ref, egf_ref), (ub_ref, wqb_ref, akb_ref, egb_ref))
    half = len(stage_refs) // 2
    sets = (stage_refs[:half], stage_refs[half:])

    @pl.when(s == 0)
    def _():
        for r in sets[1]:
            r[...] = jnp.zeros_like(r)

    for parity in range(2):
        @pl.when(s % 2 == parity)
        def _():
            _gdn_local_body(seg_start, seg_end, x_ref, prev_ref, next_ref, ba_ref, cw_ref, alog_ref, dtb_ref,
                            out_refs, sets[1 - parity], sets[parity])


def _gdn_local_call(qkv, ba, conv_w, alog_vec, dtb_vec, n_lat_tok):
    bsz, t_all, width = qkv.shape
    c = GDN_CHUNK
    tile = GDN_TILE_CHUNKS * c
    n_all = t_all // tile
    n_lat = n_lat_tok // tile
    per_tile = tile // HALO
    last_halo = t_all // HALO - 1
    n_chunks = t_all // c
    n_tiles = bsz * n_all
    pairs = GDN_HEADS // 2
    units = GDN_TILE_CHUNKS * 2 * pairs
    t_in = lambda s: jnp.minimum(s, n_tiles - 1)
    t_out = lambda s: jnp.maximum(s - 1, 0)
    in_tok = lambda w: pl.BlockSpec((1, tile, w), lambda s: (t_in(s) // n_all, t_in(s) % n_all, 0))
    cblk = lambda rows, w: pl.BlockSpec((GDN_TILE_CHUNKS, 1, rows, w),
                                        lambda s: (t_out(s) % n_all, t_out(s) // n_all, 0, 0))
    vec = pl.BlockSpec((1, BA_PAD), lambda s: (0, 0))
    outs = [(jax.ShapeDtypeStruct((n_chunks, bsz, c, GDN_WIDTH), F32), cblk(c, GDN_WIDTH)),
            (jax.ShapeDtypeStruct((n_chunks, bsz, 2 * c, GDN_WIDTH), BF16), cblk(2 * c, GDN_WIDTH)),
            (jax.ShapeDtypeStruct((n_chunks, bsz, 3 * c, GDN_HEADS * c), BF16), cblk(3 * c, GDN_HEADS * c)),
            (jax.ShapeDtypeStruct((n_chunks, bsz, HALO, BA_PAD), F32), cblk(HALO, BA_PAD))] * 2
    stage = [pltpu.VMEM((units, 2 * c, 2 * GDN_DK), BF16),
             pltpu.VMEM((GDN_TILE_CHUNKS * pairs, 2 * c, GDN_DK), BF16),
             pltpu.VMEM((units, 2 * c, 2 * GDN_DK), BF16),
             pltpu.VMEM((units, c, 2 * c), F32),
             pltpu.VMEM((units, GDN_DK, 2 * c), BF16),
             pltpu.VMEM((units, 2 * c, GDN_DK), BF16),
             pltpu.VMEM((units, HALO, BA_PAD), F32)]
    return pl.pallas_call(
        functools.partial(_gdn_local_kernel, n_lat, n_all, n_tiles),
        grid=(n_tiles + 1,),
        in_specs=[in_tok(width),
                  pl.BlockSpec((1, HALO, width),
                               lambda s: (t_in(s) // n_all, jnp.maximum((t_in(s) % n_all) * per_tile - 1, 0), 0)),
                  pl.BlockSpec((1, HALO, width),
                               lambda s: (t_in(s) // n_all,
                                          jnp.minimum((t_in(s) % n_all + 1) * per_tile, last_halo), 0)),
                  in_tok(BA_PAD), pl.BlockSpec((CONV_W, width), lambda s: (0, 0)), vec, vec],
        out_specs=[o[1] for o in outs],
        out_shape=[o[0] for o in outs],
        scratch_shapes=stage + stage,
        compiler_params=pltpu.CompilerParams(dimension_semantics=("arbitrary",),
                                             vmem_limit_bytes=VMEM_LIMIT),
        name="gdn_local",
    )(qkv, qkv, qkv, ba, conv_w, alog_vec, dtb_vec)


def _gdn_scan_steps(uf_ref, wqf_ref, akf_ref, egf_ref, ub_ref, wqb_ref, akb_ref, egb_ref,
                    of_ref, ob_ref, sf_ref, sb_ref):
    c = GDN_CHUNK
    bsz = uf_ref.shape[1]
    hs = lambda h: slice(h * GDN_DK, (h + 1) * GDN_DK)
    for sub in range(GDN_SCAN_CHUNKS):
        chains = [(refs, ci, b, h)
                  for refs, ci in (((uf_ref, wqf_ref, akf_ref, egf_ref, of_ref, sf_ref), sub),
                                   ((ub_ref, wqb_ref, akb_ref, egb_ref, ob_ref, sb_ref),
                                    GDN_SCAN_CHUNKS - 1 - sub))
                  for b in range(bsz) for h in range(GDN_HEADS)]
        states = [refs[5][b * GDN_HEADS + h] for refs, ci, b, h in chains]
        with_s = yield [lambda refs=refs, ci=ci, b=b, h=h, s=s:
                        jnp.dot(refs[1][ci, b, :, hs(h)], s.astype(BF16), preferred_element_type=F32)
                        for (refs, ci, b, h), s in zip(chains, states)]
        v_new = [refs[0][ci, b, :, hs(h)] - r[:c] for (refs, ci, b, h), r in zip(chains, with_s)]
        with_v = yield [lambda refs=refs, ci=ci, b=b, h=h, v=v:
                        jnp.dot(refs[2][ci, b, :, h * c:(h + 1) * c], v.astype(BF16), preferred_element_type=F32)
                        for (refs, ci, b, h), v in zip(chains, v_new)]
        for (refs, ci, b, h), s, r, a in zip(chains, states, with_s, with_v):
            refs[4][b, ci * c:(ci + 1) * c, hs(h)] = r[c:] + a[:c]
            refs[5][b * GDN_HEADS + h] = s * refs[3][ci, b, h:h + 1, :] + a[c:]


def _lin_scan(a, b, reverse):
    r = a.shape[0]
    row = lax.broadcasted_iota(jnp.int32, a.shape, 0)
    s = 1
    while s < r:
        ok = (row < r - s) if reverse else (row >= s)
        shift = r - s if reverse else s
        a_sh = jnp.where(ok, pltpu.roll(a, shift, 0), 1.0)
        b_sh = jnp.where(ok, pltpu.roll(b, shift, 0), 0.0)
        b = a * b_sh + b
        a = a * a_sh
        s *= 2
    return a, b


def _lru_ctx_direction(reverse, n_slabs, slab, x_ref, prev_ref, next_ref, cw, cb, wr_ref, wi_ref,
                       br, bi, lam, carry_ref):
    bsz, r, width = x_ref.shape
    step = pl.program_id(0)
    has_prev = slab > 0
    has_next = slab < n_slabs - 1
    xs = []
    for b in range(bsz):
        prev = jnp.where(has_prev, prev_ref[b], 0.0)
        nxt = jnp.where(has_next, next_ref[b], 0.0)
        xs.append(_dwconv(x_ref[b], prev, nxt, cw) + cb)
    x = jnp.concatenate(xs, axis=0)
    xb = x.astype(BF16)
    gate_r = _sigmoid(jnp.dot(xb, wr_ref[...], preferred_element_type=F32) + br)
    gate_i = _sigmoid(jnp.dot(xb, wi_ref[...], preferred_element_type=F32) + bi)
    log_a = (-LRU_C * _softplus(-lam)) * gate_r
    a = jnp.exp(log_a)
    mult = jnp.sqrt(1.0 - jnp.exp(2.0 * log_a))
    row = lax.broadcasted_iota(jnp.int32, (r, width), 0)
    is_first = jnp.logical_and(step == 0, row == ((r - 1) if reverse else 0))
    edge = 0 if reverse else r - 1
    for b in range(bsz):
        sl = slice(b * r, (b + 1) * r)
        m = jnp.where(is_first, 1.0, mult[sl])
        a_cum, h = _lin_scan(a[sl], m * (gate_i[sl] * x[sl]), reverse)
        h = h + a_cum * carry_ref[b:b + 1]
        carry_ref[b:b + 1] = h[edge:edge + 1]


def _lru_ctx_kernel(n_slabs, xf_ref, pf_ref, nf_ref, xb_ref, pb_ref, nb_ref, cw_ref, cb_ref,
                    wrf_ref, wif_ref, wrb_ref, wib_ref, bg_ref, lam_ref, fin_ref, cf_ref, cbk_ref):
    step = pl.program_id(0)

    @pl.when(step == 0)
    def _():
        cf_ref[...] = jnp.zeros_like(cf_ref)
        cbk_ref[...] = jnp.zeros_like(cbk_ref)

    cw = cw_ref[...]
    cb = cb_ref[...]
    _lru_ctx_direction(False, n_slabs, step, xf_ref, pf_ref, nf_ref, cw, cb, wrf_ref, wif_ref,
                       bg_ref[0:1], bg_ref[1:2], lam_ref[0:1], cf_ref)
    _lru_ctx_direction(True, n_slabs, n_slabs - 1 - step, xb_ref, pb_ref, nb_ref, cw, cb, wrb_ref, wib_ref,
                       bg_ref[2:3], bg_ref[3:4], lam_ref[1:2], cbk_ref)
    fin_ref[0] = cf_ref[...]
    fin_ref[1] = cbk_ref[...]


def _lru_ctx_call(rx, n_lat_tok, conv_w, conv_b, w_dense, b_gate, lam):
    bsz, t_all, width = rx.shape
    r = GRID_W
    per = r // HALO
    n_slabs = (t_all - n_lat_tok) // r
    first = n_lat_tok // r
    main = lambda f: pl.BlockSpec((bsz, r, width), lambda s: (0, first + f(s), 0))
    prev = lambda f: pl.BlockSpec((bsz, HALO, width), lambda s: (0, (first + f(s)) * per - 1, 0))
    nxt = lambda f: pl.BlockSpec(
        (bsz, HALO, width), lambda s: (0, jnp.minimum((first + f(s) + 1) * per, t_all // HALO - 1), 0))
    fwd = lambda s: s
    bwd = lambda s: n_slabs - 1 - s
    full = lambda shape: pl.BlockSpec(shape, lambda s: (0,) * len(shape))
    carry = pltpu.VMEM((bsz, width), F32)
    return pl.pallas_call(
        functools.partial(_lru_ctx_kernel, n_slabs),
        grid=(n_slabs,),
        in_specs=[main(fwd), prev(fwd), nxt(fwd), main(bwd), prev(bwd), nxt(bwd),
                  full((CONV_W, width)), full((1, width))]
                 + [full((width, width))] * 4
                 + [full((4, width)), full((2, width))],
        out_specs=full((2, bsz, width)),
        out_shape=jax.ShapeDtypeStruct((2, bsz, width), F32),
        scratch_shapes=[carry, carry],
        compiler_params=pltpu.CompilerParams(dimension_semantics=("arbitrary",),
                                             vmem_limit_bytes=VMEM_LIMIT),
        name="lru_ctx",
    )(rx, rx, rx, rx, rx, rx, conv_w, conv_b, *w_dense, b_gate, lam)


def _lru_grid_gates(x_ref, cw_ref, cb_ref, w_ref, bg_ref, lam_ref, af_ref, bf_ref, ab_ref, bb_ref):
    n = GRID_W
    rp = LRU_ROWS_PER_PASS
    lanes = x_ref.shape[-1]
    crow = lax.broadcasted_iota(jnp.int32, (n, lanes), 0)
    cw = cw_ref[...]
    cb = cb_ref[...]

    def grid_rows(lo, hi):
        parts = []
        for r in range(lo, min(hi, 0)):
            parts.append(jnp.where(crow == 0, 0.0, pltpu.roll(x_ref[0, (n + r) * n:(n + r + 1) * n, :], 1, 0)))
        if max(lo, 0) < min(hi, n):
            parts.append(x_ref[0, max(lo, 0) * n:min(hi, n) * n, :])
        for r in range(max(lo, n), hi):
            parts.append(jnp.where(crow == n - 1, 0.0,
                                   pltpu.roll(x_ref[0, (r - n) * n:(r - n + 1) * n, :], n - 1, 0)))
        return parts[0] if len(parts) == 1 else jnp.concatenate(parts, axis=0)

    for r0 in range(0, n, rp):
        sl = slice(r0 * n, (r0 + rp) * n)
        xc = (grid_rows(r0 - 2, r0 + rp - 2) * cw[0:1] + grid_rows(r0 - 1, r0 + rp - 1) * cw[1:2]
              + grid_rows(r0, r0 + rp) * cw[2:3] + grid_rows(r0 + 1, r0 + rp + 1) * cw[3:4] + cb)
        xb = xc.astype(BF16)
        gates = yield [lambda i=i: jnp.dot(xb, w_ref[i, 0], preferred_element_type=F32) for i in range(4)]
        for d, (a_ref, b_ref) in enumerate(((af_ref, bf_ref), (ab_ref, bb_ref))):
            gate_r = _sigmoid(gates[2 * d] + bg_ref[2 * d:2 * d + 1])
            gate_i = _sigmoid(gates[2 * d + 1] + bg_ref[2 * d + 1:2 * d + 2])
            log_a = (-LRU_C * _softplus(-lam_ref[d:d + 1])) * gate_r
            a_ref[sl] = jnp.exp(log_a)
            b_ref[sl] = jnp.sqrt(1.0 - jnp.exp(2.0 * log_a)) * (gate_i * xc)


def _lru_grid_finish(h0_ref, out_ref, af_ref, bf_ref, ab_ref, bb_ref):
    n = GRID_W
    lanes = out_ref.shape[-1]
    crow = lax.broadcasted_iota(jnp.int32, (n, lanes), 0)
    hf = jnp.zeros((n, lanes), F32)
    hb = jnp.zeros((n, lanes), F32)
    pf = jnp.ones((n, lanes), F32)
    pb = jnp.ones((n, lanes), F32)
    for i in range(n):
        sf = slice(i * n, (i + 1) * n)
        sb = slice((n - 1 - i) * n, (n - i) * n)
        a = af_ref[sf]
        hf = a * hf + bf_ref[sf]
        pf = a * pf
        bf_ref[sf] = hf
        af_ref[sf] = pf
        a = ab_ref[sb]
        hb = a * hb + bb_ref[sb]
        pb = a * pb
        bb_ref[sb] = hb
        ab_ref[sb] = pb

    h0f = h0_ref[0, 0]
    h0b = h0_ref[1, 0]
    acc_a, acc_h = _lin_scan(pf, hf, False)
    in_f = jnp.where(crow == 0, h0f, pltpu.roll(acc_h + acc_a * h0f, 1, 0))
    acc_a, acc_h = _lin_scan(pb, hb, True)
    in_b = jnp.where(crow == n - 1, h0b, pltpu.roll(acc_h + acc_a * h0b, n - 1, 0))

    for r in range(n):
        sl = slice(r * n, (r + 1) * n)
        out_ref[0, sl, :] = (bf_ref[sl] + af_ref[sl] * in_f) + (bb_ref[sl] + ab_ref[sl] * in_b)


def _mixers_kernel(*refs):
    scan_in, lru_in, outs, scratch = refs[:8], refs[8:15], refs[15:18], refs[18:]
    of_ref, ob_ref, hs_ref = outs
    sf_ref, sb_ref = scratch[:2]

    @pl.when(pl.program_id(0) == 0)
    def _():
        sf_ref[...] = jnp.zeros_like(sf_ref)
        sb_ref[...] = jnp.zeros_like(sb_ref)

    x_ref, h0_ref = lru_in[:2]
    _run_interleaved([_gdn_scan_steps(*scan_in, of_ref, ob_ref, sf_ref, sb_ref),
                      _lru_grid_gates(x_ref, *lru_in[2:], *scratch[2:])])
    _lru_grid_finish(h0_ref, hs_ref, *scratch[2:])


def _mixers_call(local, rx, n_lat_tok, h0, conv_w, conv_b, w_tiles, b_gate, lam):
    n_chunks, bsz = local[0].shape[:2]
    width = rx.shape[2]
    c = GDN_CHUNK
    g = GDN_SCAN_CHUNKS
    n_all = n_chunks // g
    n_lat = n_lat_tok // (c * g)
    n_ctx = n_all - n_lat
    fwd = lambda s: jnp.where(s < n_ctx, n_lat + s, s - n_ctx)
    bwd = lambda s: n_all - 1 - s
    fwd_out = lambda s: jnp.maximum(s - n_ctx, 0)
    bwd_out = lambda s: jnp.minimum(n_all - 1 - s, n_lat - 1)
    cblk = lambda rows, w, f: pl.BlockSpec((g, bsz, rows, w), lambda s: (f(s), 0, 0, 0))
    ins = lambda f: [cblk(c, GDN_WIDTH, f), cblk(2 * c, GDN_WIDTH, f), cblk(3 * c, GDN_HEADS * c, f),
                     cblk(HALO, BA_PAD, f)]
    tok = lambda f: pl.BlockSpec((bsz, g * c, GDN_WIDTH), lambda s: (0, f(s), 0))
    state = pltpu.VMEM((bsz * GDN_HEADS, GDN_DK, GDN_DK), F32)

    nt = width // LRU_LANES
    unit = lambda s: jnp.minimum(s, bsz * nt - 1)
    assert n_all >= bsz * nt
    lane = lambda rows: pl.BlockSpec((rows, LRU_LANES), lambda s: (0, unit(s) % nt))
    grid_tok = pl.BlockSpec((1, n_lat_tok, LRU_LANES), lambda s: (unit(s) // nt, 0, unit(s) % nt))
    lru_scratch = pltpu.VMEM((n_lat_tok, LRU_LANES), F32)
    return pl.pallas_call(
        _mixers_kernel,
        grid=(n_all,),
        in_specs=ins(fwd) + ins(bwd) + [
            grid_tok,
            pl.BlockSpec((2, 1, 1, LRU_LANES), lambda s: (0, unit(s) // nt, 0, unit(s) % nt)),
            lane(CONV_W), lane(1),
            pl.BlockSpec((4, 1, LRU_LANES, LRU_LANES), lambda s: (0, unit(s) % nt, 0, 0)),
            lane(4), lane(2)],
        out_specs=[tok(fwd_out), tok(bwd_out), grid_tok],
        out_shape=[jax.ShapeDtypeStruct((bsz, n_lat_tok, GDN_WIDTH), F32)] * 2
                  + [jax.ShapeDtypeStruct((bsz, n_lat_tok, width), F32)],
        scratch_shapes=[state, state] + [lru_scratch] * 4,
        compiler_params=pltpu.CompilerParams(dimension_semantics=("arbitrary",),
                                             vmem_limit_bytes=VMEM_LIMIT),
        name="mixers",
    )(*local, rx, h0.reshape(2, bsz, 1, width), conv_w, conv_b, w_tiles, b_gate, lam)


def _out_ffn2_kernel(h1_ref, of_ref, ob_ref, z_ref, hs_ref, rg_ref, mod_ref, ng_ref, gnw_ref,
                     wout_ref, w1_ref, w3_ref, w2_ref, fg_ref, out_ref):
    mod = mod_ref[0]

    def sub_tile(rows):
        o = of_ref[0, rows] + ob_ref[0, rows]
        z = z_ref[0, rows]
        parts = []
        for h in range(GDN_HEADS):
            sl = slice(h * GDN_DK, (h + 1) * GDN_DK)
            oh = o[:, sl]
            parts.append(oh * lax.rsqrt(jnp.mean(oh * oh, axis=-1, keepdims=True) + EPS)
                         * gnw_ref[...] * _silu(z[:, sl]))
        parts.append(hs_ref[0, rows] * _gelu_tanh(rg_ref[0, rows]))
        mixed = jnp.concatenate(parts, axis=1).astype(BF16)
        (y,) = yield [lambda: jnp.dot(mixed, wout_ref[...], preferred_element_type=F32)]
        h2 = h1_ref[0, rows] + mod[5:6] * y
        ub = _rms_mod(h2, ng_ref[2:3], mod[6:7], mod[7:8]).astype(BF16)
        a, b = yield [lambda: jnp.dot(ub, w1_ref[...], preferred_element_type=F32),
                      lambda: jnp.dot(ub, w3_ref[...], preferred_element_type=F32)]
        g = (_silu(a) * b).astype(BF16)
        (f,) = yield [lambda: jnp.dot(g, w2_ref[...], preferred_element_type=F32)]
        h3 = h2 + FFN_RESIDUAL * mod[8:9] * f
        out_ref[0, rows] = h3 * lax.rsqrt(jnp.mean(h3 * h3, axis=-1, keepdims=True) + EPS) * fg_ref[...]

    _run_interleaved([sub_tile(slice(i * TOKEN_TILE, (i + 1) * TOKEN_TILE))
                      for i in range(out_ref.shape[1] // TOKEN_TILE)])


def _out_ffn2_call(h1, o_f, o_b, z, h_sum, rg, mods, norm_g, gdn_norm_w, w_out, w1, w3, w2, final_g):
    bsz, seq, _ = o_f.shape
    rows = OUT_SUB_TILES * TOKEN_TILE
    tok = lambda w: pl.BlockSpec((1, rows, w), lambda b, j: (b, j, 0))
    return pl.pallas_call(
        _out_ffn2_kernel,
        grid=(bsz, seq // rows),
        in_specs=[tok(D_MODEL), tok(GDN_WIDTH), tok(GDN_WIDTH), tok(GDN_WIDTH),
                  tok(LRU_WIDTH), tok(LRU_WIDTH),
                  pl.BlockSpec((1, N_MOD, D_MODEL), lambda b, j: (b + 1, 0, 0)),
                  _resident((3, D_MODEL)), _resident((1, GDN_DK)),
                  _resident((D_MODEL, D_MODEL)),
                  _resident((None, D_MODEL, D_FF), (1, 0, 0)), _resident((None, D_MODEL, D_FF), (1, 0, 0)),
                  _resident((None, D_FF, D_MODEL), (1, 0, 0)),
                  _resident((1, D_MODEL))],
        out_specs=tok(D_MODEL),
        out_shape=jax.ShapeDtypeStruct((bsz, seq, D_MODEL), F32),
        compiler_params=pltpu.CompilerParams(dimension_semantics=("arbitrary", "arbitrary"),
                                             vmem_limit_bytes=VMEM_LIMIT),
        name="out_ffn2",
    )(h1, o_f, o_b, z, h_sum, rg, mods, norm_g, gdn_norm_w, w_out, w1, w3, w2, final_g)


def _block_diag(w):
    nb, bw, _ = w.shape
    eye = jnp.eye(nb, dtype=w.dtype)
    return jnp.einsum('ncd,nm->ncmd', w, eye).reshape(nb * bw, nb * bw)


def kernel(x, c, ctx, c_ctx, w_ada, b_ada, norm_g, ffn_w1, ffn_w3, ffn_w2, w_in, w_out, gdn_conv_w, gdn_a_log,
           gdn_dt_bias, gdn_norm_w, lru_conv_w, lru_conv_b, lru_w_gate, lru_b_gate, lru_lambda, final_norm_g):
    bsz, seq, _ = x.shape
    assert w_ada.shape[0] == 1 and seq == GRID_W * GRID_W and ctx.shape[1] == TOKEN_TILE

    rows = -(-(bsz + 1) // 8) * 8
    cvec = jnp.concatenate([c_ctx[None, :], c, jnp.zeros((rows - bsz - 1, D_MODEL), F32)], axis=0)
    mods = _ada_call(cvec, w_ada[0], b_ada).reshape(rows, N_MOD, D_MODEL)

    w_in_bf = w_in[0].astype(BF16)
    o_ba = P_HEAD_COLS
    o_rx = o_ba + 4 * GDN_HEADS
    w_tail = jnp.concatenate([w_in_bf[:, o_rx:], w_in_bf[:, o_ba:o_rx],
                              jnp.zeros((D_MODEL, BA_PAD - 4 * GDN_HEADS), BF16)], axis=1)
    w1 = ffn_w1[0].astype(BF16)
    w3 = ffn_w3[0].astype(BF16)
    w2 = ffn_w2[0].astype(BF16)

    h1, qkv, z, rx, rg, ba = _ffn1_in_call(x, ctx, mods, norm_g[0], w1, w3, w2, w_in_bf, w_tail)

    pad_g = lambda t: jnp.zeros((1, BA_PAD), F32).at[0, 2 * GDN_HEADS:4 * GDN_HEADS].set(t.reshape(-1))
    local = _gdn_local_call(qkv, ba, gdn_conv_w[0], pad_g(gdn_a_log[0]), pad_g(gdn_dt_bias[0]), seq)

    wg = lru_w_gate[0]
    w_dense = [_block_diag(wg[d, g]).astype(BF16) for d in range(2) for g in range(2)]
    b_gate = lru_b_gate[0].reshape(4, LRU_WIDTH)
    lam = lru_lambda[0]
    cw, cb = lru_conv_w[0], lru_conv_b
    h_ctx = _lru_ctx_call(rx, seq, cw, cb, w_dense, b_gate, lam)
    nt = LRU_WIDTH // LRU_LANES
    w_tiles = jnp.stack([jnp.stack([w[t * LRU_LANES:(t + 1) * LRU_LANES, t * LRU_LANES:(t + 1) * LRU_LANES]
                                    for t in range(nt)]) for w in w_dense])
    o_f, o_b, h_sum = _mixers_call(local, rx, seq, h_ctx, cw, cb, w_tiles, b_gate, lam)

    return _out_ffn2_call(h1, o_f, o_b, z, h_sum, rg, mods, norm_g[0], gdn_norm_w,
                          w_out[0].astype(BF16), w1, w3, w2, final_norm_g[None, :])
```

```python
import functools

import jax
import jax.numpy as jnp
from jax import lax
from jax.experimental import pallas as pl
from jax.experimental.pallas import tpu as pltpu

D_MODEL = 1024
D_FF = 2816
N_MOD = 9
EPS = 1e-6
FFN_RESIDUAL = 0.5

GDN_WIDTH = 512
GDN_HEADS = 4
GDN_DK = 128
GDN_CHUNK = 64
CONV_W = 4
LRU_WIDTH = 512
LRU_C = 8.0
GRID_W = 64

TOKEN_TILE = 256
GDN_TILE_CHUNKS = TOKEN_TILE // GDN_CHUNK
GDN_SCAN_CHUNKS = 4
OUT_SUB_TILES = 2
FFN1_SUB_TILES = 2
WEIGHT_CAST_BLOCKS = 8
HALO = 8
BA_PAD = 128
P_HEAD_COLS = 4 * GDN_WIDTH
P_TAIL_COLS = 2 * LRU_WIDTH + BA_PAD
LRU_LANES = 128
LRU_ROWS_PER_PASS = 8
VMEM_LIMIT = 56 * 1024 * 1024

BF16 = jnp.bfloat16
F32 = jnp.float32
NEG_BIG = -1e30
NT_DIMS = (((1,), (1,)), ((), ()))


def _mm(a, b):
    return jnp.dot(a.astype(BF16), b.astype(BF16), preferred_element_type=F32)


def _sigmoid(x):
    return 1.0 / (1.0 + jnp.exp(-x))


def _silu(x):
    return x * _sigmoid(x)


def _softplus(x):
    return jnp.maximum(x, 0.0) + jnp.log(1.0 + jnp.exp(-jnp.abs(x)))


def _gelu_tanh(x):
    return 0.5 * x * (1.0 + jnp.tanh(0.7978845608028654 * (x + 0.044715 * x * x * x)))


def _rms_mod(h, g, shift, scale):
    y = h * lax.rsqrt(jnp.mean(h * h, axis=-1, keepdims=True) + EPS) * g
    return y * (1.0 + scale) + shift


def _swiglu(u, w1_ref, w3_ref, w2_ref):
    ub = u.astype(BF16)
    a = jnp.dot(ub, w1_ref[...], preferred_element_type=F32)
    b = jnp.dot(ub, w3_ref[...], preferred_element_type=F32)
    g = (_silu(a) * b).astype(BF16)
    return jnp.dot(g, w2_ref[...], preferred_element_type=F32)


def _dwconv(x, prev, nxt, w):
    t = x.shape[0]
    ext = jnp.concatenate([prev, x, nxt], axis=0)
    shifted = lambda k: pltpu.roll(ext, k % (t + 2 * HALO), 0)[HALO:HALO + t]
    return shifted(2) * w[0:1] + shifted(1) * w[1:2] + x * w[2:3] + shifted(-1) * w[3:4]


def _resident(shape, index=None):
    index = index or (0,) * len(shape)
    return pl.BlockSpec(shape, lambda *_: index, pipeline_mode=pl.Buffered(1))


def _ada_kernel(c_ref, w_ref, b_ref, *refs):
    n_cast = (len(refs) - 1) // 2
    o_ref = refs[n_cast]
    o_ref[...] = _mm(_silu(c_ref[...]), w_ref[...]) + b_ref[...]
    _cast_blocks(refs[:n_cast] + refs[n_cast + 1:])


def _cast_blocks(refs):
    half = len(refs) // 2
    for src, dst in zip(refs[:half], refs[half:]):
        dst[...] = src[...].astype(BF16)


def _cast_specs(weights, n_blocks):
    ins, outs, shapes = [], [], []
    for w, lead, nb in weights:
        rows, cols = w.shape[-2:]
        assert rows % nb == 0 and nb <= n_blocks
        ins.append(pl.BlockSpec((None,) * len(lead) + (rows // nb, cols),
                                lambda s, lead=lead, nb=nb: lead + (jnp.minimum(s, nb - 1), 0)))
        outs.append(pl.BlockSpec((rows // nb, cols), lambda s, nb=nb: (jnp.minimum(s, nb - 1), 0)))
        shapes.append(jax.ShapeDtypeStruct((rows, cols), BF16))
    return ins, outs, shapes


def _ada_call(cvec, w_ada, b_ada, cast_weights):
    rows = cvec.shape[0]
    cast_in, cast_out, cast_shapes = _cast_specs(cast_weights, N_MOD)
    return pl.pallas_call(
        _ada_kernel,
        grid=(N_MOD,),
        in_specs=[pl.BlockSpec((rows, D_MODEL), lambda k: (0, 0)),
                  pl.BlockSpec((D_MODEL, D_MODEL), lambda k: (0, k)),
                  pl.BlockSpec((1, D_MODEL), lambda k: (0, k))] + cast_in,
        out_specs=[pl.BlockSpec((rows, D_MODEL), lambda k: (0, k))] + cast_out,
        out_shape=[jax.ShapeDtypeStruct((rows, N_MOD * D_MODEL), F32)] + cast_shapes,
        compiler_params=pltpu.CompilerParams(dimension_semantics=("arbitrary",),
                                             vmem_limit_bytes=VMEM_LIMIT),
        name="ada",
    )(cvec, w_ada, b_ada, *[w for w, _, _ in cast_weights])


def _segment_edges(j, n_lat_tiles, n_all_tiles):
    return (jnp.logical_or(j == 0, j == n_lat_tiles),
            jnp.logical_or(j == n_lat_tiles - 1, j == n_all_tiles - 1))


def _ffn1_in_kernel(n_lat_tiles, n_all_tiles, *refs):
    n_sub = FFN1_SUB_TILES
    ins, refs = refs[:3 * n_sub], refs[3 * n_sub:]
    ng_ref, w1_ref, w3_ref, w2_ref, win_ref, wtail_ref = refs[:6]
    n_cast = (len(refs) - 12) // 2
    h1_ref, qkv_ref, z_ref, rx_ref, rg_ref, ba_ref = refs[6 + n_cast:12 + n_cast]
    _cast_blocks(refs[6:6 + n_cast] + refs[12 + n_cast:])
    s = pl.program_id(0)

    def sub_tile(i):
        x_ref, ctx_ref, mod_ref = ins[3 * i:3 * i + 3]
        rows = slice(i * TOKEN_TILE, (i + 1) * TOKEN_TILE)
        h = jnp.where((s * n_sub + i) % n_all_tiles == n_lat_tiles, ctx_ref[0], x_ref[0])
        mod = mod_ref[0]
        ub = _rms_mod(h, ng_ref[0:1], mod[0:1], mod[1:2]).astype(BF16)
        a, b = yield [lambda: jnp.dot(ub, w1_ref[...], preferred_element_type=F32),
                      lambda: jnp.dot(ub, w3_ref[...], preferred_element_type=F32)]
        g = (_silu(a) * b).astype(BF16)
        (f,) = yield [lambda: jnp.dot(g, w2_ref[...], preferred_element_type=F32)]
        h1 = h + FFN_RESIDUAL * mod[2:3] * f
        h1_ref[rows] = h1
        ub2 = _rms_mod(h1, ng_ref[1:2], mod[3:4], mod[4:5]).astype(BF16)
        head, tail = yield [lambda: jnp.dot(ub2, win_ref[...], preferred_element_type=F32),
                            lambda: jnp.dot(ub2, wtail_ref[...], preferred_element_type=F32)]
        qkv_ref[rows] = head[:, :3 * GDN_WIDTH]
        z_ref[rows] = head[:, 3 * GDN_WIDTH:]
        rx_ref[rows] = tail[:, :LRU_WIDTH]
        rg_ref[rows] = tail[:, LRU_WIDTH:2 * LRU_WIDTH]
        ba_ref[rows] = tail[:, 2 * LRU_WIDTH:]

    _run_interleaved([sub_tile(i) for i in range(n_sub)])


def _ffn1_in_call(x, ctx, mods, norm_g, w1, w3, w2, w_in_bf, w_tail, cast_weights):
    bsz, seq, _ = x.shape
    n_lat = seq // TOKEN_TILE
    n_all = n_lat + 1
    t_all = n_all * TOKEN_TILE
    n_sub = FFN1_SUB_TILES
    assert (bsz * n_all) % n_sub == 0
    per_tile = []
    for i in range(n_sub):
        tile = lambda s, i=i: s * n_sub + i
        per_tile += [
            pl.BlockSpec((1, TOKEN_TILE, D_MODEL),
                         lambda s, t=tile: (t(s) // n_all, jnp.minimum(t(s) % n_all, n_lat - 1), 0)),
            pl.BlockSpec((1, TOKEN_TILE, D_MODEL), lambda s, t=tile: (t(s) // n_all, 0, 0),
                         pipeline_mode=pl.Buffered(1)),
            pl.BlockSpec((1, N_MOD, D_MODEL),
                         lambda s, t=tile: (jnp.where(t(s) % n_all == n_lat, 0, t(s) // n_all + 1), 0, 0))]
    widths = (D_MODEL, 3 * GDN_WIDTH, GDN_WIDTH, LRU_WIDTH, LRU_WIDTH, BA_PAD)
    n_steps = bsz * n_all // n_sub
    cast_in, cast_out, cast_shapes = _cast_specs(cast_weights, n_steps)
    outs = pl.pallas_call(
        functools.partial(_ffn1_in_kernel, n_lat, n_all),
        grid=(n_steps,),
        in_specs=per_tile + [
            _resident((3, D_MODEL)),
            _resident((D_MODEL, D_FF)), _resident((D_MODEL, D_FF)), _resident((D_FF, D_MODEL)),
            _resident((D_MODEL, P_HEAD_COLS)), _resident((D_MODEL, P_TAIL_COLS))] + cast_in,
        out_specs=[pl.BlockSpec((n_sub * TOKEN_TILE, w), lambda s: (s, 0)) for w in widths] + cast_out,
        out_shape=[jax.ShapeDtypeStruct((bsz * t_all, w), F32) for w in widths] + cast_shapes,
        compiler_params=pltpu.CompilerParams(dimension_semantics=("arbitrary",),
                                             vmem_limit_bytes=VMEM_LIMIT),
        name="ffn1_in",
    )(*([x, ctx, mods] * n_sub), norm_g, w1, w3, w2, w_in_bf, w_tail, *[w for w, _, _ in cast_weights])
    return [o.reshape(bsz, t_all, o.shape[-1]) for o in outs[:len(widths)]] + list(outs[len(widths):])


def _pair_block_diag(x, first):
    return jnp.concatenate([jnp.where(first, x, 0.0), jnp.where(first, 0.0, x)], axis=0)


def _run_interleaved(coroutines):
    active = [[gen, next(gen)] for gen in coroutines]
    while active:
        still = []
        for item in active:
            results = [dot() for dot in item[1]]
            try:
                item[1] = item[0].send(results)
                still.append(item)
            except StopIteration:
                pass
        active = still


def _gdn_local_body(seg_start, seg_end, x_ref, prev_ref, next_ref, ba_ref, cw_ref, alog_ref, dtb_ref,
                    out_refs, stage_in, stage_out):
    c = GDN_CHUNK
    pairs = GDN_HEADS // 2
    row = lax.broadcasted_iota(jnp.int32, (c, 2 * c), 0)
    lane2 = lax.broadcasted_iota(jnp.int32, (c, 2 * c), 1)
    col = lane2 & (c - 1)
    first = lane2 < c
    first_row = first[0:1]
    eye2 = (row == col).astype(F32)
    zeros_k = jnp.zeros((c, GDN_DK), BF16)
    unit_index = lambda ci, d, p: (ci * 2 + d) * pairs + p

    def coupling(k):
        return jnp.logical_and((row >> k) != (col >> k), (row >> (k + 1)) == (col >> (k + 1)))

    lhs_in, kr_in, rhs_in, dec_in, kdt_in, qe_in, eg_in = stage_in

    def pair_unit(ci, d, p):
        u_ref, wq_ref, ak_ref, eg_ref = out_refs[d]
        un = unit_index(ci, d, p)
        strict = (row < col) if d else (row > col)
        k_rows = kr_in[ci * pairs + p]
        k_bd = jnp.concatenate([jnp.concatenate([k_rows[:c], zeros_k], axis=1),
                                jnp.concatenate([zeros_k, k_rows[c:]], axis=1)], axis=0)
        lhs = lhs_in[un]
        (prods,) = yield [lambda: lax.dot_general(lhs, k_bd, NT_DIMS, preferred_element_type=F32)]
        decay = dec_in[un]
        a = jnp.where(strict, prods[:c] * decay, 0.0)
        ps = slice(p * 2 * c, (p + 1) * 2 * c)
        ak_ref[ci, 0, 0:c, ps] = (prods[c:] * decay).astype(BF16)
        ak_ref[ci, 0, c:3 * c, ps] = kdt_in[un]
        t = eye2 - jnp.where(coupling(0), a, 0.0)
        k = 1
        while 2 ** k < c:
            l = jnp.where(coupling(k), a, 0.0)
            t_bd = _pair_block_diag(t, first)
            (lt,) = yield [lambda: _mm(l, t_bd)]
            lt_bd = _pair_block_diag(lt, first)
            (tlt,) = yield [lambda: _mm(t, lt_bd)]
            t = t - tlt
            k += 1
        t_bd = _pair_block_diag(t, first).astype(BF16)
        rhs = rhs_in[un]
        (sol,) = yield [lambda: jnp.dot(t_bd, rhs, preferred_element_type=F32)]
        for i in range(2):
            h = 2 * p + i
            hs = slice(h * GDN_DK, (h + 1) * GDN_DK)
            u_ref[ci, 0, :, hs] = sol[i * c:(i + 1) * c, :GDN_DK]
            wq_ref[ci, 0, 0:c, hs] = sol[i * c:(i + 1) * c, GDN_DK:].astype(BF16)
            wq_ref[ci, 0, c:2 * c, hs] = qe_in[un, i * c:(i + 1) * c]
            eg_ref[ci, 0, h:h + 1, :] = eg_in[un, i:i + 1]

    lhs_out, kr_out, rhs_out, dec_out, kdt_out, qe_out, eg_out = stage_out
    y = _silu(_dwconv(x_ref[0], jnp.where(seg_start, 0.0, prev_ref[0]), jnp.where(seg_end, 0.0, next_ref[0]),
                      cw_ref[...]))
    ba = ba_ref[0]
    lane = lax.broadcasted_iota(jnp.int32, (1, BA_PAD), 1)
    is_g = jnp.logical_and(lane >= 2 * GDN_HEADS, lane < 4 * GDN_HEADS)
    beta_all = _sigmoid(ba)
    g_all = jnp.where(is_g, -jnp.exp(alog_ref[...]) * _softplus(ba + dtb_ref[...]), 0.0)

    def chunk(ci):
        rs = slice(ci * c, (ci + 1) * c)
        heads = []
        for h in range(GDN_HEADS):
            lo = h * GDN_DK
            q = y[rs, lo:lo + GDN_DK]
            k = y[rs, GDN_WIDTH + lo:GDN_WIDTH + lo + GDN_DK]
            v = y[rs, 2 * GDN_WIDTH + lo:2 * GDN_WIDTH + lo + GDN_DK]
            q = q * (lax.rsqrt(jnp.sum(q * q, axis=-1, keepdims=True) + EPS) * (GDN_DK ** -0.5))
            k = k * lax.rsqrt(jnp.sum(k * k, axis=-1, keepdims=True) + EPS)
            heads.append((q, k, v))
        g = g_all[rs]
        k_rows = [jnp.concatenate([heads[2 * p][1], heads[2 * p + 1][1]], axis=0) for p in range(pairs)]
        k_ts = [kr.T for kr in k_rows]
        grow = lax.broadcasted_iota(jnp.int32, g.shape, 0)
        gcum_f = g
        step = 1
        while step < c:
            gcum_f = gcum_f + jnp.where(grow >= step, pltpu.roll(gcum_f, step, 0), 0.0)
            step *= 2
        gcum_b = gcum_f[c - 1:c] - gcum_f + g
        gcum_t = jnp.concatenate([gcum_f, gcum_b], axis=0).T
        beta_c = beta_all[rs]
        for p in range(pairs):
            kr_out[ci * pairs + p] = k_rows[p].astype(BF16)
        for d in range(2):
            gcum = gcum_b if d else gcum_f
            incl = (row <= col) if d else (row >= col)
            last = 0 if d else c - 1
            for p in range(pairs):
                un = unit_index(ci, d, p)
                (q0, k0, v0), (q1, k1, v1) = heads[2 * p], heads[2 * p + 1]
                cb = d * GDN_HEADS + 2 * p
                cg = 2 * GDN_HEADS + cb
                beta0, beta1 = beta_c[:, cb:cb + 1], beta_c[:, cb + 1:cb + 2]
                gc0, gc1 = gcum[:, cg:cg + 1], gcum[:, cg + 1:cg + 2]
                gt0, gt1 = gcum_t[cg:cg + 1], gcum_t[cg + 1:cg + 2]
                gr = (jnp.where(first_row, pltpu.roll(gt0, c, 1), gt1) if d
                      else jnp.where(first_row, gt0, pltpu.roll(gt1, c, 1)))
                gc = jnp.where(first, gc0, gc1)
                e0, e1 = jnp.exp(gc0), jnp.exp(gc1)
                kb0, kb1 = k0 * beta0, k1 * beta1
                dec_out[un] = jnp.exp(jnp.where(incl, gc - gr, NEG_BIG))
                lhs_out[un] = jnp.concatenate([jnp.concatenate([kb0, kb1], axis=1),
                                               jnp.concatenate([q0, q1], axis=1)], axis=0).astype(BF16)
                rhs_out[un] = jnp.concatenate([jnp.concatenate([v0 * beta0, kb0 * e0], axis=1),
                                               jnp.concatenate([v1 * beta1, kb1 * e1], axis=1)],
                                              axis=0).astype(BF16)
                kdt_out[un] = (k_ts[p] * jnp.exp(gc[last:last + 1] - gr)).astype(BF16)
                qe_out[un] = jnp.concatenate([q0 * e0, q1 * e1], axis=0).astype(BF16)
                eg_out[un] = jnp.concatenate(
                    [jnp.broadcast_to(jnp.exp(gc0[last:last + 1]), (1, BA_PAD)),
                     jnp.broadcast_to(jnp.exp(gc1[last:last + 1]), (1, BA_PAD)),
                     jnp.zeros((HALO - 2, BA_PAD), F32)], axis=0)

    for refs in out_refs:
        refs[3][...] = jnp.zeros_like(refs[3])
    _run_interleaved([pair_unit(ci, d, p) for ci in range(GDN_TILE_CHUNKS) for d in range(2)
                      for p in range(pairs)])
    for ci in range(GDN_TILE_CHUNKS):
        chunk(ci)


def _gdn_local_kernel(n_lat_tiles, n_all_tiles, n_tiles, x_ref, prev_ref, next_ref, ba_ref, cw_ref, alog_ref,
                      dtb_ref, uf_ref, wqf_ref, akf_ref, egf_ref, ub_ref, wqb_ref, akb_ref, egb_ref, *stage_refs):
    s = pl.program_id(0)
    seg_start, seg_end = _segment_edges(jnp.minimum(s, n_tiles - 1) % n_all_tiles, n_lat_tiles, n_all_tiles)
    out_refs = ((uf_ref, wqf_ref, akf_ref, egf_ref), (ub_ref, wqb_ref, akb_ref, egb_ref))
    half = len(stage_refs) // 2
    sets = (stage_refs[:half], stage_refs[half:])

    @pl.when(s == 0)
    def _():
        for r in sets[1]:
            r[...] = jnp.zeros_like(r)

    for parity in range(2):
        @pl.when(s % 2 == parity)
        def _():
            _gdn_local_body(seg_start, seg_end, x_ref, prev_ref, next_ref, ba_ref, cw_ref, alog_ref, dtb_ref,
                            out_refs, sets[1 - parity], sets[parity])


def _gdn_local_call(qkv, ba, conv_w, alog_vec, dtb_vec, n_lat_tok):
    bsz, t_all, width = qkv.shape
    c = GDN_CHUNK
    tile = GDN_TILE_CHUNKS * c
    n_all = t_all // tile
    n_lat = n_lat_tok // tile
    per_tile = tile // HALO
    last_halo = t_all // HALO - 1
    n_chunks = t_all // c
    n_tiles = bsz * n_all
    pairs = GDN_HEADS // 2
    units = GDN_TILE_CHUNKS * 2 * pairs
    t_in = lambda s: jnp.minimum(s, n_tiles - 1)
    t_out = lambda s: jnp.maximum(s - 1, 0)
    in_tok = lambda w: pl.BlockSpec((1, tile, w), lambda s: (t_in(s) // n_all, t_in(s) % n_all, 0))
    cblk = lambda rows, w: pl.BlockSpec((GDN_TILE_CHUNKS, 1, rows, w),
                                        lambda s: (t_out(s) % n_all, t_out(s) // n_all, 0, 0))
    vec = pl.BlockSpec((1, BA_PAD), lambda s: (0, 0))
    outs = [(jax.ShapeDtypeStruct((n_chunks, bsz, c, GDN_WIDTH), F32), cblk(c, GDN_WIDTH)),
            (jax.ShapeDtypeStruct((n_chunks, bsz, 2 * c, GDN_WIDTH), BF16), cblk(2 * c, GDN_WIDTH)),
            (jax.ShapeDtypeStruct((n_chunks, bsz, 3 * c, GDN_HEADS * c), BF16), cblk(3 * c, GDN_HEADS * c)),
            (jax.ShapeDtypeStruct((n_chunks, bsz, HALO, BA_PAD), F32), cblk(HALO, BA_PAD))] * 2
    stage = [pltpu.VMEM((units, 2 * c, 2 * GDN_DK), BF16),
             pltpu.VMEM((GDN_TILE_CHUNKS * pairs, 2 * c, GDN_DK), BF16),
             pltpu.VMEM((units, 2 * c, 2 * GDN_DK), BF16),
             pltpu.VMEM((units, c, 2 * c), F32),
             pltpu.VMEM((units, GDN_DK, 2 * c), BF16),
             pltpu.VMEM((units, 2 * c, GDN_DK), BF16),
             pltpu.VMEM((units, HALO, BA_PAD), F32)]
    return pl.pallas_call(
        functools.partial(_gdn_local_kernel, n_lat, n_all, n_tiles),
        grid=(n_tiles + 1,),
        in_specs=[in_tok(width),
                  pl.BlockSpec((1, HALO, width),
                               lambda s: (t_in(s) // n_all, jnp.maximum((t_in(s) % n_all) * per_tile - 1, 0), 0)),
                  pl.BlockSpec((1, HALO, width),
                               lambda s: (t_in(s) // n_all,
                                          jnp.minimum((t_in(s) % n_all + 1) * per_tile, last_halo), 0)),
                  in_tok(BA_PAD), pl.BlockSpec((CONV_W, width), lambda s: (0, 0)), vec, vec],
        out_specs=[o[1] for o in outs],
        out_shape=[o[0] for o in outs],
        scratch_shapes=stage + stage,
        compiler_params=pltpu.CompilerParams(dimension_semantics=("arbitrary",),
                                             vmem_limit_bytes=VMEM_LIMIT),
        name="gdn_local",
    )(qkv, qkv, qkv, ba, conv_w, alog_vec, dtb_vec)


def _gdn_scan_steps(uf_ref, wqf_ref, akf_ref, egf_ref, ub_ref, wqb_ref, akb_ref, egb_ref,
                    of_ref, ob_ref, sf_ref, sb_ref):
    c = GDN_CHUNK
    bsz = uf_ref.shape[1]
    hs = lambda h: slice(h * GDN_DK, (h + 1) * GDN_DK)
    for sub in range(GDN_SCAN_CHUNKS):
        chains = [(refs, ci, b, h)
                  for refs, ci in (((uf_ref, wqf_ref, akf_ref, egf_ref, of_ref, sf_ref), sub),
                                   ((ub_ref, wqb_ref, akb_ref, egb_ref, ob_ref, sb_ref),
                                    GDN_SCAN_CHUNKS - 1 - sub))
                  for b in range(bsz) for h in range(GDN_HEADS)]
        states = [refs[5][b * GDN_HEADS + h] for refs, ci, b, h in chains]
        with_s = yield [lambda refs=refs, ci=ci, b=b, h=h, s=s:
                        jnp.dot(refs[1][ci, b, :, hs(h)], s.astype(BF16), preferred_element_type=F32)
                        for (refs, ci, b, h), s in zip(chains, states)]
        v_new = [refs[0][ci, b, :, hs(h)] - r[:c] for (refs, ci, b, h), r in zip(chains, with_s)]
        with_v = yield [lambda refs=refs, ci=ci, b=b, h=h, v=v:
                        jnp.dot(refs[2][ci, b, :, h * c:(h + 1) * c], v.astype(BF16), preferred_element_type=F32)
                        for (refs, ci, b, h), v in zip(chains, v_new)]
        for (refs, ci, b, h), s, r, a in zip(chains, states, with_s, with_v):
            refs[4][b, ci * c:(ci + 1) * c, hs(h)] = r[c:] + a[:c]
            refs[5][b * GDN_HEADS + h] = s * refs[3][ci, b, h:h + 1, :] + a[c:]


def _lin_scan(a, b, reverse):
    r = a.shape[0]
    row = lax.broadcasted_iota(jnp.int32, a.shape, 0)
    s = 1
    while s < r:
        ok = (row < r - s) if reverse else (row >= s)
        shift = r - s if reverse else s
        a_sh = jnp.where(ok, pltpu.roll(a, shift, 0), 1.0)
        b_sh = jnp.where(ok, pltpu.roll(b, shift, 0), 0.0)
        b = a * b_sh + b
        a = a * a_sh
        s *= 2
    return a, b


def _lru_ctx_direction(reverse, n_slabs, slab, x_ref, prev_ref, next_ref, cw, cb, wr_ref, wi_ref,
                       br, bi, lam, carry_ref):
    bsz, r, width = x_ref.shape
    step = pl.program_id(0)
    has_prev = slab > 0
    has_next = slab < n_slabs - 1
    xs = []
    for b in range(bsz):
        prev = jnp.where(has_prev, prev_ref[b], 0.0)
        nxt = jnp.where(has_next, next_ref[b], 0.0)
        xs.append(_dwconv(x_ref[b], prev, nxt, cw) + cb)
    x = jnp.concatenate(xs, axis=0)
    xb = x.astype(BF16)
    gate_r = _sigmoid(jnp.dot(xb, wr_ref[...], preferred_element_type=F32) + br)
    gate_i = _sigmoid(jnp.dot(xb, wi_ref[...], preferred_element_type=F32) + bi)
    log_a = (-LRU_C * _softplus(-lam)) * gate_r
    a = jnp.exp(log_a)
    mult = jnp.sqrt(1.0 - jnp.exp(2.0 * log_a))
    row = lax.broadcasted_iota(jnp.int32, (r, width), 0)
    is_first = jnp.logical_and(step == 0, row == ((r - 1) if reverse else 0))
    edge = 0 if reverse else r - 1
    for b in range(bsz):
        sl = slice(b * r, (b + 1) * r)
        m = jnp.where(is_first, 1.0, mult[sl])
        a_cum, h = _lin_scan(a[sl], m * (gate_i[sl] * x[sl]), reverse)
        h = h + a_cum * carry_ref[b:b + 1]
        carry_ref[b:b + 1] = h[edge:edge + 1]


def _lru_ctx_kernel(n_slabs, xf_ref, pf_ref, nf_ref, xb_ref, pb_ref, nb_ref, cw_ref, cb_ref,
                    wrf_ref, wif_ref, wrb_ref, wib_ref, bg_ref, lam_ref, fin_ref, cf_ref, cbk_ref):
    step = pl.program_id(0)

    @pl.when(step == 0)
    def _():
        cf_ref[...] = jnp.zeros_like(cf_ref)
        cbk_ref[...] = jnp.zeros_like(cbk_ref)

    cw = cw_ref[...]
    cb = cb_ref[...]
    _lru_ctx_direction(False, n_slabs, step, xf_ref, pf_ref, nf_ref, cw, cb, wrf_ref, wif_ref,
                       bg_ref[0:1], bg_ref[1:2], lam_ref[0:1], cf_ref)
    _lru_ctx_direction(True, n_slabs, n_slabs - 1 - step, xb_ref, pb_ref, nb_ref, cw, cb, wrb_ref, wib_ref,
                       bg_ref[2:3], bg_ref[3:4], lam_ref[1:2], cbk_ref)
    fin_ref[0] = cf_ref[...]
    fin_ref[1] = cbk_ref[...]


def _lru_ctx_call(rx, n_lat_tok, conv_w, conv_b, w_dense, b_gate, lam):
    bsz, t_all, width = rx.shape
    r = GRID_W
    per = r // HALO
    n_slabs = (t_all - n_lat_tok) // r
    first = n_lat_tok // r
    main = lambda f: pl.BlockSpec((bsz, r, width), lambda s: (0, first + f(s), 0))
    prev = lambda f: pl.BlockSpec((bsz, HALO, width), lambda s: (0, (first + f(s)) * per - 1, 0))
    nxt = lambda f: pl.BlockSpec(
        (bsz, HALO, width), lambda s: (0, jnp.minimum((first + f(s) + 1) * per, t_all // HALO - 1), 0))
    fwd = lambda s: s
    bwd = lambda s: n_slabs - 1 - s
    full = lambda shape: pl.BlockSpec(shape, lambda s: (0,) * len(shape))
    carry = pltpu.VMEM((bsz, width), F32)
    return pl.pallas_call(
        functools.partial(_lru_ctx_kernel, n_slabs),
        grid=(n_slabs,),
        in_specs=[main(fwd), prev(fwd), nxt(fwd), main(bwd), prev(bwd), nxt(bwd),
                  full((CONV_W, width)), full((1, width))]
                 + [full((width, width))] * 4
                 + [full((4, width)), full((2, width))],
        out_specs=full((2, bsz, width)),
        out_shape=jax.ShapeDtypeStruct((2, bsz, width), F32),
        scratch_shapes=[carry, carry],
        compiler_params=pltpu.CompilerParams(dimension_semantics=("arbitrary",),
                                             vmem_limit_bytes=VMEM_LIMIT),
        name="lru_ctx",
    )(rx, rx, rx, rx, rx, rx, conv_w, conv_b, *w_dense, b_gate, lam)


def _lru_grid_gates(x_ref, cw_ref, cb_ref, w_ref, bg_ref, lam_ref, af_ref, bf_ref, ab_ref, bb_ref):
    n = GRID_W
    rp = LRU_ROWS_PER_PASS
    lanes = x_ref.shape[-1]
    crow = lax.broadcasted_iota(jnp.int32, (n, lanes), 0)
    cw = cw_ref[...]
    cb = cb_ref[...]

    def grid_rows(lo, hi):
        parts = []
        for r in range(lo, min(hi, 0)):
            parts.append(jnp.where(crow == 0, 0.0, pltpu.roll(x_ref[0, (n + r) * n:(n + r + 1) * n, :], 1, 0)))
        if max(lo, 0) < min(hi, n):
            parts.append(x_ref[0, max(lo, 0) * n:min(hi, n) * n, :])
        for r in range(max(lo, n), hi):
            parts.append(jnp.where(crow == n - 1, 0.0,
                                   pltpu.roll(x_ref[0, (r - n) * n:(r - n + 1) * n, :], n - 1, 0)))
        return parts[0] if len(parts) == 1 else jnp.concatenate(parts, axis=0)

    for r0 in range(0, n, rp):
        sl = slice(r0 * n, (r0 + rp) * n)
        xc = (grid_rows(r0 - 2, r0 + rp - 2) * cw[0:1] + grid_rows(r0 - 1, r0 + rp - 1) * cw[1:2]
              + grid_rows(r0, r0 + rp) * cw[2:3] + grid_rows(r0 + 1, r0 + rp + 1) * cw[3:4] + cb)
        xb = xc.astype(BF16)
        gates = yield [lambda i=i: jnp.dot(xb, w_ref[i, 0], preferred_element_type=F32) for i in range(4)]
        for d, (a_ref, b_ref) in enumerate(((af_ref, bf_ref), (ab_ref, bb_ref))):
            gate_r = _sigmoid(gates[2 * d] + bg_ref[2 * d:2 * d + 1])
            gate_i = _sigmoid(gates[2 * d + 1] + bg_ref[2 * d + 1:2 * d + 2])
            log_a = (-LRU_C * _softplus(-lam_ref[d:d + 1])) * gate_r
            a_ref[sl] = jnp.exp(log_a)
            b_ref[sl] = jnp.sqrt(1.0 - jnp.exp(2.0 * log_a)) * (gate_i * xc)


def _lru_grid_finish(h0_ref, out_ref, af_ref, bf_ref, ab_ref, bb_ref):
    n = GRID_W
    lanes = out_ref.shape[-1]
    crow = lax.broadcasted_iota(jnp.int32, (n, lanes), 0)
    hf = jnp.zeros((n, lanes), F32)
    hb = jnp.zeros((n, lanes), F32)
    pf = jnp.ones((n, lanes), F32)
    pb = jnp.ones((n, lanes), F32)
    for i in range(n):
        sf = slice(i * n, (i + 1) * n)
        sb = slice((n - 1 - i) * n, (n - i) * n)
        a = af_ref[sf]
        hf = a * hf + bf_ref[sf]
        pf = a * pf
        bf_ref[sf] = hf
        af_ref[sf] = pf
        a = ab_ref[sb]
        hb = a * hb + bb_ref[sb]
        pb = a * pb
        bb_ref[sb] = hb
        ab_ref[sb] = pb

    h0f = h0_ref[0, 0]
    h0b = h0_ref[1, 0]
    acc_a, acc_h = _lin_scan(pf, hf, False)
    in_f = jnp.where(crow == 0, h0f, pltpu.roll(acc_h + acc_a * h0f, 1, 0))
    acc_a, acc_h = _lin_scan(pb, hb, True)
    in_b = jnp.where(crow == n - 1, h0b, pltpu.roll(acc_h + acc_a * h0b, n - 1, 0))

    for r in range(n):
        sl = slice(r * n, (r + 1) * n)
        out_ref[0, sl, :] = (bf_ref[sl] + af_ref[sl] * in_f) + (bb_ref[sl] + ab_ref[sl] * in_b)


def _mixers_kernel(*refs):
    scan_in, lru_in, outs, scratch = refs[:8], refs[8:15], refs[15:18], refs[18:]
    of_ref, ob_ref, hs_ref = outs
    sf_ref, sb_ref = scratch[:2]

    @pl.when(pl.program_id(0) == 0)
    def _():
        sf_ref[...] = jnp.zeros_like(sf_ref)
        sb_ref[...] = jnp.zeros_like(sb_ref)

    x_ref, h0_ref = lru_in[:2]
    _run_interleaved([_gdn_scan_steps(*scan_in, of_ref, ob_ref, sf_ref, sb_ref),
                      _lru_grid_gates(x_ref, *lru_in[2:], *scratch[2:])])
    _lru_grid_finish(h0_ref, hs_ref, *scratch[2:])


def _mixers_call(local, rx, n_lat_tok, h0, conv_w, conv_b, w_tiles, b_gate, lam):
    n_chunks, bsz = local[0].shape[:2]
    width = rx.shape[2]
    c = GDN_CHUNK
    g = GDN_SCAN_CHUNKS
    n_all = n_chunks // g
    n_lat = n_lat_tok // (c * g)
    n_ctx = n_all - n_lat
    fwd = lambda s: jnp.where(s < n_ctx, n_lat + s, s - n_ctx)
    bwd = lambda s: n_all - 1 - s
    fwd_out = lambda s: jnp.maximum(s - n_ctx, 0)
    bwd_out = lambda s: jnp.minimum(n_all - 1 - s, n_lat - 1)
    cblk = lambda rows, w, f: pl.BlockSpec((g, bsz, rows, w), lambda s: (f(s), 0, 0, 0))
    ins = lambda f: [cblk(c, GDN_WIDTH, f), cblk(2 * c, GDN_WIDTH, f), cblk(3 * c, GDN_HEADS * c, f),
                     cblk(HALO, BA_PAD, f)]
    tok = lambda f: pl.BlockSpec((bsz, g * c, GDN_WIDTH), lambda s: (0, f(s), 0))
    state = pltpu.VMEM((bsz * GDN_HEADS, GDN_DK, GDN_DK), F32)

    nt = width // LRU_LANES
    unit = lambda s: jnp.minimum(s, bsz * nt - 1)
    assert n_all >= bsz * nt
    lane = lambda rows: pl.BlockSpec((rows, LRU_LANES), lambda s: (0, unit(s) % nt))
    grid_tok = pl.BlockSpec((1, n_lat_tok, LRU_LANES), lambda s: (unit(s) // nt, 0, unit(s) % nt))
    lru_scratch = pltpu.VMEM((n_lat_tok, LRU_LANES), F32)
    return pl.pallas_call(
        _mixers_kernel,
        grid=(n_all,),
        in_specs=ins(fwd) + ins(bwd) + [
            grid_tok,
            pl.BlockSpec((2, 1, 1, LRU_LANES), lambda s: (0, unit(s) // nt, 0, unit(s) % nt)),
            lane(CONV_W), lane(1),
            pl.BlockSpec((4, 1, LRU_LANES, LRU_LANES), lambda s: (0, unit(s) % nt, 0, 0)),
            lane(4), lane(2)],
        out_specs=[tok(fwd_out), tok(bwd_out), grid_tok],
        out_shape=[jax.ShapeDtypeStruct((bsz, n_lat_tok, GDN_WIDTH), F32)] * 2
                  + [jax.ShapeDtypeStruct((bsz, n_lat_tok, width), F32)],
        scratch_shapes=[state, state] + [lru_scratch] * 4,
        compiler_params=pltpu.CompilerParams(dimension_semantics=("arbitrary",),
                                             vmem_limit_bytes=VMEM_LIMIT),
        name="mixers",
    )(*local, rx, h0.reshape(2, bsz, 1, width), conv_w, conv_b, w_tiles, b_gate, lam)


def _out_ffn2_kernel(h1_ref, of_ref, ob_ref, z_ref, hs_ref, rg_ref, mod_ref, ng_ref, gnw_ref,
                     wout_ref, w1_ref, w3_ref, w2_ref, fg_ref, out_ref):
    mod = mod_ref[0]

    def sub_tile(rows):
        o = of_ref[0, rows] + ob_ref[0, rows]
        z = z_ref[0, rows]
        parts = []
        for h in range(GDN_HEADS):
            sl = slice(h * GDN_DK, (h + 1) * GDN_DK)
            oh = o[:, sl]
            parts.append(oh * lax.rsqrt(jnp.mean(oh * oh, axis=-1, keepdims=True) + EPS)
                         * gnw_ref[...] * _silu(z[:, sl]))
        parts.append(hs_ref[0, rows] * _gelu_tanh(rg_ref[0, rows]))
        mixed = jnp.concatenate(parts, axis=1).astype(BF16)
        (y,) = yield [lambda: jnp.dot(mixed, wout_ref[...], preferred_element_type=F32)]
        h2 = h1_ref[0, rows] + mod[5:6] * y
        ub = _rms_mod(h2, ng_ref[2:3], mod[6:7], mod[7:8]).astype(BF16)
        a, b = yield [lambda: jnp.dot(ub, w1_ref[...], preferred_element_type=F32),
                      lambda: jnp.dot(ub, w3_ref[...], preferred_element_type=F32)]
        g = (_silu(a) * b).astype(BF16)
        (f,) = yield [lambda: jnp.dot(g, w2_ref[...], preferred_element_type=F32)]
        h3 = h2 + FFN_RESIDUAL * mod[8:9] * f
        out_ref[0, rows] = h3 * lax.rsqrt(jnp.mean(h3 * h3, axis=-1, keepdims=True) + EPS) * fg_ref[...]

    _run_interleaved([sub_tile(slice(i * TOKEN_TILE, (i + 1) * TOKEN_TILE))
                      for i in range(out_ref.shape[1] // TOKEN_TILE)])


def _out_ffn2_call(h1, o_f, o_b, z, h_sum, rg, mods, norm_g, gdn_norm_w, w_out, w1, w3, w2, final_g):
    bsz, seq, _ = o_f.shape
    rows = OUT_SUB_TILES * TOKEN_TILE
    tok = lambda w: pl.BlockSpec((1, rows, w), lambda b, j: (b, j, 0))
    return pl.pallas_call(
        _out_ffn2_kernel,
        grid=(bsz, seq // rows),
        in_specs=[tok(D_MODEL), tok(GDN_WIDTH), tok(GDN_WIDTH), tok(GDN_WIDTH),
                  tok(LRU_WIDTH), tok(LRU_WIDTH),
                  pl.BlockSpec((1, N_MOD, D_MODEL), lambda b, j: (b + 1, 0, 0)),
                  _resident((3, D_MODEL)), _resident((1, GDN_DK)),
                  _resident((D_MODEL, D_MODEL)),
                  _resident((D_MODEL, D_FF)), _resident((D_MODEL, D_FF)), _resident((D_FF, D_MODEL)),
                  _resident((1, D_MODEL))],
        out_specs=tok(D_MODEL),
        out_shape=jax.ShapeDtypeStruct((bsz, seq, D_MODEL), F32),
        compiler_params=pltpu.CompilerParams(dimension_semantics=("arbitrary", "arbitrary"),
                                             vmem_limit_bytes=VMEM_LIMIT),
        name="out_ffn2",
    )(h1, o_f, o_b, z, h_sum, rg, mods, norm_g, gdn_norm_w, w_out, w1, w3, w2, final_g)


def _block_diag(w):
    nb, bw, _ = w.shape
    eye = jnp.eye(nb, dtype=w.dtype)
    return jnp.einsum('ncd,nm->ncmd', w, eye).reshape(nb * bw, nb * bw)


def kernel(x, c, ctx, c_ctx, w_ada, b_ada, norm_g, ffn_w1, ffn_w3, ffn_w2, w_in, w_out, gdn_conv_w, gdn_a_log,
           gdn_dt_bias, gdn_norm_w, lru_conv_w, lru_conv_b, lru_w_gate, lru_b_gate, lru_lambda, final_norm_g):
    bsz, seq, _ = x.shape
    assert w_ada.shape[0] == 1 and seq == GRID_W * GRID_W and ctx.shape[1] == TOKEN_TILE

    rows = -(-(bsz + 1) // 8) * 8
    cvec = jnp.concatenate([c_ctx[None, :], c, jnp.zeros((rows - bsz - 1, D_MODEL), F32)], axis=0)
    mods, w1a, w3a, w2a, w_in_bf, w2b, w_out_bf = _ada_call(
        cvec, w_ada[0], b_ada,
        [(ffn_w1, (0, 0), WEIGHT_CAST_BLOCKS), (ffn_w3, (0, 0), WEIGHT_CAST_BLOCKS),
         (ffn_w2, (0, 0), WEIGHT_CAST_BLOCKS), (w_in, (0,), WEIGHT_CAST_BLOCKS),
         (ffn_w2, (0, 1), WEIGHT_CAST_BLOCKS), (w_out, (0,), WEIGHT_CAST_BLOCKS)])
    mods = mods.reshape(rows, N_MOD, D_MODEL)

    o_ba = P_HEAD_COLS
    o_rx = o_ba + 4 * GDN_HEADS
    w_tail = jnp.concatenate([w_in_bf[:, o_rx:], w_in_bf[:, o_ba:o_rx],
                              jnp.zeros((D_MODEL, BA_PAD - 4 * GDN_HEADS), BF16)], axis=1)

    h1, qkv, z, rx, rg, ba, w1b, w3b = _ffn1_in_call(
        x, ctx, mods, norm_g[0], w1a, w3a, w2a, w_in_bf, w_tail,
        [(ffn_w1, (0, 1), 4 * WEIGHT_CAST_BLOCKS), (ffn_w3, (0, 1), 4 * WEIGHT_CAST_BLOCKS)])

    pad_g = lambda t: jnp.zeros((1, BA_PAD), F32).at[0, 2 * GDN_HEADS:4 * GDN_HEADS].set(t.reshape(-1))
    local = _gdn_local_call(qkv, ba, gdn_conv_w[0], pad_g(gdn_a_log[0]), pad_g(gdn_dt_bias[0]), seq)

    wg = lru_w_gate[0]
    w_dense = [_block_diag(wg[d, g]).astype(BF16) for d in range(2) for g in range(2)]
    b_gate = lru_b_gate[0].reshape(4, LRU_WIDTH)
    lam = lru_lambda[0]
    cw, cb = lru_conv_w[0], lru_conv_b
    h_ctx = _lru_ctx_call(rx, seq, cw, cb, w_dense, b_gate, lam)
    nt = LRU_WIDTH // LRU_LANES
    w_tiles = jnp.stack([jnp.stack([w[t * LRU_LANES:(t + 1) * LRU_LANES, t * LRU_LANES:(t + 1) * LRU_LANES]
                                    for t in range(nt)]) for w in w_dense])
    o_f, o_b, h_sum = _mixers_call(local, rx, seq, h_ctx, cw, cb, w_tiles, b_gate, lam)

    return _out_ffn2_call(h1, o_f, o_b, z, h_sum, rg, mods, norm_g[0], gdn_norm_w,
                          w_out_bf, w1b, w3b, w2b, final_norm_g[None, :])
```

```python
import functools

import jax
import jax.numpy as jnp
from jax import lax
from jax.experimental import pallas as pl
from jax.experimental.pallas import tpu as pltpu

D_MODEL = 1024
D_FF = 2816
N_MOD = 9
EPS = 1e-6
FFN_RESIDUAL = 0.5

GDN_WIDTH = 512
GDN_HEADS = 4
GDN_DK = 128
GDN_CHUNK = 64
CONV_W = 4
LRU_WIDTH = 512
LRU_C = 8.0
GRID_W = 64

TOKEN_TILE = 256
GDN_TILE_CHUNKS = TOKEN_TILE // GDN_CHUNK
GDN_SCAN_CHUNKS = 4
OUT_SUB_TILES = 2
FFN1_SUB_TILES = 2
HALO = 8
BA_PAD = 128
P_HEAD_COLS = 4 * GDN_WIDTH
P_TAIL_COLS = 2 * LRU_WIDTH + BA_PAD
LRU_LANES = 128
LRU_ROWS_PER_PASS = 8
VMEM_LIMIT = 56 * 1024 * 1024

BF16 = jnp.bfloat16
F32 = jnp.float32
NEG_BIG = -1e30
NT_DIMS = (((1,), (1,)), ((), ()))


def _mm(a, b):
    return jnp.dot(a.astype(BF16), b.astype(BF16), preferred_element_type=F32)


def _sigmoid(x):
    return 1.0 / (1.0 + jnp.exp(-x))


def _silu(x):
    return x * _sigmoid(x)


def _softplus(x):
    return jnp.maximum(x, 0.0) + jnp.log(1.0 + jnp.exp(-jnp.abs(x)))


def _gelu_tanh(x):
    return 0.5 * x * (1.0 + jnp.tanh(0.7978845608028654 * (x + 0.044715 * x * x * x)))


def _rms_mod(h, g, shift, scale):
    y = h * lax.rsqrt(jnp.mean(h * h, axis=-1, keepdims=True) + EPS) * g
    return y * (1.0 + scale) + shift


def _swiglu(u, w1_ref, w3_ref, w2_ref):
    ub = u.astype(BF16)
    a = jnp.dot(ub, w1_ref[...], preferred_element_type=F32)
    b = jnp.dot(ub, w3_ref[...], preferred_element_type=F32)
    g = (_silu(a) * b).astype(BF16)
    return jnp.dot(g, w2_ref[...], preferred_element_type=F32)


def _dwconv(x, prev, nxt, w):
    t = x.shape[0]
    row = lax.broadcasted_iota(jnp.int32, x.shape, 0)
    xm1 = jnp.where(row == 0, prev[HALO - 1:HALO], pltpu.roll(x, 1, 0))
    xm2 = jnp.where(row == 0, prev[HALO - 2:HALO - 1],
                    jnp.where(row == 1, prev[HALO - 1:HALO], pltpu.roll(x, 2, 0)))
    xp1 = jnp.where(row == t - 1, nxt[0:1], pltpu.roll(x, t - 1, 0))
    return xm2 * w[0:1] + xm1 * w[1:2] + x * w[2:3] + xp1 * w[3:4]


def _resident(shape, index=None):
    index = index or (0,) * len(shape)
    return pl.BlockSpec(shape, lambda *_: index, pipeline_mode=pl.Buffered(1))


def _ada_kernel(c_ref, w_ref, b_ref, o_ref):
    o_ref[...] = _mm(_silu(c_ref[...]), w_ref[...]) + b_ref[...]


def _ada_call(cvec, w_ada, b_ada):
    rows = cvec.shape[0]
    return pl.pallas_call(
        _ada_kernel,
        grid=(N_MOD,),
        in_specs=[pl.BlockSpec((rows, D_MODEL), lambda k: (0, 0)),
                  pl.BlockSpec((D_MODEL, D_MODEL), lambda k: (0, k)),
                  pl.BlockSpec((1, D_MODEL), lambda k: (0, k))],
        out_specs=pl.BlockSpec((rows, D_MODEL), lambda k: (0, k)),
        out_shape=jax.ShapeDtypeStruct((rows, N_MOD * D_MODEL), F32),
        name="ada",
    )(cvec, w_ada, b_ada)


def _segment_edges(j, n_lat_tiles, n_all_tiles):
    return (jnp.logical_or(j == 0, j == n_lat_tiles),
            jnp.logical_or(j == n_lat_tiles - 1, j == n_all_tiles - 1))


def _ffn1_in_kernel(n_lat_tiles, n_all_tiles, *refs):
    n_sub = FFN1_SUB_TILES
    ins, refs = refs[:3 * n_sub], refs[3 * n_sub:]
    ng_ref, w1_ref, w3_ref, w2_ref, win_ref, wtail_ref, h1_ref, qkv_ref, z_ref, rx_ref, rg_ref, ba_ref = refs
    s = pl.program_id(0)

    def sub_tile(i):
        x_ref, ctx_ref, mod_ref = ins[3 * i:3 * i + 3]
        rows = slice(i * TOKEN_TILE, (i + 1) * TOKEN_TILE)
        h = jnp.where((s * n_sub + i) % n_all_tiles == n_lat_tiles, ctx_ref[0], x_ref[0])
        mod = mod_ref[0]
        ub = _rms_mod(h, ng_ref[0:1], mod[0:1], mod[1:2]).astype(BF16)
        a, b = yield [lambda: jnp.dot(ub, w1_ref[...], preferred_element_type=F32),
                      lambda: jnp.dot(ub, w3_ref[...], preferred_element_type=F32)]
        g = (_silu(a) * b).astype(BF16)
        (f,) = yield [lambda: jnp.dot(g, w2_ref[...], preferred_element_type=F32)]
        h1 = h + FFN_RESIDUAL * mod[2:3] * f
        h1_ref[rows] = h1
        ub2 = _rms_mod(h1, ng_ref[1:2], mod[3:4], mod[4:5]).astype(BF16)
        head, tail = yield [lambda: jnp.dot(ub2, win_ref[...], preferred_element_type=F32),
                            lambda: jnp.dot(ub2, wtail_ref[...], preferred_element_type=F32)]
        qkv_ref[rows] = head[:, :3 * GDN_WIDTH]
        z_ref[rows] = head[:, 3 * GDN_WIDTH:]
        rx_ref[rows] = tail[:, :LRU_WIDTH]
        rg_ref[rows] = tail[:, LRU_WIDTH:2 * LRU_WIDTH]
        ba_ref[rows] = tail[:, 2 * LRU_WIDTH:]

    _run_interleaved([sub_tile(i) for i in range(n_sub)])


def _ffn1_in_call(x, ctx, mods, norm_g, w1, w3, w2, w_in_bf, w_tail):
    bsz, seq, _ = x.shape
    n_lat = seq // TOKEN_TILE
    n_all = n_lat + 1
    t_all = n_all * TOKEN_TILE
    n_sub = FFN1_SUB_TILES
    assert (bsz * n_all) % n_sub == 0
    per_tile = []
    for i in range(n_sub):
        tile = lambda s, i=i: s * n_sub + i
        per_tile += [
            pl.BlockSpec((1, TOKEN_TILE, D_MODEL),
                         lambda s, t=tile: (t(s) // n_all, jnp.minimum(t(s) % n_all, n_lat - 1), 0)),
            pl.BlockSpec((1, TOKEN_TILE, D_MODEL), lambda s, t=tile: (t(s) // n_all, 0, 0)),
            pl.BlockSpec((1, N_MOD, D_MODEL),
                         lambda s, t=tile: (jnp.where(t(s) % n_all == n_lat, 0, t(s) // n_all + 1), 0, 0))]
    widths = (D_MODEL, 3 * GDN_WIDTH, GDN_WIDTH, LRU_WIDTH, LRU_WIDTH, BA_PAD)
    outs = pl.pallas_call(
        functools.partial(_ffn1_in_kernel, n_lat, n_all),
        grid=(bsz * n_all // n_sub,),
        in_specs=per_tile + [
            _resident((3, D_MODEL)),
            _resident((None, D_MODEL, D_FF), (0, 0, 0)), _resident((None, D_MODEL, D_FF), (0, 0, 0)),
            _resident((None, D_FF, D_MODEL), (0, 0, 0)),
            _resident((D_MODEL, P_HEAD_COLS)), _resident((D_MODEL, P_TAIL_COLS))],
        out_specs=[pl.BlockSpec((n_sub * TOKEN_TILE, w), lambda s: (s, 0)) for w in widths],
        out_shape=[jax.ShapeDtypeStruct((bsz * t_all, w), F32) for w in widths],
        compiler_params=pltpu.CompilerParams(dimension_semantics=("arbitrary",),
                                             vmem_limit_bytes=VMEM_LIMIT),
        name="ffn1_in",
    )(*([x, ctx, mods] * n_sub), norm_g, w1, w3, w2, w_in_bf, w_tail)
    return [o.reshape(bsz, t_all, o.shape[-1]) for o in outs]


def _pair_block_diag(x, first):
    return jnp.concatenate([jnp.where(first, x, 0.0), jnp.where(first, 0.0, x)], axis=0)


def _run_interleaved(coroutines):
    active = [[gen, next(gen)] for gen in coroutines]
    while active:
        still = []
        for item in active:
            results = [dot() for dot in item[1]]
            try:
                item[1] = item[0].send(results)
                still.append(item)
            except StopIteration:
                pass
        active = still


def _gdn_local_body(seg_start, seg_end, x_ref, prev_ref, next_ref, ba_ref, cw_ref, alog_ref, dtb_ref,
                    out_refs, stage_in, stage_out):
    c = GDN_CHUNK
    pairs = GDN_HEADS // 2
    row = lax.broadcasted_iota(jnp.int32, (c, 2 * c), 0)
    lane2 = lax.broadcasted_iota(jnp.int32, (c, 2 * c), 1)
    col = lane2 & (c - 1)
    first = lane2 < c
    first_row = first[0:1]
    eye2 = (row == col).astype(F32)
    zeros_k = jnp.zeros((c, GDN_DK), BF16)
    unit_index = lambda ci, d, p: (ci * 2 + d) * pairs + p

    def coupling(k):
        return jnp.logical_and((row >> k) != (col >> k), (row >> (k + 1)) == (col >> (k + 1)))

    lhs_in, kr_in, rhs_in, dec_in, kdt_in, qe_in, eg_in = stage_in

    def pair_unit(ci, d, p):
        u_ref, wq_ref, ak_ref, eg_ref = out_refs[d]
        un = unit_index(ci, d, p)
        strict = (row < col) if d else (row > col)
        k_rows = kr_in[ci * pairs + p]
        k_bd = jnp.concatenate([jnp.concatenate([k_rows[:c], zeros_k], axis=1),
                                jnp.concatenate([zeros_k, k_rows[c:]], axis=1)], axis=0)
        lhs = lhs_in[un]
        (prods,) = yield [lambda: lax.dot_general(lhs, k_bd, NT_DIMS, preferred_element_type=F32)]
        decay = dec_in[un]
        a = jnp.where(strict, prods[:c] * decay, 0.0)
        ps = slice(p * 2 * c, (p + 1) * 2 * c)
        ak_ref[ci, 0, 0:c, ps] = (prods[c:] * decay).astype(BF16)
        ak_ref[ci, 0, c:3 * c, ps] = kdt_in[un]
        t = eye2 - jnp.where(coupling(0), a, 0.0)
        k = 1
        while 2 ** k < c:
            l = jnp.where(coupling(k), a, 0.0)
            t_bd = _pair_block_diag(t, first)
            (lt,) = yield [lambda: _mm(l, t_bd)]
            lt_bd = _pair_block_diag(lt, first)
            (tlt,) = yield [lambda: _mm(t, lt_bd)]
            t = t - tlt
            k += 1
        t_bd = _pair_block_diag(t, first).astype(BF16)
        rhs = rhs_in[un]
        (sol,) = yield [lambda: jnp.dot(t_bd, rhs, preferred_element_type=F32)]
        for i in range(2):
            h = 2 * p + i
            hs = slice(h * GDN_DK, (h + 1) * GDN_DK)
            u_ref[ci, 0, :, hs] = sol[i * c:(i + 1) * c, :GDN_DK]
            wq_ref[ci, 0, 0:c, hs] = sol[i * c:(i + 1) * c, GDN_DK:].astype(BF16)
            wq_ref[ci, 0, c:2 * c, hs] = qe_in[un, i * c:(i + 1) * c]
            eg_ref[ci, 0, h:h + 1, :] = eg_in[un, i:i + 1]

    lhs_out, kr_out, rhs_out, dec_out, kdt_out, qe_out, eg_out = stage_out
    y = _silu(_dwconv(x_ref[0], jnp.where(seg_start, 0.0, prev_ref[0]), jnp.where(seg_end, 0.0, next_ref[0]),
                      cw_ref[...]))
    ba = ba_ref[0]
    lane = lax.broadcasted_iota(jnp.int32, (1, BA_PAD), 1)
    is_g = jnp.logical_and(lane >= 2 * GDN_HEADS, lane < 4 * GDN_HEADS)
    beta_all = _sigmoid(ba)
    g_all = jnp.where(is_g, -jnp.exp(alog_ref[...]) * _softplus(ba + dtb_ref[...]), 0.0)

    def chunk(ci):
        rs = slice(ci * c, (ci + 1) * c)
        heads = []
        for h in range(GDN_HEADS):
            lo = h * GDN_DK
            q = y[rs, lo:lo + GDN_DK]
            k = y[rs, GDN_WIDTH + lo:GDN_WIDTH + lo + GDN_DK]
            v = y[rs, 2 * GDN_WIDTH + lo:2 * GDN_WIDTH + lo + GDN_DK]
            q = q * (lax.rsqrt(jnp.sum(q * q, axis=-1, keepdims=True) + EPS) * (GDN_DK ** -0.5))
            k = k * lax.rsqrt(jnp.sum(k * k, axis=-1, keepdims=True) + EPS)
            heads.append((q, k, v))
        g = g_all[rs]
        k_rows = [jnp.concatenate([heads[2 * p][1], heads[2 * p + 1][1]], axis=0) for p in range(pairs)]
        k_ts = [kr.T for kr in k_rows]
        grow = lax.broadcasted_iota(jnp.int32, g.shape, 0)
        gcum_f = g
        step = 1
        while step < c:
            gcum_f = gcum_f + jnp.where(grow >= step, pltpu.roll(gcum_f, step, 0), 0.0)
            step *= 2
        gcum_b = gcum_f[c - 1:c] - gcum_f + g
        gcum_t = jnp.concatenate([gcum_f, gcum_b], axis=0).T
        beta_c = beta_all[rs]
        for p in range(pairs):
            kr_out[ci * pairs + p] = k_rows[p].astype(BF16)
        for d in range(2):
            gcum = gcum_b if d else gcum_f
            incl = (row <= col) if d else (row >= col)
            last = 0 if d else c - 1
            for p in range(pairs):
                un = unit_index(ci, d, p)
                (q0, k0, v0), (q1, k1, v1) = heads[2 * p], heads[2 * p + 1]
                cb = d * GDN_HEADS + 2 * p
                cg = 2 * GDN_HEADS + cb
                beta0, beta1 = beta_c[:, cb:cb + 1], beta_c[:, cb + 1:cb + 2]
                gc0, gc1 = gcum[:, cg:cg + 1], gcum[:, cg + 1:cg + 2]
                gt0, gt1 = gcum_t[cg:cg + 1], gcum_t[cg + 1:cg + 2]
                gr = (jnp.where(first_row, pltpu.roll(gt0, c, 1), gt1) if d
                      else jnp.where(first_row, gt0, pltpu.roll(gt1, c, 1)))
                gc = jnp.where(first, gc0, gc1)
                e0, e1 = jnp.exp(gc0), jnp.exp(gc1)
                kb0, kb1 = k0 * beta0, k1 * beta1
                dec_out[un] = jnp.exp(jnp.where(incl, gc - gr, NEG_BIG))
                lhs_out[un] = jnp.concatenate([jnp.concatenate([kb0, kb1], axis=1),
                                               jnp.concatenate([q0, q1], axis=1)], axis=0).astype(BF16)
                rhs_out[un] = jnp.concatenate([jnp.concatenate([v0 * beta0, kb0 * e0], axis=1),
                                               jnp.concatenate([v1 * beta1, kb1 * e1], axis=1)],
                                              axis=0).astype(BF16)
                kdt_out[un] = (k_ts[p] * jnp.exp(gc[last:last + 1] - gr)).astype(BF16)
                qe_out[un] = jnp.concatenate([q0 * e0, q1 * e1], axis=0).astype(BF16)
                eg_out[un] = jnp.concatenate(
                    [jnp.broadcast_to(jnp.exp(gc0[last:last + 1]), (1, BA_PAD)),
                     jnp.broadcast_to(jnp.exp(gc1[last:last + 1]), (1, BA_PAD)),
                     jnp.zeros((HALO - 2, BA_PAD), F32)], axis=0)

    for refs in out_refs:
        refs[3][...] = jnp.zeros_like(refs[3])
    _run_interleaved([pair_unit(ci, d, p) for ci in range(GDN_TILE_CHUNKS) for d in range(2)
                      for p in range(pairs)])
    for ci in range(GDN_TILE_CHUNKS):
        chunk(ci)


def _gdn_local_kernel(n_lat_tiles, n_all_tiles, n_tiles, x_ref, prev_ref, next_ref, ba_ref, cw_ref, alog_ref,
                      dtb_ref, uf_ref, wqf_ref, akf_ref, egf_ref, ub_ref, wqb_ref, akb_ref, egb_ref, *stage_refs):
    s = pl.program_id(0)
    seg_start, seg_end = _segment_edges(jnp.minimum(s, n_tiles - 1) % n_all_tiles, n_lat_tiles, n_all_tiles)
    out_refs = ((uf_ref, wqf_ref, akf_ref, egf_ref), (ub_ref, wqb_ref, akb_ref, egb_ref))
    half = len(stage_refs) // 2
    sets = (stage_refs[:half], stage_refs[half:])

    @pl.when(s == 0)
    def _():
        for r in sets[1]:
            r[...] = jnp.zeros_like(r)

    for parity in range(2):
        @pl.when(s % 2 == parity)
        def _():
            _gdn_local_body(seg_start, seg_end, x_ref, prev_ref, next_ref, ba_ref, cw_ref, alog_ref, dtb_ref,
                            out_refs, sets[1 - parity], sets[parity])


def _gdn_local_call(qkv, ba, conv_w, alog_vec, dtb_vec, n_lat_tok):
    bsz, t_all, width = qkv.shape
    c = GDN_CHUNK
    tile = GDN_TILE_CHUNKS * c
    n_all = t_all // tile
    n_lat = n_lat_tok // tile
    per_tile = tile // HALO
    last_halo = t_all // HALO - 1
    n_chunks = t_all // c
    n_tiles = bsz * n_all
    pairs = GDN_HEADS // 2
    units = GDN_TILE_CHUNKS * 2 * pairs
    t_in = lambda s: jnp.minimum(s, n_tiles - 1)
    t_out = lambda s: jnp.maximum(s - 1, 0)
    in_tok = lambda w: pl.BlockSpec((1, tile, w), lambda s: (t_in(s) // n_all, t_in(s) % n_all, 0))
    cblk = lambda rows, w: pl.BlockSpec((GDN_TILE_CHUNKS, 1, rows, w),
                                        lambda s: (t_out(s) % n_all, t_out(s) // n_all, 0, 0))
    vec = pl.BlockSpec((1, BA_PAD), lambda s: (0, 0))
    outs = [(jax.ShapeDtypeStruct((n_chunks, bsz, c, GDN_WIDTH), F32), cblk(c, GDN_WIDTH)),
            (jax.ShapeDtypeStruct((n_chunks, bsz, 2 * c, GDN_WIDTH), BF16), cblk(2 * c, GDN_WIDTH)),
            (jax.ShapeDtypeStruct((n_chunks, bsz, 3 * c, GDN_HEADS * c), BF16), cblk(3 * c, GDN_HEADS * c)),
            (jax.ShapeDtypeStruct((n_chunks, bsz, HALO, BA_PAD), F32), cblk(HALO, BA_PAD))] * 2
    stage = [pltpu.VMEM((units, 2 * c, 2 * GDN_DK), BF16),
             pltpu.VMEM((GDN_TILE_CHUNKS * pairs, 2 * c, GDN_DK), BF16),
             pltpu.VMEM((units, 2 * c, 2 * GDN_DK), BF16),
             pltpu.VMEM((units, c, 2 * c), F32),
             pltpu.VMEM((units, GDN_DK, 2 * c), BF16),
             pltpu.VMEM((units, 2 * c, GDN_DK), BF16),
             pltpu.VMEM((units, HALO, BA_PAD), F32)]
    return pl.pallas_call(
        functools.partial(_gdn_local_kernel, n_lat, n_all, n_tiles),
        grid=(n_tiles + 1,),
        in_specs=[in_tok(width),
                  pl.BlockSpec((1, HALO, width),
                               lambda s: (t_in(s) // n_all, jnp.maximum((t_in(s) % n_all) * per_tile - 1, 0), 0)),
                  pl.BlockSpec((1, HALO, width),
                               lambda s: (t_in(s) // n_all,
                                          jnp.minimum((t_in(s) % n_all + 1) * per_tile, last_halo), 0)),
                  in_tok(BA_PAD), pl.BlockSpec((CONV_W, width), lambda s: (0, 0)), vec, vec],
        out_specs=[o[1] for o in outs],
        out_shape=[o[0] for o in outs],
        scratch_shapes=stage + stage,
        compiler_params=pltpu.CompilerParams(dimension_semantics=("arbitrary",),
                                             vmem_limit_bytes=VMEM_LIMIT),
        name="gdn_local",
    )(qkv, qkv, qkv, ba, conv_w, alog_vec, dtb_vec)


def _gdn_scan_steps(uf_ref, wqf_ref, akf_ref, egf_ref, ub_ref, wqb_ref, akb_ref, egb_ref,
                    of_ref, ob_ref, sf_ref, sb_ref):
    c = GDN_CHUNK
    bsz = uf_ref.shape[1]
    hs = lambda h: slice(h * GDN_DK, (h + 1) * GDN_DK)
    for sub in range(GDN_SCAN_CHUNKS):
        chains = [(refs, ci, b, h)
                  for refs, ci in (((uf_ref, wqf_ref, akf_ref, egf_ref, of_ref, sf_ref), sub),
                                   ((ub_ref, wqb_ref, akb_ref, egb_ref, ob_ref, sb_ref),
                                    GDN_SCAN_CHUNKS - 1 - sub))
                  for b in range(bsz) for h in range(GDN_HEADS)]
        states = [refs[5][b * GDN_HEADS + h] for refs, ci, b, h in chains]
        with_s = yield [lambda refs=refs, ci=ci, b=b, h=h, s=s:
                        jnp.dot(refs[1][ci, b, :, hs(h)], s.astype(BF16), preferred_element_type=F32)
                        for (refs, ci, b, h), s in zip(chains, states)]
        v_new = [refs[0][ci, b, :, hs(h)] - r[:c] for (refs, ci, b, h), r in zip(chains, with_s)]
        with_v = yield [lambda refs=refs, ci=ci, b=b, h=h, v=v:
                        jnp.dot(refs[2][ci, b, :, h * c:(h + 1) * c], v.astype(BF16), preferred_element_type=F32)
                        for (refs, ci, b, h), v in zip(chains, v_new)]
        for (refs, ci, b, h), s, r, a in zip(chains, states, with_s, with_v):
            refs[4][b, ci * c:(ci + 1) * c, hs(h)] = r[c:] + a[:c]
            refs[5][b * GDN_HEADS + h] = s * refs[3][ci, b, h:h + 1, :] + a[c:]


def _lin_scan(a, b, reverse):
    r = a.shape[0]
    row = lax.broadcasted_iota(jnp.int32, a.shape, 0)
    s = 1
    while s < r:
        ok = (row < r - s) if reverse else (row >= s)
        shift = r - s if reverse else s
        a_sh = jnp.where(ok, pltpu.roll(a, shift, 0), 1.0)
        b_sh = jnp.where(ok, pltpu.roll(b, shift, 0), 0.0)
        b = a * b_sh + b
        a = a * a_sh
        s *= 2
    return a, b


def _lru_ctx_direction(reverse, n_slabs, slab, x_ref, prev_ref, next_ref, cw, cb, wr_ref, wi_ref,
                       br, bi, lam, carry_ref):
    bsz, r, width = x_ref.shape
    step = pl.program_id(0)
    has_prev = slab > 0
    has_next = slab < n_slabs - 1
    xs = []
    for b in range(bsz):
        prev = jnp.where(has_prev, prev_ref[b], 0.0)
        nxt = jnp.where(has_next, next_ref[b], 0.0)
        xs.append(_dwconv(x_ref[b], prev, nxt, cw) + cb)
    x = jnp.concatenate(xs, axis=0)
    xb = x.astype(BF16)
    gate_r = _sigmoid(jnp.dot(xb, wr_ref[...], preferred_element_type=F32) + br)
    gate_i = _sigmoid(jnp.dot(xb, wi_ref[...], preferred_element_type=F32) + bi)
    log_a = (-LRU_C * _softplus(-lam)) * gate_r
    a = jnp.exp(log_a)
    mult = jnp.sqrt(1.0 - a * a)
    row = lax.broadcasted_iota(jnp.int32, (r, width), 0)
    is_first = jnp.logical_and(step == 0, row == ((r - 1) if reverse else 0))
    edge = 0 if reverse else r - 1
    for b in range(bsz):
        sl = slice(b * r, (b + 1) * r)
        m = jnp.where(is_first, 1.0, mult[sl])
        a_cum, h = _lin_scan(a[sl], m * (gate_i[sl] * x[sl]), reverse)
        h = h + a_cum * carry_ref[b:b + 1]
        carry_ref[b:b + 1] = h[edge:edge + 1]


def _lru_ctx_kernel(n_slabs, xf_ref, pf_ref, nf_ref, xb_ref, pb_ref, nb_ref, cw_ref, cb_ref,
                    wrf_ref, wif_ref, wrb_ref, wib_ref, bg_ref, lam_ref, fin_ref, cf_ref, cbk_ref):
    step = pl.program_id(0)

    @pl.when(step == 0)
    def _():
        cf_ref[...] = jnp.zeros_like(cf_ref)
        cbk_ref[...] = jnp.zeros_like(cbk_ref)

    cw = cw_ref[...]
    cb = cb_ref[...]
    _lru_ctx_direction(False, n_slabs, step, xf_ref, pf_ref, nf_ref, cw, cb, wrf_ref, wif_ref,
                       bg_ref[0:1], bg_ref[1:2], lam_ref[0:1], cf_ref)
    _lru_ctx_direction(True, n_slabs, n_slabs - 1 - step, xb_ref, pb_ref, nb_ref, cw, cb, wrb_ref, wib_ref,
                       bg_ref[2:3], bg_ref[3:4], lam_ref[1:2], cbk_ref)
    fin_ref[0] = cf_ref[...]
    fin_ref[1] = cbk_ref[...]


def _lru_ctx_call(rx, n_lat_tok, conv_w, conv_b, w_dense, b_gate, lam):
    bsz, t_all, width = rx.shape
    r = GRID_W
    per = r // HALO
    n_slabs = (t_all - n_lat_tok) // r
    first = n_lat_tok // r
    main = lambda f: pl.BlockSpec((bsz, r, width), lambda s: (0, first + f(s), 0))
    prev = lambda f: pl.BlockSpec((bsz, HALO, width), lambda s: (0, (first + f(s)) * per - 1, 0))
    nxt = lambda f: pl.BlockSpec(
        (bsz, HALO, width), lambda s: (0, jnp.minimum((first + f(s) + 1) * per, t_all // HALO - 1), 0))
    fwd = lambda s: s
    bwd = lambda s: n_slabs - 1 - s
    full = lambda shape: pl.BlockSpec(shape, lambda s: (0,) * len(shape))
    carry = pltpu.VMEM((bsz, width), F32)
    return pl.pallas_call(
        functools.partial(_lru_ctx_kernel, n_slabs),
        grid=(n_slabs,),
        in_specs=[main(fwd), prev(fwd), nxt(fwd), main(bwd), prev(bwd), nxt(bwd),
                  full((CONV_W, width)), full((1, width))]
                 + [full((width, width))] * 4
                 + [full((4, width)), full((2, width))],
        out_specs=full((2, bsz, width)),
        out_shape=jax.ShapeDtypeStruct((2, bsz, width), F32),
        scratch_shapes=[carry, carry],
        compiler_params=pltpu.CompilerParams(dimension_semantics=("arbitrary",),
                                             vmem_limit_bytes=VMEM_LIMIT),
        name="lru_ctx",
    )(rx, rx, rx, rx, rx, rx, conv_w, conv_b, *w_dense, b_gate, lam)


def _lru_grid_gates(x_ref, cw_ref, cb_ref, w_ref, bg_ref, lam_ref, af_ref, bf_ref, ab_ref, bb_ref):
    n = GRID_W
    rp = LRU_ROWS_PER_PASS
    lanes = x_ref.shape[-1]
    crow = lax.broadcasted_iota(jnp.int32, (n, lanes), 0)
    cw = cw_ref[...]
    cb = cb_ref[...]

    def grid_rows(lo, hi):
        parts = []
        for r in range(lo, min(hi, 0)):
            parts.append(jnp.where(crow == 0, 0.0, pltpu.roll(x_ref[0, (n + r) * n:(n + r + 1) * n, :], 1, 0)))
        if max(lo, 0) < min(hi, n):
            parts.append(x_ref[0, max(lo, 0) * n:min(hi, n) * n, :])
        for r in range(max(lo, n), hi):
            parts.append(jnp.where(crow == n - 1, 0.0,
                                   pltpu.roll(x_ref[0, (r - n) * n:(r - n + 1) * n, :], n - 1, 0)))
        return parts[0] if len(parts) == 1 else jnp.concatenate(parts, axis=0)

    for r0 in range(0, n, rp):
        sl = slice(r0 * n, (r0 + rp) * n)
        xc = (grid_rows(r0 - 2, r0 + rp - 2) * cw[0:1] + grid_rows(r0 - 1, r0 + rp - 1) * cw[1:2]
              + grid_rows(r0, r0 + rp) * cw[2:3] + grid_rows(r0 + 1, r0 + rp + 1) * cw[3:4] + cb)
        xb = xc.astype(BF16)
        gates = yield [lambda i=i: jnp.dot(xb, w_ref[i, 0], preferred_element_type=F32) for i in range(4)]
        for d, (a_ref, b_ref) in enumerate(((af_ref, bf_ref), (ab_ref, bb_ref))):
            gate_r = _sigmoid(gates[2 * d] + bg_ref[2 * d:2 * d + 1])
            gate_i = _sigmoid(gates[2 * d + 1] + bg_ref[2 * d + 1:2 * d + 2])
            log_a = (-LRU_C * _softplus(-lam_ref[d:d + 1])) * gate_r
            a = jnp.exp(log_a)
            a_ref[sl] = a
            b_ref[sl] = jnp.sqrt(1.0 - a * a) * (gate_i * xc)


def _lru_grid_finish(h0_ref, out_ref, af_ref, bf_ref, ab_ref, bb_ref):
    n = GRID_W
    lanes = out_ref.shape[-1]
    crow = lax.broadcasted_iota(jnp.int32, (n, lanes), 0)
    hf = jnp.zeros((n, lanes), F32)
    hb = jnp.zeros((n, lanes), F32)
    pf = jnp.ones((n, lanes), F32)
    pb = jnp.ones((n, lanes), F32)
    for i in range(n):
        sf = slice(i * n, (i + 1) * n)
        sb = slice((n - 1 - i) * n, (n - i) * n)
        a = af_ref[sf]
        hf = a * hf + bf_ref[sf]
        pf = a * pf
        bf_ref[sf] = hf
        af_ref[sf] = pf
        a = ab_ref[sb]
        hb = a * hb + bb_ref[sb]
        pb = a * pb
        bb_ref[sb] = hb
        ab_ref[sb] = pb

    h0f = h0_ref[0, 0]
    h0b = h0_ref[1, 0]
    acc_a, acc_h = _lin_scan(pf, hf, False)
    in_f = jnp.where(crow == 0, h0f, pltpu.roll(acc_h + acc_a * h0f, 1, 0))
    acc_a, acc_h = _lin_scan(pb, hb, True)
    in_b = jnp.where(crow == n - 1, h0b, pltpu.roll(acc_h + acc_a * h0b, n - 1, 0))

    for r in range(n):
        sl = slice(r * n, (r + 1) * n)
        out_ref[0, sl, :] = (bf_ref[sl] + af_ref[sl] * in_f) + (bb_ref[sl] + ab_ref[sl] * in_b)


def _mixers_kernel(*refs):
    scan_in, lru_in, outs, scratch = refs[:8], refs[8:15], refs[15:18], refs[18:]
    of_ref, ob_ref, hs_ref = outs
    sf_ref, sb_ref = scratch[:2]

    @pl.when(pl.program_id(0) == 0)
    def _():
        sf_ref[...] = jnp.zeros_like(sf_ref)
        sb_ref[...] = jnp.zeros_like(sb_ref)

    x_ref, h0_ref = lru_in[:2]
    _run_interleaved([_gdn_scan_steps(*scan_in, of_ref, ob_ref, sf_ref, sb_ref),
                      _lru_grid_gates(x_ref, *lru_in[2:], *scratch[2:])])
    _lru_grid_finish(h0_ref, hs_ref, *scratch[2:])


def _mixers_call(local, rx, n_lat_tok, h0, conv_w, conv_b, w_tiles, b_gate, lam):
    n_chunks, bsz = local[0].shape[:2]
    width = rx.shape[2]
    c = GDN_CHUNK
    g = GDN_SCAN_CHUNKS
    n_all = n_chunks // g
    n_lat = n_lat_tok // (c * g)
    n_ctx = n_all - n_lat
    fwd = lambda s: jnp.where(s < n_ctx, n_lat + s, s - n_ctx)
    bwd = lambda s: n_all - 1 - s
    fwd_out = lambda s: jnp.maximum(s - n_ctx, 0)
    bwd_out = lambda s: jnp.minimum(n_all - 1 - s, n_lat - 1)
    cblk = lambda rows, w, f: pl.BlockSpec((g, bsz, rows, w), lambda s: (f(s), 0, 0, 0))
    ins = lambda f: [cblk(c, GDN_WIDTH, f), cblk(2 * c, GDN_WIDTH, f), cblk(3 * c, GDN_HEADS * c, f),
                     cblk(HALO, BA_PAD, f)]
    tok = lambda f: pl.BlockSpec((bsz, g * c, GDN_WIDTH), lambda s: (0, f(s), 0))
    state = pltpu.VMEM((bsz * GDN_HEADS, GDN_DK, GDN_DK), F32)

    nt = width // LRU_LANES
    unit = lambda s: jnp.minimum(s, bsz * nt - 1)
    assert n_all >= bsz * nt
    lane = lambda rows: pl.BlockSpec((rows, LRU_LANES), lambda s: (0, unit(s) % nt))
    grid_tok = pl.BlockSpec((1, n_lat_tok, LRU_LANES), lambda s: (unit(s) // nt, 0, unit(s) % nt))
    lru_scratch = pltpu.VMEM((n_lat_tok, LRU_LANES), F32)
    return pl.pallas_call(
        _mixers_kernel,
        grid=(n_all,),
        in_specs=ins(fwd) + ins(bwd) + [
            grid_tok,
            pl.BlockSpec((2, 1, 1, LRU_LANES), lambda s: (0, unit(s) // nt, 0, unit(s) % nt)),
            lane(CONV_W), lane(1),
            pl.BlockSpec((4, 1, LRU_LANES, LRU_LANES), lambda s: (0, unit(s) % nt, 0, 0)),
            lane(4), lane(2)],
        out_specs=[tok(fwd_out), tok(bwd_out), grid_tok],
        out_shape=[jax.ShapeDtypeStruct((bsz, n_lat_tok, GDN_WIDTH), F32)] * 2
                  + [jax.ShapeDtypeStruct((bsz, n_lat_tok, width), F32)],
        scratch_shapes=[state, state] + [lru_scratch] * 4,
        compiler_params=pltpu.CompilerParams(dimension_semantics=("arbitrary",),
                                             vmem_limit_bytes=VMEM_LIMIT),
        name="mixers",
    )(*local, rx, h0.reshape(2, bsz, 1, width), conv_w, conv_b, w_tiles, b_gate, lam)


def _out_ffn2_kernel(h1_ref, of_ref, ob_ref, z_ref, hs_ref, rg_ref, mod_ref, ng_ref, gnw_ref,
                     wout_ref, w1_ref, w3_ref, w2_ref, fg_ref, out_ref):
    mod = mod_ref[0]

    def sub_tile(rows):
        o = of_ref[0, rows] + ob_ref[0, rows]
        z = z_ref[0, rows]
        parts = []
        for h in range(GDN_HEADS):
            sl = slice(h * GDN_DK, (h + 1) * GDN_DK)
            oh = o[:, sl]
            parts.append(oh * lax.rsqrt(jnp.mean(oh * oh, axis=-1, keepdims=True) + EPS)
                         * gnw_ref[...] * _silu(z[:, sl]))
        parts.append(hs_ref[0, rows] * _gelu_tanh(rg_ref[0, rows]))
        mixed = jnp.concatenate(parts, axis=1).astype(BF16)
        (y,) = yield [lambda: jnp.dot(mixed, wout_ref[...], preferred_element_type=F32)]
        h2 = h1_ref[0, rows] + mod[5:6] * y
        ub = _rms_mod(h2, ng_ref[2:3], mod[6:7], mod[7:8]).astype(BF16)
        a, b = yield [lambda: jnp.dot(ub, w1_ref[...], preferred_element_type=F32),
                      lambda: jnp.dot(ub, w3_ref[...], preferred_element_type=F32)]
        g = (_silu(a) * b).astype(BF16)
        (f,) = yield [lambda: jnp.dot(g, w2_ref[...], preferred_element_type=F32)]
        h3 = h2 + FFN_RESIDUAL * mod[8:9] * f
        out_ref[0, rows] = h3 * lax.rsqrt(jnp.mean(h3 * h3, axis=-1, keepdims=True) + EPS) * fg_ref[...]

    _run_interleaved([sub_tile(slice(i * TOKEN_TILE, (i + 1) * TOKEN_TILE))
                      for i in range(out_ref.shape[1] // TOKEN_TILE)])


def _out_ffn2_call(h1, o_f, o_b, z, h_sum, rg, mods, norm_g, gdn_norm_w, w_out, w1, w3, w2, final_g):
    bsz, seq, _ = o_f.shape
    rows = OUT_SUB_TILES * TOKEN_TILE
    tok = lambda w: pl.BlockSpec((1, rows, w), lambda b, j: (b, j, 0))
    return pl.pallas_call(
        _out_ffn2_kernel,
        grid=(bsz, seq // rows),
        in_specs=[tok(D_MODEL), tok(GDN_WIDTH), tok(GDN_WIDTH), tok(GDN_WIDTH),
                  tok(LRU_WIDTH), tok(LRU_WIDTH),
                  pl.BlockSpec((1, N_MOD, D_MODEL), lambda b, j: (b + 1, 0, 0)),
                  _resident((3, D_MODEL)), _resident((1, GDN_DK)),
                  _resident((D_MODEL, D_MODEL)),
                  _resident((None, D_MODEL, D_FF), (1, 0, 0)), _resident((None, D_MODEL, D_FF), (1, 0, 0)),
                  _resident((None, D_FF, D_MODEL), (1, 0, 0)),
                  _resident((1, D_MODEL))],
        out_specs=tok(D_MODEL),
        out_shape=jax.ShapeDtypeStruct((bsz, seq, D_MODEL), F32),
        compiler_params=pltpu.CompilerParams(dimension_semantics=("arbitrary", "arbitrary"),
                                             vmem_limit_bytes=VMEM_LIMIT),
        name="out_ffn2",
    )(h1, o_f, o_b, z, h_sum, rg, mods, norm_g, gdn_norm_w, w_out, w1, w3, w2, final_g)


def _block_diag(w):
    nb, bw, _ = w.shape
    eye = jnp.eye(nb, dtype=w.dtype)
    return jnp.einsum('ncd,nm->ncmd', w, eye).reshape(nb * bw, nb * bw)


def kernel(x, c, ctx, c_ctx, w_ada, b_ada, norm_g, ffn_w1, ffn_w3, ffn_w2, w_in, w_out, gdn_conv_w, gdn_a_log,
           gdn_dt_bias, gdn_norm_w, lru_conv_w, lru_conv_b, lru_w_gate, lru_b_gate, lru_lambda, final_norm_g):
    bsz, seq, _ = x.shape
    assert w_ada.shape[0] == 1 and seq == GRID_W * GRID_W and ctx.shape[1] == TOKEN_TILE

    rows = -(-(bsz + 1) // 8) * 8
    cvec = jnp.concatenate([c_ctx[None, :], c, jnp.zeros((rows - bsz - 1, D_MODEL), F32)], axis=0)
    mods = _ada_call(cvec, w_ada[0], b_ada).reshape(rows, N_MOD, D_MODEL)

    w_in_bf = w_in[0].astype(BF16)
    o_ba = P_HEAD_COLS
    o_rx = o_ba + 4 * GDN_HEADS
    w_tail = jnp.concatenate([w_in_bf[:, o_rx:], w_in_bf[:, o_ba:o_rx],
                              jnp.zeros((D_MODEL, BA_PAD - 4 * GDN_HEADS), BF16)], axis=1)
    w1 = ffn_w1[0].astype(BF16)
    w3 = ffn_w3[0].astype(BF16)
    w2 = ffn_w2[0].astype(BF16)

    h1, qkv, z, rx, rg, ba = _ffn1_in_call(x, ctx, mods, norm_g[0], w1, w3, w2, w_in_bf, w_tail)

    pad_g = lambda t: jnp.zeros((1, BA_PAD), F32).at[0, 2 * GDN_HEADS:4 * GDN_HEADS].set(t.reshape(-1))
    local = _gdn_local_call(qkv, ba, gdn_conv_w[0], pad_g(gdn_a_log[0]), pad_g(gdn_dt_bias[0]), seq)

    wg = lru_w_gate[0]
    w_dense = [_block_diag(wg[d, g]).astype(BF16) for d in range(2) for g in range(2)]
    b_gate = lru_b_gate[0].reshape(4, LRU_WIDTH)
    lam = lru_lambda[0]
    cw, cb = lru_conv_w[0], lru_conv_b
    h_ctx = _lru_ctx_call(rx, seq, cw, cb, w_dense, b_gate, lam)
    nt = LRU_WIDTH // LRU_LANES
    w_tiles = jnp.stack([jnp.stack([w[t * LRU_LANES:(t + 1) * LRU_LANES, t * LRU_LANES:(t + 1) * LRU_LANES]
                                    for t in range(nt)]) for w in w_dense])
    o_f, o_b, h_sum = _mixers_call(local, rx, seq, h_ctx, cw, cb, w_tiles, b_gate, lam)

    return _out_ffn2_call(h1, o_f, o_b, z, h_sum, rg, mods, norm_g[0], gdn_norm_w,
                          w_out[0].astype(BF16), w1, w3, w2, final_norm_g[None, :])
```

```python
import functools

import jax
import jax.numpy as jnp
from jax import lax
from jax.experimental import pallas as pl
from jax.experimental.pallas import tpu as pltpu

D_MODEL = 1024
D_FF = 2816
N_MOD = 9
EPS = 1e-6
FFN_RESIDUAL = 0.5

GDN_WIDTH = 512
GDN_HEADS = 4
GDN_DK = 128
GDN_CHUNK = 64
CONV_W = 4
LRU_WIDTH = 512
LRU_BW = 64
LRU_C = 8.0
GRID_W = 64

ADA_MODS_PER_STEP = 3
TOKEN_TILE = 256
GDN_TILE_CHUNKS = TOKEN_TILE // GDN_CHUNK
GDN_SCAN_CHUNKS = 4
OUT_SUB_TILES = 2
FFN1_SUB_TILES = 2
HALO = 8
BA_PAD = 128
P_HEAD_COLS = 4 * GDN_WIDTH
P_TAIL_COLS = 2 * LRU_WIDTH + BA_PAD
LRU_LANES = 128
LRU_ROWS_PER_PASS = 8
VMEM_LIMIT = 56 * 1024 * 1024

BF16 = jnp.bfloat16
F32 = jnp.float32
NEG_BIG = -1e30
NT_DIMS = (((1,), (1,)), ((), ()))


def _mm(a, b):
    return jnp.dot(a.astype(BF16), b.astype(BF16), preferred_element_type=F32)


def _sigmoid(x):
    return 1.0 / (1.0 + jnp.exp(-x))


def _silu(x):
    return x * _sigmoid(x)


def _softplus(x):
    return jnp.maximum(x, 0.0) + jnp.log(1.0 + jnp.exp(-jnp.abs(x)))


def _gelu_tanh(x):
    return 0.5 * x * (1.0 + jnp.tanh(0.7978845608028654 * (x + 0.044715 * x * x * x)))


def _rms_mod(h, g, shift, scale):
    y = h * lax.rsqrt(jnp.mean(h * h, axis=-1, keepdims=True) + EPS) * g
    return y * (1.0 + scale) + shift


def _swiglu(u, w1_ref, w3_ref, w2_ref):
    ub = u.astype(BF16)
    a = jnp.dot(ub, w1_ref[...], preferred_element_type=F32)
    b = jnp.dot(ub, w3_ref[...], preferred_element_type=F32)
    g = (_silu(a) * b).astype(BF16)
    return jnp.dot(g, w2_ref[...], preferred_element_type=F32)


def _dwconv(x, prev, nxt, w):
    t = x.shape[0]
    row = lax.broadcasted_iota(jnp.int32, x.shape, 0)
    xm1 = jnp.where(row == 0, prev[HALO - 1:HALO], pltpu.roll(x, 1, 0))
    xm2 = jnp.where(row == 0, prev[HALO - 2:HALO - 1],
                    jnp.where(row == 1, prev[HALO - 1:HALO], pltpu.roll(x, 2, 0)))
    xp1 = jnp.where(row == t - 1, nxt[0:1], pltpu.roll(x, t - 1, 0))
    return xm2 * w[0:1] + xm1 * w[1:2] + x * w[2:3] + xp1 * w[3:4]


def _resident(shape, index=None):
    index = index or (0,) * len(shape)
    return pl.BlockSpec(shape, lambda *_: index, pipeline_mode=pl.Buffered(1))


def _ada_kernel(c_ref, w_ref, b_ref, o_ref):
    o_ref[...] = _mm(_silu(c_ref[...]), w_ref[...]) + b_ref[...]


def _ada_call(cvec, w_ada, b_ada):
    rows = cvec.shape[0]
    cols = ADA_MODS_PER_STEP * D_MODEL
    return pl.pallas_call(
        _ada_kernel,
        grid=(N_MOD // ADA_MODS_PER_STEP,),
        in_specs=[pl.BlockSpec((rows, D_MODEL), lambda k: (0, 0)),
                  pl.BlockSpec((D_MODEL, cols), lambda k: (0, k)),
                  pl.BlockSpec((1, cols), lambda k: (0, k))],
        out_specs=pl.BlockSpec((rows, cols), lambda k: (0, k)),
        out_shape=jax.ShapeDtypeStruct((rows, N_MOD * D_MODEL), F32),
        compiler_params=pltpu.CompilerParams(dimension_semantics=("arbitrary",),
                                             vmem_limit_bytes=VMEM_LIMIT),
        name="ada",
    )(cvec, w_ada, b_ada)


def _segment_edges(j, n_lat_tiles, n_all_tiles):
    return (jnp.logical_or(j == 0, j == n_lat_tiles),
            jnp.logical_or(j == n_lat_tiles - 1, j == n_all_tiles - 1))


def _ffn1_in_kernel(n_lat_tiles, n_all_tiles, *refs):
    n_sub = FFN1_SUB_TILES
    ins, refs = refs[:3 * n_sub], refs[3 * n_sub:]
    ng_ref, w1_ref, w3_ref, w2_ref, win_ref, wtail_ref, h1_ref, qkv_ref, z_ref, rx_ref, rg_ref, ba_ref = refs
    s = pl.program_id(0)

    def sub_tile(i):
        x_ref, ctx_ref, mod_ref = ins[3 * i:3 * i + 3]
        rows = slice(i * TOKEN_TILE, (i + 1) * TOKEN_TILE)
        h = jnp.where((s * n_sub + i) % n_all_tiles == n_lat_tiles, ctx_ref[0], x_ref[0])
        mod = mod_ref[0]
        ub = _rms_mod(h, ng_ref[0:1], mod[0:1], mod[1:2]).astype(BF16)
        a, b = yield [lambda: jnp.dot(ub, w1_ref[...], preferred_element_type=F32),
                      lambda: jnp.dot(ub, w3_ref[...], preferred_element_type=F32)]
        g = (_silu(a) * b).astype(BF16)
        (f,) = yield [lambda: jnp.dot(g, w2_ref[...], preferred_element_type=F32)]
        h1 = h + FFN_RESIDUAL * mod[2:3] * f
        h1_ref[rows] = h1
        ub2 = _rms_mod(h1, ng_ref[1:2], mod[3:4], mod[4:5]).astype(BF16)
        head, tail = yield [lambda: jnp.dot(ub2, win_ref[...], preferred_element_type=F32),
                            lambda: jnp.dot(ub2, wtail_ref[...], preferred_element_type=F32)]
        qkv_ref[rows] = head[:, :3 * GDN_WIDTH]
        z_ref[rows] = head[:, 3 * GDN_WIDTH:]
        rx_ref[rows] = tail[:, :LRU_WIDTH]
        rg_ref[rows] = tail[:, LRU_WIDTH:2 * LRU_WIDTH]
        ba_ref[rows] = tail[:, 2 * LRU_WIDTH:]

    _run_interleaved([sub_tile(i) for i in range(n_sub)])


def _ffn1_in_call(x, ctx, mods, norm_g, w1, w3, w2, w_in_bf, w_tail):
    bsz, seq, _ = x.shape
    n_lat = seq // TOKEN_TILE
    n_all = n_lat + 1
    t_all = n_all * TOKEN_TILE
    n_sub = FFN1_SUB_TILES
    assert (bsz * n_all) % n_sub == 0
    per_tile = []
    for i in range(n_sub):
        tile = lambda s, i=i: s * n_sub + i
        per_tile += [
            pl.BlockSpec((1, TOKEN_TILE, D_MODEL),
                         lambda s, t=tile: (t(s) // n_all, jnp.minimum(t(s) % n_all, n_lat - 1), 0)),
            pl.BlockSpec((1, TOKEN_TILE, D_MODEL), lambda s, t=tile: (t(s) // n_all, 0, 0)),
            pl.BlockSpec((1, N_MOD, D_MODEL),
                         lambda s, t=tile: (jnp.where(t(s) % n_all == n_lat, 0, t(s) // n_all + 1), 0, 0))]
    widths = (D_MODEL, 3 * GDN_WIDTH, GDN_WIDTH, LRU_WIDTH, LRU_WIDTH, BA_PAD)
    outs = pl.pallas_call(
        functools.partial(_ffn1_in_kernel, n_lat, n_all),
        grid=(bsz * n_all // n_sub,),
        in_specs=per_tile + [
            _resident((3, D_MODEL)),
            _resident((None, D_MODEL, D_FF), (0, 0, 0)), _resident((None, D_MODEL, D_FF), (0, 0, 0)),
            _resident((None, D_FF, D_MODEL), (0, 0, 0)),
            _resident((D_MODEL, P_HEAD_COLS)), _resident((D_MODEL, P_TAIL_COLS))],
        out_specs=[pl.BlockSpec((n_sub * TOKEN_TILE, w), lambda s: (s, 0)) for w in widths],
        out_shape=[jax.ShapeDtypeStruct((bsz * t_all, w), F32) for w in widths],
        compiler_params=pltpu.CompilerParams(dimension_semantics=("arbitrary",),
                                             vmem_limit_bytes=VMEM_LIMIT),
        name="ffn1_in",
    )(*([x, ctx, mods] * n_sub), norm_g, w1, w3, w2, w_in_bf, w_tail)
    return [o.reshape(bsz, t_all, o.shape[-1]) for o in outs]


def _pair_block_diag(x, first):
    return jnp.concatenate([jnp.where(first, x, 0.0), jnp.where(first, 0.0, x)], axis=0)


def _run_interleaved(coroutines):
    active = [[gen, next(gen)] for gen in coroutines]
    while active:
        still = []
        for item in active:
            results = [dot() for dot in item[1]]
            try:
                item[1] = item[0].send(results)
                still.append(item)
            except StopIteration:
                pass
        active = still


def _gdn_local_body(seg_start, seg_end, x_ref, prev_ref, next_ref, ba_ref, cw_ref, gp_ref,
                    out_refs, stage_in, stage_out):
    c = GDN_CHUNK
    pairs = GDN_HEADS // 2
    row = lax.broadcasted_iota(jnp.int32, (c, 2 * c), 0)
    lane2 = lax.broadcasted_iota(jnp.int32, (c, 2 * c), 1)
    col = lane2 & (c - 1)
    first = lane2 < c
    first_row = first[0:1]
    eye2 = (row == col).astype(F32)
    zeros_k = jnp.zeros((c, GDN_DK), BF16)
    unit_index = lambda ci, d, p: (ci * 2 + d) * pairs + p

    def coupling(k):
        return jnp.logical_and((row >> k) != (col >> k), (row >> (k + 1)) == (col >> (k + 1)))

    lhs_in, kr_in, rhs_in, dec_in, kdt_in, qe_in, eg_in = stage_in

    def pair_unit(ci, d, p):
        u_ref, wq_ref, ak_ref, eg_ref = out_refs[d]
        un = unit_index(ci, d, p)
        strict = (row < col) if d else (row > col)
        k_rows = kr_in[ci * pairs + p]
        k_bd = jnp.concatenate([jnp.concatenate([k_rows[:c], zeros_k], axis=1),
                                jnp.concatenate([zeros_k, k_rows[c:]], axis=1)], axis=0)
        lhs = lhs_in[un]
        (prods,) = yield [lambda: lax.dot_general(lhs, k_bd, NT_DIMS, preferred_element_type=F32)]
        decay = dec_in[un]
        a = jnp.where(strict, prods[:c] * decay, 0.0)
        ps = slice(p * 2 * c, (p + 1) * 2 * c)
        ak_ref[ci, 0, 0:c, ps] = (prods[c:] * decay).astype(BF16)
        ak_ref[ci, 0, c:3 * c, ps] = kdt_in[un]
        t = eye2 - jnp.where(coupling(0), a, 0.0)
        k = 1
        while 2 ** k < c:
            l = jnp.where(coupling(k), a, 0.0)
            t_bd = _pair_block_diag(t, first)
            (lt,) = yield [lambda: _mm(l, t_bd)]
            lt_bd = _pair_block_diag(lt, first)
            (tlt,) = yield [lambda: _mm(t, lt_bd)]
            t = t - tlt
            k += 1
        t_bd = _pair_block_diag(t, first).astype(BF16)
        rhs = rhs_in[un]
        (sol,) = yield [lambda: jnp.dot(t_bd, rhs, preferred_element_type=F32)]
        for i in range(2):
            h = 2 * p + i
            hs = slice(h * GDN_DK, (h + 1) * GDN_DK)
            u_ref[ci, 0, :, hs] = sol[i * c:(i + 1) * c, :GDN_DK]
            wq_ref[ci, 0, 0:c, hs] = sol[i * c:(i + 1) * c, GDN_DK:].astype(BF16)
            wq_ref[ci, 0, c:2 * c, hs] = qe_in[un, i * c:(i + 1) * c]
            eg_ref[ci, 0, h:h + 1, :] = eg_in[un, i:i + 1]

    lhs_out, kr_out, rhs_out, dec_out, kdt_out, qe_out, eg_out = stage_out
    y = _silu(_dwconv(x_ref[0], jnp.where(seg_start, 0.0, prev_ref[0]), jnp.where(seg_end, 0.0, next_ref[0]),
                      cw_ref[...]))
    ba = ba_ref[0]
    lane = lax.broadcasted_iota(jnp.int32, (1, BA_PAD), 1)
    is_g = jnp.logical_and(lane >= 2 * GDN_HEADS, lane < 4 * GDN_HEADS)
    beta_all = _sigmoid(ba)
    g_all = jnp.where(is_g, -jnp.exp(gp_ref[0:1]) * _softplus(ba + gp_ref[1:2]), 0.0)

    def chunk(ci):
        rs = slice(ci * c, (ci + 1) * c)
        heads = []
        for h in range(GDN_HEADS):
            lo = h * GDN_DK
            q = y[rs, lo:lo + GDN_DK]
            k = y[rs, GDN_WIDTH + lo:GDN_WIDTH + lo + GDN_DK]
            v = y[rs, 2 * GDN_WIDTH + lo:2 * GDN_WIDTH + lo + GDN_DK]
            q = q * (lax.rsqrt(jnp.sum(q * q, axis=-1, keepdims=True) + EPS) * (GDN_DK ** -0.5))
            k = k * lax.rsqrt(jnp.sum(k * k, axis=-1, keepdims=True) + EPS)
            heads.append((q, k, v))
        g = g_all[rs]
        k_rows = [jnp.concatenate([heads[2 * p][1], heads[2 * p + 1][1]], axis=0) for p in range(pairs)]
        k_ts = [kr.T for kr in k_rows]
        grow = lax.broadcasted_iota(jnp.int32, g.shape, 0)
        gcum_f = g
        step = 1
        while step < c:
            gcum_f = gcum_f + jnp.where(grow >= step, pltpu.roll(gcum_f, step, 0), 0.0)
            step *= 2
        gcum_b = gcum_f[c - 1:c] - gcum_f + g
        gcum_t = jnp.concatenate([gcum_f, gcum_b], axis=0).T
        beta_c = beta_all[rs]
        for p in range(pairs):
            kr_out[ci * pairs + p] = k_rows[p].astype(BF16)
        for d in range(2):
            gcum = gcum_b if d else gcum_f
            incl = (row <= col) if d else (row >= col)
            last = 0 if d else c - 1
            for p in range(pairs):
                un = unit_index(ci, d, p)
                (q0, k0, v0), (q1, k1, v1) = heads[2 * p], heads[2 * p + 1]
                cb = d * GDN_HEADS + 2 * p
                cg = 2 * GDN_HEADS + cb
                beta0, beta1 = beta_c[:, cb:cb + 1], beta_c[:, cb + 1:cb + 2]
                gc0, gc1 = gcum[:, cg:cg + 1], gcum[:, cg + 1:cg + 2]
                gt0, gt1 = gcum_t[cg:cg + 1], gcum_t[cg + 1:cg + 2]
                gr = (jnp.where(first_row, pltpu.roll(gt0, c, 1), gt1) if d
                      else jnp.where(first_row, gt0, pltpu.roll(gt1, c, 1)))
                gc = jnp.where(first, gc0, gc1)
                e0, e1 = jnp.exp(gc0), jnp.exp(gc1)
                kb0, kb1 = k0 * beta0, k1 * beta1
                dec_out[un] = jnp.exp(jnp.where(incl, gc - gr, NEG_BIG))
                lhs_out[un] = jnp.concatenate([jnp.concatenate([kb0, kb1], axis=1),
                                               jnp.concatenate([q0, q1], axis=1)], axis=0).astype(BF16)
                rhs_out[un] = jnp.concatenate([jnp.concatenate([v0 * beta0, kb0 * e0], axis=1),
                                               jnp.concatenate([v1 * beta1, kb1 * e1], axis=1)],
                                              axis=0).astype(BF16)
                kdt_out[un] = (k_ts[p] * jnp.exp(gc[last:last + 1] - gr)).astype(BF16)
                qe_out[un] = jnp.concatenate([q0 * e0, q1 * e1], axis=0).astype(BF16)
                eg_out[un] = jnp.concatenate(
                    [jnp.broadcast_to(jnp.exp(gc0[last:last + 1]), (1, BA_PAD)),
                     jnp.broadcast_to(jnp.exp(gc1[last:last + 1]), (1, BA_PAD)),
                     jnp.zeros((HALO - 2, BA_PAD), F32)], axis=0)

    for refs in out_refs:
        refs[3][...] = jnp.zeros_like(refs[3])
    _run_interleaved([pair_unit(ci, d, p) for ci in range(GDN_TILE_CHUNKS) for d in range(2)
                      for p in range(pairs)])
    for ci in range(GDN_TILE_CHUNKS):
        chunk(ci)


def _gdn_local_kernel(n_lat_tiles, n_all_tiles, n_tiles, x_ref, prev_ref, next_ref, ba_ref, cw_ref, gp_ref,
                      uf_ref, wqf_ref, akf_ref, egf_ref, ub_ref, wqb_ref, akb_ref, egb_ref, *stage_refs):
    s = pl.program_id(0)
    seg_start, seg_end = _segment_edges(jnp.minimum(s, n_tiles - 1) % n_all_tiles, n_lat_tiles, n_all_tiles)
    out_refs = ((uf_ref, wqf_ref, akf_ref, egf_ref), (ub_ref, wqb_ref, akb_ref, egb_ref))
    half = len(stage_refs) // 2
    sets = (stage_refs[:half], stage_refs[half:])

    @pl.when(s == 0)
    def _():
        for r in sets[1]:
            r[...] = jnp.zeros_like(r)

    for parity in range(2):
        @pl.when(s % 2 == parity)
        def _():
            _gdn_local_body(seg_start, seg_end, x_ref, prev_ref, next_ref, ba_ref, cw_ref, gp_ref,
                            out_refs, sets[1 - parity], sets[parity])


def _gdn_local_call(qkv, ba, conv_w, gate_params, n_lat_tok):
    bsz, t_all, width = qkv.shape
    c = GDN_CHUNK
    tile = GDN_TILE_CHUNKS * c
    n_all = t_all // tile
    n_lat = n_lat_tok // tile
    per_tile = tile // HALO
    last_halo = t_all // HALO - 1
    n_chunks = t_all // c
    n_tiles = bsz * n_all
    pairs = GDN_HEADS // 2
    units = GDN_TILE_CHUNKS * 2 * pairs
    t_in = lambda s: jnp.minimum(s, n_tiles - 1)
    t_out = lambda s: jnp.maximum(s - 1, 0)
    in_tok = lambda w: pl.BlockSpec((1, tile, w), lambda s: (t_in(s) // n_all, t_in(s) % n_all, 0))
    cblk = lambda rows, w: pl.BlockSpec((GDN_TILE_CHUNKS, 1, rows, w),
                                        lambda s: (t_out(s) % n_all, t_out(s) // n_all, 0, 0))
    vec = pl.BlockSpec((2, BA_PAD), lambda s: (0, 0))
    outs = [(jax.ShapeDtypeStruct((n_chunks, bsz, c, GDN_WIDTH), F32), cblk(c, GDN_WIDTH)),
            (jax.ShapeDtypeStruct((n_chunks, bsz, 2 * c, GDN_WIDTH), BF16), cblk(2 * c, GDN_WIDTH)),
            (jax.ShapeDtypeStruct((n_chunks, bsz, 3 * c, GDN_HEADS * c), BF16), cblk(3 * c, GDN_HEADS * c)),
            (jax.ShapeDtypeStruct((n_chunks, bsz, HALO, BA_PAD), F32), cblk(HALO, BA_PAD))] * 2
    stage = [pltpu.VMEM((units, 2 * c, 2 * GDN_DK), BF16),
             pltpu.VMEM((GDN_TILE_CHUNKS * pairs, 2 * c, GDN_DK), BF16),
             pltpu.VMEM((units, 2 * c, 2 * GDN_DK), BF16),
             pltpu.VMEM((units, c, 2 * c), F32),
             pltpu.VMEM((units, GDN_DK, 2 * c), BF16),
             pltpu.VMEM((units, 2 * c, GDN_DK), BF16),
             pltpu.VMEM((units, HALO, BA_PAD), F32)]
    return pl.pallas_call(
        functools.partial(_gdn_local_kernel, n_lat, n_all, n_tiles),
        grid=(n_tiles + 1,),
        in_specs=[in_tok(width),
                  pl.BlockSpec((1, HALO, width),
                               lambda s: (t_in(s) // n_all, jnp.maximum((t_in(s) % n_all) * per_tile - 1, 0), 0)),
                  pl.BlockSpec((1, HALO, width),
                               lambda s: (t_in(s) // n_all,
                                          jnp.minimum((t_in(s) % n_all + 1) * per_tile, last_halo), 0)),
                  in_tok(BA_PAD), pl.BlockSpec((CONV_W, width), lambda s: (0, 0)), vec],
        out_specs=[o[1] for o in outs],
        out_shape=[o[0] for o in outs],
        scratch_shapes=stage + stage,
        compiler_params=pltpu.CompilerParams(dimension_semantics=("arbitrary",),
                                             vmem_limit_bytes=VMEM_LIMIT),
        name="gdn_local",
    )(qkv, qkv, qkv, ba, conv_w, gate_params)


def _gdn_scan_steps(uf_ref, wqf_ref, akf_ref, egf_ref, ub_ref, wqb_ref, akb_ref, egb_ref,
                    of_ref, ob_ref, sf_ref, sb_ref):
    c = GDN_CHUNK
    bsz = uf_ref.shape[1]
    hs = lambda h: slice(h * GDN_DK, (h + 1) * GDN_DK)
    for sub in range(GDN_SCAN_CHUNKS):
        chains = [(refs, ci, b, h)
                  for refs, ci in (((uf_ref, wqf_ref, akf_ref, egf_ref, of_ref, sf_ref), sub),
                                   ((ub_ref, wqb_ref, akb_ref, egb_ref, ob_ref, sb_ref),
                                    GDN_SCAN_CHUNKS - 1 - sub))
                  for b in range(bsz) for h in range(GDN_HEADS)]
        states = [refs[5][b * GDN_HEADS + h] for refs, ci, b, h in chains]
        with_s = yield [lambda refs=refs, ci=ci, b=b, h=h, s=s:
                        jnp.dot(refs[1][ci, b, :, hs(h)], s.astype(BF16), preferred_element_type=F32)
                        for (refs, ci, b, h), s in zip(chains, states)]
        v_new = [refs[0][ci, b, :, hs(h)] - r[:c] for (refs, ci, b, h), r in zip(chains, with_s)]
        with_v = yield [lambda refs=refs, ci=ci, b=b, h=h, v=v:
                        jnp.dot(refs[2][ci, b, :, h * c:(h + 1) * c], v.astype(BF16), preferred_element_type=F32)
                        for (refs, ci, b, h), v in zip(chains, v_new)]
        for (refs, ci, b, h), s, r, a in zip(chains, states, with_s, with_v):
            refs[4][b, ci * c:(ci + 1) * c, hs(h)] = r[c:] + a[:c]
            refs[5][b * GDN_HEADS + h] = s * refs[3][ci, b, h:h + 1, :] + a[c:]


def _lin_scan(a, b, reverse):
    r = a.shape[0]
    row = lax.broadcasted_iota(jnp.int32, a.shape, 0)
    s = 1
    while s < r:
        ok = (row < r - s) if reverse else (row >= s)
        shift = r - s if reverse else s
        a_sh = jnp.where(ok, pltpu.roll(a, shift, 0), 1.0)
        b_sh = jnp.where(ok, pltpu.roll(b, shift, 0), 0.0)
        b = a * b_sh + b
        a = a * a_sh
        s *= 2
    return a, b


def _lru_ctx_direction(reverse, n_slabs, slab, x_ref, prev_ref, next_ref, cw, cb, w_ref, br, bi, lam, carry_ref):
    bsz, r, width = x_ref.shape
    step = pl.program_id(0)
    has_prev = slab > 0
    has_next = slab < n_slabs - 1
    xs = []
    for b in range(bsz):
        prev = jnp.where(has_prev, prev_ref[b], 0.0)
        nxt = jnp.where(has_next, next_ref[b], 0.0)
        xs.append(_dwconv(x_ref[b], prev, nxt, cw) + cb)
    x = jnp.concatenate(xs, axis=0)
    xb = x.astype(BF16)
    d = 1 if reverse else 0
    gate = lambda g: jnp.concatenate(
        [jnp.dot(xb[:, t * LRU_LANES:(t + 1) * LRU_LANES], w_ref[2 * d + g, t], preferred_element_type=F32)
         for t in range(width // LRU_LANES)], axis=1)
    gate_r = _sigmoid(gate(0) + br)
    gate_i = _sigmoid(gate(1) + bi)
    log_a = (-LRU_C * _softplus(-lam)) * gate_r
    a = jnp.exp(log_a)
    mult = jnp.sqrt(1.0 - a * a)
    row = lax.broadcasted_iota(jnp.int32, (r, width), 0)
    is_first = jnp.logical_and(step == 0, row == ((r - 1) if reverse else 0))
    edge = 0 if reverse else r - 1
    for b in range(bsz):
        sl = slice(b * r, (b + 1) * r)
        m = jnp.where(is_first, 1.0, mult[sl])
        a_cum, h = _lin_scan(a[sl], m * (gate_i[sl] * x[sl]), reverse)
        h = h + a_cum * carry_ref[b:b + 1]
        carry_ref[b:b + 1] = h[edge:edge + 1]


def _lru_ctx_kernel(n_slabs, xf_ref, pf_ref, nf_ref, xb_ref, pb_ref, nb_ref, cw_ref, cb_ref,
                    w_ref, bg_ref, lam_ref, fin_ref, cf_ref, cbk_ref):
    step = pl.program_id(0)

    @pl.when(step == 0)
    def _():
        cf_ref[...] = jnp.zeros_like(cf_ref)
        cbk_ref[...] = jnp.zeros_like(cbk_ref)

    cw = cw_ref[...]
    cb = cb_ref[...]
    _lru_ctx_direction(False, n_slabs, step, xf_ref, pf_ref, nf_ref, cw, cb, w_ref,
                       bg_ref[0:1], bg_ref[1:2], lam_ref[0:1], cf_ref)
    _lru_ctx_direction(True, n_slabs, n_slabs - 1 - step, xb_ref, pb_ref, nb_ref, cw, cb, w_ref,
                       bg_ref[2:3], bg_ref[3:4], lam_ref[1:2], cbk_ref)
    fin_ref[0] = cf_ref[...]
    fin_ref[1] = cbk_ref[...]


def _lru_ctx_call(rx, n_lat_tok, conv_w, conv_b, w_tiles, b_gate, lam):
    bsz, t_all, width = rx.shape
    r = GRID_W
    per = r // HALO
    n_slabs = (t_all - n_lat_tok) // r
    first = n_lat_tok // r
    main = lambda f: pl.BlockSpec((bsz, r, width), lambda s: (0, first + f(s), 0))
    prev = lambda f: pl.BlockSpec((bsz, HALO, width), lambda s: (0, (first + f(s)) * per - 1, 0))
    nxt = lambda f: pl.BlockSpec(
        (bsz, HALO, width), lambda s: (0, jnp.minimum((first + f(s) + 1) * per, t_all // HALO - 1), 0))
    fwd = lambda s: s
    bwd = lambda s: n_slabs - 1 - s
    full = lambda shape: pl.BlockSpec(shape, lambda s: (0,) * len(shape))
    carry = pltpu.VMEM((bsz, width), F32)
    return pl.pallas_call(
        functools.partial(_lru_ctx_kernel, n_slabs),
        grid=(n_slabs,),
        in_specs=[main(fwd), prev(fwd), nxt(fwd), main(bwd), prev(bwd), nxt(bwd),
                  full((CONV_W, width)), full((1, width)), full(w_tiles.shape),
                  full((4, width)), full((2, width))],
        out_specs=full((2, bsz, width)),
        out_shape=jax.ShapeDtypeStruct((2, bsz, width), F32),
        scratch_shapes=[carry, carry],
        compiler_params=pltpu.CompilerParams(dimension_semantics=("arbitrary",),
                                             vmem_limit_bytes=VMEM_LIMIT),
        name="lru_ctx",
    )(rx, rx, rx, rx, rx, rx, conv_w, conv_b, w_tiles, b_gate, lam)


def _lru_grid_gates(x_ref, cw_ref, cb_ref, w_ref, bg_ref, lam_ref, af_ref, bf_ref, ab_ref, bb_ref):
    n = GRID_W
    rp = LRU_ROWS_PER_PASS
    lanes = x_ref.shape[-1]
    crow = lax.broadcasted_iota(jnp.int32, (n, lanes), 0)
    cw = cw_ref[...]
    cb = cb_ref[...]

    def grid_rows(lo, hi):
        parts = []
        for r in range(lo, min(hi, 0)):
            parts.append(jnp.where(crow == 0, 0.0, pltpu.roll(x_ref[0, (n + r) * n:(n + r + 1) * n, :], 1, 0)))
        if max(lo, 0) < min(hi, n):
            parts.append(x_ref[0, max(lo, 0) * n:min(hi, n) * n, :])
        for r in range(max(lo, n), hi):
            parts.append(jnp.where(crow == n - 1, 0.0,
                                   pltpu.roll(x_ref[0, (r - n) * n:(r - n + 1) * n, :], n - 1, 0)))
        return parts[0] if len(parts) == 1 else jnp.concatenate(parts, axis=0)

    for r0 in range(0, n, rp):
        sl = slice(r0 * n, (r0 + rp) * n)
        xc = (grid_rows(r0 - 2, r0 + rp - 2) * cw[0:1] + grid_rows(r0 - 1, r0 + rp - 1) * cw[1:2]
              + grid_rows(r0, r0 + rp) * cw[2:3] + grid_rows(r0 + 1, r0 + rp + 1) * cw[3:4] + cb)
        xb = xc.astype(BF16)
        gates = yield [lambda i=i: jnp.dot(xb, w_ref[i, 0], preferred_element_type=F32) for i in range(4)]
        for d, (a_ref, b_ref) in enumerate(((af_ref, bf_ref), (ab_ref, bb_ref))):
            gate_r = _sigmoid(gates[2 * d] + bg_ref[2 * d:2 * d + 1])
            gate_i = _sigmoid(gates[2 * d + 1] + bg_ref[2 * d + 1:2 * d + 2])
            log_a = (-LRU_C * _softplus(-lam_ref[d:d + 1])) * gate_r
            a = jnp.exp(log_a)
            a_ref[sl] = a
            b_ref[sl] = jnp.sqrt(1.0 - a * a) * (gate_i * xc)


def _lru_grid_finish(h0_ref, out_ref, af_ref, bf_ref, ab_ref, bb_ref):
    n = GRID_W
    lanes = out_ref.shape[-1]
    crow = lax.broadcasted_iota(jnp.int32, (n, lanes), 0)
    hf = jnp.zeros((n, lanes), F32)
    hb = jnp.zeros((n, lanes), F32)
    pf = jnp.ones((n, lanes), F32)
    pb = jnp.ones((n, lanes), F32)
    for i in range(n):
        sf = slice(i * n, (i + 1) * n)
        sb = slice((n - 1 - i) * n, (n - i) * n)
        a = af_ref[sf]
        hf = a * hf + bf_ref[sf]
        pf = a * pf
        bf_ref[sf] = hf
        af_ref[sf] = pf
        a = ab_ref[sb]
        hb = a * hb + bb_ref[sb]
        pb = a * pb
        bb_ref[sb] = hb
        ab_ref[sb] = pb

    h0f = h0_ref[0, 0]
    h0b = h0_ref[1, 0]
    acc_a, acc_h = _lin_scan(pf, hf, False)
    in_f = jnp.where(crow == 0, h0f, pltpu.roll(acc_h + acc_a * h0f, 1, 0))
    acc_a, acc_h = _lin_scan(pb, hb, True)
    in_b = jnp.where(crow == n - 1, h0b, pltpu.roll(acc_h + acc_a * h0b, n - 1, 0))

    for r in range(n):
        sl = slice(r * n, (r + 1) * n)
        out_ref[0, sl, :] = (bf_ref[sl] + af_ref[sl] * in_f) + (bb_ref[sl] + ab_ref[sl] * in_b)


def _mixers_kernel(*refs):
    scan_in, lru_in, outs, scratch = refs[:8], refs[8:15], refs[15:18], refs[18:]
    of_ref, ob_ref, hs_ref = outs
    sf_ref, sb_ref = scratch[:2]

    @pl.when(pl.program_id(0) == 0)
    def _():
        sf_ref[...] = jnp.zeros_like(sf_ref)
        sb_ref[...] = jnp.zeros_like(sb_ref)

    x_ref, h0_ref = lru_in[:2]
    _run_interleaved([_gdn_scan_steps(*scan_in, of_ref, ob_ref, sf_ref, sb_ref),
                      _lru_grid_gates(x_ref, *lru_in[2:], *scratch[2:])])
    _lru_grid_finish(h0_ref, hs_ref, *scratch[2:])


def _mixers_call(local, rx, n_lat_tok, h0, conv_w, conv_b, w_tiles, b_gate, lam):
    n_chunks, bsz = local[0].shape[:2]
    width = rx.shape[2]
    c = GDN_CHUNK
    g = GDN_SCAN_CHUNKS
    n_all = n_chunks // g
    n_lat = n_lat_tok // (c * g)
    n_ctx = n_all - n_lat
    fwd = lambda s: jnp.where(s < n_ctx, n_lat + s, s - n_ctx)
    bwd = lambda s: n_all - 1 - s
    fwd_out = lambda s: jnp.maximum(s - n_ctx, 0)
    bwd_out = lambda s: jnp.minimum(n_all - 1 - s, n_lat - 1)
    cblk = lambda rows, w, f: pl.BlockSpec((g, bsz, rows, w), lambda s: (f(s), 0, 0, 0))
    ins = lambda f: [cblk(c, GDN_WIDTH, f), cblk(2 * c, GDN_WIDTH, f), cblk(3 * c, GDN_HEADS * c, f),
                     cblk(HALO, BA_PAD, f)]
    tok = lambda f: pl.BlockSpec((bsz, g * c, GDN_WIDTH), lambda s: (0, f(s), 0))
    state = pltpu.VMEM((bsz * GDN_HEADS, GDN_DK, GDN_DK), F32)

    nt = width // LRU_LANES
    unit = lambda s: jnp.minimum(s, bsz * nt - 1)
    assert n_all >= bsz * nt
    lane = lambda rows: pl.BlockSpec((rows, LRU_LANES), lambda s: (0, unit(s) % nt))
    grid_tok = pl.BlockSpec((1, n_lat_tok, LRU_LANES), lambda s: (unit(s) // nt, 0, unit(s) % nt))
    lru_scratch = pltpu.VMEM((n_lat_tok, LRU_LANES), F32)
    return pl.pallas_call(
        _mixers_kernel,
        grid=(n_all,),
        in_specs=ins(fwd) + ins(bwd) + [
            grid_tok,
            pl.BlockSpec((2, 1, 1, LRU_LANES), lambda s: (0, unit(s) // nt, 0, unit(s) % nt)),
            lane(CONV_W), lane(1),
            pl.BlockSpec((4, 1, LRU_LANES, LRU_LANES), lambda s: (0, unit(s) % nt, 0, 0)),
            lane(4), lane(2)],
        out_specs=[tok(fwd_out), tok(bwd_out), grid_tok],
        out_shape=[jax.ShapeDtypeStruct((bsz, n_lat_tok, GDN_WIDTH), F32)] * 2
                  + [jax.ShapeDtypeStruct((bsz, n_lat_tok, width), F32)],
        scratch_shapes=[state, state] + [lru_scratch] * 4,
        compiler_params=pltpu.CompilerParams(dimension_semantics=("arbitrary",),
                                             vmem_limit_bytes=VMEM_LIMIT),
        name="mixers",
    )(*local, rx, h0.reshape(2, bsz, 1, width), conv_w, conv_b, w_tiles, b_gate, lam)


def _out_ffn2_kernel(h1_ref, of_ref, ob_ref, z_ref, hs_ref, rg_ref, mod_ref, ng_ref, gnw_ref,
                     wout_ref, w1_ref, w3_ref, w2_ref, fg_ref, out_ref):
    mod = mod_ref[0]

    def sub_tile(rows):
        o = of_ref[0, rows] + ob_ref[0, rows]
        z = z_ref[0, rows]
        parts = []
        for h in range(GDN_HEADS):
            sl = slice(h * GDN_DK, (h + 1) * GDN_DK)
            oh = o[:, sl]
            parts.append(oh * lax.rsqrt(jnp.mean(oh * oh, axis=-1, keepdims=True) + EPS)
                         * gnw_ref[...] * _silu(z[:, sl]))
        parts.append(hs_ref[0, rows] * _gelu_tanh(rg_ref[0, rows]))
        mixed = jnp.concatenate(parts, axis=1).astype(BF16)
        (y,) = yield [lambda: jnp.dot(mixed, wout_ref[...], preferred_element_type=F32)]
        h2 = h1_ref[0, rows] + mod[5:6] * y
        ub = _rms_mod(h2, ng_ref[2:3], mod[6:7], mod[7:8]).astype(BF16)
        a, b = yield [lambda: jnp.dot(ub, w1_ref[...], preferred_element_type=F32),
                      lambda: jnp.dot(ub, w3_ref[...], preferred_element_type=F32)]
        g = (_silu(a) * b).astype(BF16)
        (f,) = yield [lambda: jnp.dot(g, w2_ref[...], preferred_element_type=F32)]
        h3 = h2 + FFN_RESIDUAL * mod[8:9] * f
        out_ref[0, rows] = h3 * lax.rsqrt(jnp.mean(h3 * h3, axis=-1, keepdims=True) + EPS) * fg_ref[...]

    _run_interleaved([sub_tile(slice(i * TOKEN_TILE, (i + 1) * TOKEN_TILE))
                      for i in range(out_ref.shape[1] // TOKEN_TILE)])


def _out_ffn2_call(h1, o_f, o_b, z, h_sum, rg, mods, norm_g, gdn_norm_w, w_out, w1, w3, w2, final_g):
    bsz, seq, _ = o_f.shape
    rows = OUT_SUB_TILES * TOKEN_TILE
    tok = lambda w: pl.BlockSpec((1, rows, w), lambda b, j: (b, j, 0))
    return pl.pallas_call(
        _out_ffn2_kernel,
        grid=(bsz, seq // rows),
        in_specs=[tok(D_MODEL), tok(GDN_WIDTH), tok(GDN_WIDTH), tok(GDN_WIDTH),
                  tok(LRU_WIDTH), tok(LRU_WIDTH),
                  pl.BlockSpec((1, N_MOD, D_MODEL), lambda b, j: (b + 1, 0, 0)),
                  _resident((3, D_MODEL)), _resident((1, GDN_DK)),
                  _resident((D_MODEL, D_MODEL)),
                  _resident((None, D_MODEL, D_FF), (1, 0, 0)), _resident((None, D_MODEL, D_FF), (1, 0, 0)),
                  _resident((None, D_FF, D_MODEL), (1, 0, 0)),
                  _resident((1, D_MODEL))],
        out_specs=tok(D_MODEL),
        out_shape=jax.ShapeDtypeStruct((bsz, seq, D_MODEL), F32),
        compiler_params=pltpu.CompilerParams(dimension_semantics=("arbitrary", "arbitrary"),
                                             vmem_limit_bytes=VMEM_LIMIT),
        name="out_ffn2",
    )(h1, o_f, o_b, z, h_sum, rg, mods, norm_g, gdn_norm_w, w_out, w1, w3, w2, final_g)


def kernel(x, c, ctx, c_ctx, w_ada, b_ada, norm_g, ffn_w1, ffn_w3, ffn_w2, w_in, w_out, gdn_conv_w, gdn_a_log,
           gdn_dt_bias, gdn_norm_w, lru_conv_w, lru_conv_b, lru_w_gate, lru_b_gate, lru_lambda, final_norm_g):
    bsz, seq, _ = x.shape
    assert w_ada.shape[0] == 1 and seq == GRID_W * GRID_W and ctx.shape[1] == TOKEN_TILE

    rows = -(-(bsz + 1) // 8) * 8
    cvec = jnp.concatenate([c_ctx[None, :], c, jnp.zeros((rows - bsz - 1, D_MODEL), F32)], axis=0)
    mods = _ada_call(cvec, w_ada[0], b_ada).reshape(rows, N_MOD, D_MODEL)

    w_in_bf = w_in[0].astype(BF16)
    o_ba = P_HEAD_COLS
    o_rx = o_ba + 4 * GDN_HEADS
    w_tail = jnp.concatenate([w_in_bf[:, o_rx:], w_in_bf[:, o_ba:o_rx],
                              jnp.zeros((D_MODEL, BA_PAD - 4 * GDN_HEADS), BF16)], axis=1)
    w1 = ffn_w1[0].astype(BF16)
    w3 = ffn_w3[0].astype(BF16)
    w2 = ffn_w2[0].astype(BF16)

    h1, qkv, z, rx, rg, ba = _ffn1_in_call(x, ctx, mods, norm_g[0], w1, w3, w2, w_in_bf, w_tail)

    gate_params = jnp.pad(jnp.stack([gdn_a_log[0].reshape(-1), gdn_dt_bias[0].reshape(-1)]),
                          ((0, 0), (2 * GDN_HEADS, BA_PAD - 4 * GDN_HEADS)))
    local = _gdn_local_call(qkv, ba, gdn_conv_w[0], gate_params, seq)

    nt = LRU_WIDTH // LRU_LANES
    per_tile = LRU_LANES // LRU_BW
    w_tiles = jnp.einsum('gticd,ij->gticjd', lru_w_gate[0].reshape(4, nt, per_tile, LRU_BW, LRU_BW),
                         jnp.eye(per_tile, dtype=F32)).reshape(4, nt, LRU_LANES, LRU_LANES).astype(BF16)
    b_gate = lru_b_gate[0].reshape(4, LRU_WIDTH)
    lam = lru_lambda[0]
    cw, cb = lru_conv_w[0], lru_conv_b
    h_ctx = _lru_ctx_call(rx, seq, cw, cb, w_tiles, b_gate, lam)
    o_f, o_b, h_sum = _mixers_call(local, rx, seq, h_ctx, cw, cb, w_tiles, b_gate, lam)

    return _out_ffn2_call(h1, o_f, o_b, z, h_sum, rg, mods, norm_g[0], gdn_norm_w,
                          w_out[0].astype(BF16), w1, w3, w2, final_norm_g[None, :])
```

```python
import functools

import jax
import jax.numpy as jnp
from jax import lax
from jax.experimental import pallas as pl
from jax.experimental.pallas import tpu as pltpu

D_MODEL = 1024
D_FF = 2816
N_MOD = 9
EPS = 1e-6
FFN_RESIDUAL = 0.5

GDN_WIDTH = 512
GDN_HEADS = 4
GDN_DK = 128
GDN_CHUNK = 64
CONV_W = 4
LRU_WIDTH = 512
LRU_BW = 64
LRU_C = 8.0
GRID_W = 64

ADA_MODS_PER_STEP = 3
TOKEN_TILE = 256
GDN_TILE_CHUNKS = TOKEN_TILE // GDN_CHUNK
GDN_SCAN_CHUNKS = 4
OUT_SUB_TILES = 2
FFN1_SUB_TILES = 2
HALO = 8
BA_PAD = 128
P_HEAD_COLS = 4 * GDN_WIDTH
P_TAIL_COLS = 2 * LRU_WIDTH + BA_PAD
LRU_LANES = 128
LRU_ROWS_PER_PASS = 8
VMEM_LIMIT = 56 * 1024 * 1024

BF16 = jnp.bfloat16
F32 = jnp.float32
NEG_BIG = -1e30
NT_DIMS = (((1,), (1,)), ((), ()))


def _mm(a, b):
    return jnp.dot(a.astype(BF16), b.astype(BF16), preferred_element_type=F32)


def _sigmoid(x):
    return 1.0 / (1.0 + jnp.exp(-x))


def _silu(x):
    return x * _sigmoid(x)


def _softplus(x):
    return jnp.maximum(x, 0.0) + jnp.log(1.0 + jnp.exp(-jnp.abs(x)))


def _gelu_tanh(x):
    return 0.5 * x * (1.0 + jnp.tanh(0.7978845608028654 * (x + 0.044715 * x * x * x)))


def _rms_mod(h, g, shift, scale):
    y = h * lax.rsqrt(jnp.mean(h * h, axis=-1, keepdims=True) + EPS) * g
    return y * (1.0 + scale) + shift


def _swiglu(u, w1_ref, w3_ref, w2_ref):
    ub = u.astype(BF16)
    a = jnp.dot(ub, w1_ref[...], preferred_element_type=F32)
    b = jnp.dot(ub, w3_ref[...], preferred_element_type=F32)
    g = (_silu(a) * b).astype(BF16)
    return jnp.dot(g, w2_ref[...], preferred_element_type=F32)


def _dwconv(x, prev, nxt, w):
    t = x.shape[0]
    row = lax.broadcasted_iota(jnp.int32, x.shape, 0)
    xm1 = jnp.where(row == 0, prev[HALO - 1:HALO], pltpu.roll(x, 1, 0))
    xm2 = jnp.where(row == 0, prev[HALO - 2:HALO - 1],
                    jnp.where(row == 1, prev[HALO - 1:HALO], pltpu.roll(x, 2, 0)))
    xp1 = jnp.where(row == t - 1, nxt[0:1], pltpu.roll(x, t - 1, 0))
    return xm2 * w[0:1] + xm1 * w[1:2] + x * w[2:3] + xp1 * w[3:4]


def _resident(shape, index=None):
    index = index or (0,) * len(shape)
    return pl.BlockSpec(shape, lambda *_: index, pipeline_mode=pl.Buffered(1))


def _ada_kernel(c_ref, w_ref, b_ref, o_ref):
    o_ref[...] = _mm(_silu(c_ref[...]), w_ref[...]) + b_ref[...]


def _ada_call(cvec, w_ada, b_ada):
    rows = cvec.shape[0]
    cols = ADA_MODS_PER_STEP * D_MODEL
    return pl.pallas_call(
        _ada_kernel,
        grid=(N_MOD // ADA_MODS_PER_STEP,),
        in_specs=[pl.BlockSpec((rows, D_MODEL), lambda k: (0, 0)),
                  pl.BlockSpec((D_MODEL, cols), lambda k: (0, k)),
                  pl.BlockSpec((1, cols), lambda k: (0, k))],
        out_specs=pl.BlockSpec((rows, cols), lambda k: (0, k)),
        out_shape=jax.ShapeDtypeStruct((rows, N_MOD * D_MODEL), F32),
        compiler_params=pltpu.CompilerParams(dimension_semantics=("arbitrary",),
                                             vmem_limit_bytes=VMEM_LIMIT),
        name="ada",
    )(cvec, w_ada, b_ada)


def _segment_edges(j, n_lat_tiles, n_all_tiles):
    return (jnp.logical_or(j == 0, j == n_lat_tiles),
            jnp.logical_or(j == n_lat_tiles - 1, j == n_all_tiles - 1))


def _ffn1_in_kernel(n_lat_tiles, n_all_tiles, *refs):
    n_sub = FFN1_SUB_TILES
    ins, refs = refs[:3 * n_sub], refs[3 * n_sub:]
    ng_ref, w1_ref, w3_ref, w2_ref, win_ref, wtail_ref, h1_ref, qkv_ref, z_ref, rx_ref, rg_ref, ba_ref = refs
    s = pl.program_id(0)

    def sub_tile(i):
        x_ref, ctx_ref, mod_ref = ins[3 * i:3 * i + 3]
        rows = slice(i * TOKEN_TILE, (i + 1) * TOKEN_TILE)
        h = jnp.where((s * n_sub + i) % n_all_tiles == n_lat_tiles, ctx_ref[0], x_ref[0])
        mod = mod_ref[0]
        ub = _rms_mod(h, ng_ref[0:1], mod[0:1], mod[1:2]).astype(BF16)
        a, b = yield [lambda: jnp.dot(ub, w1_ref[...], preferred_element_type=F32),
                      lambda: jnp.dot(ub, w3_ref[...], preferred_element_type=F32)]
        g = (_silu(a) * b).astype(BF16)
        (f,) = yield [lambda: jnp.dot(g, w2_ref[...], preferred_element_type=F32)]
        h1 = h + FFN_RESIDUAL * mod[2:3] * f
        h1_ref[rows] = h1
        ub2 = _rms_mod(h1, ng_ref[1:2], mod[3:4], mod[4:5]).astype(BF16)
        head, tail = yield [lambda: jnp.dot(ub2, win_ref[...], preferred_element_type=F32),
                            lambda: jnp.dot(ub2, wtail_ref[...], preferred_element_type=F32)]
        qkv_ref[rows] = head[:, :3 * GDN_WIDTH]
        z_ref[rows] = head[:, 3 * GDN_WIDTH:]
        rx_ref[rows] = tail[:, :LRU_WIDTH]
        rg_ref[rows] = tail[:, LRU_WIDTH:2 * LRU_WIDTH]
        ba_ref[rows] = tail[:, 2 * LRU_WIDTH:]

    _run_interleaved([sub_tile(i) for i in range(n_sub)])


def _ffn1_in_call(x, ctx, mods, norm_g, w1, w3, w2, w_head, w_tail):
    bsz, seq, _ = x.shape
    n_lat = seq // TOKEN_TILE
    n_all = n_lat + 1
    t_all = n_all * TOKEN_TILE
    n_sub = FFN1_SUB_TILES
    assert (bsz * n_all) % n_sub == 0
    per_tile = []
    for i in range(n_sub):
        tile = lambda s, i=i: s * n_sub + i
        per_tile += [
            pl.BlockSpec((1, TOKEN_TILE, D_MODEL),
                         lambda s, t=tile: (t(s) // n_all, jnp.minimum(t(s) % n_all, n_lat - 1), 0)),
            pl.BlockSpec((1, TOKEN_TILE, D_MODEL), lambda s, t=tile: (t(s) // n_all, 0, 0)),
            pl.BlockSpec((1, N_MOD, D_MODEL),
                         lambda s, t=tile: (jnp.where(t(s) % n_all == n_lat, 0, t(s) // n_all + 1), 0, 0))]
    widths = (D_MODEL, 3 * GDN_WIDTH, GDN_WIDTH, LRU_WIDTH, LRU_WIDTH, BA_PAD)
    outs = pl.pallas_call(
        functools.partial(_ffn1_in_kernel, n_lat, n_all),
        grid=(bsz * n_all // n_sub,),
        in_specs=per_tile + [
            _resident((3, D_MODEL)),
            _resident((None, D_MODEL, D_FF), (0, 0, 0)), _resident((None, D_MODEL, D_FF), (0, 0, 0)),
            _resident((None, D_FF, D_MODEL), (0, 0, 0)),
            _resident((D_MODEL, P_HEAD_COLS)), _resident((D_MODEL, P_TAIL_COLS))],
        out_specs=[pl.BlockSpec((n_sub * TOKEN_TILE, w), lambda s: (s, 0)) for w in widths],
        out_shape=[jax.ShapeDtypeStruct((bsz * t_all, w), F32) for w in widths],
        compiler_params=pltpu.CompilerParams(dimension_semantics=("arbitrary",),
                                             vmem_limit_bytes=VMEM_LIMIT),
        name="ffn1_in",
    )(*([x, ctx, mods] * n_sub), norm_g, w1, w3, w2, w_head, w_tail)
    return [o.reshape(bsz, t_all, o.shape[-1]) for o in outs]


def _pair_block_diag(x, first):
    return jnp.concatenate([jnp.where(first, x, 0.0), jnp.where(first, 0.0, x)], axis=0)


def _run_interleaved(coroutines):
    active = [[gen, next(gen)] for gen in coroutines]
    while active:
        still = []
        for item in active:
            results = [dot() for dot in item[1]]
            try:
                item[1] = item[0].send(results)
                still.append(item)
            except StopIteration:
                pass
        active = still


def _gdn_local_body(seg_start, seg_end, x_ref, prev_ref, next_ref, ba_ref, cw_ref, gp_ref,
                    out_refs, stage_in, stage_out):
    c = GDN_CHUNK
    pairs = GDN_HEADS // 2
    row = lax.broadcasted_iota(jnp.int32, (c, 2 * c), 0)
    lane2 = lax.broadcasted_iota(jnp.int32, (c, 2 * c), 1)
    col = lane2 & (c - 1)
    first = lane2 < c
    first_row = first[0:1]
    eye2 = (row == col).astype(F32)
    zeros_k = jnp.zeros((c, GDN_DK), BF16)
    unit_index = lambda ci, d, p: (ci * 2 + d) * pairs + p

    def coupling(k):
        return jnp.logical_and((row >> k) != (col >> k), (row >> (k + 1)) == (col >> (k + 1)))

    lhs_in, kr_in, rhs_in, dec_in, kdt_in, qe_in, eg_in = stage_in

    def pair_unit(ci, d, p):
        u_ref, wq_ref, ak_ref, eg_ref = out_refs[d]
        un = unit_index(ci, d, p)
        strict = (row < col) if d else (row > col)
        k_rows = kr_in[ci * pairs + p]
        k_bd = jnp.concatenate([jnp.concatenate([k_rows[:c], zeros_k], axis=1),
                                jnp.concatenate([zeros_k, k_rows[c:]], axis=1)], axis=0)
        lhs = lhs_in[un]
        (prods,) = yield [lambda: lax.dot_general(lhs, k_bd, NT_DIMS, preferred_element_type=F32)]
        decay = dec_in[un]
        a = jnp.where(strict, prods[:c] * decay, 0.0)
        ps = slice(p * 2 * c, (p + 1) * 2 * c)
        ak_ref[ci, 0, 0:c, ps] = (prods[c:] * decay).astype(BF16)
        ak_ref[ci, 0, c:3 * c, ps] = kdt_in[un]
        t = eye2 - jnp.where(coupling(0), a, 0.0)
        k = 1
        while 2 ** k < c:
            l = jnp.where(coupling(k), a, 0.0)
            t_bd = _pair_block_diag(t, first)
            (lt,) = yield [lambda: _mm(l, t_bd)]
            lt_bd = _pair_block_diag(lt, first)
            (tlt,) = yield [lambda: _mm(t, lt_bd)]
            t = t - tlt
            k += 1
        t_bd = _pair_block_diag(t, first).astype(BF16)
        rhs = rhs_in[un]
        (sol,) = yield [lambda: jnp.dot(t_bd, rhs, preferred_element_type=F32)]
        for i in range(2):
            h = 2 * p + i
            hs = slice(h * GDN_DK, (h + 1) * GDN_DK)
            u_ref[ci, 0, :, hs] = sol[i * c:(i + 1) * c, :GDN_DK]
            wq_ref[ci, 0, 0:c, hs] = sol[i * c:(i + 1) * c, GDN_DK:].astype(BF16)
            wq_ref[ci, 0, c:2 * c, hs] = qe_in[un, i * c:(i + 1) * c]
            eg_ref[ci, 0, h:h + 1, :] = eg_in[un, i:i + 1]

    lhs_out, kr_out, rhs_out, dec_out, kdt_out, qe_out, eg_out = stage_out
    y = _silu(_dwconv(x_ref[0], jnp.where(seg_start, 0.0, prev_ref[0]), jnp.where(seg_end, 0.0, next_ref[0]),
                      cw_ref[...]))
    ba = ba_ref[0]
    lane = lax.broadcasted_iota(jnp.int32, (1, BA_PAD), 1)
    is_g = jnp.logical_and(lane >= 2 * GDN_HEADS, lane < 4 * GDN_HEADS)
    beta_all = _sigmoid(ba)
    g_all = jnp.where(is_g, -jnp.exp(gp_ref[0:1]) * _softplus(ba + gp_ref[1:2]), 0.0)

    def chunk(ci):
        rs = slice(ci * c, (ci + 1) * c)
        heads = []
        for h in range(GDN_HEADS):
            lo = h * GDN_DK
            q = y[rs, lo:lo + GDN_DK]
            k = y[rs, GDN_WIDTH + lo:GDN_WIDTH + lo + GDN_DK]
            v = y[rs, 2 * GDN_WIDTH + lo:2 * GDN_WIDTH + lo + GDN_DK]
            q = q * (lax.rsqrt(jnp.sum(q * q, axis=-1, keepdims=True) + EPS) * (GDN_DK ** -0.5))
            k = k * lax.rsqrt(jnp.sum(k * k, axis=-1, keepdims=True) + EPS)
            heads.append((q, k, v))
        g = g_all[rs]
        k_rows = [jnp.concatenate([heads[2 * p][1], heads[2 * p + 1][1]], axis=0) for p in range(pairs)]
        k_ts = [kr.T for kr in k_rows]
        grow = lax.broadcasted_iota(jnp.int32, g.shape, 0)
        gcum_f = g
        step = 1
        while step < c:
            gcum_f = gcum_f + jnp.where(grow >= step, pltpu.roll(gcum_f, step, 0), 0.0)
            step *= 2
        gcum_b = gcum_f[c - 1:c] - gcum_f + g
        gcum_t = jnp.concatenate([gcum_f, gcum_b], axis=0).T
        beta_c = beta_all[rs]
        for p in range(pairs):
            kr_out[ci * pairs + p] = k_rows[p].astype(BF16)
        for d in range(2):
            gcum = gcum_b if d else gcum_f
            incl = (row <= col) if d else (row >= col)
            last = 0 if d else c - 1
            for p in range(pairs):
                un = unit_index(ci, d, p)
                (q0, k0, v0), (q1, k1, v1) = heads[2 * p], heads[2 * p + 1]
                cb = d * GDN_HEADS + 2 * p
                cg = 2 * GDN_HEADS + cb
                beta0, beta1 = beta_c[:, cb:cb + 1], beta_c[:, cb + 1:cb + 2]
                gc0, gc1 = gcum[:, cg:cg + 1], gcum[:, cg + 1:cg + 2]
                gt0, gt1 = gcum_t[cg:cg + 1], gcum_t[cg + 1:cg + 2]
                gr = (jnp.where(first_row, pltpu.roll(gt0, c, 1), gt1) if d
                      else jnp.where(first_row, gt0, pltpu.roll(gt1, c, 1)))
                gc = jnp.where(first, gc0, gc1)
                e0, e1 = jnp.exp(gc0), jnp.exp(gc1)
                kb0, kb1 = k0 * beta0, k1 * beta1
                dec_out[un] = jnp.exp(jnp.where(incl, gc - gr, NEG_BIG))
                lhs_out[un] = jnp.concatenate([jnp.concatenate([kb0, kb1], axis=1),
                                               jnp.concatenate([q0, q1], axis=1)], axis=0).astype(BF16)
                rhs_out[un] = jnp.concatenate([jnp.concatenate([v0 * beta0, kb0 * e0], axis=1),
                                               jnp.concatenate([v1 * beta1, kb1 * e1], axis=1)],
                                              axis=0).astype(BF16)
                kdt_out[un] = (k_ts[p] * jnp.exp(gc[last:last + 1] - gr)).astype(BF16)
                qe_out[un] = jnp.concatenate([q0 * e0, q1 * e1], axis=0).astype(BF16)
                eg_out[un] = jnp.concatenate(
                    [jnp.broadcast_to(jnp.exp(gc0[last:last + 1]), (1, BA_PAD)),
                     jnp.broadcast_to(jnp.exp(gc1[last:last + 1]), (1, BA_PAD)),
                     jnp.zeros((HALO - 2, BA_PAD), F32)], axis=0)

    for refs in out_refs:
        refs[3][...] = jnp.zeros_like(refs[3])
    _run_interleaved([pair_unit(ci, d, p) for ci in range(GDN_TILE_CHUNKS) for d in range(2)
                      for p in range(pairs)])
    for ci in range(GDN_TILE_CHUNKS):
        chunk(ci)


def _gdn_local_kernel(n_lat_tiles, n_all_tiles, n_tiles, x_ref, prev_ref, next_ref, ba_ref, cw_ref, gp_ref,
                      uf_ref, wqf_ref, akf_ref, egf_ref, ub_ref, wqb_ref, akb_ref, egb_ref, *stage_refs):
    s = pl.program_id(0)
    seg_start, seg_end = _segment_edges(jnp.minimum(s, n_tiles - 1) % n_all_tiles, n_lat_tiles, n_all_tiles)
    out_refs = ((uf_ref, wqf_ref, akf_ref, egf_ref), (ub_ref, wqb_ref, akb_ref, egb_ref))
    half = len(stage_refs) // 2
    sets = (stage_refs[:half], stage_refs[half:])

    @pl.when(s == 0)
    def _():
        for r in sets[1]:
            r[...] = jnp.zeros_like(r)

    for parity in range(2):
        @pl.when(s % 2 == parity)
        def _():
            _gdn_local_body(seg_start, seg_end, x_ref, prev_ref, next_ref, ba_ref, cw_ref, gp_ref,
                            out_refs, sets[1 - parity], sets[parity])


def _gdn_local_call(qkv, ba, conv_w, gate_params, n_lat_tok):
    bsz, t_all, width = qkv.shape
    c = GDN_CHUNK
    tile = GDN_TILE_CHUNKS * c
    n_all = t_all // tile
    n_lat = n_lat_tok // tile
    per_tile = tile // HALO
    last_halo = t_all // HALO - 1
    n_chunks = t_all // c
    n_tiles = bsz * n_all
    pairs = GDN_HEADS // 2
    units = GDN_TILE_CHUNKS * 2 * pairs
    t_in = lambda s: jnp.minimum(s, n_tiles - 1)
    t_out = lambda s: jnp.maximum(s - 1, 0)
    in_tok = lambda w: pl.BlockSpec((1, tile, w), lambda s: (t_in(s) // n_all, t_in(s) % n_all, 0))
    cblk = lambda rows, w: pl.BlockSpec((GDN_TILE_CHUNKS, 1, rows, w),
                                        lambda s: (t_out(s) % n_all, t_out(s) // n_all, 0, 0))
    vec = pl.BlockSpec((2, BA_PAD), lambda s: (0, 0))
    outs = [(jax.ShapeDtypeStruct((n_chunks, bsz, c, GDN_WIDTH), F32), cblk(c, GDN_WIDTH)),
            (jax.ShapeDtypeStruct((n_chunks, bsz, 2 * c, GDN_WIDTH), BF16), cblk(2 * c, GDN_WIDTH)),
            (jax.ShapeDtypeStruct((n_chunks, bsz, 3 * c, GDN_HEADS * c), BF16), cblk(3 * c, GDN_HEADS * c)),
            (jax.ShapeDtypeStruct((n_chunks, bsz, HALO, BA_PAD), F32), cblk(HALO, BA_PAD))] * 2
    stage = [pltpu.VMEM((units, 2 * c, 2 * GDN_DK), BF16),
             pltpu.VMEM((GDN_TILE_CHUNKS * pairs, 2 * c, GDN_DK), BF16),
             pltpu.VMEM((units, 2 * c, 2 * GDN_DK), BF16),
             pltpu.VMEM((units, c, 2 * c), F32),
             pltpu.VMEM((units, GDN_DK, 2 * c), BF16),
             pltpu.VMEM((units, 2 * c, GDN_DK), BF16),
             pltpu.VMEM((units, HALO, BA_PAD), F32)]
    return pl.pallas_call(
        functools.partial(_gdn_local_kernel, n_lat, n_all, n_tiles),
        grid=(n_tiles + 1,),
        in_specs=[in_tok(width),
                  pl.BlockSpec((1, HALO, width),
                               lambda s: (t_in(s) // n_all, jnp.maximum((t_in(s) % n_all) * per_tile - 1, 0), 0)),
                  pl.BlockSpec((1, HALO, width),
                               lambda s: (t_in(s) // n_all,
                                          jnp.minimum((t_in(s) % n_all + 1) * per_tile, last_halo), 0)),
                  in_tok(BA_PAD), pl.BlockSpec((CONV_W, width), lambda s: (0, 0)), vec],
        out_specs=[o[1] for o in outs],
        out_shape=[o[0] for o in outs],
        scratch_shapes=stage + stage,
        compiler_params=pltpu.CompilerParams(dimension_semantics=("arbitrary",),
                                             vmem_limit_bytes=VMEM_LIMIT),
        name="gdn_local",
    )(qkv, qkv, qkv, ba, conv_w, gate_params)


def _gdn_scan_steps(uf_ref, wqf_ref, akf_ref, egf_ref, ub_ref, wqb_ref, akb_ref, egb_ref,
                    of_ref, ob_ref, sf_ref, sb_ref):
    c = GDN_CHUNK
    bsz = uf_ref.shape[1]
    hs = lambda h: slice(h * GDN_DK, (h + 1) * GDN_DK)
    for sub in range(GDN_SCAN_CHUNKS):
        chains = [(refs, ci, b, h)
                  for refs, ci in (((uf_ref, wqf_ref, akf_ref, egf_ref, of_ref, sf_ref), sub),
                                   ((ub_ref, wqb_ref, akb_ref, egb_ref, ob_ref, sb_ref),
                                    GDN_SCAN_CHUNKS - 1 - sub))
                  for b in range(bsz) for h in range(GDN_HEADS)]
        states = [refs[5][b * GDN_HEADS + h] for refs, ci, b, h in chains]
        with_s = yield [lambda refs=refs, ci=ci, b=b, h=h, s=s:
                        jnp.dot(refs[1][ci, b, :, hs(h)], s.astype(BF16), preferred_element_type=F32)
                        for (refs, ci, b, h), s in zip(chains, states)]
        v_new = [refs[0][ci, b, :, hs(h)] - r[:c] for (refs, ci, b, h), r in zip(chains, with_s)]
        with_v = yield [lambda refs=refs, ci=ci, b=b, h=h, v=v:
                        jnp.dot(refs[2][ci, b, :, h * c:(h + 1) * c], v.astype(BF16), preferred_element_type=F32)
                        for (refs, ci, b, h), v in zip(chains, v_new)]
        for (refs, ci, b, h), s, r, a in zip(chains, states, with_s, with_v):
            refs[4][b, ci * c:(ci + 1) * c, hs(h)] = r[c:] + a[:c]
            refs[5][b * GDN_HEADS + h] = s * refs[3][ci, b, h:h + 1, :] + a[c:]


def _lin_scan(a, b, reverse):
    r = a.shape[0]
    row = lax.broadcasted_iota(jnp.int32, a.shape, 0)
    s = 1
    while s < r:
        ok = (row < r - s) if reverse else (row >= s)
        shift = r - s if reverse else s
        a_sh = jnp.where(ok, pltpu.roll(a, shift, 0), 1.0)
        b_sh = jnp.where(ok, pltpu.roll(b, shift, 0), 0.0)
        b = a * b_sh + b
        a = a * a_sh
        s *= 2
    return a, b


def _lru_ctx_direction(reverse, n_slabs, slab, x_ref, prev_ref, next_ref, cw, cb, w_ref, br, bi, lam, carry_ref):
    bsz, r, width = x_ref.shape
    step = pl.program_id(0)
    has_prev = slab > 0
    has_next = slab < n_slabs - 1
    xs = []
    for b in range(bsz):
        prev = jnp.where(has_prev, prev_ref[b], 0.0)
        nxt = jnp.where(has_next, next_ref[b], 0.0)
        xs.append(_dwconv(x_ref[b], prev, nxt, cw) + cb)
    x = jnp.concatenate(xs, axis=0)
    xb = x.astype(BF16)
    d = 1 if reverse else 0
    gate = lambda g: jnp.concatenate(
        [jnp.dot(xb[:, t * LRU_LANES:(t + 1) * LRU_LANES], w_ref[2 * d + g, t], preferred_element_type=F32)
         for t in range(width // LRU_LANES)], axis=1)
    gate_r = _sigmoid(gate(0) + br)
    gate_i = _sigmoid(gate(1) + bi)
    log_a = (-LRU_C * _softplus(-lam)) * gate_r
    a = jnp.exp(log_a)
    mult = jnp.sqrt(1.0 - a * a)
    row = lax.broadcasted_iota(jnp.int32, (r, width), 0)
    is_first = jnp.logical_and(step == 0, row == ((r - 1) if reverse else 0))
    edge = 0 if reverse else r - 1
    for b in range(bsz):
        sl = slice(b * r, (b + 1) * r)
        m = jnp.where(is_first, 1.0, mult[sl])
        a_cum, h = _lin_scan(a[sl], m * (gate_i[sl] * x[sl]), reverse)
        h = h + a_cum * carry_ref[b:b + 1]
        carry_ref[b:b + 1] = h[edge:edge + 1]


def _lru_ctx_kernel(n_slabs, xf_ref, pf_ref, nf_ref, xb_ref, pb_ref, nb_ref, cw_ref, cb_ref,
                    w_ref, bg_ref, lam_ref, fin_ref, cf_ref, cbk_ref):
    step = pl.program_id(0)

    @pl.when(step == 0)
    def _():
        cf_ref[...] = jnp.zeros_like(cf_ref)
        cbk_ref[...] = jnp.zeros_like(cbk_ref)

    cw = cw_ref[...]
    cb = cb_ref[...]
    _lru_ctx_direction(False, n_slabs, step, xf_ref, pf_ref, nf_ref, cw, cb, w_ref,
                       bg_ref[0:1], bg_ref[1:2], lam_ref[0:1], cf_ref)
    _lru_ctx_direction(True, n_slabs, n_slabs - 1 - step, xb_ref, pb_ref, nb_ref, cw, cb, w_ref,
                       bg_ref[2:3], bg_ref[3:4], lam_ref[1:2], cbk_ref)
    fin_ref[0] = cf_ref[...]
    fin_ref[1] = cbk_ref[...]


def _lru_ctx_call(rx, n_lat_tok, conv_w, conv_b, w_tiles, b_gate, lam):
    bsz, t_all, width = rx.shape
    r = GRID_W
    per = r // HALO
    n_slabs = (t_all - n_lat_tok) // r
    first = n_lat_tok // r
    main = lambda f: pl.BlockSpec((bsz, r, width), lambda s: (0, first + f(s), 0))
    prev = lambda f: pl.BlockSpec((bsz, HALO, width), lambda s: (0, (first + f(s)) * per - 1, 0))
    nxt = lambda f: pl.BlockSpec(
        (bsz, HALO, width), lambda s: (0, jnp.minimum((first + f(s) + 1) * per, t_all // HALO - 1), 0))
    fwd = lambda s: s
    bwd = lambda s: n_slabs - 1 - s
    full = lambda shape: pl.BlockSpec(shape, lambda s: (0,) * len(shape))
    carry = pltpu.VMEM((bsz, width), F32)
    return pl.pallas_call(
        functools.partial(_lru_ctx_kernel, n_slabs),
        grid=(n_slabs,),
        in_specs=[main(fwd), prev(fwd), nxt(fwd), main(bwd), prev(bwd), nxt(bwd),
                  full((CONV_W, width)), full((1, width)), full(w_tiles.shape),
                  full((4, width)), full((2, width))],
        out_specs=full((2, bsz, width)),
        out_shape=jax.ShapeDtypeStruct((2, bsz, width), F32),
        scratch_shapes=[carry, carry],
        compiler_params=pltpu.CompilerParams(dimension_semantics=("arbitrary",),
                                             vmem_limit_bytes=VMEM_LIMIT),
        name="lru_ctx",
    )(rx, rx, rx, rx, rx, rx, conv_w, conv_b, w_tiles, b_gate, lam)


def _lru_grid_gates(x_ref, cw_ref, cb_ref, w_ref, bg_ref, lam_ref, af_ref, bf_ref, ab_ref, bb_ref):
    n = GRID_W
    rp = LRU_ROWS_PER_PASS
    lanes = x_ref.shape[-1]
    crow = lax.broadcasted_iota(jnp.int32, (n, lanes), 0)
    cw = cw_ref[...]
    cb = cb_ref[...]

    def grid_rows(lo, hi):
        parts = []
        for r in range(lo, min(hi, 0)):
            parts.append(jnp.where(crow == 0, 0.0, pltpu.roll(x_ref[0, (n + r) * n:(n + r + 1) * n, :], 1, 0)))
        if max(lo, 0) < min(hi, n):
            parts.append(x_ref[0, max(lo, 0) * n:min(hi, n) * n, :])
        for r in range(max(lo, n), hi):
            parts.append(jnp.where(crow == n - 1, 0.0,
                                   pltpu.roll(x_ref[0, (r - n) * n:(r - n + 1) * n, :], n - 1, 0)))
        return parts[0] if len(parts) == 1 else jnp.concatenate(parts, axis=0)

    for r0 in range(0, n, rp):
        sl = slice(r0 * n, (r0 + rp) * n)
        xc = (grid_rows(r0 - 2, r0 + rp - 2) * cw[0:1] + grid_rows(r0 - 1, r0 + rp - 1) * cw[1:2]
              + grid_rows(r0, r0 + rp) * cw[2:3] + grid_rows(r0 + 1, r0 + rp + 1) * cw[3:4] + cb)
        xb = xc.astype(BF16)
        gates = yield [lambda i=i: jnp.dot(xb, w_ref[i, 0], preferred_element_type=F32) for i in range(4)]
        for d, (a_ref, b_ref) in enumerate(((af_ref, bf_ref), (ab_ref, bb_ref))):
            gate_r = _sigmoid(gates[2 * d] + bg_ref[2 * d:2 * d + 1])
            gate_i = _sigmoid(gates[2 * d + 1] + bg_ref[2 * d + 1:2 * d + 2])
            log_a = (-LRU_C * _softplus(-lam_ref[d:d + 1])) * gate_r
            a = jnp.exp(log_a)
            a_ref[sl] = a
            b_ref[sl] = jnp.sqrt(1.0 - a * a) * (gate_i * xc)


def _lru_grid_finish(h0_ref, out_ref, af_ref, bf_ref, ab_ref, bb_ref):
    n = GRID_W
    lanes = out_ref.shape[-1]
    crow = lax.broadcasted_iota(jnp.int32, (n, lanes), 0)
    hf = jnp.zeros((n, lanes), F32)
    hb = jnp.zeros((n, lanes), F32)
    pf = jnp.ones((n, lanes), F32)
    pb = jnp.ones((n, lanes), F32)
    for i in range(n):
        sf = slice(i * n, (i + 1) * n)
        sb = slice((n - 1 - i) * n, (n - i) * n)
        a = af_ref[sf]
        hf = a * hf + bf_ref[sf]
        pf = a * pf
        bf_ref[sf] = hf
        af_ref[sf] = pf
        a = ab_ref[sb]
        hb = a * hb + bb_ref[sb]
        pb = a * pb
        bb_ref[sb] = hb
        ab_ref[sb] = pb

    h0f = h0_ref[0, 0]
    h0b = h0_ref[1, 0]
    acc_a, acc_h = _lin_scan(pf, hf, False)
    in_f = jnp.where(crow == 0, h0f, pltpu.roll(acc_h + acc_a * h0f, 1, 0))
    acc_a, acc_h = _lin_scan(pb, hb, True)
    in_b = jnp.where(crow == n - 1, h0b, pltpu.roll(acc_h + acc_a * h0b, n - 1, 0))

    for r in range(n):
        sl = slice(r * n, (r + 1) * n)
        out_ref[0, sl, :] = (bf_ref[sl] + af_ref[sl] * in_f) + (bb_ref[sl] + ab_ref[sl] * in_b)


def _mixers_kernel(*refs):
    scan_in, lru_in, outs, scratch = refs[:8], refs[8:15], refs[15:18], refs[18:]
    of_ref, ob_ref, hs_ref = outs
    sf_ref, sb_ref = scratch[:2]

    @pl.when(pl.program_id(0) == 0)
    def _():
        sf_ref[...] = jnp.zeros_like(sf_ref)
        sb_ref[...] = jnp.zeros_like(sb_ref)

    x_ref, h0_ref = lru_in[:2]
    _run_interleaved([_gdn_scan_steps(*scan_in, of_ref, ob_ref, sf_ref, sb_ref),
                      _lru_grid_gates(x_ref, *lru_in[2:], *scratch[2:])])
    _lru_grid_finish(h0_ref, hs_ref, *scratch[2:])


def _mixers_call(local, rx, n_lat_tok, h0, conv_w, conv_b, w_tiles, b_gate, lam):
    n_chunks, bsz = local[0].shape[:2]
    width = rx.shape[2]
    c = GDN_CHUNK
    g = GDN_SCAN_CHUNKS
    n_all = n_chunks // g
    n_lat = n_lat_tok // (c * g)
    n_ctx = n_all - n_lat
    fwd = lambda s: jnp.where(s < n_ctx, n_lat + s, s - n_ctx)
    bwd = lambda s: n_all - 1 - s
    fwd_out = lambda s: jnp.maximum(s - n_ctx, 0)
    bwd_out = lambda s: jnp.minimum(n_all - 1 - s, n_lat - 1)
    cblk = lambda rows, w, f: pl.BlockSpec((g, bsz, rows, w), lambda s: (f(s), 0, 0, 0))
    ins = lambda f: [cblk(c, GDN_WIDTH, f), cblk(2 * c, GDN_WIDTH, f), cblk(3 * c, GDN_HEADS * c, f),
                     cblk(HALO, BA_PAD, f)]
    tok = lambda f: pl.BlockSpec((bsz, g * c, GDN_WIDTH), lambda s: (0, f(s), 0))
    state = pltpu.VMEM((bsz * GDN_HEADS, GDN_DK, GDN_DK), F32)

    nt = width // LRU_LANES
    unit = lambda s: jnp.minimum(s, bsz * nt - 1)
    assert n_all >= bsz * nt
    lane = lambda rows: pl.BlockSpec((rows, LRU_LANES), lambda s: (0, unit(s) % nt))
    grid_tok = pl.BlockSpec((1, n_lat_tok, LRU_LANES), lambda s: (unit(s) // nt, 0, unit(s) % nt))
    lru_scratch = pltpu.VMEM((n_lat_tok, LRU_LANES), F32)
    return pl.pallas_call(
        _mixers_kernel,
        grid=(n_all,),
        in_specs=ins(fwd) + ins(bwd) + [
            grid_tok,
            pl.BlockSpec((2, 1, 1, LRU_LANES), lambda s: (0, unit(s) // nt, 0, unit(s) % nt)),
            lane(CONV_W), lane(1),
            pl.BlockSpec((4, 1, LRU_LANES, LRU_LANES), lambda s: (0, unit(s) % nt, 0, 0)),
            lane(4), lane(2)],
        out_specs=[tok(fwd_out), tok(bwd_out), grid_tok],
        out_shape=[jax.ShapeDtypeStruct((bsz, n_lat_tok, GDN_WIDTH), F32)] * 2
                  + [jax.ShapeDtypeStruct((bsz, n_lat_tok, width), F32)],
        scratch_shapes=[state, state] + [lru_scratch] * 4,
        compiler_params=pltpu.CompilerParams(dimension_semantics=("arbitrary",),
                                             vmem_limit_bytes=VMEM_LIMIT),
        name="mixers",
    )(*local, rx, h0.reshape(2, bsz, 1, width), conv_w, conv_b, w_tiles, b_gate, lam)


def _out_ffn2_kernel(h1_ref, of_ref, ob_ref, z_ref, hs_ref, rg_ref, mod_ref, ng_ref, gnw_ref,
                     wout_ref, w1_ref, w3_ref, w2_ref, fg_ref, out_ref):
    mod = mod_ref[0]

    def sub_tile(rows):
        o = of_ref[0, rows] + ob_ref[0, rows]
        z = z_ref[0, rows]
        parts = []
        for h in range(GDN_HEADS):
            sl = slice(h * GDN_DK, (h + 1) * GDN_DK)
            oh = o[:, sl]
            parts.append(oh * lax.rsqrt(jnp.mean(oh * oh, axis=-1, keepdims=True) + EPS)
                         * gnw_ref[...] * _silu(z[:, sl]))
        parts.append(hs_ref[0, rows] * _gelu_tanh(rg_ref[0, rows]))
        mixed = jnp.concatenate(parts, axis=1).astype(BF16)
        (y,) = yield [lambda: jnp.dot(mixed, wout_ref[...], preferred_element_type=F32)]
        h2 = h1_ref[0, rows] + mod[5:6] * y
        ub = _rms_mod(h2, ng_ref[2:3], mod[6:7], mod[7:8]).astype(BF16)
        a, b = yield [lambda: jnp.dot(ub, w1_ref[...], preferred_element_type=F32),
                      lambda: jnp.dot(ub, w3_ref[...], preferred_element_type=F32)]
        g = (_silu(a) * b).astype(BF16)
        (f,) = yield [lambda: jnp.dot(g, w2_ref[...], preferred_element_type=F32)]
        h3 = h2 + FFN_RESIDUAL * mod[8:9] * f
        out_ref[0, rows] = h3 * lax.rsqrt(jnp.mean(h3 * h3, axis=-1, keepdims=True) + EPS) * fg_ref[...]

    _run_interleaved([sub_tile(slice(i * TOKEN_TILE, (i + 1) * TOKEN_TILE))
                      for i in range(out_ref.shape[1] // TOKEN_TILE)])


def _out_ffn2_call(h1, o_f, o_b, z, h_sum, rg, mods, norm_g, gdn_norm_w, w_out, w1, w3, w2, final_g):
    bsz, seq, _ = o_f.shape
    rows = OUT_SUB_TILES * TOKEN_TILE
    tok = lambda w: pl.BlockSpec((1, rows, w), lambda b, j: (b, j, 0))
    return pl.pallas_call(
        _out_ffn2_kernel,
        grid=(bsz, seq // rows),
        in_specs=[tok(D_MODEL), tok(GDN_WIDTH), tok(GDN_WIDTH), tok(GDN_WIDTH),
                  tok(LRU_WIDTH), tok(LRU_WIDTH),
                  pl.BlockSpec((1, N_MOD, D_MODEL), lambda b, j: (b + 1, 0, 0)),
                  _resident((3, D_MODEL)), _resident((1, GDN_DK)),
                  _resident((D_MODEL, D_MODEL)),
                  _resident((None, D_MODEL, D_FF), (1, 0, 0)), _resident((None, D_MODEL, D_FF), (1, 0, 0)),
                  _resident((None, D_FF, D_MODEL), (1, 0, 0)),
                  _resident((1, D_MODEL))],
        out_specs=tok(D_MODEL),
        out_shape=jax.ShapeDtypeStruct((bsz, seq, D_MODEL), F32),
        compiler_params=pltpu.CompilerParams(dimension_semantics=("arbitrary", "arbitrary"),
                                             vmem_limit_bytes=VMEM_LIMIT),
        name="out_ffn2",
    )(h1, o_f, o_b, z, h_sum, rg, mods, norm_g, gdn_norm_w, w_out, w1, w3, w2, final_g)


def kernel(x, c, ctx, c_ctx, w_ada, b_ada, norm_g, ffn_w1, ffn_w3, ffn_w2, w_in, w_out, gdn_conv_w, gdn_a_log,
           gdn_dt_bias, gdn_norm_w, lru_conv_w, lru_conv_b, lru_w_gate, lru_b_gate, lru_lambda, final_norm_g):
    bsz, seq, _ = x.shape
    assert w_ada.shape[0] == 1 and seq == GRID_W * GRID_W and ctx.shape[1] == TOKEN_TILE

    rows = -(-(bsz + 1) // 8) * 8
    cvec = jnp.concatenate([c_ctx[None, :], c, jnp.zeros((rows - bsz - 1, D_MODEL), F32)], axis=0)
    mods = _ada_call(cvec, w_ada[0], b_ada).reshape(rows, N_MOD, D_MODEL)

    o_ba = P_HEAD_COLS
    o_rx = o_ba + 4 * GDN_HEADS
    w_head = w_in[0, :, :o_ba].astype(BF16)
    w_tail = jnp.concatenate([w_in[0, :, o_rx:], w_in[0, :, o_ba:o_rx],
                              jnp.zeros((D_MODEL, BA_PAD - 4 * GDN_HEADS), F32)], axis=1).astype(BF16)
    w1 = ffn_w1[0].astype(BF16)
    w3 = ffn_w3[0].astype(BF16)
    w2 = ffn_w2[0].astype(BF16)

    h1, qkv, z, rx, rg, ba = _ffn1_in_call(x, ctx, mods, norm_g[0], w1, w3, w2, w_head, w_tail)

    gate_params = jnp.pad(jnp.stack([gdn_a_log[0].reshape(-1), gdn_dt_bias[0].reshape(-1)]),
                          ((0, 0), (2 * GDN_HEADS, BA_PAD - 4 * GDN_HEADS)))
    local = _gdn_local_call(qkv, ba, gdn_conv_w[0], gate_params, seq)

    nt = LRU_WIDTH // LRU_LANES
    per_tile = LRU_LANES // LRU_BW
    w_tiles = jnp.einsum('gticd,ij->gticjd', lru_w_gate[0].reshape(4, nt, per_tile, LRU_BW, LRU_BW),
                         jnp.eye(per_tile, dtype=F32)).reshape(4, nt, LRU_LANES, LRU_LANES).astype(BF16)
    b_gate = lru_b_gate[0].reshape(4, LRU_WIDTH)
    lam = lru_lambda[0]
    cw, cb = lru_conv_w[0], lru_conv_b
    h_ctx = _lru_ctx_call(rx, seq, cw, cb, w_tiles, b_gate, lam)
    o_f, o_b, h_sum = _mixers_call(local, rx, seq, h_ctx, cw, cb, w_tiles, b_gate, lam)

    return _out_ffn2_call(h1, o_f, o_b, z, h_sum, rg, mods, norm_g[0], gdn_norm_w,
                          w_out[0].astype(BF16), w1, w3, w2, final_norm_g[None, :])
```

```python
import functools

import jax
import jax.numpy as jnp
from jax import lax
from jax.experimental import pallas as pl
from jax.experimental.pallas import tpu as pltpu

D_MODEL = 1024
D_FF = 2816
N_MOD = 9
EPS = 1e-6
FFN_RESIDUAL = 0.5

GDN_WIDTH = 512
GDN_HEADS = 4
GDN_DK = 128
GDN_CHUNK = 64
CONV_W = 4
LRU_WIDTH = 512
LRU_BW = 64
LRU_C = 8.0
GRID_W = 64

ADA_MODS_PER_STEP = 3
TOKEN_TILE = 256
GDN_TILE_CHUNKS = TOKEN_TILE // GDN_CHUNK
GDN_SCAN_CHUNKS = 4
OUT_SUB_TILES = 2
FFN1_SUB_TILES = 2
HALO = 8
BA_PAD = 128
P_HEAD_COLS = 4 * GDN_WIDTH
P_TAIL_COLS = 2 * LRU_WIDTH + BA_PAD
LRU_LANES = 128
LRU_ROWS_PER_PASS = 8
VMEM_LIMIT = 56 * 1024 * 1024

BF16 = jnp.bfloat16
F32 = jnp.float32
NEG_BIG = -1e30
NT_DIMS = (((1,), (1,)), ((), ()))


def _mm(a, b):
    return jnp.dot(a.astype(BF16), b.astype(BF16), preferred_element_type=F32)


def _sigmoid(x):
    return 1.0 / (1.0 + jnp.exp(-x))


def _sigmoid_tanh(x):
    return 0.5 * jnp.tanh(0.5 * x) + 0.5


def _silu(x):
    return x * _sigmoid(x)


def _softplus(x):
    return jnp.maximum(x, 0.0) + jnp.log(1.0 + jnp.exp(-jnp.abs(x)))


def _gelu_tanh(x):
    return 0.5 * x * (1.0 + jnp.tanh(0.7978845608028654 * (x + 0.044715 * x * x * x)))


def _rms_mod(h, g, shift, scale):
    y = h * lax.rsqrt(jnp.mean(h * h, axis=-1, keepdims=True) + EPS) * g
    return y * (1.0 + scale) + shift


def _swiglu(u, w1_ref, w3_ref, w2_ref):
    ub = u.astype(BF16)
    a = jnp.dot(ub, w1_ref[...], preferred_element_type=F32)
    b = jnp.dot(ub, w3_ref[...], preferred_element_type=F32)
    g = (_silu(a) * b).astype(BF16)
    return jnp.dot(g, w2_ref[...], preferred_element_type=F32)


def _dwconv(x, prev, nxt, w):
    t = x.shape[0]
    row = lax.broadcasted_iota(jnp.int32, x.shape, 0)
    xm1 = jnp.where(row == 0, prev[HALO - 1:HALO], pltpu.roll(x, 1, 0))
    xm2 = jnp.where(row == 0, prev[HALO - 2:HALO - 1],
                    jnp.where(row == 1, prev[HALO - 1:HALO], pltpu.roll(x, 2, 0)))
    xp1 = jnp.where(row == t - 1, nxt[0:1], pltpu.roll(x, t - 1, 0))
    return xm2 * w[0:1] + xm1 * w[1:2] + x * w[2:3] + xp1 * w[3:4]


def _resident(shape, index=None):
    index = index or (0,) * len(shape)
    return pl.BlockSpec(shape, lambda *_: index, pipeline_mode=pl.Buffered(1))


def _ada_kernel(c_ref, w_ref, b_ref, o_ref):
    o_ref[...] = _mm(_silu(c_ref[...]), w_ref[...]) + b_ref[...]


def _ada_call(cvec, w_ada, b_ada):
    rows = cvec.shape[0]
    cols = ADA_MODS_PER_STEP * D_MODEL
    return pl.pallas_call(
        _ada_kernel,
        grid=(N_MOD // ADA_MODS_PER_STEP,),
        in_specs=[pl.BlockSpec((rows, D_MODEL), lambda k: (0, 0)),
                  pl.BlockSpec((D_MODEL, cols), lambda k: (0, k)),
                  pl.BlockSpec((1, cols), lambda k: (0, k))],
        out_specs=pl.BlockSpec((rows, cols), lambda k: (0, k)),
        out_shape=jax.ShapeDtypeStruct((rows, N_MOD * D_MODEL), F32),
        compiler_params=pltpu.CompilerParams(dimension_semantics=("arbitrary",),
                                             vmem_limit_bytes=VMEM_LIMIT),
        name="ada",
    )(cvec, w_ada, b_ada)


def _segment_edges(j, n_lat_tiles, n_all_tiles):
    return (jnp.logical_or(j == 0, j == n_lat_tiles),
            jnp.logical_or(j == n_lat_tiles - 1, j == n_all_tiles - 1))


def _ffn1_in_kernel(n_lat_tiles, n_all_tiles, *refs):
    n_sub = FFN1_SUB_TILES
    ins, refs = refs[:3 * n_sub], refs[3 * n_sub:]
    ng_ref, w1_ref, w3_ref, w2_ref, win_ref, wtail_ref, h1_ref, qkv_ref, z_ref, rx_ref, rg_ref, ba_ref = refs
    s = pl.program_id(0)

    def sub_tile(i):
        x_ref, ctx_ref, mod_ref = ins[3 * i:3 * i + 3]
        rows = slice(i * TOKEN_TILE, (i + 1) * TOKEN_TILE)
        h = jnp.where((s * n_sub + i) % n_all_tiles == n_lat_tiles, ctx_ref[0], x_ref[0])
        mod = mod_ref[0]
        ub = _rms_mod(h, ng_ref[0:1], mod[0:1], mod[1:2]).astype(BF16)
        a, b = yield [lambda: jnp.dot(ub, w1_ref[...], preferred_element_type=F32),
                      lambda: jnp.dot(ub, w3_ref[...], preferred_element_type=F32)]
        g = (_silu(a) * b).astype(BF16)
        (f,) = yield [lambda: jnp.dot(g, w2_ref[...], preferred_element_type=F32)]
        h1 = h + FFN_RESIDUAL * mod[2:3] * f
        h1_ref[rows] = h1
        ub2 = _rms_mod(h1, ng_ref[1:2], mod[3:4], mod[4:5]).astype(BF16)
        head, tail = yield [lambda: jnp.dot(ub2, win_ref[...], preferred_element_type=F32),
                            lambda: jnp.dot(ub2, wtail_ref[...], preferred_element_type=F32)]
        qkv_ref[rows] = head[:, :3 * GDN_WIDTH]
        z_ref[rows] = head[:, 3 * GDN_WIDTH:]
        rx_ref[rows] = tail[:, :LRU_WIDTH]
        rg_ref[rows] = tail[:, LRU_WIDTH:2 * LRU_WIDTH]
        ba_ref[rows] = tail[:, 2 * LRU_WIDTH:]

    _run_interleaved([sub_tile(i) for i in range(n_sub)])


def _ffn1_in_call(x, ctx, mods, norm_g, w1, w3, w2, w_in_bf, w_tail):
    bsz, seq, _ = x.shape
    n_lat = seq // TOKEN_TILE
    n_all = n_lat + 1
    t_all = n_all * TOKEN_TILE
    n_sub = FFN1_SUB_TILES
    assert (bsz * n_all) % n_sub == 0
    per_tile = []
    for i in range(n_sub):
        tile = lambda s, i=i: s * n_sub + i
        per_tile += [
            pl.BlockSpec((1, TOKEN_TILE, D_MODEL),
                         lambda s, t=tile: (t(s) // n_all, jnp.minimum(t(s) % n_all, n_lat - 1), 0)),
            pl.BlockSpec((1, TOKEN_TILE, D_MODEL), lambda s, t=tile: (t(s) // n_all, 0, 0)),
            pl.BlockSpec((1, N_MOD, D_MODEL),
                         lambda s, t=tile: (jnp.where(t(s) % n_all == n_lat, 0, t(s) // n_all + 1), 0, 0))]
    widths = (D_MODEL, 3 * GDN_WIDTH, GDN_WIDTH, LRU_WIDTH, LRU_WIDTH, BA_PAD)
    outs = pl.pallas_call(
        functools.partial(_ffn1_in_kernel, n_lat, n_all),
        grid=(bsz * n_all // n_sub,),
        in_specs=per_tile + [
            _resident((3, D_MODEL)),
            _resident((None, D_MODEL, D_FF), (0, 0, 0)), _resident((None, D_MODEL, D_FF), (0, 0, 0)),
            _resident((None, D_FF, D_MODEL), (0, 0, 0)),
            _resident((D_MODEL, P_HEAD_COLS)), _resident((D_MODEL, P_TAIL_COLS))],
        out_specs=[pl.BlockSpec((n_sub * TOKEN_TILE, w), lambda s: (s, 0)) for w in widths],
        out_shape=[jax.ShapeDtypeStruct((bsz * t_all, w), F32) for w in widths],
        compiler_params=pltpu.CompilerParams(dimension_semantics=("arbitrary",),
                                             vmem_limit_bytes=VMEM_LIMIT),
        name="ffn1_in",
    )(*([x, ctx, mods] * n_sub), norm_g, w1, w3, w2, w_in_bf, w_tail)
    return [o.reshape(bsz, t_all, o.shape[-1]) for o in outs]


def _pair_block_diag(x, first):
    return jnp.concatenate([jnp.where(first, x, 0.0), jnp.where(first, 0.0, x)], axis=0)


def _run_interleaved(coroutines):
    active = [[gen, next(gen)] for gen in coroutines]
    while active:
        still = []
        for item in active:
            results = [dot() for dot in item[1]]
            try:
                item[1] = item[0].send(results)
                still.append(item)
            except StopIteration:
                pass
        active = still


def _gdn_local_body(seg_start, seg_end, x_ref, prev_ref, next_ref, ba_ref, cw_ref, gp_ref,
                    out_refs, stage_in, stage_out):
    c = GDN_CHUNK
    pairs = GDN_HEADS // 2
    row = lax.broadcasted_iota(jnp.int32, (c, 2 * c), 0)
    lane2 = lax.broadcasted_iota(jnp.int32, (c, 2 * c), 1)
    col = lane2 & (c - 1)
    first = lane2 < c
    first_row = first[0:1]
    eye2 = (row == col).astype(F32)
    zeros_k = jnp.zeros((c, GDN_DK), BF16)
    unit_index = lambda ci, d, p: (ci * 2 + d) * pairs + p

    def coupling(k):
        return jnp.logical_and((row >> k) != (col >> k), (row >> (k + 1)) == (col >> (k + 1)))

    lhs_in, kr_in, rhs_in, dec_in, kdt_in, qe_in, eg_in = stage_in

    def pair_unit(ci, d, p):
        u_ref, wq_ref, ak_ref, eg_ref = out_refs[d]
        un = unit_index(ci, d, p)
        strict = (row < col) if d else (row > col)
        k_rows = kr_in[ci * pairs + p]
        k_bd = jnp.concatenate([jnp.concatenate([k_rows[:c], zeros_k], axis=1),
                                jnp.concatenate([zeros_k, k_rows[c:]], axis=1)], axis=0)
        lhs = lhs_in[un]
        (prods,) = yield [lambda: lax.dot_general(lhs, k_bd, NT_DIMS, preferred_element_type=F32)]
        decay = dec_in[un]
        a = jnp.where(strict, prods[:c] * decay, 0.0)
        ps = slice(p * 2 * c, (p + 1) * 2 * c)
        ak_ref[ci, 0, 0:c, ps] = (prods[c:] * decay).astype(BF16)
        ak_ref[ci, 0, c:3 * c, ps] = kdt_in[un]
        t = eye2 - jnp.where(coupling(0), a, 0.0)
        k = 1
        while 2 ** k < c:
            l = jnp.where(coupling(k), a, 0.0)
            t_bd = _pair_block_diag(t, first)
            (lt,) = yield [lambda: _mm(l, t_bd)]
            lt_bd = _pair_block_diag(lt, first)
            (tlt,) = yield [lambda: _mm(t, lt_bd)]
            t = t - tlt
            k += 1
        t_bd = _pair_block_diag(t, first).astype(BF16)
        rhs = rhs_in[un]
        (sol,) = yield [lambda: jnp.dot(t_bd, rhs, preferred_element_type=F32)]
        for i in range(2):
            h = 2 * p + i
            hs = slice(h * GDN_DK, (h + 1) * GDN_DK)
            u_ref[ci, 0, :, hs] = sol[i * c:(i + 1) * c, :GDN_DK]
            wq_ref[ci, 0, 0:c, hs] = sol[i * c:(i + 1) * c, GDN_DK:].astype(BF16)
            wq_ref[ci, 0, c:2 * c, hs] = qe_in[un, i * c:(i + 1) * c]
            eg_ref[ci, 0, h:h + 1, :] = eg_in[un, i:i + 1]

    lhs_out, kr_out, rhs_out, dec_out, kdt_out, qe_out, eg_out = stage_out
    y = _silu(_dwconv(x_ref[0], jnp.where(seg_start, 0.0, prev_ref[0]), jnp.where(seg_end, 0.0, next_ref[0]),
                      cw_ref[...]))
    ba = ba_ref[0]
    lane = lax.broadcasted_iota(jnp.int32, (1, BA_PAD), 1)
    is_g = jnp.logical_and(lane >= 2 * GDN_HEADS, lane < 4 * GDN_HEADS)
    beta_all = _sigmoid(ba)
    g_all = jnp.where(is_g, -jnp.exp(gp_ref[0:1]) * _softplus(ba + gp_ref[1:2]), 0.0)

    def chunk(ci):
        rs = slice(ci * c, (ci + 1) * c)
        heads = []
        for h in range(GDN_HEADS):
            lo = h * GDN_DK
            q = y[rs, lo:lo + GDN_DK]
            k = y[rs, GDN_WIDTH + lo:GDN_WIDTH + lo + GDN_DK]
            v = y[rs, 2 * GDN_WIDTH + lo:2 * GDN_WIDTH + lo + GDN_DK]
            q = q * (lax.rsqrt(jnp.sum(q * q, axis=-1, keepdims=True) + EPS) * (GDN_DK ** -0.5))
            k = k * lax.rsqrt(jnp.sum(k * k, axis=-1, keepdims=True) + EPS)
            heads.append((q, k, v))
        g = g_all[rs]
        k_rows = [jnp.concatenate([heads[2 * p][1], heads[2 * p + 1][1]], axis=0) for p in range(pairs)]
        k_ts = [kr.T for kr in k_rows]
        grow = lax.broadcasted_iota(jnp.int32, g.shape, 0)
        gcum_f = g
        step = 1
        while step < c:
            gcum_f = gcum_f + jnp.where(grow >= step, pltpu.roll(gcum_f, step, 0), 0.0)
            step *= 2
        gcum_b = gcum_f[c - 1:c] - gcum_f + g
        gcum_t = jnp.concatenate([gcum_f, gcum_b], axis=0).T
        beta_c = beta_all[rs]
        for p in range(pairs):
            kr_out[ci * pairs + p] = k_rows[p].astype(BF16)
        for d in range(2):
            gcum = gcum_b if d else gcum_f
            incl = (row <= col) if d else (row >= col)
            last = 0 if d else c - 1
            for p in range(pairs):
                un = unit_index(ci, d, p)
                (q0, k0, v0), (q1, k1, v1) = heads[2 * p], heads[2 * p + 1]
                cb = d * GDN_HEADS + 2 * p
                cg = 2 * GDN_HEADS + cb
                beta0, beta1 = beta_c[:, cb:cb + 1], beta_c[:, cb + 1:cb + 2]
                gc0, gc1 = gcum[:, cg:cg + 1], gcum[:, cg + 1:cg + 2]
                gt0, gt1 = gcum_t[cg:cg + 1], gcum_t[cg + 1:cg + 2]
                gr = (jnp.where(first_row, pltpu.roll(gt0, c, 1), gt1) if d
                      else jnp.where(first_row, gt0, pltpu.roll(gt1, c, 1)))
                gc = jnp.where(first, gc0, gc1)
                e0, e1 = jnp.exp(gc0), jnp.exp(gc1)
                kb0, kb1 = k0 * beta0, k1 * beta1
                dec_out[un] = jnp.exp(jnp.where(incl, gc - gr, NEG_BIG))
                lhs_out[un] = jnp.concatenate([jnp.concatenate([kb0, kb1], axis=1),
                                               jnp.concatenate([q0, q1], axis=1)], axis=0).astype(BF16)
                rhs_out[un] = jnp.concatenate([jnp.concatenate([v0 * beta0, kb0 * e0], axis=1),
                                               jnp.concatenate([v1 * beta1, kb1 * e1], axis=1)],
                                              axis=0).astype(BF16)
                kdt_out[un] = (k_ts[p] * jnp.exp(gc[last:last + 1] - gr)).astype(BF16)
                qe_out[un] = jnp.concatenate([q0 * e0, q1 * e1], axis=0).astype(BF16)
                eg_out[un] = jnp.concatenate(
                    [jnp.broadcast_to(jnp.exp(gc0[last:last + 1]), (1, BA_PAD)),
                     jnp.broadcast_to(jnp.exp(gc1[last:last + 1]), (1, BA_PAD)),
                     jnp.zeros((HALO - 2, BA_PAD), F32)], axis=0)

    for refs in out_refs:
        refs[3][...] = jnp.zeros_like(refs[3])
    _run_interleaved([pair_unit(ci, d, p) for ci in range(GDN_TILE_CHUNKS) for d in range(2)
                      for p in range(pairs)])
    for ci in range(GDN_TILE_CHUNKS):
        chunk(ci)


def _gdn_local_kernel(n_lat_tiles, n_all_tiles, n_tiles, x_ref, prev_ref, next_ref, ba_ref, cw_ref, gp_ref,
                      uf_ref, wqf_ref, akf_ref, egf_ref, ub_ref, wqb_ref, akb_ref, egb_ref, *stage_refs):
    s = pl.program_id(0)
    seg_start, seg_end = _segment_edges(jnp.minimum(s, n_tiles - 1) % n_all_tiles, n_lat_tiles, n_all_tiles)
    out_refs = ((uf_ref, wqf_ref, akf_ref, egf_ref), (ub_ref, wqb_ref, akb_ref, egb_ref))
    half = len(stage_refs) // 2
    sets = (stage_refs[:half], stage_refs[half:])

    @pl.when(s == 0)
    def _():
        for r in sets[1]:
            r[...] = jnp.zeros_like(r)

    for parity in range(2):
        @pl.when(s % 2 == parity)
        def _():
            _gdn_local_body(seg_start, seg_end, x_ref, prev_ref, next_ref, ba_ref, cw_ref, gp_ref,
                            out_refs, sets[1 - parity], sets[parity])


def _gdn_local_call(qkv, ba, conv_w, gate_params, n_lat_tok):
    bsz, t_all, width = qkv.shape
    c = GDN_CHUNK
    tile = GDN_TILE_CHUNKS * c
    n_all = t_all // tile
    n_lat = n_lat_tok // tile
    per_tile = tile // HALO
    last_halo = t_all // HALO - 1
    n_chunks = t_all // c
    n_tiles = bsz * n_all
    pairs = GDN_HEADS // 2
    units = GDN_TILE_CHUNKS * 2 * pairs
    t_in = lambda s: jnp.minimum(s, n_tiles - 1)
    t_out = lambda s: jnp.maximum(s - 1, 0)
    in_tok = lambda w: pl.BlockSpec((1, tile, w), lambda s: (t_in(s) // n_all, t_in(s) % n_all, 0))
    cblk = lambda rows, w: pl.BlockSpec((GDN_TILE_CHUNKS, 1, rows, w),
                                        lambda s: (t_out(s) % n_all, t_out(s) // n_all, 0, 0))
    vec = pl.BlockSpec((2, BA_PAD), lambda s: (0, 0))
    outs = [(jax.ShapeDtypeStruct((n_chunks, bsz, c, GDN_WIDTH), F32), cblk(c, GDN_WIDTH)),
            (jax.ShapeDtypeStruct((n_chunks, bsz, 2 * c, GDN_WIDTH), BF16), cblk(2 * c, GDN_WIDTH)),
            (jax.ShapeDtypeStruct((n_chunks, bsz, 3 * c, GDN_HEADS * c), BF16), cblk(3 * c, GDN_HEADS * c)),
            (jax.ShapeDtypeStruct((n_chunks, bsz, HALO, BA_PAD), F32), cblk(HALO, BA_PAD))] * 2
    stage = [pltpu.VMEM((units, 2 * c, 2 * GDN_DK), BF16),
             pltpu.VMEM((GDN_TILE_CHUNKS * pairs, 2 * c, GDN_DK), BF16),
             pltpu.VMEM((units, 2 * c, 2 * GDN_DK), BF16),
             pltpu.VMEM((units, c, 2 * c), F32),
             pltpu.VMEM((units, GDN_DK, 2 * c), BF16),
             pltpu.VMEM((units, 2 * c, GDN_DK), BF16),
             pltpu.VMEM((units, HALO, BA_PAD), F32)]
    return pl.pallas_call(
        functools.partial(_gdn_local_kernel, n_lat, n_all, n_tiles),
        grid=(n_tiles + 1,),
        in_specs=[in_tok(width),
                  pl.BlockSpec((1, HALO, width),
                               lambda s: (t_in(s) // n_all, jnp.maximum((t_in(s) % n_all) * per_tile - 1, 0), 0)),
                  pl.BlockSpec((1, HALO, width),
                               lambda s: (t_in(s) // n_all,
                                          jnp.minimum((t_in(s) % n_all + 1) * per_tile, last_halo), 0)),
                  in_tok(BA_PAD), pl.BlockSpec((CONV_W, width), lambda s: (0, 0)), vec],
        out_specs=[o[1] for o in outs],
        out_shape=[o[0] for o in outs],
        scratch_shapes=stage + stage,
        compiler_params=pltpu.CompilerParams(dimension_semantics=("arbitrary",),
                                             vmem_limit_bytes=VMEM_LIMIT),
        name="gdn_local",
    )(qkv, qkv, qkv, ba, conv_w, gate_params)


def _gdn_scan_steps(uf_ref, wqf_ref, akf_ref, egf_ref, ub_ref, wqb_ref, akb_ref, egb_ref,
                    of_ref, ob_ref, sf_ref, sb_ref):
    c = GDN_CHUNK
    bsz = uf_ref.shape[1]
    hs = lambda h: slice(h * GDN_DK, (h + 1) * GDN_DK)
    for sub in range(GDN_SCAN_CHUNKS):
        chains = [(refs, ci, b, h)
                  for refs, ci in (((uf_ref, wqf_ref, akf_ref, egf_ref, of_ref, sf_ref), sub),
                                   ((ub_ref, wqb_ref, akb_ref, egb_ref, ob_ref, sb_ref),
                                    GDN_SCAN_CHUNKS - 1 - sub))
                  for b in range(bsz) for h in range(GDN_HEADS)]
        states = [refs[5][b * GDN_HEADS + h] for refs, ci, b, h in chains]
        with_s = yield [lambda refs=refs, ci=ci, b=b, h=h, s=s:
                        jnp.dot(refs[1][ci, b, :, hs(h)], s.astype(BF16), preferred_element_type=F32)
                        for (refs, ci, b, h), s in zip(chains, states)]
        v_new = [refs[0][ci, b, :, hs(h)] - r[:c] for (refs, ci, b, h), r in zip(chains, with_s)]
        with_v = yield [lambda refs=refs, ci=ci, b=b, h=h, v=v:
                        jnp.dot(refs[2][ci, b, :, h * c:(h + 1) * c], v.astype(BF16), preferred_element_type=F32)
                        for (refs, ci, b, h), v in zip(chains, v_new)]
        for (refs, ci, b, h), s, r, a in zip(chains, states, with_s, with_v):
            refs[4][b, ci * c:(ci + 1) * c, hs(h)] = r[c:] + a[:c]
            refs[5][b * GDN_HEADS + h] = s * refs[3][ci, b, h:h + 1, :] + a[c:]


def _lin_scan(a, b, reverse):
    r = a.shape[0]
    row = lax.broadcasted_iota(jnp.int32, a.shape, 0)
    s = 1
    while s < r:
        ok = (row < r - s) if reverse else (row >= s)
        shift = r - s if reverse else s
        a_sh = jnp.where(ok, pltpu.roll(a, shift, 0), 1.0)
        b_sh = jnp.where(ok, pltpu.roll(b, shift, 0), 0.0)
        b = a * b_sh + b
        a = a * a_sh
        s *= 2
    return a, b


def _lru_ctx_direction(reverse, n_slabs, slab, x_ref, prev_ref, next_ref, cw, cb, w_ref, br, bi, lam, carry_ref):
    bsz, r, width = x_ref.shape
    step = pl.program_id(0)
    has_prev = slab > 0
    has_next = slab < n_slabs - 1
    xs = []
    for b in range(bsz):
        prev = jnp.where(has_prev, prev_ref[b], 0.0)
        nxt = jnp.where(has_next, next_ref[b], 0.0)
        xs.append(_dwconv(x_ref[b], prev, nxt, cw) + cb)
    x = jnp.concatenate(xs, axis=0)
    xb = x.astype(BF16)
    d = 1 if reverse else 0
    gate = lambda g: jnp.concatenate(
        [jnp.dot(xb[:, t * LRU_LANES:(t + 1) * LRU_LANES], w_ref[2 * d + g, t], preferred_element_type=F32)
         for t in range(width // LRU_LANES)], axis=1)
    gate_r = _sigmoid_tanh(gate(0) + br)
    gate_i = _sigmoid_tanh(gate(1) + bi)
    log_a = (-LRU_C * _softplus(-lam)) * gate_r
    a = jnp.exp(log_a)
    mult = jnp.sqrt(1.0 - a * a)
    row = lax.broadcasted_iota(jnp.int32, (r, width), 0)
    is_first = jnp.logical_and(step == 0, row == ((r - 1) if reverse else 0))
    edge = 0 if reverse else r - 1
    for b in range(bsz):
        sl = slice(b * r, (b + 1) * r)
        m = jnp.where(is_first, 1.0, mult[sl])
        a_cum, h = _lin_scan(a[sl], m * (gate_i[sl] * x[sl]), reverse)
        h = h + a_cum * carry_ref[b:b + 1]
        carry_ref[b:b + 1] = h[edge:edge + 1]


def _lru_ctx_kernel(n_slabs, xf_ref, pf_ref, nf_ref, xb_ref, pb_ref, nb_ref, cw_ref, cb_ref,
                    w_ref, bg_ref, lam_ref, fin_ref, cf_ref, cbk_ref):
    step = pl.program_id(0)

    @pl.when(step == 0)
    def _():
        cf_ref[...] = jnp.zeros_like(cf_ref)
        cbk_ref[...] = jnp.zeros_like(cbk_ref)

    cw = cw_ref[...]
    cb = cb_ref[...]
    _lru_ctx_direction(False, n_slabs, step, xf_ref, pf_ref, nf_ref, cw, cb, w_ref,
                       bg_ref[0:1], bg_ref[1:2], lam_ref[0:1], cf_ref)
    _lru_ctx_direction(True, n_slabs, n_slabs - 1 - step, xb_ref, pb_ref, nb_ref, cw, cb, w_ref,
                       bg_ref[2:3], bg_ref[3:4], lam_ref[1:2], cbk_ref)
    fin_ref[0] = cf_ref[...]
    fin_ref[1] = cbk_ref[...]


def _lru_ctx_call(rx, n_lat_tok, conv_w, conv_b, w_tiles, b_gate, lam):
    bsz, t_all, width = rx.shape
    r = GRID_W
    per = r // HALO
    n_slabs = (t_all - n_lat_tok) // r
    first = n_lat_tok // r
    main = lambda f: pl.BlockSpec((bsz, r, width), lambda s: (0, first + f(s), 0))
    prev = lambda f: pl.BlockSpec((bsz, HALO, width), lambda s: (0, (first + f(s)) * per - 1, 0))
    nxt = lambda f: pl.BlockSpec(
        (bsz, HALO, width), lambda s: (0, jnp.minimum((first + f(s) + 1) * per, t_all // HALO - 1), 0))
    fwd = lambda s: s
    bwd = lambda s: n_slabs - 1 - s
    full = lambda shape: pl.BlockSpec(shape, lambda s: (0,) * len(shape))
    carry = pltpu.VMEM((bsz, width), F32)
    return pl.pallas_call(
        functools.partial(_lru_ctx_kernel, n_slabs),
        grid=(n_slabs,),
        in_specs=[main(fwd), prev(fwd), nxt(fwd), main(bwd), prev(bwd), nxt(bwd),
                  full((CONV_W, width)), full((1, width)), full(w_tiles.shape),
                  full((4, width)), full((2, width))],
        out_specs=full((2, bsz, width)),
        out_shape=jax.ShapeDtypeStruct((2, bsz, width), F32),
        scratch_shapes=[carry, carry],
        compiler_params=pltpu.CompilerParams(dimension_semantics=("arbitrary",),
                                             vmem_limit_bytes=VMEM_LIMIT),
        name="lru_ctx",
    )(rx, rx, rx, rx, rx, rx, conv_w, conv_b, w_tiles, b_gate, lam)


def _lru_grid_gates(x_ref, cw_ref, cb_ref, w_ref, bg_ref, lam_ref, af_ref, bf_ref, ab_ref, bb_ref):
    n = GRID_W
    rp = LRU_ROWS_PER_PASS
    lanes = x_ref.shape[-1]
    crow = lax.broadcasted_iota(jnp.int32, (n, lanes), 0)
    cw = cw_ref[...]
    cb = cb_ref[...]

    def grid_rows(lo, hi):
        parts = []
        for r in range(lo, min(hi, 0)):
            parts.append(jnp.where(crow == 0, 0.0, pltpu.roll(x_ref[0, (n + r) * n:(n + r + 1) * n, :], 1, 0)))
        if max(lo, 0) < min(hi, n):
            parts.append(x_ref[0, max(lo, 0) * n:min(hi, n) * n, :])
        for r in range(max(lo, n), hi):
            parts.append(jnp.where(crow == n - 1, 0.0,
                                   pltpu.roll(x_ref[0, (r - n) * n:(r - n + 1) * n, :], n - 1, 0)))
        return parts[0] if len(parts) == 1 else jnp.concatenate(parts, axis=0)

    for r0 in range(0, n, rp):
        sl = slice(r0 * n, (r0 + rp) * n)
        xc = (grid_rows(r0 - 2, r0 + rp - 2) * cw[0:1] + grid_rows(r0 - 1, r0 + rp - 1) * cw[1:2]
              + grid_rows(r0, r0 + rp) * cw[2:3] + grid_rows(r0 + 1, r0 + rp + 1) * cw[3:4] + cb)
        xb = xc.astype(BF16)
        gates = yield [lambda i=i: jnp.dot(xb, w_ref[i, 0], preferred_element_type=F32) for i in range(4)]
        for d, (a_ref, b_ref) in enumerate(((af_ref, bf_ref), (ab_ref, bb_ref))):
            gate_r = _sigmoid_tanh(gates[2 * d] + bg_ref[2 * d:2 * d + 1])
            gate_i = _sigmoid_tanh(gates[2 * d + 1] + bg_ref[2 * d + 1:2 * d + 2])
            log_a = (-LRU_C * _softplus(-lam_ref[d:d + 1])) * gate_r
            a = jnp.exp(log_a)
            a_ref[sl] = a
            b_ref[sl] = jnp.sqrt(1.0 - a * a) * (gate_i * xc)


def _lru_grid_finish(h0_ref, out_ref, af_ref, bf_ref, ab_ref, bb_ref):
    n = GRID_W
    lanes = out_ref.shape[-1]
    crow = lax.broadcasted_iota(jnp.int32, (n, lanes), 0)
    hf = jnp.zeros((n, lanes), F32)
    hb = jnp.zeros((n, lanes), F32)
    pf = jnp.ones((n, lanes), F32)
    pb = jnp.ones((n, lanes), F32)
    for i in range(n):
        sf = slice(i * n, (i + 1) * n)
        sb = slice((n - 1 - i) * n, (n - i) * n)
        a = af_ref[sf]
        hf = a * hf + bf_ref[sf]
        pf = a * pf
        bf_ref[sf] = hf
        af_ref[sf] = pf
        a = ab_ref[sb]
        hb = a * hb + bb_ref[sb]
        pb = a * pb
        bb_ref[sb] = hb
        ab_ref[sb] = pb

    h0f = h0_ref[0, 0]
    h0b = h0_ref[1, 0]
    acc_a, acc_h = _lin_scan(pf, hf, False)
    in_f = jnp.where(crow == 0, h0f, pltpu.roll(acc_h + acc_a * h0f, 1, 0))
    acc_a, acc_h = _lin_scan(pb, hb, True)
    in_b = jnp.where(crow == n - 1, h0b, pltpu.roll(acc_h + acc_a * h0b, n - 1, 0))

    for r in range(n):
        sl = slice(r * n, (r + 1) * n)
        out_ref[0, sl, :] = (bf_ref[sl] + af_ref[sl] * in_f) + (bb_ref[sl] + ab_ref[sl] * in_b)


def _mixers_kernel(*refs):
    scan_in, lru_in, outs, scratch = refs[:8], refs[8:15], refs[15:18], refs[18:]
    of_ref, ob_ref, hs_ref = outs
    sf_ref, sb_ref = scratch[:2]

    @pl.when(pl.program_id(0) == 0)
    def _():
        sf_ref[...] = jnp.zeros_like(sf_ref)
        sb_ref[...] = jnp.zeros_like(sb_ref)

    x_ref, h0_ref = lru_in[:2]
    _run_interleaved([_gdn_scan_steps(*scan_in, of_ref, ob_ref, sf_ref, sb_ref),
                      _lru_grid_gates(x_ref, *lru_in[2:], *scratch[2:])])
    _lru_grid_finish(h0_ref, hs_ref, *scratch[2:])


def _mixers_call(local, rx, n_lat_tok, h0, conv_w, conv_b, w_tiles, b_gate, lam):
    n_chunks, bsz = local[0].shape[:2]
    width = rx.shape[2]
    c = GDN_CHUNK
    g = GDN_SCAN_CHUNKS
    n_all = n_chunks // g
    n_lat = n_lat_tok // (c * g)
    n_ctx = n_all - n_lat
    fwd = lambda s: jnp.where(s < n_ctx, n_lat + s, s - n_ctx)
    bwd = lambda s: n_all - 1 - s
    fwd_out = lambda s: jnp.maximum(s - n_ctx, 0)
    bwd_out = lambda s: jnp.minimum(n_all - 1 - s, n_lat - 1)
    cblk = lambda rows, w, f: pl.BlockSpec((g, bsz, rows, w), lambda s: (f(s), 0, 0, 0))
    ins = lambda f: [cblk(c, GDN_WIDTH, f), cblk(2 * c, GDN_WIDTH, f), cblk(3 * c, GDN_HEADS * c, f),
                     cblk(HALO, BA_PAD, f)]
    tok = lambda f: pl.BlockSpec((bsz, g * c, GDN_WIDTH), lambda s: (0, f(s), 0))
    state = pltpu.VMEM((bsz * GDN_HEADS, GDN_DK, GDN_DK), F32)

    nt = width // LRU_LANES
    unit = lambda s: jnp.minimum(s, bsz * nt - 1)
    assert n_all >= bsz * nt
    lane = lambda rows: pl.BlockSpec((rows, LRU_LANES), lambda s: (0, unit(s) % nt))
    grid_tok = pl.BlockSpec((1, n_lat_tok, LRU_LANES), lambda s: (unit(s) // nt, 0, unit(s) % nt))
    lru_scratch = pltpu.VMEM((n_lat_tok, LRU_LANES), F32)
    return pl.pallas_call(
        _mixers_kernel,
        grid=(n_all,),
        in_specs=ins(fwd) + ins(bwd) + [
            grid_tok,
            pl.BlockSpec((2, 1, 1, LRU_LANES), lambda s: (0, unit(s) // nt, 0, unit(s) % nt)),
            lane(CONV_W), lane(1),
            pl.BlockSpec((4, 1, LRU_LANES, LRU_LANES), lambda s: (0, unit(s) % nt, 0, 0)),
            lane(4), lane(2)],
        out_specs=[tok(fwd_out), tok(bwd_out), grid_tok],
        out_shape=[jax.ShapeDtypeStruct((bsz, n_lat_tok, GDN_WIDTH), F32)] * 2
                  + [jax.ShapeDtypeStruct((bsz, n_lat_tok, width), F32)],
        scratch_shapes=[state, state] + [lru_scratch] * 4,
        compiler_params=pltpu.CompilerParams(dimension_semantics=("arbitrary",),
                                             vmem_limit_bytes=VMEM_LIMIT),
        name="mixers",
    )(*local, rx, h0.reshape(2, bsz, 1, width), conv_w, conv_b, w_tiles, b_gate, lam)


def _out_ffn2_kernel(h1_ref, of_ref, ob_ref, z_ref, hs_ref, rg_ref, mod_ref, ng_ref, gnw_ref,
                     wout_ref, w1_ref, w3_ref, w2_ref, fg_ref, out_ref):
    mod = mod_ref[0]

    def sub_tile(rows):
        o = of_ref[0, rows] + ob_ref[0, rows]
        z = z_ref[0, rows]
        parts = []
        for h in range(GDN_HEADS):
            sl = slice(h * GDN_DK, (h + 1) * GDN_DK)
            oh = o[:, sl]
            parts.append(oh * lax.rsqrt(jnp.mean(oh * oh, axis=-1, keepdims=True) + EPS)
                         * gnw_ref[...] * _silu(z[:, sl]))
        parts.append(hs_ref[0, rows] * _gelu_tanh(rg_ref[0, rows]))
        mixed = jnp.concatenate(parts, axis=1).astype(BF16)
        (y,) = yield [lambda: jnp.dot(mixed, wout_ref[...], preferred_element_type=F32)]
        h2 = h1_ref[0, rows] + mod[5:6] * y
        ub = _rms_mod(h2, ng_ref[2:3], mod[6:7], mod[7:8]).astype(BF16)
        a, b = yield [lambda: jnp.dot(ub, w1_ref[...], preferred_element_type=F32),
                      lambda: jnp.dot(ub, w3_ref[...], preferred_element_type=F32)]
        g = (_silu(a) * b).astype(BF16)
        (f,) = yield [lambda: jnp.dot(g, w2_ref[...], preferred_element_type=F32)]
        h3 = h2 + FFN_RESIDUAL * mod[8:9] * f
        out_ref[0, rows] = h3 * lax.rsqrt(jnp.mean(h3 * h3, axis=-1, keepdims=True) + EPS) * fg_ref[...]

    _run_interleaved([sub_tile(slice(i * TOKEN_TILE, (i + 1) * TOKEN_TILE))
                      for i in range(out_ref.shape[1] // TOKEN_TILE)])


def _out_ffn2_call(h1, o_f, o_b, z, h_sum, rg, mods, norm_g, gdn_norm_w, w_out, w1, w3, w2, final_g):
    bsz, seq, _ = o_f.shape
    rows = OUT_SUB_TILES * TOKEN_TILE
    tok = lambda w: pl.BlockSpec((1, rows, w), lambda b, j: (b, j, 0))
    return pl.pallas_call(
        _out_ffn2_kernel,
        grid=(bsz, seq // rows),
        in_specs=[tok(D_MODEL), tok(GDN_WIDTH), tok(GDN_WIDTH), tok(GDN_WIDTH),
                  tok(LRU_WIDTH), tok(LRU_WIDTH),
                  pl.BlockSpec((1, N_MOD, D_MODEL), lambda b, j: (b + 1, 0, 0)),
                  _resident((3, D_MODEL)), _resident((1, GDN_DK)),
                  _resident((D_MODEL, D_MODEL)),
                  _resident((None, D_MODEL, D_FF), (1, 0, 0)), _resident((None, D_MODEL, D_FF), (1, 0, 0)),
                  _resident((None, D_FF, D_MODEL), (1, 0, 0)),
                  _resident((1, D_MODEL))],
        out_specs=tok(D_MODEL),
        out_shape=jax.ShapeDtypeStruct((bsz, seq, D_MODEL), F32),
        compiler_params=pltpu.CompilerParams(dimension_semantics=("arbitrary", "arbitrary"),
                                             vmem_limit_bytes=VMEM_LIMIT),
        name="out_ffn2",
    )(h1, o_f, o_b, z, h_sum, rg, mods, norm_g, gdn_norm_w, w_out, w1, w3, w2, final_g)


def kernel(x, c, ctx, c_ctx, w_ada, b_ada, norm_g, ffn_w1, ffn_w3, ffn_w2, w_in, w_out, gdn_conv_w, gdn_a_log,
           gdn_dt_bias, gdn_norm_w, lru_conv_w, lru_conv_b, lru_w_gate, lru_b_gate, lru_lambda, final_norm_g):
    bsz, seq, _ = x.shape
    assert w_ada.shape[0] == 1 and seq == GRID_W * GRID_W and ctx.shape[1] == TOKEN_TILE

    rows = -(-(bsz + 1) // 8) * 8
    cvec = jnp.concatenate([c_ctx[None, :], c, jnp.zeros((rows - bsz - 1, D_MODEL), F32)], axis=0)
    mods = _ada_call(cvec, w_ada[0], b_ada).reshape(rows, N_MOD, D_MODEL)

    w_in_bf = w_in[0].astype(BF16)
    o_ba = P_HEAD_COLS
    o_rx = o_ba + 4 * GDN_HEADS
    w_tail = jnp.concatenate([w_in_bf[:, o_rx:], w_in_bf[:, o_ba:o_rx],
                              jnp.zeros((D_MODEL, BA_PAD - 4 * GDN_HEADS), BF16)], axis=1)
    w1 = ffn_w1[0].astype(BF16)
    w3 = ffn_w3[0].astype(BF16)
    w2 = ffn_w2[0].astype(BF16)

    h1, qkv, z, rx, rg, ba = _ffn1_in_call(x, ctx, mods, norm_g[0], w1, w3, w2, w_in_bf, w_tail)

    gate_params = jnp.pad(jnp.stack([gdn_a_log[0].reshape(-1), gdn_dt_bias[0].reshape(-1)]),
                          ((0, 0), (2 * GDN_HEADS, BA_PAD - 4 * GDN_HEADS)))
    local = _gdn_local_call(qkv, ba, gdn_conv_w[0], gate_params, seq)

    nt = LRU_WIDTH // LRU_LANES
    per_tile = LRU_LANES // LRU_BW
    w_tiles = jnp.einsum('gticd,ij->gticjd', lru_w_gate[0].reshape(4, nt, per_tile, LRU_BW, LRU_BW),
                         jnp.eye(per_tile, dtype=F32)).reshape(4, nt, LRU_LANES, LRU_LANES).astype(BF16)
    b_gate = lru_b_gate[0].reshape(4, LRU_WIDTH)
    lam = lru_lambda[0]
    cw, cb = lru_conv_w[0], lru_conv_b
    h_ctx = _lru_ctx_call(rx, seq, cw, cb, w_tiles, b_gate, lam)
    o_f, o_b, h_sum = _mixers_call(local, rx, seq, h_ctx, cw, cb, w_tiles, b_gate, lam)

    return _out_ffn2_call(h1, o_f, o_b, z, h_sum, rg, mods, norm_g[0], gdn_norm_w,
                          w_out[0].astype(BF16), w1, w3, w2, final_norm_g[None, :])
```

```python
import functools

import jax
import jax.numpy as jnp
from jax import lax
from jax.experimental import pallas as pl
from jax.experimental.pallas import tpu as pltpu

D_MODEL = 1024
D_FF = 2816
N_MOD = 9
EPS = 1e-6
FFN_RESIDUAL = 0.5

GDN_WIDTH = 512
GDN_HEADS = 4
GDN_DK = 128
GDN_CHUNK = 64
CONV_W = 4
LRU_WIDTH = 512
LRU_BW = 64
LRU_C = 8.0
GRID_W = 64

ADA_MODS_PER_STEP = 3
TOKEN_TILE = 256
GDN_TILE_CHUNKS = TOKEN_TILE // GDN_CHUNK
GDN_SCAN_CHUNKS = 4
OUT_SUB_TILES = 2
FFN1_SUB_TILES = 2
HALO = 8
BA_PAD = 128
P_HEAD_COLS = 4 * GDN_WIDTH
P_TAIL_COLS = 2 * LRU_WIDTH + BA_PAD
LRU_LANES = 128
LRU_ROWS_PER_PASS = 8
VMEM_LIMIT = 56 * 1024 * 1024

BF16 = jnp.bfloat16
F32 = jnp.float32
NEG_BIG = -1e30
NT_DIMS = (((1,), (1,)), ((), ()))


def _mm(a, b):
    return jnp.dot(a.astype(BF16), b.astype(BF16), preferred_element_type=F32)


def _sigmoid(x):
    return 1.0 / (1.0 + jnp.exp(-x))


def _lru_a_b(half_r, half_i, bias_r, bias_i, lam, half_x):
    t_r = jnp.tanh(half_r + 0.5 * bias_r)
    t_i = jnp.tanh(half_i + 0.5 * bias_i)
    k = (-0.5 * LRU_C) * _softplus(-lam)
    a = jnp.exp(k * t_r + k)
    return a, jnp.sqrt(1.0 - a * a), t_i * half_x + half_x


def _silu(x):
    return x * _sigmoid(x)


def _softplus(x):
    return jnp.maximum(x, 0.0) + jnp.log(1.0 + jnp.exp(-jnp.abs(x)))


def _gelu_tanh(x):
    return 0.5 * x * (1.0 + jnp.tanh(0.7978845608028654 * (x + 0.044715 * x * x * x)))


def _rms_mod(h, g, shift, scale):
    y = h * lax.rsqrt(jnp.mean(h * h, axis=-1, keepdims=True) + EPS) * g
    return y * (1.0 + scale) + shift


def _swiglu(u, w1_ref, w3_ref, w2_ref):
    ub = u.astype(BF16)
    a = jnp.dot(ub, w1_ref[...], preferred_element_type=F32)
    b = jnp.dot(ub, w3_ref[...], preferred_element_type=F32)
    g = (_silu(a) * b).astype(BF16)
    return jnp.dot(g, w2_ref[...], preferred_element_type=F32)


def _dwconv(x, prev, nxt, w):
    t = x.shape[0]
    row = lax.broadcasted_iota(jnp.int32, x.shape, 0)
    xm1 = jnp.where(row == 0, prev[HALO - 1:HALO], pltpu.roll(x, 1, 0))
    xm2 = jnp.where(row == 0, prev[HALO - 2:HALO - 1],
                    jnp.where(row == 1, prev[HALO - 1:HALO], pltpu.roll(x, 2, 0)))
    xp1 = jnp.where(row == t - 1, nxt[0:1], pltpu.roll(x, t - 1, 0))
    return xm2 * w[0:1] + xm1 * w[1:2] + x * w[2:3] + xp1 * w[3:4]


def _resident(shape, index=None):
    index = index or (0,) * len(shape)
    return pl.BlockSpec(shape, lambda *_: index, pipeline_mode=pl.Buffered(1))


def _ada_kernel(c_ref, w_ref, b_ref, o_ref):
    o_ref[...] = _mm(_silu(c_ref[...]), w_ref[...]) + b_ref[...]


def _ada_call(cvec, w_ada, b_ada):
    rows = cvec.shape[0]
    cols = ADA_MODS_PER_STEP * D_MODEL
    return pl.pallas_call(
        _ada_kernel,
        grid=(N_MOD // ADA_MODS_PER_STEP,),
        in_specs=[pl.BlockSpec((rows, D_MODEL), lambda k: (0, 0)),
                  pl.BlockSpec((D_MODEL, cols), lambda k: (0, k)),
                  pl.BlockSpec((1, cols), lambda k: (0, k))],
        out_specs=pl.BlockSpec((rows, cols), lambda k: (0, k)),
        out_shape=jax.ShapeDtypeStruct((rows, N_MOD * D_MODEL), F32),
        compiler_params=pltpu.CompilerParams(dimension_semantics=("arbitrary",),
                                             vmem_limit_bytes=VMEM_LIMIT),
        name="ada",
    )(cvec, w_ada, b_ada)


def _segment_edges(j, n_lat_tiles, n_all_tiles):
    return (jnp.logical_or(j == 0, j == n_lat_tiles),
            jnp.logical_or(j == n_lat_tiles - 1, j == n_all_tiles - 1))


def _ffn1_in_kernel(n_lat_tiles, n_all_tiles, *refs):
    n_sub = FFN1_SUB_TILES
    ins, refs = refs[:3 * n_sub], refs[3 * n_sub:]
    ng_ref, w1_ref, w3_ref, w2_ref, win_ref, wtail_ref, h1_ref, qkv_ref, z_ref, rx_ref, rg_ref, ba_ref = refs
    s = pl.program_id(0)

    def sub_tile(i):
        x_ref, ctx_ref, mod_ref = ins[3 * i:3 * i + 3]
        rows = slice(i * TOKEN_TILE, (i + 1) * TOKEN_TILE)
        h = jnp.where((s * n_sub + i) % n_all_tiles == n_lat_tiles, ctx_ref[0], x_ref[0])
        mod = mod_ref[0]
        ub = _rms_mod(h, ng_ref[0:1], mod[0:1], mod[1:2]).astype(BF16)
        a, b = yield [lambda: jnp.dot(ub, w1_ref[...], preferred_element_type=F32),
                      lambda: jnp.dot(ub, w3_ref[...], preferred_element_type=F32)]
        g = (_silu(a) * b).astype(BF16)
        (f,) = yield [lambda: jnp.dot(g, w2_ref[...], preferred_element_type=F32)]
        h1 = h + FFN_RESIDUAL * mod[2:3] * f
        h1_ref[rows] = h1
        ub2 = _rms_mod(h1, ng_ref[1:2], mod[3:4], mod[4:5]).astype(BF16)
        head, tail = yield [lambda: jnp.dot(ub2, win_ref[...], preferred_element_type=F32),
                            lambda: jnp.dot(ub2, wtail_ref[...], preferred_element_type=F32)]
        qkv_ref[rows] = head[:, :3 * GDN_WIDTH]
        z_ref[rows] = head[:, 3 * GDN_WIDTH:]
        rx_ref[rows] = tail[:, :LRU_WIDTH]
        rg_ref[rows] = tail[:, LRU_WIDTH:2 * LRU_WIDTH]
        ba_ref[rows] = tail[:, 2 * LRU_WIDTH:]

    _run_interleaved([sub_tile(i) for i in range(n_sub)])


def _ffn1_in_call(x, ctx, mods, norm_g, w1, w3, w2, w_in_bf, w_tail):
    bsz, seq, _ = x.shape
    n_lat = seq // TOKEN_TILE
    n_all = n_lat + 1
    t_all = n_all * TOKEN_TILE
    n_sub = FFN1_SUB_TILES
    assert (bsz * n_all) % n_sub == 0
    per_tile = []
    for i in range(n_sub):
        tile = lambda s, i=i: s * n_sub + i
        per_tile += [
            pl.BlockSpec((1, TOKEN_TILE, D_MODEL),
                         lambda s, t=tile: (t(s) // n_all, jnp.minimum(t(s) % n_all, n_lat - 1), 0)),
            pl.BlockSpec((1, TOKEN_TILE, D_MODEL), lambda s, t=tile: (t(s) // n_all, 0, 0)),
            pl.BlockSpec((1, N_MOD, D_MODEL),
                         lambda s, t=tile: (jnp.where(t(s) % n_all == n_lat, 0, t(s) // n_all + 1), 0, 0))]
    widths = (D_MODEL, 3 * GDN_WIDTH, GDN_WIDTH, LRU_WIDTH, LRU_WIDTH, BA_PAD)
    outs = pl.pallas_call(
        functools.partial(_ffn1_in_kernel, n_lat, n_all),
        grid=(bsz * n_all // n_sub,),
        in_specs=per_tile + [
            _resident((3, D_MODEL)),
            _resident((None, D_MODEL, D_FF), (0, 0, 0)), _resident((None, D_MODEL, D_FF), (0, 0, 0)),
            _resident((None, D_FF, D_MODEL), (0, 0, 0)),
            _resident((D_MODEL, P_HEAD_COLS)), _resident((D_MODEL, P_TAIL_COLS))],
        out_specs=[pl.BlockSpec((n_sub * TOKEN_TILE, w), lambda s: (s, 0)) for w in widths],
        out_shape=[jax.ShapeDtypeStruct((bsz * t_all, w), F32) for w in widths],
        compiler_params=pltpu.CompilerParams(dimension_semantics=("arbitrary",),
                                             vmem_limit_bytes=VMEM_LIMIT),
        name="ffn1_in",
    )(*([x, ctx, mods] * n_sub), norm_g, w1, w3, w2, w_in_bf, w_tail)
    return [o.reshape(bsz, t_all, o.shape[-1]) for o in outs]


def _pair_block_diag(x, first):
    return jnp.concatenate([jnp.where(first, x, 0.0), jnp.where(first, 0.0, x)], axis=0)


def _run_interleaved(coroutines):
    active = [[gen, next(gen)] for gen in coroutines]
    while active:
        still = []
        for item in active:
            results = [dot() for dot in item[1]]
            try:
                item[1] = item[0].send(results)
                still.append(item)
            except StopIteration:
                pass
        active = still


def _gdn_local_body(seg_start, seg_end, x_ref, prev_ref, next_ref, ba_ref, cw_ref, gp_ref,
                    out_refs, stage_in, stage_out):
    c = GDN_CHUNK
    pairs = GDN_HEADS // 2
    row = lax.broadcasted_iota(jnp.int32, (c, 2 * c), 0)
    lane2 = lax.broadcasted_iota(jnp.int32, (c, 2 * c), 1)
    col = lane2 & (c - 1)
    first = lane2 < c
    first_row = first[0:1]
    eye2 = (row == col).astype(F32)
    zeros_k = jnp.zeros((c, GDN_DK), BF16)
    unit_index = lambda ci, d, p: (ci * 2 + d) * pairs + p

    def coupling(k):
        return jnp.logical_and((row >> k) != (col >> k), (row >> (k + 1)) == (col >> (k + 1)))

    lhs_in, kr_in, rhs_in, dec_in, kdt_in, qe_in, eg_in = stage_in

    def pair_unit(ci, d, p):
        u_ref, wq_ref, ak_ref, eg_ref = out_refs[d]
        un = unit_index(ci, d, p)
        strict = (row < col) if d else (row > col)
        k_rows = kr_in[ci * pairs + p]
        k_bd = jnp.concatenate([jnp.concatenate([k_rows[:c], zeros_k], axis=1),
                                jnp.concatenate([zeros_k, k_rows[c:]], axis=1)], axis=0)
        lhs = lhs_in[un]
        (prods,) = yield [lambda: lax.dot_general(lhs, k_bd, NT_DIMS, preferred_element_type=F32)]
        decay = dec_in[un]
        a = jnp.where(strict, prods[:c] * decay, 0.0)
        ps = slice(p * 2 * c, (p + 1) * 2 * c)
        ak_ref[ci, 0, 0:c, ps] = (prods[c:] * decay).astype(BF16)
        ak_ref[ci, 0, c:3 * c, ps] = kdt_in[un]
        t = eye2 - jnp.where(coupling(0), a, 0.0)
        k = 1
        while 2 ** k < c:
            l = jnp.where(coupling(k), a, 0.0)
            t_bd = _pair_block_diag(t, first)
            (lt,) = yield [lambda: _mm(l, t_bd)]
            lt_bd = _pair_block_diag(lt, first)
            (tlt,) = yield [lambda: _mm(t, lt_bd)]
            t = t - tlt
            k += 1
        t_bd = _pair_block_diag(t, first).astype(BF16)
        rhs = rhs_in[un]
        (sol,) = yield [lambda: jnp.dot(t_bd, rhs, preferred_element_type=F32)]
        for i in range(2):
            h = 2 * p + i
            hs = slice(h * GDN_DK, (h + 1) * GDN_DK)
            u_ref[ci, 0, :, hs] = sol[i * c:(i + 1) * c, :GDN_DK]
            wq_ref[ci, 0, 0:c, hs] = sol[i * c:(i + 1) * c, GDN_DK:].astype(BF16)
            wq_ref[ci, 0, c:2 * c, hs] = qe_in[un, i * c:(i + 1) * c]
            eg_ref[ci, 0, h:h + 1, :] = eg_in[un, i:i + 1]

    lhs_out, kr_out, rhs_out, dec_out, kdt_out, qe_out, eg_out = stage_out
    y = _silu(_dwconv(x_ref[0], jnp.where(seg_start, 0.0, prev_ref[0]), jnp.where(seg_end, 0.0, next_ref[0]),
                      cw_ref[...]))
    ba = ba_ref[0]
    lane = lax.broadcasted_iota(jnp.int32, (1, BA_PAD), 1)
    is_g = jnp.logical_and(lane >= 2 * GDN_HEADS, lane < 4 * GDN_HEADS)
    beta_all = _sigmoid(ba)
    g_all = jnp.where(is_g, -jnp.exp(gp_ref[0:1]) * _softplus(ba + gp_ref[1:2]), 0.0)

    def chunk(ci):
        rs = slice(ci * c, (ci + 1) * c)
        heads = []
        for h in range(GDN_HEADS):
            lo = h * GDN_DK
            q = y[rs, lo:lo + GDN_DK]
            k = y[rs, GDN_WIDTH + lo:GDN_WIDTH + lo + GDN_DK]
            v = y[rs, 2 * GDN_WIDTH + lo:2 * GDN_WIDTH + lo + GDN_DK]
            q = q * (lax.rsqrt(jnp.sum(q * q, axis=-1, keepdims=True) + EPS) * (GDN_DK ** -0.5))
            k = k * lax.rsqrt(jnp.sum(k * k, axis=-1, keepdims=True) + EPS)
            heads.append((q, k, v))
        g = g_all[rs]
        k_rows = [jnp.concatenate([heads[2 * p][1], heads[2 * p + 1][1]], axis=0) for p in range(pairs)]
        k_ts = [kr.T for kr in k_rows]
        grow = lax.broadcasted_iota(jnp.int32, g.shape, 0)
        gcum_f = g
        step = 1
        while step < c:
            gcum_f = gcum_f + jnp.where(grow >= step, pltpu.roll(gcum_f, step, 0), 0.0)
            step *= 2
        gcum_b = gcum_f[c - 1:c] - gcum_f + g
        gcum_t = jnp.concatenate([gcum_f, gcum_b], axis=0).T
        beta_c = beta_all[rs]
        for p in range(pairs):
            kr_out[ci * pairs + p] = k_rows[p].astype(BF16)
        for d in range(2):
            gcum = gcum_b if d else gcum_f
            incl = (row <= col) if d else (row >= col)
            last = 0 if d else c - 1
            for p in range(pairs):
                un = unit_index(ci, d, p)
                (q0, k0, v0), (q1, k1, v1) = heads[2 * p], heads[2 * p + 1]
                cb = d * GDN_HEADS + 2 * p
                cg = 2 * GDN_HEADS + cb
                beta0, beta1 = beta_c[:, cb:cb + 1], beta_c[:, cb + 1:cb + 2]
                gc0, gc1 = gcum[:, cg:cg + 1], gcum[:, cg + 1:cg + 2]
                gt0, gt1 = gcum_t[cg:cg + 1], gcum_t[cg + 1:cg + 2]
                gr = (jnp.where(first_row, pltpu.roll(gt0, c, 1), gt1) if d
                      else jnp.where(first_row, gt0, pltpu.roll(gt1, c, 1)))
                gc = jnp.where(first, gc0, gc1)
                e0, e1 = jnp.exp(gc0), jnp.exp(gc1)
                kb0, kb1 = k0 * beta0, k1 * beta1
                dec_out[un] = jnp.exp(jnp.where(incl, gc - gr, NEG_BIG))
                lhs_out[un] = jnp.concatenate([jnp.concatenate([kb0, kb1], axis=1),
                                               jnp.concatenate([q0, q1], axis=1)], axis=0).astype(BF16)
                rhs_out[un] = jnp.concatenate([jnp.concatenate([v0 * beta0, kb0 * e0], axis=1),
                                               jnp.concatenate([v1 * beta1, kb1 * e1], axis=1)],
                                              axis=0).astype(BF16)
                kdt_out[un] = (k_ts[p] * jnp.exp(gc[last:last + 1] - gr)).astype(BF16)
                qe_out[un] = jnp.concatenate([q0 * e0, q1 * e1], axis=0).astype(BF16)
                eg_out[un] = jnp.concatenate(
                    [jnp.broadcast_to(jnp.exp(gc0[last:last + 1]), (1, BA_PAD)),
                     jnp.broadcast_to(jnp.exp(gc1[last:last + 1]), (1, BA_PAD)),
                     jnp.zeros((HALO - 2, BA_PAD), F32)], axis=0)

    for refs in out_refs:
        refs[3][...] = jnp.zeros_like(refs[3])
    _run_interleaved([pair_unit(ci, d, p) for ci in range(GDN_TILE_CHUNKS) for d in range(2)
                      for p in range(pairs)])
    for ci in range(GDN_TILE_CHUNKS):
        chunk(ci)


def _gdn_local_kernel(n_lat_tiles, n_all_tiles, n_tiles, x_ref, prev_ref, next_ref, ba_ref, cw_ref, gp_ref,
                      uf_ref, wqf_ref, akf_ref, egf_ref, ub_ref, wqb_ref, akb_ref, egb_ref, *stage_refs):
    s = pl.program_id(0)
    seg_start, seg_end = _segment_edges(jnp.minimum(s, n_tiles - 1) % n_all_tiles, n_lat_tiles, n_all_tiles)
    out_refs = ((uf_ref, wqf_ref, akf_ref, egf_ref), (ub_ref, wqb_ref, akb_ref, egb_ref))
    half = len(stage_refs) // 2
    sets = (stage_refs[:half], stage_refs[half:])

    @pl.when(s == 0)
    def _():
        for r in sets[1]:
            r[...] = jnp.zeros_like(r)

    for parity in range(2):
        @pl.when(s % 2 == parity)
        def _():
            _gdn_local_body(seg_start, seg_end, x_ref, prev_ref, next_ref, ba_ref, cw_ref, gp_ref,
                            out_refs, sets[1 - parity], sets[parity])


def _gdn_local_call(qkv, ba, conv_w, gate_params, n_lat_tok):
    bsz, t_all, width = qkv.shape
    c = GDN_CHUNK
    tile = GDN_TILE_CHUNKS * c
    n_all = t_all // tile
    n_lat = n_lat_tok // tile
    per_tile = tile // HALO
    last_halo = t_all // HALO - 1
    n_chunks = t_all // c
    n_tiles = bsz * n_all
    pairs = GDN_HEADS // 2
    units = GDN_TILE_CHUNKS * 2 * pairs
    t_in = lambda s: jnp.minimum(s, n_tiles - 1)
    t_out = lambda s: jnp.maximum(s - 1, 0)
    in_tok = lambda w: pl.BlockSpec((1, tile, w), lambda s: (t_in(s) // n_all, t_in(s) % n_all, 0))
    cblk = lambda rows, w: pl.BlockSpec((GDN_TILE_CHUNKS, 1, rows, w),
                                        lambda s: (t_out(s) % n_all, t_out(s) // n_all, 0, 0))
    vec = pl.BlockSpec((2, BA_PAD), lambda s: (0, 0))
    outs = [(jax.ShapeDtypeStruct((n_chunks, bsz, c, GDN_WIDTH), F32), cblk(c, GDN_WIDTH)),
            (jax.ShapeDtypeStruct((n_chunks, bsz, 2 * c, GDN_WIDTH), BF16), cblk(2 * c, GDN_WIDTH)),
            (jax.ShapeDtypeStruct((n_chunks, bsz, 3 * c, GDN_HEADS * c), BF16), cblk(3 * c, GDN_HEADS * c)),
            (jax.ShapeDtypeStruct((n_chunks, bsz, HALO, BA_PAD), F32), cblk(HALO, BA_PAD))] * 2
    stage = [pltpu.VMEM((units, 2 * c, 2 * GDN_DK), BF16),
             pltpu.VMEM((GDN_TILE_CHUNKS * pairs, 2 * c, GDN_DK), BF16),
             pltpu.VMEM((units, 2 * c, 2 * GDN_DK), BF16),
             pltpu.VMEM((units, c, 2 * c), F32),
             pltpu.VMEM((units, GDN_DK, 2 * c), BF16),
             pltpu.VMEM((units, 2 * c, GDN_DK), BF16),
             pltpu.VMEM((units, HALO, BA_PAD), F32)]
    return pl.pallas_call(
        functools.partial(_gdn_local_kernel, n_lat, n_all, n_tiles),
        grid=(n_tiles + 1,),
        in_specs=[in_tok(width),
                  pl.BlockSpec((1, HALO, width),
                               lambda s: (t_in(s) // n_all, jnp.maximum((t_in(s) % n_all) * per_tile - 1, 0), 0)),
                  pl.BlockSpec((1, HALO, width),
                               lambda s: (t_in(s) // n_all,
                                          jnp.minimum((t_in(s) % n_all + 1) * per_tile, last_halo), 0)),
                  in_tok(BA_PAD), pl.BlockSpec((CONV_W, width), lambda s: (0, 0)), vec],
        out_specs=[o[1] for o in outs],
        out_shape=[o[0] for o in outs],
        scratch_shapes=stage + stage,
        compiler_params=pltpu.CompilerParams(dimension_semantics=("arbitrary",),
                                             vmem_limit_bytes=VMEM_LIMIT),
        name="gdn_local",
    )(qkv, qkv, qkv, ba, conv_w, gate_params)


def _gdn_scan_steps(uf_ref, wqf_ref, akf_ref, egf_ref, ub_ref, wqb_ref, akb_ref, egb_ref,
                    of_ref, ob_ref, sf_ref, sb_ref):
    c = GDN_CHUNK
    bsz = uf_ref.shape[1]
    hs = lambda h: slice(h * GDN_DK, (h + 1) * GDN_DK)
    for sub in range(GDN_SCAN_CHUNKS):
        chains = [(refs, ci, b, h)
                  for refs, ci in (((uf_ref, wqf_ref, akf_ref, egf_ref, of_ref, sf_ref), sub),
                                   ((ub_ref, wqb_ref, akb_ref, egb_ref, ob_ref, sb_ref),
                                    GDN_SCAN_CHUNKS - 1 - sub))
                  for b in range(bsz) for h in range(GDN_HEADS)]
        states = [refs[5][b * GDN_HEADS + h] for refs, ci, b, h in chains]
        with_s = yield [lambda refs=refs, ci=ci, b=b, h=h, s=s:
                        jnp.dot(refs[1][ci, b, :, hs(h)], s.astype(BF16), preferred_element_type=F32)
                        for (refs, ci, b, h), s in zip(chains, states)]
        v_new = [refs[0][ci, b, :, hs(h)] - r[:c] for (refs, ci, b, h), r in zip(chains, with_s)]
        with_v = yield [lambda refs=refs, ci=ci, b=b, h=h, v=v:
                        jnp.dot(refs[2][ci, b, :, h * c:(h + 1) * c], v.astype(BF16), preferred_element_type=F32)
                        for (refs, ci, b, h), v in zip(chains, v_new)]
        for (refs, ci, b, h), s, r, a in zip(chains, states, with_s, with_v):
            refs[4][b, ci * c:(ci + 1) * c, hs(h)] = r[c:] + a[:c]
            refs[5][b * GDN_HEADS + h] = s * refs[3][ci, b, h:h + 1, :] + a[c:]


def _lin_scan(a, b, reverse):
    r = a.shape[0]
    row = lax.broadcasted_iota(jnp.int32, a.shape, 0)
    s = 1
    while s < r:
        ok = (row < r - s) if reverse else (row >= s)
        shift = r - s if reverse else s
        a_sh = jnp.where(ok, pltpu.roll(a, shift, 0), 1.0)
        b_sh = jnp.where(ok, pltpu.roll(b, shift, 0), 0.0)
        b = a * b_sh + b
        a = a * a_sh
        s *= 2
    return a, b


def _lru_ctx_direction(reverse, n_slabs, slab, x_ref, prev_ref, next_ref, cw, cb, w_ref, br, bi, lam, carry_ref):
    bsz, r, width = x_ref.shape
    step = pl.program_id(0)
    has_prev = slab > 0
    has_next = slab < n_slabs - 1
    xs = []
    for b in range(bsz):
        prev = jnp.where(has_prev, prev_ref[b], 0.0)
        nxt = jnp.where(has_next, next_ref[b], 0.0)
        xs.append(_dwconv(x_ref[b], prev, nxt, cw) + cb)
    x = jnp.concatenate(xs, axis=0)
    xb = x.astype(BF16)
    d = 1 if reverse else 0
    gate = lambda g: jnp.concatenate(
        [jnp.dot(xb[:, t * LRU_LANES:(t + 1) * LRU_LANES], w_ref[2 * d + g, t], preferred_element_type=F32)
         for t in range(width // LRU_LANES)], axis=1)
    a, mult, gated_x = _lru_a_b(gate(0), gate(1), br, bi, lam, 0.5 * x)
    row = lax.broadcasted_iota(jnp.int32, (r, width), 0)
    is_first = jnp.logical_and(step == 0, row == ((r - 1) if reverse else 0))
    edge = 0 if reverse else r - 1
    for b in range(bsz):
        sl = slice(b * r, (b + 1) * r)
        m = jnp.where(is_first, 1.0, mult[sl])
        a_cum, h = _lin_scan(a[sl], m * gated_x[sl], reverse)
        h = h + a_cum * carry_ref[b:b + 1]
        carry_ref[b:b + 1] = h[edge:edge + 1]


def _lru_ctx_kernel(n_slabs, xf_ref, pf_ref, nf_ref, xb_ref, pb_ref, nb_ref, cw_ref, cb_ref,
                    w_ref, bg_ref, lam_ref, fin_ref, cf_ref, cbk_ref):
    step = pl.program_id(0)

    @pl.when(step == 0)
    def _():
        cf_ref[...] = jnp.zeros_like(cf_ref)
        cbk_ref[...] = jnp.zeros_like(cbk_ref)

    cw = cw_ref[...]
    cb = cb_ref[...]
    _lru_ctx_direction(False, n_slabs, step, xf_ref, pf_ref, nf_ref, cw, cb, w_ref,
                       bg_ref[0:1], bg_ref[1:2], lam_ref[0:1], cf_ref)
    _lru_ctx_direction(True, n_slabs, n_slabs - 1 - step, xb_ref, pb_ref, nb_ref, cw, cb, w_ref,
                       bg_ref[2:3], bg_ref[3:4], lam_ref[1:2], cbk_ref)
    fin_ref[0] = cf_ref[...]
    fin_ref[1] = cbk_ref[...]


def _lru_ctx_call(rx, n_lat_tok, conv_w, conv_b, w_tiles, b_gate, lam):
    bsz, t_all, width = rx.shape
    r = GRID_W
    per = r // HALO
    n_slabs = (t_all - n_lat_tok) // r
    first = n_lat_tok // r
    main = lambda f: pl.BlockSpec((bsz, r, width), lambda s: (0, first + f(s), 0))
    prev = lambda f: pl.BlockSpec((bsz, HALO, width), lambda s: (0, (first + f(s)) * per - 1, 0))
    nxt = lambda f: pl.BlockSpec(
        (bsz, HALO, width), lambda s: (0, jnp.minimum((first + f(s) + 1) * per, t_all // HALO - 1), 0))
    fwd = lambda s: s
    bwd = lambda s: n_slabs - 1 - s
    full = lambda shape: pl.BlockSpec(shape, lambda s: (0,) * len(shape))
    carry = pltpu.VMEM((bsz, width), F32)
    return pl.pallas_call(
        functools.partial(_lru_ctx_kernel, n_slabs),
        grid=(n_slabs,),
        in_specs=[main(fwd), prev(fwd), nxt(fwd), main(bwd), prev(bwd), nxt(bwd),
                  full((CONV_W, width)), full((1, width)), full(w_tiles.shape),
                  full((4, width)), full((2, width))],
        out_specs=full((2, bsz, width)),
        out_shape=jax.ShapeDtypeStruct((2, bsz, width), F32),
        scratch_shapes=[carry, carry],
        compiler_params=pltpu.CompilerParams(dimension_semantics=("arbitrary",),
                                             vmem_limit_bytes=VMEM_LIMIT),
        name="lru_ctx",
    )(rx, rx, rx, rx, rx, rx, conv_w, conv_b, w_tiles, b_gate, lam)


def _lru_grid_gates(x_ref, cw_ref, cb_ref, w_ref, bg_ref, lam_ref, af_ref, bf_ref, ab_ref, bb_ref):
    n = GRID_W
    rp = LRU_ROWS_PER_PASS
    lanes = x_ref.shape[-1]
    crow = lax.broadcasted_iota(jnp.int32, (n, lanes), 0)
    cw = cw_ref[...]
    cb = cb_ref[...]

    def grid_rows(lo, hi):
        parts = []
        for r in range(lo, min(hi, 0)):
            parts.append(jnp.where(crow == 0, 0.0, pltpu.roll(x_ref[0, (n + r) * n:(n + r + 1) * n, :], 1, 0)))
        if max(lo, 0) < min(hi, n):
            parts.append(x_ref[0, max(lo, 0) * n:min(hi, n) * n, :])
        for r in range(max(lo, n), hi):
            parts.append(jnp.where(crow == n - 1, 0.0,
                                   pltpu.roll(x_ref[0, (r - n) * n:(r - n + 1) * n, :], n - 1, 0)))
        return parts[0] if len(parts) == 1 else jnp.concatenate(parts, axis=0)

    for r0 in range(0, n, rp):
        sl = slice(r0 * n, (r0 + rp) * n)
        xc = (grid_rows(r0 - 2, r0 + rp - 2) * cw[0:1] + grid_rows(r0 - 1, r0 + rp - 1) * cw[1:2]
              + grid_rows(r0, r0 + rp) * cw[2:3] + grid_rows(r0 + 1, r0 + rp + 1) * cw[3:4] + cb)
        xb = xc.astype(BF16)
        gates = yield [lambda i=i: jnp.dot(xb, w_ref[i, 0], preferred_element_type=F32) for i in range(4)]
        half_x = 0.5 * xc
        for d, (a_ref, b_ref) in enumerate(((af_ref, bf_ref), (ab_ref, bb_ref))):
            a, gain, gated_x = _lru_a_b(gates[2 * d], gates[2 * d + 1], bg_ref[2 * d:2 * d + 1],
                                        bg_ref[2 * d + 1:2 * d + 2], lam_ref[d:d + 1], half_x)
            a_ref[sl] = a
            b_ref[sl] = gain * gated_x


def _lru_grid_finish(h0_ref, out_ref, af_ref, bf_ref, ab_ref, bb_ref):
    n = GRID_W
    lanes = out_ref.shape[-1]
    crow = lax.broadcasted_iota(jnp.int32, (n, lanes), 0)
    hf = jnp.zeros((n, lanes), F32)
    hb = jnp.zeros((n, lanes), F32)
    pf = jnp.ones((n, lanes), F32)
    pb = jnp.ones((n, lanes), F32)
    for i in range(n):
        sf = slice(i * n, (i + 1) * n)
        sb = slice((n - 1 - i) * n, (n - i) * n)
        a = af_ref[sf]
        hf = a * hf + bf_ref[sf]
        pf = a * pf
        bf_ref[sf] = hf
        af_ref[sf] = pf
        a = ab_ref[sb]
        hb = a * hb + bb_ref[sb]
        pb = a * pb
        bb_ref[sb] = hb
        ab_ref[sb] = pb

    h0f = h0_ref[0, 0]
    h0b = h0_ref[1, 0]
    acc_a, acc_h = _lin_scan(pf, hf, False)
    in_f = jnp.where(crow == 0, h0f, pltpu.roll(acc_h + acc_a * h0f, 1, 0))
    acc_a, acc_h = _lin_scan(pb, hb, True)
    in_b = jnp.where(crow == n - 1, h0b, pltpu.roll(acc_h + acc_a * h0b, n - 1, 0))

    for r in range(n):
        sl = slice(r * n, (r + 1) * n)
        out_ref[0, sl, :] = (bf_ref[sl] + af_ref[sl] * in_f) + (bb_ref[sl] + ab_ref[sl] * in_b)


def _mixers_kernel(*refs):
    scan_in, lru_in, outs, scratch = refs[:8], refs[8:15], refs[15:18], refs[18:]
    of_ref, ob_ref, hs_ref = outs
    sf_ref, sb_ref = scratch[:2]

    @pl.when(pl.program_id(0) == 0)
    def _():
        sf_ref[...] = jnp.zeros_like(sf_ref)
        sb_ref[...] = jnp.zeros_like(sb_ref)

    x_ref, h0_ref = lru_in[:2]
    _run_interleaved([_gdn_scan_steps(*scan_in, of_ref, ob_ref, sf_ref, sb_ref),
                      _lru_grid_gates(x_ref, *lru_in[2:], *scratch[2:])])
    _lru_grid_finish(h0_ref, hs_ref, *scratch[2:])


def _mixers_call(local, rx, n_lat_tok, h0, conv_w, conv_b, w_tiles, b_gate, lam):
    n_chunks, bsz = local[0].shape[:2]
    width = rx.shape[2]
    c = GDN_CHUNK
    g = GDN_SCAN_CHUNKS
    n_all = n_chunks // g
    n_lat = n_lat_tok // (c * g)
    n_ctx = n_all - n_lat
    fwd = lambda s: jnp.where(s < n_ctx, n_lat + s, s - n_ctx)
    bwd = lambda s: n_all - 1 - s
    fwd_out = lambda s: jnp.maximum(s - n_ctx, 0)
    bwd_out = lambda s: jnp.minimum(n_all - 1 - s, n_lat - 1)
    cblk = lambda rows, w, f: pl.BlockSpec((g, bsz, rows, w), lambda s: (f(s), 0, 0, 0))
    ins = lambda f: [cblk(c, GDN_WIDTH, f), cblk(2 * c, GDN_WIDTH, f), cblk(3 * c, GDN_HEADS * c, f),
                     cblk(HALO, BA_PAD, f)]
    tok = lambda f: pl.BlockSpec((bsz, g * c, GDN_WIDTH), lambda s: (0, f(s), 0))
    state = pltpu.VMEM((bsz * GDN_HEADS, GDN_DK, GDN_DK), F32)

    nt = width // LRU_LANES
    unit = lambda s: jnp.minimum(s, bsz * nt - 1)
    assert n_all >= bsz * nt
    lane = lambda rows: pl.BlockSpec((rows, LRU_LANES), lambda s: (0, unit(s) % nt))
    grid_tok = pl.BlockSpec((1, n_lat_tok, LRU_LANES), lambda s: (unit(s) // nt, 0, unit(s) % nt))
    lru_scratch = pltpu.VMEM((n_lat_tok, LRU_LANES), F32)
    return pl.pallas_call(
        _mixers_kernel,
        grid=(n_all,),
        in_specs=ins(fwd) + ins(bwd) + [
            grid_tok,
            pl.BlockSpec((2, 1, 1, LRU_LANES), lambda s: (0, unit(s) // nt, 0, unit(s) % nt)),
            lane(CONV_W), lane(1),
            pl.BlockSpec((4, 1, LRU_LANES, LRU_LANES), lambda s: (0, unit(s) % nt, 0, 0)),
            lane(4), lane(2)],
        out_specs=[tok(fwd_out), tok(bwd_out), grid_tok],
        out_shape=[jax.ShapeDtypeStruct((bsz, n_lat_tok, GDN_WIDTH), F32)] * 2
                  + [jax.ShapeDtypeStruct((bsz, n_lat_tok, width), F32)],
        scratch_shapes=[state, state] + [lru_scratch] * 4,
        compiler_params=pltpu.CompilerParams(dimension_semantics=("arbitrary",),
                                             vmem_limit_bytes=VMEM_LIMIT),
        name="mixers",
    )(*local, rx, h0.reshape(2, bsz, 1, width), conv_w, conv_b, w_tiles, b_gate, lam)


def _out_ffn2_kernel(h1_ref, of_ref, ob_ref, z_ref, hs_ref, rg_ref, mod_ref, ng_ref, gnw_ref,
                     wout_ref, w1_ref, w3_ref, w2_ref, fg_ref, out_ref):
    mod = mod_ref[0]

    def sub_tile(rows):
        o = of_ref[0, rows] + ob_ref[0, rows]
        z = z_ref[0, rows]
        parts = []
        for h in range(GDN_HEADS):
            sl = slice(h * GDN_DK, (h + 1) * GDN_DK)
            oh = o[:, sl]
            parts.append(oh * lax.rsqrt(jnp.mean(oh * oh, axis=-1, keepdims=True) + EPS)
                         * gnw_ref[...] * _silu(z[:, sl]))
        parts.append(hs_ref[0, rows] * _gelu_tanh(rg_ref[0, rows]))
        mixed = jnp.concatenate(parts, axis=1).astype(BF16)
        (y,) = yield [lambda: jnp.dot(mixed, wout_ref[...], preferred_element_type=F32)]
        h2 = h1_ref[0, rows] + mod[5:6] * y
        ub = _rms_mod(h2, ng_ref[2:3], mod[6:7], mod[7:8]).astype(BF16)
        a, b = yield [lambda: jnp.dot(ub, w1_ref[...], preferred_element_type=F32),
                      lambda: jnp.dot(ub, w3_ref[...], preferred_element_type=F32)]
        g = (_silu(a) * b).astype(BF16)
        (f,) = yield [lambda: jnp.dot(g, w2_ref[...], preferred_element_type=F32)]
        h3 = h2 + FFN_RESIDUAL * mod[8:9] * f
        out_ref[0, rows] = h3 * lax.rsqrt(jnp.mean(h3 * h3, axis=-1, keepdims=True) + EPS) * fg_ref[...]

    _run_interleaved([sub_tile(slice(i * TOKEN_TILE, (i + 1) * TOKEN_TILE))
                      for i in range(out_ref.shape[1] // TOKEN_TILE)])


def _out_ffn2_call(h1, o_f, o_b, z, h_sum, rg, mods, norm_g, gdn_norm_w, w_out, w1, w3, w2, final_g):
    bsz, seq, _ = o_f.shape
    rows = OUT_SUB_TILES * TOKEN_TILE
    tok = lambda w: pl.BlockSpec((1, rows, w), lambda b, j: (b, j, 0))
    return pl.pallas_call(
        _out_ffn2_kernel,
        grid=(bsz, seq // rows),
        in_specs=[tok(D_MODEL), tok(GDN_WIDTH), tok(GDN_WIDTH), tok(GDN_WIDTH),
                  tok(LRU_WIDTH), tok(LRU_WIDTH),
                  pl.BlockSpec((1, N_MOD, D_MODEL), lambda b, j: (b + 1, 0, 0)),
                  _resident((3, D_MODEL)), _resident((1, GDN_DK)),
                  _resident((D_MODEL, D_MODEL)),
                  _resident((None, D_MODEL, D_FF), (1, 0, 0)), _resident((None, D_MODEL, D_FF), (1, 0, 0)),
                  _resident((None, D_FF, D_MODEL), (1, 0, 0)),
                  _resident((1, D_MODEL))],
        out_specs=tok(D_MODEL),
        out_shape=jax.ShapeDtypeStruct((bsz, seq, D_MODEL), F32),
        compiler_params=pltpu.CompilerParams(dimension_semantics=("arbitrary", "arbitrary"),
                                             vmem_limit_bytes=VMEM_LIMIT),
        name="out_ffn2",
    )(h1, o_f, o_b, z, h_sum, rg, mods, norm_g, gdn_norm_w, w_out, w1, w3, w2, final_g)


def kernel(x, c, ctx, c_ctx, w_ada, b_ada, norm_g, ffn_w1, ffn_w3, ffn_w2, w_in, w_out, gdn_conv_w, gdn_a_log,
           gdn_dt_bias, gdn_norm_w, lru_conv_w, lru_conv_b, lru_w_gate, lru_b_gate, lru_lambda, final_norm_g):
    bsz, seq, _ = x.shape
    assert w_ada.shape[0] == 1 and seq == GRID_W * GRID_W and ctx.shape[1] == TOKEN_TILE

    rows = -(-(bsz + 1) // 8) * 8
    cvec = jnp.concatenate([c_ctx[None, :], c, jnp.zeros((rows - bsz - 1, D_MODEL), F32)], axis=0)
    mods = _ada_call(cvec, w_ada[0], b_ada).reshape(rows, N_MOD, D_MODEL)

    w_in_bf = w_in[0].astype(BF16)
    o_ba = P_HEAD_COLS
    o_rx = o_ba + 4 * GDN_HEADS
    w_tail = jnp.concatenate([w_in_bf[:, o_rx:], w_in_bf[:, o_ba:o_rx],
                              jnp.zeros((D_MODEL, BA_PAD - 4 * GDN_HEADS), BF16)], axis=1)
    w1 = ffn_w1[0].astype(BF16)
    w3 = ffn_w3[0].astype(BF16)
    w2 = ffn_w2[0].astype(BF16)

    h1, qkv, z, rx, rg, ba = _ffn1_in_call(x, ctx, mods, norm_g[0], w1, w3, w2, w_in_bf, w_tail)

    gate_params = jnp.pad(jnp.stack([gdn_a_log[0].reshape(-1), gdn_dt_bias[0].reshape(-1)]),
                          ((0, 0), (2 * GDN_HEADS, BA_PAD - 4 * GDN_HEADS)))
    local = _gdn_local_call(qkv, ba, gdn_conv_w[0], gate_params, seq)

    nt = LRU_WIDTH // LRU_LANES
    per_tile = LRU_LANES // LRU_BW
    w_tiles = jnp.einsum('gticd,ij->gticjd', lru_w_gate[0].reshape(4, nt, per_tile, LRU_BW, LRU_BW),
                         0.5 * jnp.eye(per_tile, dtype=F32)).reshape(4, nt, LRU_LANES, LRU_LANES).astype(BF16)
    b_gate = lru_b_gate[0].reshape(4, LRU_WIDTH)
    lam = lru_lambda[0]
    cw, cb = lru_conv_w[0], lru_conv_b
    h_ctx = _lru_ctx_call(rx, seq, cw, cb, w_tiles, b_gate, lam)
    o_f, o_b, h_sum = _mixers_call(local, rx, seq, h_ctx, cw, cb, w_tiles, b_gate, lam)

    return _out_ffn2_call(h1, o_f, o_b, z, h_sum, rg, mods, norm_g[0], gdn_norm_w,
                          w_out[0].astype(BF16), w1, w3, w2, final_norm_g[None, :])
```

```python
import functools

import jax
import jax.numpy as jnp
from jax import lax
from jax.experimental import pallas as pl
from jax.experimental.pallas import tpu as pltpu

D_MODEL = 1024
D_FF = 2816
N_MOD = 9
EPS = 1e-6
FFN_RESIDUAL = 0.5

GDN_WIDTH = 512
GDN_HEADS = 4
GDN_DK = 128
GDN_CHUNK = 64
CONV_W = 4
LRU_WIDTH = 512
LRU_BW = 64
LRU_C = 8.0
GRID_W = 64

ADA_MODS_PER_STEP = 3
TOKEN_TILE = 256
GDN_TILE_CHUNKS = TOKEN_TILE // GDN_CHUNK
GDN_SCAN_CHUNKS = 4
OUT_SUB_TILES = 2
FFN1_SUB_TILES = 2
HALO = 8
BA_PAD = 128
P_HEAD_COLS = 4 * GDN_WIDTH
P_TAIL_COLS = 2 * LRU_WIDTH + BA_PAD
LRU_LANES = 128
LRU_ROWS_PER_PASS = 8
VMEM_LIMIT = 56 * 1024 * 1024

BF16 = jnp.bfloat16
F32 = jnp.float32
NEG_BIG = -1e30
NT_DIMS = (((1,), (1,)), ((), ()))


def _mm(a, b):
    return jnp.dot(a.astype(BF16), b.astype(BF16), preferred_element_type=F32)


def _sigmoid(x):
    return 1.0 / (1.0 + jnp.exp(-x))


def _lru_a_b(half_r, half_i, bias_r, bias_i, lam, half_x):
    t_r = jnp.tanh(half_r + 0.5 * bias_r)
    t_i = jnp.tanh(half_i + 0.5 * bias_i)
    k = (-0.5 * LRU_C) * _softplus(-lam)
    a = jnp.exp(k * t_r + k)
    return a, jnp.sqrt(1.0 - a * a), t_i * half_x + half_x


def _silu(x):
    return x * _sigmoid(x)


def _softplus(x):
    return jnp.maximum(x, 0.0) + jnp.log(1.0 + jnp.exp(-jnp.abs(x)))


def _gelu_tanh(x):
    return 0.5 * x * (1.0 + jnp.tanh(0.7978845608028654 * (x + 0.044715 * x * x * x)))


def _rms_mod(h, g, shift, scale):
    y = h * lax.rsqrt(jnp.mean(h * h, axis=-1, keepdims=True) + EPS) * g
    return y * (1.0 + scale) + shift


def _swiglu(u, w1_ref, w3_ref, w2_ref):
    ub = u.astype(BF16)
    a = jnp.dot(ub, w1_ref[...], preferred_element_type=F32)
    b = jnp.dot(ub, w3_ref[...], preferred_element_type=F32)
    g = (_silu(a) * b).astype(BF16)
    return jnp.dot(g, w2_ref[...], preferred_element_type=F32)


def _dwconv(x, prev, nxt, w):
    t = x.shape[0]
    row = lax.broadcasted_iota(jnp.int32, x.shape, 0)
    xm1 = jnp.where(row == 0, prev[HALO - 1:HALO], pltpu.roll(x, 1, 0))
    xm2 = jnp.where(row == 0, prev[HALO - 2:HALO - 1],
                    jnp.where(row == 1, prev[HALO - 1:HALO], pltpu.roll(x, 2, 0)))
    xp1 = jnp.where(row == t - 1, nxt[0:1], pltpu.roll(x, t - 1, 0))
    return xm2 * w[0:1] + xm1 * w[1:2] + x * w[2:3] + xp1 * w[3:4]


def _resident(shape, index=None):
    index = index or (0,) * len(shape)
    return pl.BlockSpec(shape, lambda *_: index, pipeline_mode=pl.Buffered(1))


def _ada_kernel(c_ref, w_ref, b_ref, o_ref):
    o_ref[...] = _mm(_silu(c_ref[...]), w_ref[...]) + b_ref[...]


def _ada_call(cvec, w_ada, b_ada):
    rows = cvec.shape[0]
    cols = ADA_MODS_PER_STEP * D_MODEL
    return pl.pallas_call(
        _ada_kernel,
        grid=(N_MOD // ADA_MODS_PER_STEP,),
        in_specs=[pl.BlockSpec((rows, D_MODEL), lambda k: (0, 0)),
                  pl.BlockSpec((D_MODEL, cols), lambda k: (0, k)),
                  pl.BlockSpec((1, cols), lambda k: (0, k))],
        out_specs=pl.BlockSpec((rows, cols), lambda k: (0, k)),
        out_shape=jax.ShapeDtypeStruct((rows, N_MOD * D_MODEL), F32),
        compiler_params=pltpu.CompilerParams(dimension_semantics=("arbitrary",),
                                             vmem_limit_bytes=VMEM_LIMIT),
        name="ada",
    )(cvec, w_ada, b_ada)


def _segment_edges(j, n_lat_tiles, n_all_tiles):
    return (jnp.logical_or(j == 0, j == n_lat_tiles),
            jnp.logical_or(j == n_lat_tiles - 1, j == n_all_tiles - 1))


def _ffn1_in_kernel(n_lat_tiles, n_all_tiles, *refs):
    n_sub = FFN1_SUB_TILES
    ins, refs = refs[:3 * n_sub], refs[3 * n_sub:]
    ng_ref, w1_ref, w3_ref, w2_ref, win_ref, wtail_ref, h1_ref, qkv_ref, z_ref, rx_ref, rg_ref, ba_ref = refs
    s = pl.program_id(0)

    def sub_tile(i):
        x_ref, ctx_ref, mod_ref = ins[3 * i:3 * i + 3]
        rows = slice(i * TOKEN_TILE, (i + 1) * TOKEN_TILE)
        h = jnp.where((s * n_sub + i) % n_all_tiles == n_lat_tiles, ctx_ref[0], x_ref[0])
        mod = mod_ref[0]
        ub = _rms_mod(h, ng_ref[0:1], mod[0:1], mod[1:2]).astype(BF16)
        a, b = yield [lambda: jnp.dot(ub, w1_ref[...], preferred_element_type=F32),
                      lambda: jnp.dot(ub, w3_ref[...], preferred_element_type=F32)]
        g = (_silu(a) * b).astype(BF16)
        (f,) = yield [lambda: jnp.dot(g, w2_ref[...], preferred_element_type=F32)]
        h1 = h + FFN_RESIDUAL * mod[2:3] * f
        h1_ref[rows] = h1
        ub2 = _rms_mod(h1, ng_ref[1:2], mod[3:4], mod[4:5]).astype(BF16)
        head, tail = yield [lambda: jnp.dot(ub2, win_ref[...], preferred_element_type=F32),
                            lambda: jnp.dot(ub2, wtail_ref[...], preferred_element_type=F32)]
        qkv_ref[rows] = head[:, :3 * GDN_WIDTH]
        z_ref[rows] = head[:, 3 * GDN_WIDTH:]
        rx_ref[rows] = tail[:, :LRU_WIDTH]
        rg_ref[rows] = tail[:, LRU_WIDTH:2 * LRU_WIDTH]
        ba_ref[rows] = tail[:, 2 * LRU_WIDTH:]

    _run_interleaved([sub_tile(i) for i in range(n_sub)])


def _ffn1_in_call(x, ctx, mods, norm_g, w1, w3, w2, w_in_bf, w_tail):
    bsz, seq, _ = x.shape
    n_lat = seq // TOKEN_TILE
    n_all = n_lat + 1
    t_all = n_all * TOKEN_TILE
    n_sub = FFN1_SUB_TILES
    assert (bsz * n_all) % n_sub == 0
    per_tile = []
    for i in range(n_sub):
        tile = lambda s, i=i: s * n_sub + i
        per_tile += [
            pl.BlockSpec((1, TOKEN_TILE, D_MODEL),
                         lambda s, t=tile: (t(s) // n_all, jnp.minimum(t(s) % n_all, n_lat - 1), 0)),
            pl.BlockSpec((1, TOKEN_TILE, D_MODEL), lambda s, t=tile: (t(s) // n_all, 0, 0)),
            pl.BlockSpec((1, N_MOD, D_MODEL),
                         lambda s, t=tile: (jnp.where(t(s) % n_all == n_lat, 0, t(s) // n_all + 1), 0, 0))]
    widths = (D_MODEL, 3 * GDN_WIDTH, GDN_WIDTH, LRU_WIDTH, LRU_WIDTH, BA_PAD)
    outs = pl.pallas_call(
        functools.partial(_ffn1_in_kernel, n_lat, n_all),
        grid=(bsz * n_all // n_sub,),
        in_specs=per_tile + [
            _resident((3, D_MODEL)),
            _resident((None, D_MODEL, D_FF), (0, 0, 0)), _resident((None, D_MODEL, D_FF), (0, 0, 0)),
            _resident((None, D_FF, D_MODEL), (0, 0, 0)),
            _resident((D_MODEL, P_HEAD_COLS)), _resident((D_MODEL, P_TAIL_COLS))],
        out_specs=[pl.BlockSpec((n_sub * TOKEN_TILE, w), lambda s: (s, 0)) for w in widths],
        out_shape=[jax.ShapeDtypeStruct((bsz * t_all, w), F32) for w in widths],
        compiler_params=pltpu.CompilerParams(dimension_semantics=("arbitrary",),
                                             vmem_limit_bytes=VMEM_LIMIT),
        name="ffn1_in",
    )(*([x, ctx, mods] * n_sub), norm_g, w1, w3, w2, w_in_bf, w_tail)
    return [o.reshape(bsz, t_all, o.shape[-1]) for o in outs]


def _pair_block_diag(x, first):
    return jnp.concatenate([jnp.where(first, x, 0.0), jnp.where(first, 0.0, x)], axis=0)


def _run_interleaved(coroutines):
    active = [[gen, next(gen)] for gen in coroutines]
    while active:
        still = []
        for item in active:
            results = [dot() for dot in item[1]]
            try:
                item[1] = item[0].send(results)
                still.append(item)
            except StopIteration:
                pass
        active = still


def _gdn_local_body(seg_start, seg_end, x_ref, prev_ref, next_ref, ba_ref, cw_ref, gp_ref,
                    out_refs, stage_in, stage_out):
    c = GDN_CHUNK
    pairs = GDN_HEADS // 2
    row = lax.broadcasted_iota(jnp.int32, (c, 2 * c), 0)
    lane2 = lax.broadcasted_iota(jnp.int32, (c, 2 * c), 1)
    col = lane2 & (c - 1)
    first = lane2 < c
    first_row = first[0:1]
    eye2 = (row == col).astype(F32)
    zeros_k = jnp.zeros((c, GDN_DK), BF16)
    unit_index = lambda ci, d, p: (ci * 2 + d) * pairs + p

    def coupling(k):
        return jnp.logical_and((row >> k) != (col >> k), (row >> (k + 1)) == (col >> (k + 1)))

    lhs_in, kr_in, rhs_in, dec_in, kdt_in, qe_in, eg_in = stage_in

    def pair_unit(ci, d, p):
        u_ref, wq_ref, ak_ref, eg_ref = out_refs[d]
        un = unit_index(ci, d, p)
        strict = (row < col) if d else (row > col)
        k_rows = kr_in[ci * pairs + p]
        k_bd = jnp.concatenate([jnp.concatenate([k_rows[:c], zeros_k], axis=1),
                                jnp.concatenate([zeros_k, k_rows[c:]], axis=1)], axis=0)
        lhs = lhs_in[un]
        (prods,) = yield [lambda: lax.dot_general(lhs, k_bd, NT_DIMS, preferred_element_type=F32)]
        decay = dec_in[un]
        a = jnp.where(strict, prods[:c] * decay, 0.0)
        ps = slice(p * 2 * c, (p + 1) * 2 * c)
        ak_ref[ci, 0, 0:c, ps] = (prods[c:] * decay).astype(BF16)
        ak_ref[ci, 0, c:3 * c, ps] = kdt_in[un]
        t = eye2 - jnp.where(coupling(0), a, 0.0)
        k = 1
        while 2 ** k < c:
            l = jnp.where(coupling(k), a, 0.0)
            t_bd = _pair_block_diag(t, first)
            (lt,) = yield [lambda: _mm(l, t_bd)]
            lt_bd = _pair_block_diag(lt, first)
            (tlt,) = yield [lambda: _mm(t, lt_bd)]
            t = t - tlt
            k += 1
        t_bd = _pair_block_diag(t, first).astype(BF16)
        rhs = rhs_in[un]
        (sol,) = yield [lambda: jnp.dot(t_bd, rhs, preferred_element_type=F32)]
        for i in range(2):
            h = 2 * p + i
            hs = slice(h * GDN_DK, (h + 1) * GDN_DK)
            u_ref[ci, 0, :, hs] = sol[i * c:(i + 1) * c, :GDN_DK]
            wq_ref[ci, 0, 0:c, hs] = sol[i * c:(i + 1) * c, GDN_DK:].astype(BF16)
            wq_ref[ci, 0, c:2 * c, hs] = qe_in[un, i * c:(i + 1) * c]
            eg_ref[ci, 0, h:h + 1, :] = eg_in[un, i:i + 1]

    lhs_out, kr_out, rhs_out, dec_out, kdt_out, qe_out, eg_out = stage_out
    half = _dwconv(x_ref[0], jnp.where(seg_start, 0.0, prev_ref[0]), jnp.where(seg_end, 0.0, next_ref[0]),
                   0.5 * cw_ref[...])
    y = half * jnp.tanh(half) + half
    ba = ba_ref[0]
    lane = lax.broadcasted_iota(jnp.int32, (1, BA_PAD), 1)
    is_g = jnp.logical_and(lane >= 2 * GDN_HEADS, lane < 4 * GDN_HEADS)
    beta_all = _sigmoid(ba)
    g_all = jnp.where(is_g, -jnp.exp(gp_ref[0:1]) * _softplus(ba + gp_ref[1:2]), 0.0)

    def chunk(ci):
        rs = slice(ci * c, (ci + 1) * c)
        heads = []
        for h in range(GDN_HEADS):
            lo = h * GDN_DK
            q = y[rs, lo:lo + GDN_DK]
            k = y[rs, GDN_WIDTH + lo:GDN_WIDTH + lo + GDN_DK]
            v = y[rs, 2 * GDN_WIDTH + lo:2 * GDN_WIDTH + lo + GDN_DK]
            q = q * (lax.rsqrt(jnp.sum(q * q, axis=-1, keepdims=True) + EPS) * (GDN_DK ** -0.5))
            k = k * lax.rsqrt(jnp.sum(k * k, axis=-1, keepdims=True) + EPS)
            heads.append((q, k, v))
        g = g_all[rs]
        k_rows = [jnp.concatenate([heads[2 * p][1], heads[2 * p + 1][1]], axis=0) for p in range(pairs)]
        k_ts = [kr.T for kr in k_rows]
        grow = lax.broadcasted_iota(jnp.int32, g.shape, 0)
        gcum_f = g
        step = 1
        while step < c:
            gcum_f = gcum_f + jnp.where(grow >= step, pltpu.roll(gcum_f, step, 0), 0.0)
            step *= 2
        gcum_b = gcum_f[c - 1:c] - gcum_f + g
        gcum_t = jnp.concatenate([gcum_f, gcum_b], axis=0).T
        beta_c = beta_all[rs]
        for p in range(pairs):
            kr_out[ci * pairs + p] = k_rows[p].astype(BF16)
        for d in range(2):
            gcum = gcum_b if d else gcum_f
            incl = (row <= col) if d else (row >= col)
            last = 0 if d else c - 1
            for p in range(pairs):
                un = unit_index(ci, d, p)
                (q0, k0, v0), (q1, k1, v1) = heads[2 * p], heads[2 * p + 1]
                cb = d * GDN_HEADS + 2 * p
                cg = 2 * GDN_HEADS + cb
                beta0, beta1 = beta_c[:, cb:cb + 1], beta_c[:, cb + 1:cb + 2]
                gc0, gc1 = gcum[:, cg:cg + 1], gcum[:, cg + 1:cg + 2]
                gt0, gt1 = gcum_t[cg:cg + 1], gcum_t[cg + 1:cg + 2]
                gr = (jnp.where(first_row, pltpu.roll(gt0, c, 1), gt1) if d
                      else jnp.where(first_row, gt0, pltpu.roll(gt1, c, 1)))
                gc = jnp.where(first, gc0, gc1)
                e0, e1 = jnp.exp(gc0), jnp.exp(gc1)
                kb0, kb1 = k0 * beta0, k1 * beta1
                dec_out[un] = jnp.exp(jnp.where(incl, gc - gr, NEG_BIG))
                lhs_out[un] = jnp.concatenate([jnp.concatenate([kb0, kb1], axis=1),
                                               jnp.concatenate([q0, q1], axis=1)], axis=0).astype(BF16)
                rhs_out[un] = jnp.concatenate([jnp.concatenate([v0 * beta0, kb0 * e0], axis=1),
                                               jnp.concatenate([v1 * beta1, kb1 * e1], axis=1)],
                                              axis=0).astype(BF16)
                kdt_out[un] = (k_ts[p] * jnp.exp(gc[last:last + 1] - gr)).astype(BF16)
                qe_out[un] = jnp.concatenate([q0 * e0, q1 * e1], axis=0).astype(BF16)
                eg_out[un] = jnp.concatenate(
                    [jnp.broadcast_to(jnp.exp(gc0[last:last + 1]), (1, BA_PAD)),
                     jnp.broadcast_to(jnp.exp(gc1[last:last + 1]), (1, BA_PAD)),
                     jnp.zeros((HALO - 2, BA_PAD), F32)], axis=0)

    for refs in out_refs:
        refs[3][...] = jnp.zeros_like(refs[3])
    _run_interleaved([pair_unit(ci, d, p) for ci in range(GDN_TILE_CHUNKS) for d in range(2)
                      for p in range(pairs)])
    for ci in range(GDN_TILE_CHUNKS):
        chunk(ci)


def _gdn_local_kernel(n_lat_tiles, n_all_tiles, n_tiles, x_ref, prev_ref, next_ref, ba_ref, cw_ref, gp_ref,
                      uf_ref, wqf_ref, akf_ref, egf_ref, ub_ref, wqb_ref, akb_ref, egb_ref, *stage_refs):
    s = pl.program_id(0)
    seg_start, seg_end = _segment_edges(jnp.minimum(s, n_tiles - 1) % n_all_tiles, n_lat_tiles, n_all_tiles)
    out_refs = ((uf_ref, wqf_ref, akf_ref, egf_ref), (ub_ref, wqb_ref, akb_ref, egb_ref))
    half = len(stage_refs) // 2
    sets = (stage_refs[:half], stage_refs[half:])

    @pl.when(s == 0)
    def _():
        for r in sets[1]:
            r[...] = jnp.zeros_like(r)

    for parity in range(2):
        @pl.when(s % 2 == parity)
        def _():
            _gdn_local_body(seg_start, seg_end, x_ref, prev_ref, next_ref, ba_ref, cw_ref, gp_ref,
                            out_refs, sets[1 - parity], sets[parity])


def _gdn_local_call(qkv, ba, conv_w, gate_params, n_lat_tok):
    bsz, t_all, width = qkv.shape
    c = GDN_CHUNK
    tile = GDN_TILE_CHUNKS * c
    n_all = t_all // tile
    n_lat = n_lat_tok // tile
    per_tile = tile // HALO
    last_halo = t_all // HALO - 1
    n_chunks = t_all // c
    n_tiles = bsz * n_all
    pairs = GDN_HEADS // 2
    units = GDN_TILE_CHUNKS * 2 * pairs
    t_in = lambda s: jnp.minimum(s, n_tiles - 1)
    t_out = lambda s: jnp.maximum(s - 1, 0)
    in_tok = lambda w: pl.BlockSpec((1, tile, w), lambda s: (t_in(s) // n_all, t_in(s) % n_all, 0))
    cblk = lambda rows, w: pl.BlockSpec((GDN_TILE_CHUNKS, 1, rows, w),
                                        lambda s: (t_out(s) % n_all, t_out(s) // n_all, 0, 0))
    vec = pl.BlockSpec((2, BA_PAD), lambda s: (0, 0))
    outs = [(jax.ShapeDtypeStruct((n_chunks, bsz, c, GDN_WIDTH), F32), cblk(c, GDN_WIDTH)),
            (jax.ShapeDtypeStruct((n_chunks, bsz, 2 * c, GDN_WIDTH), BF16), cblk(2 * c, GDN_WIDTH)),
            (jax.ShapeDtypeStruct((n_chunks, bsz, 3 * c, GDN_HEADS * c), BF16), cblk(3 * c, GDN_HEADS * c)),
            (jax.ShapeDtypeStruct((n_chunks, bsz, HALO, BA_PAD), F32), cblk(HALO, BA_PAD))] * 2
    stage = [pltpu.VMEM((units, 2 * c, 2 * GDN_DK), BF16),
             pltpu.VMEM((GDN_TILE_CHUNKS * pairs, 2 * c, GDN_DK), BF16),
             pltpu.VMEM((units, 2 * c, 2 * GDN_DK), BF16),
             pltpu.VMEM((units, c, 2 * c), F32),
             pltpu.VMEM((units, GDN_DK, 2 * c), BF16),
             pltpu.VMEM((units, 2 * c, GDN_DK), BF16),
             pltpu.VMEM((units, HALO, BA_PAD), F32)]
    return pl.pallas_call(
        functools.partial(_gdn_local_kernel, n_lat, n_all, n_tiles),
        grid=(n_tiles + 1,),
        in_specs=[in_tok(width),
                  pl.BlockSpec((1, HALO, width),
                               lambda s: (t_in(s) // n_all, jnp.maximum((t_in(s) % n_all) * per_tile - 1, 0), 0)),
                  pl.BlockSpec((1, HALO, width),
                               lambda s: (t_in(s) // n_all,
                                          jnp.minimum((t_in(s) % n_all + 1) * per_tile, last_halo), 0)),
                  in_tok(BA_PAD), pl.BlockSpec((CONV_W, width), lambda s: (0, 0)), vec],
        out_specs=[o[1] for o in outs],
        out_shape=[o[0] for o in outs],
        scratch_shapes=stage + stage,
        compiler_params=pltpu.CompilerParams(dimension_semantics=("arbitrary",),
                                             vmem_limit_bytes=VMEM_LIMIT),
        name="gdn_local",
    )(qkv, qkv, qkv, ba, conv_w, gate_params)


def _gdn_scan_steps(uf_ref, wqf_ref, akf_ref, egf_ref, ub_ref, wqb_ref, akb_ref, egb_ref,
                    of_ref, ob_ref, sf_ref, sb_ref):
    c = GDN_CHUNK
    bsz = uf_ref.shape[1]
    hs = lambda h: slice(h * GDN_DK, (h + 1) * GDN_DK)
    for sub in range(GDN_SCAN_CHUNKS):
        chains = [(refs, ci, b, h)
                  for refs, ci in (((uf_ref, wqf_ref, akf_ref, egf_ref, of_ref, sf_ref), sub),
                                   ((ub_ref, wqb_ref, akb_ref, egb_ref, ob_ref, sb_ref),
                                    GDN_SCAN_CHUNKS - 1 - sub))
                  for b in range(bsz) for h in range(GDN_HEADS)]
        states = [refs[5][b * GDN_HEADS + h] for refs, ci, b, h in chains]
        with_s = yield [lambda refs=refs, ci=ci, b=b, h=h, s=s:
                        jnp.dot(refs[1][ci, b, :, hs(h)], s.astype(BF16), preferred_element_type=F32)
                        for (refs, ci, b, h), s in zip(chains, states)]
        v_new = [refs[0][ci, b, :, hs(h)] - r[:c] for (refs, ci, b, h), r in zip(chains, with_s)]
        with_v = yield [lambda refs=refs, ci=ci, b=b, h=h, v=v:
                        jnp.dot(refs[2][ci, b, :, h * c:(h + 1) * c], v.astype(BF16), preferred_element_type=F32)
                        for (refs, ci, b, h), v in zip(chains, v_new)]
        for (refs, ci, b, h), s, r, a in zip(chains, states, with_s, with_v):
            refs[4][b, ci * c:(ci + 1) * c, hs(h)] = r[c:] + a[:c]
            refs[5][b * GDN_HEADS + h] = s * refs[3][ci, b, h:h + 1, :] + a[c:]


def _lin_scan(a, b, reverse):
    r = a.shape[0]
    row = lax.broadcasted_iota(jnp.int32, a.shape, 0)
    s = 1
    while s < r:
        ok = (row < r - s) if reverse else (row >= s)
        shift = r - s if reverse else s
        a_sh = jnp.where(ok, pltpu.roll(a, shift, 0), 1.0)
        b_sh = jnp.where(ok, pltpu.roll(b, shift, 0), 0.0)
        b = a * b_sh + b
        a = a * a_sh
        s *= 2
    return a, b


def _lru_ctx_direction(reverse, n_slabs, slab, x_ref, prev_ref, next_ref, cw, cb, w_ref, br, bi, lam, carry_ref):
    bsz, r, width = x_ref.shape
    step = pl.program_id(0)
    has_prev = slab > 0
    has_next = slab < n_slabs - 1
    xs = []
    for b in range(bsz):
        prev = jnp.where(has_prev, prev_ref[b], 0.0)
        nxt = jnp.where(has_next, next_ref[b], 0.0)
        xs.append(_dwconv(x_ref[b], prev, nxt, cw) + cb)
    x = jnp.concatenate(xs, axis=0)
    xb = x.astype(BF16)
    d = 1 if reverse else 0
    gate = lambda g: jnp.concatenate(
        [jnp.dot(xb[:, t * LRU_LANES:(t + 1) * LRU_LANES], w_ref[2 * d + g, t], preferred_element_type=F32)
         for t in range(width // LRU_LANES)], axis=1)
    a, mult, gated_x = _lru_a_b(gate(0), gate(1), br, bi, lam, 0.5 * x)
    row = lax.broadcasted_iota(jnp.int32, (r, width), 0)
    is_first = jnp.logical_and(step == 0, row == ((r - 1) if reverse else 0))
    edge = 0 if reverse else r - 1
    for b in range(bsz):
        sl = slice(b * r, (b + 1) * r)
        m = jnp.where(is_first, 1.0, mult[sl])
        a_cum, h = _lin_scan(a[sl], m * gated_x[sl], reverse)
        h = h + a_cum * carry_ref[b:b + 1]
        carry_ref[b:b + 1] = h[edge:edge + 1]


def _lru_ctx_kernel(n_slabs, xf_ref, pf_ref, nf_ref, xb_ref, pb_ref, nb_ref, cw_ref, cb_ref,
                    w_ref, bg_ref, lam_ref, fin_ref, cf_ref, cbk_ref):
    step = pl.program_id(0)

    @pl.when(step == 0)
    def _():
        cf_ref[...] = jnp.zeros_like(cf_ref)
        cbk_ref[...] = jnp.zeros_like(cbk_ref)

    cw = cw_ref[...]
    cb = cb_ref[...]
    _lru_ctx_direction(False, n_slabs, step, xf_ref, pf_ref, nf_ref, cw, cb, w_ref,
                       bg_ref[0:1], bg_ref[1:2], lam_ref[0:1], cf_ref)
    _lru_ctx_direction(True, n_slabs, n_slabs - 1 - step, xb_ref, pb_ref, nb_ref, cw, cb, w_ref,
                       bg_ref[2:3], bg_ref[3:4], lam_ref[1:2], cbk_ref)
    fin_ref[0] = cf_ref[...]
    fin_ref[1] = cbk_ref[...]


def _lru_ctx_call(rx, n_lat_tok, conv_w, conv_b, w_tiles, b_gate, lam):
    bsz, t_all, width = rx.shape
    r = GRID_W
    per = r // HALO
    n_slabs = (t_all - n_lat_tok) // r
    first = n_lat_tok // r
    main = lambda f: pl.BlockSpec((bsz, r, width), lambda s: (0, first + f(s), 0))
    prev = lambda f: pl.BlockSpec((bsz, HALO, width), lambda s: (0, (first + f(s)) * per - 1, 0))
    nxt = lambda f: pl.BlockSpec(
        (bsz, HALO, width), lambda s: (0, jnp.minimum((first + f(s) + 1) * per, t_all // HALO - 1), 0))
    fwd = lambda s: s
    bwd = lambda s: n_slabs - 1 - s
    full = lambda shape: pl.BlockSpec(shape, lambda s: (0,) * len(shape))
    carry = pltpu.VMEM((bsz, width), F32)
    return pl.pallas_call(
        functools.partial(_lru_ctx_kernel, n_slabs),
        grid=(n_slabs,),
        in_specs=[main(fwd), prev(fwd), nxt(fwd), main(bwd), prev(bwd), nxt(bwd),
                  full((CONV_W, width)), full((1, width)), full(w_tiles.shape),
                  full((4, width)), full((2, width))],
        out_specs=full((2, bsz, width)),
        out_shape=jax.ShapeDtypeStruct((2, bsz, width), F32),
        scratch_shapes=[carry, carry],
        compiler_params=pltpu.CompilerParams(dimension_semantics=("arbitrary",),
                                             vmem_limit_bytes=VMEM_LIMIT),
        name="lru_ctx",
    )(rx, rx, rx, rx, rx, rx, conv_w, conv_b, w_tiles, b_gate, lam)


def _lru_grid_gates(x_ref, cw_ref, cb_ref, w_ref, bg_ref, lam_ref, af_ref, bf_ref, ab_ref, bb_ref):
    n = GRID_W
    rp = LRU_ROWS_PER_PASS
    lanes = x_ref.shape[-1]
    crow = lax.broadcasted_iota(jnp.int32, (n, lanes), 0)
    cw = cw_ref[...]
    cb = cb_ref[...]

    def grid_rows(lo, hi):
        parts = []
        for r in range(lo, min(hi, 0)):
            parts.append(jnp.where(crow == 0, 0.0, pltpu.roll(x_ref[0, (n + r) * n:(n + r + 1) * n, :], 1, 0)))
        if max(lo, 0) < min(hi, n):
            parts.append(x_ref[0, max(lo, 0) * n:min(hi, n) * n, :])
        for r in range(max(lo, n), hi):
            parts.append(jnp.where(crow == n - 1, 0.0,
                                   pltpu.roll(x_ref[0, (r - n) * n:(r - n + 1) * n, :], n - 1, 0)))
        return parts[0] if len(parts) == 1 else jnp.concatenate(parts, axis=0)

    for r0 in range(0, n, rp):
        sl = slice(r0 * n, (r0 + rp) * n)
        xc = (grid_rows(r0 - 2, r0 + rp - 2) * cw[0:1] + grid_rows(r0 - 1, r0 + rp - 1) * cw[1:2]
              + grid_rows(r0, r0 + rp) * cw[2:3] + grid_rows(r0 + 1, r0 + rp + 1) * cw[3:4] + cb)
        xb = xc.astype(BF16)
        gates = yield [lambda i=i: jnp.dot(xb, w_ref[i, 0], preferred_element_type=F32) for i in range(4)]
        half_x = 0.5 * xc
        for d, (a_ref, b_ref) in enumerate(((af_ref, bf_ref), (ab_ref, bb_ref))):
            a, gain, gated_x = _lru_a_b(gates[2 * d], gates[2 * d + 1], bg_ref[2 * d:2 * d + 1],
                                        bg_ref[2 * d + 1:2 * d + 2], lam_ref[d:d + 1], half_x)
            a_ref[sl] = a
            b_ref[sl] = gain * gated_x


def _lru_grid_finish(h0_ref, out_ref, af_ref, bf_ref, ab_ref, bb_ref):
    n = GRID_W
    lanes = out_ref.shape[-1]
    crow = lax.broadcasted_iota(jnp.int32, (n, lanes), 0)
    hf = jnp.zeros((n, lanes), F32)
    hb = jnp.zeros((n, lanes), F32)
    pf = jnp.ones((n, lanes), F32)
    pb = jnp.ones((n, lanes), F32)
    for i in range(n):
        sf = slice(i * n, (i + 1) * n)
        sb = slice((n - 1 - i) * n, (n - i) * n)
        a = af_ref[sf]
        hf = a * hf + bf_ref[sf]
        pf = a * pf
        bf_ref[sf] = hf
        af_ref[sf] = pf
        a = ab_ref[sb]
        hb = a * hb + bb_ref[sb]
        pb = a * pb
        bb_ref[sb] = hb
        ab_ref[sb] = pb

    h0f = h0_ref[0, 0]
    h0b = h0_ref[1, 0]
    acc_a, acc_h = _lin_scan(pf, hf, False)
    in_f = jnp.where(crow == 0, h0f, pltpu.roll(acc_h + acc_a * h0f, 1, 0))
    acc_a, acc_h = _lin_scan(pb, hb, True)
    in_b = jnp.where(crow == n - 1, h0b, pltpu.roll(acc_h + acc_a * h0b, n - 1, 0))

    for r in range(n):
        sl = slice(r * n, (r + 1) * n)
        out_ref[0, sl, :] = (bf_ref[sl] + af_ref[sl] * in_f) + (bb_ref[sl] + ab_ref[sl] * in_b)


def _mixers_kernel(*refs):
    scan_in, lru_in, outs, scratch = refs[:8], refs[8:15], refs[15:18], refs[18:]
    of_ref, ob_ref, hs_ref = outs
    sf_ref, sb_ref = scratch[:2]

    @pl.when(pl.program_id(0) == 0)
    def _():
        sf_ref[...] = jnp.zeros_like(sf_ref)
        sb_ref[...] = jnp.zeros_like(sb_ref)

    x_ref, h0_ref = lru_in[:2]
    _run_interleaved([_gdn_scan_steps(*scan_in, of_ref, ob_ref, sf_ref, sb_ref),
                      _lru_grid_gates(x_ref, *lru_in[2:], *scratch[2:])])
    _lru_grid_finish(h0_ref, hs_ref, *scratch[2:])


def _mixers_call(local, rx, n_lat_tok, h0, conv_w, conv_b, w_tiles, b_gate, lam):
    n_chunks, bsz = local[0].shape[:2]
    width = rx.shape[2]
    c = GDN_CHUNK
    g = GDN_SCAN_CHUNKS
    n_all = n_chunks // g
    n_lat = n_lat_tok // (c * g)
    n_ctx = n_all - n_lat
    fwd = lambda s: jnp.where(s < n_ctx, n_lat + s, s - n_ctx)
    bwd = lambda s: n_all - 1 - s
    fwd_out = lambda s: jnp.maximum(s - n_ctx, 0)
    bwd_out = lambda s: jnp.minimum(n_all - 1 - s, n_lat - 1)
    cblk = lambda rows, w, f: pl.BlockSpec((g, bsz, rows, w), lambda s: (f(s), 0, 0, 0))
    ins = lambda f: [cblk(c, GDN_WIDTH, f), cblk(2 * c, GDN_WIDTH, f), cblk(3 * c, GDN_HEADS * c, f),
                     cblk(HALO, BA_PAD, f)]
    tok = lambda f: pl.BlockSpec((bsz, g * c, GDN_WIDTH), lambda s: (0, f(s), 0))
    state = pltpu.VMEM((bsz * GDN_HEADS, GDN_DK, GDN_DK), F32)

    nt = width // LRU_LANES
    unit = lambda s: jnp.minimum(s, bsz * nt - 1)
    assert n_all >= bsz * nt
    lane = lambda rows: pl.BlockSpec((rows, LRU_LANES), lambda s: (0, unit(s) % nt))
    grid_tok = pl.BlockSpec((1, n_lat_tok, LRU_LANES), lambda s: (unit(s) // nt, 0, unit(s) % nt))
    lru_scratch = pltpu.VMEM((n_lat_tok, LRU_LANES), F32)
    return pl.pallas_call(
        _mixers_kernel,
        grid=(n_all,),
        in_specs=ins(fwd) + ins(bwd) + [
            grid_tok,
            pl.BlockSpec((2, 1, 1, LRU_LANES), lambda s: (0, unit(s) // nt, 0, unit(s) % nt)),
            lane(CONV_W), lane(1),
            pl.BlockSpec((4, 1, LRU_LANES, LRU_LANES), lambda s: (0, unit(s) % nt, 0, 0)),
            lane(4), lane(2)],
        out_specs=[tok(fwd_out), tok(bwd_out), grid_tok],
        out_shape=[jax.ShapeDtypeStruct((bsz, n_lat_tok, GDN_WIDTH), F32)] * 2
                  + [jax.ShapeDtypeStruct((bsz, n_lat_tok, width), F32)],
        scratch_shapes=[state, state] + [lru_scratch] * 4,
        compiler_params=pltpu.CompilerParams(dimension_semantics=("arbitrary",),
                                             vmem_limit_bytes=VMEM_LIMIT),
        name="mixers",
    )(*local, rx, h0.reshape(2, bsz, 1, width), conv_w, conv_b, w_tiles, b_gate, lam)


def _out_ffn2_kernel(h1_ref, of_ref, ob_ref, z_ref, hs_ref, rg_ref, mod_ref, ng_ref, gnw_ref,
                     wout_ref, w1_ref, w3_ref, w2_ref, fg_ref, out_ref):
    mod = mod_ref[0]

    def sub_tile(rows):
        o = of_ref[0, rows] + ob_ref[0, rows]
        z = z_ref[0, rows]
        parts = []
        for h in range(GDN_HEADS):
            sl = slice(h * GDN_DK, (h + 1) * GDN_DK)
            oh = o[:, sl]
            parts.append(oh * lax.rsqrt(jnp.mean(oh * oh, axis=-1, keepdims=True) + EPS)
                         * gnw_ref[...] * _silu(z[:, sl]))
        parts.append(hs_ref[0, rows] * _gelu_tanh(rg_ref[0, rows]))
        mixed = jnp.concatenate(parts, axis=1).astype(BF16)
        (y,) = yield [lambda: jnp.dot(mixed, wout_ref[...], preferred_element_type=F32)]
        h2 = h1_ref[0, rows] + mod[5:6] * y
        ub = _rms_mod(h2, ng_ref[2:3], mod[6:7], mod[7:8]).astype(BF16)
        a, b = yield [lambda: jnp.dot(ub, w1_ref[...], preferred_element_type=F32),
                      lambda: jnp.dot(ub, w3_ref[...], preferred_element_type=F32)]
        g = (_silu(a) * b).astype(BF16)
        (f,) = yield [lambda: jnp.dot(g, w2_ref[...], preferred_element_type=F32)]
        h3 = h2 + FFN_RESIDUAL * mod[8:9] * f
        out_ref[0, rows] = h3 * lax.rsqrt(jnp.mean(h3 * h3, axis=-1, keepdims=True) + EPS) * fg_ref[...]

    _run_interleaved([sub_tile(slice(i * TOKEN_TILE, (i + 1) * TOKEN_TILE))
                      for i in range(out_ref.shape[1] // TOKEN_TILE)])


def _out_ffn2_call(h1, o_f, o_b, z, h_sum, rg, mods, norm_g, gdn_norm_w, w_out, w1, w3, w2, final_g):
    bsz, seq, _ = o_f.shape
    rows = OUT_SUB_TILES * TOKEN_TILE
    tok = lambda w: pl.BlockSpec((1, rows, w), lambda b, j: (b, j, 0))
    return pl.pallas_call(
        _out_ffn2_kernel,
        grid=(bsz, seq // rows),
        in_specs=[tok(D_MODEL), tok(GDN_WIDTH), tok(GDN_WIDTH), tok(GDN_WIDTH),
                  tok(LRU_WIDTH), tok(LRU_WIDTH),
                  pl.BlockSpec((1, N_MOD, D_MODEL), lambda b, j: (b + 1, 0, 0)),
                  _resident((3, D_MODEL)), _resident((1, GDN_DK)),
                  _resident((D_MODEL, D_MODEL)),
                  _resident((None, D_MODEL, D_FF), (1, 0, 0)), _resident((None, D_MODEL, D_FF), (1, 0, 0)),
                  _resident((None, D_FF, D_MODEL), (1, 0, 0)),
                  _resident((1, D_MODEL))],
        out_specs=tok(D_MODEL),
        out_shape=jax.ShapeDtypeStruct((bsz, seq, D_MODEL), F32),
        compiler_params=pltpu.CompilerParams(dimension_semantics=("arbitrary", "arbitrary"),
                                             vmem_limit_bytes=VMEM_LIMIT),
        name="out_ffn2",
    )(h1, o_f, o_b, z, h_sum, rg, mods, norm_g, gdn_norm_w, w_out, w1, w3, w2, final_g)


def kernel(x, c, ctx, c_ctx, w_ada, b_ada, norm_g, ffn_w1, ffn_w3, ffn_w2, w_in, w_out, gdn_conv_w, gdn_a_log,
           gdn_dt_bias, gdn_norm_w, lru_conv_w, lru_conv_b, lru_w_gate, lru_b_gate, lru_lambda, final_norm_g):
    bsz, seq, _ = x.shape
    assert w_ada.shape[0] == 1 and seq == GRID_W * GRID_W and ctx.shape[1] == TOKEN_TILE

    rows = -(-(bsz + 1) // 8) * 8
    cvec = jnp.concatenate([c_ctx[None, :], c, jnp.zeros((rows - bsz - 1, D_MODEL), F32)], axis=0)
    mods = _ada_call(cvec, w_ada[0], b_ada).reshape(rows, N_MOD, D_MODEL)

    w_in_bf = w_in[0].astype(BF16)
    o_ba = P_HEAD_COLS
    o_rx = o_ba + 4 * GDN_HEADS
    w_tail = jnp.concatenate([w_in_bf[:, o_rx:], w_in_bf[:, o_ba:o_rx],
                              jnp.zeros((D_MODEL, BA_PAD - 4 * GDN_HEADS), BF16)], axis=1)
    w1 = ffn_w1[0].astype(BF16)
    w3 = ffn_w3[0].astype(BF16)
    w2 = ffn_w2[0].astype(BF16)

    h1, qkv, z, rx, rg, ba = _ffn1_in_call(x, ctx, mods, norm_g[0], w1, w3, w2, w_in_bf, w_tail)

    gate_params = jnp.pad(jnp.stack([gdn_a_log[0].reshape(-1), gdn_dt_bias[0].reshape(-1)]),
                          ((0, 0), (2 * GDN_HEADS, BA_PAD - 4 * GDN_HEADS)))
    local = _gdn_local_call(qkv, ba, gdn_conv_w[0], gate_params, seq)

    nt = LRU_WIDTH // LRU_LANES
    per_tile = LRU_LANES // LRU_BW
    w_tiles = jnp.einsum('gticd,ij->gticjd', lru_w_gate[0].reshape(4, nt, per_tile, LRU_BW, LRU_BW),
                         0.5 * jnp.eye(per_tile, dtype=F32)).reshape(4, nt, LRU_LANES, LRU_LANES).astype(BF16)
    b_gate = lru_b_gate[0].reshape(4, LRU_WIDTH)
    lam = lru_lambda[0]
    cw, cb = lru_conv_w[0], lru_conv_b
    h_ctx = _lru_ctx_call(rx, seq, cw, cb, w_tiles, b_gate, lam)
    o_f, o_b, h_sum = _mixers_call(local, rx, seq, h_ctx, cw, cb, w_tiles, b_gate, lam)

    return _out_ffn2_call(h1, o_f, o_b, z, h_sum, rg, mods, norm_g[0], gdn_norm_w,
                          w_out[0].astype(BF16), w1, w3, w2, final_norm_g[None, :])
```

```python
import functools

import jax
import jax.numpy as jnp
from jax import lax
from jax.experimental import pallas as pl
from jax.experimental.pallas import tpu as pltpu

D_MODEL = 1024
D_FF = 2816
N_MOD = 9
EPS = 1e-6
FFN_RESIDUAL = 0.5

GDN_WIDTH = 512
GDN_HEADS = 4
GDN_DK = 128
GDN_CHUNK = 64
CONV_W = 4
LRU_WIDTH = 512
LRU_BW = 64
LRU_C = 8.0
GRID_W = 64

ADA_MODS_PER_STEP = 3
TOKEN_TILE = 256
GDN_TILE_CHUNKS = TOKEN_TILE // GDN_CHUNK
GDN_SCAN_CHUNKS = 4
OUT_SUB_TILES = 2
FFN1_SUB_TILES = 2
HALO = 8
BA_PAD = 128
P_HEAD_COLS = 4 * GDN_WIDTH
P_TAIL_COLS = 2 * LRU_WIDTH + BA_PAD
LRU_LANES = 128
LRU_ROWS_PER_PASS = 8
VMEM_LIMIT = 56 * 1024 * 1024

BF16 = jnp.bfloat16
F32 = jnp.float32
NEG_BIG = -1e30
SQRT_GUARD = 1e-30
LOG2_E = 1.4426950408889634
NT_DIMS = (((1,), (1,)), ((), ()))


def _mm(a, b):
    return jnp.dot(a.astype(BF16), b.astype(BF16), preferred_element_type=F32)


def _sigmoid(x):
    return 1.0 / (1.0 + jnp.exp(-x))


def _lru_a_b(half_r, half_i, bias_r, bias_i, lam, half_x):
    t_r = jnp.tanh(half_r + 0.5 * bias_r)
    t_i = jnp.tanh(half_i + 0.5 * bias_i)
    k = (-0.5 * LRU_C * LOG2_E) * _softplus(-lam)
    a = jnp.exp2(k * t_r + k)
    y = 1.0 - a * a
    return a, y * lax.rsqrt(y + SQRT_GUARD), t_i * half_x + half_x


def _silu(x):
    return x * _sigmoid(x)


def _softplus(x):
    return jnp.maximum(x, 0.0) + jnp.log(1.0 + jnp.exp(-jnp.abs(x)))


def _gelu_tanh(x):
    return 0.5 * x * (1.0 + jnp.tanh(0.7978845608028654 * (x + 0.044715 * x * x * x)))


def _rms_mod(h, g, shift, scale):
    y = h * lax.rsqrt(jnp.mean(h * h, axis=-1, keepdims=True) + EPS) * g
    return y * (1.0 + scale) + shift


def _swiglu(u, w1_ref, w3_ref, w2_ref):
    ub = u.astype(BF16)
    a = jnp.dot(ub, w1_ref[...], preferred_element_type=F32)
    b = jnp.dot(ub, w3_ref[...], preferred_element_type=F32)
    g = (_silu(a) * b).astype(BF16)
    return jnp.dot(g, w2_ref[...], preferred_element_type=F32)


def _dwconv(x, prev, nxt, w):
    t = x.shape[0]
    row = lax.broadcasted_iota(jnp.int32, x.shape, 0)
    xm1 = jnp.where(row == 0, prev[HALO - 1:HALO], pltpu.roll(x, 1, 0))
    xm2 = jnp.where(row == 0, prev[HALO - 2:HALO - 1],
                    jnp.where(row == 1, prev[HALO - 1:HALO], pltpu.roll(x, 2, 0)))
    xp1 = jnp.where(row == t - 1, nxt[0:1], pltpu.roll(x, t - 1, 0))
    return xm2 * w[0:1] + xm1 * w[1:2] + x * w[2:3] + xp1 * w[3:4]


def _resident(shape, index=None):
    index = index or (0,) * len(shape)
    return pl.BlockSpec(shape, lambda *_: index, pipeline_mode=pl.Buffered(1))


def _ada_kernel(c_ref, w_ref, b_ref, o_ref):
    o_ref[...] = _mm(_silu(c_ref[...]), w_ref[...]) + b_ref[...]


def _ada_call(cvec, w_ada, b_ada):
    rows = cvec.shape[0]
    cols = ADA_MODS_PER_STEP * D_MODEL
    return pl.pallas_call(
        _ada_kernel,
        grid=(N_MOD // ADA_MODS_PER_STEP,),
        in_specs=[pl.BlockSpec((rows, D_MODEL), lambda k: (0, 0)),
                  pl.BlockSpec((D_MODEL, cols), lambda k: (0, k)),
                  pl.BlockSpec((1, cols), lambda k: (0, k))],
        out_specs=pl.BlockSpec((rows, cols), lambda k: (0, k)),
        out_shape=jax.ShapeDtypeStruct((rows, N_MOD * D_MODEL), F32),
        compiler_params=pltpu.CompilerParams(dimension_semantics=("arbitrary",),
                                             vmem_limit_bytes=VMEM_LIMIT),
        name="ada",
    )(cvec, w_ada, b_ada)


def _segment_edges(j, n_lat_tiles, n_all_tiles):
    return (jnp.logical_or(j == 0, j == n_lat_tiles),
            jnp.logical_or(j == n_lat_tiles - 1, j == n_all_tiles - 1))


def _ffn1_in_kernel(n_lat_tiles, n_all_tiles, *refs):
    n_sub = FFN1_SUB_TILES
    ins, refs = refs[:3 * n_sub], refs[3 * n_sub:]
    ng_ref, w1_ref, w3_ref, w2_ref, win_ref, wtail_ref, h1_ref, qkv_ref, z_ref, rx_ref, rg_ref, ba_ref = refs
    s = pl.program_id(0)

    def sub_tile(i):
        x_ref, ctx_ref, mod_ref = ins[3 * i:3 * i + 3]
        rows = slice(i * TOKEN_TILE, (i + 1) * TOKEN_TILE)
        h = jnp.where((s * n_sub + i) % n_all_tiles == n_lat_tiles, ctx_ref[0], x_ref[0])
        mod = mod_ref[0]
        ub = _rms_mod(h, ng_ref[0:1], mod[0:1], mod[1:2]).astype(BF16)
        a, b = yield [lambda: jnp.dot(ub, w1_ref[...], preferred_element_type=F32),
                      lambda: jnp.dot(ub, w3_ref[...], preferred_element_type=F32)]
        g = (_silu(a) * b).astype(BF16)
        (f,) = yield [lambda: jnp.dot(g, w2_ref[...], preferred_element_type=F32)]
        h1 = h + FFN_RESIDUAL * mod[2:3] * f
        h1_ref[rows] = h1
        ub2 = _rms_mod(h1, ng_ref[1:2], mod[3:4], mod[4:5]).astype(BF16)
        head, tail = yield [lambda: jnp.dot(ub2, win_ref[...], preferred_element_type=F32),
                            lambda: jnp.dot(ub2, wtail_ref[...], preferred_element_type=F32)]
        qkv_ref[rows] = head[:, :3 * GDN_WIDTH]
        z_ref[rows] = head[:, 3 * GDN_WIDTH:]
        rx_ref[rows] = tail[:, :LRU_WIDTH]
        rg_ref[rows] = tail[:, LRU_WIDTH:2 * LRU_WIDTH]
        ba_ref[rows] = tail[:, 2 * LRU_WIDTH:]

    _run_interleaved([sub_tile(i) for i in range(n_sub)])


def _ffn1_in_call(x, ctx, mods, norm_g, w1, w3, w2, w_in_bf, w_tail):
    bsz, seq, _ = x.shape
    n_lat = seq // TOKEN_TILE
    n_all = n_lat + 1
    t_all = n_all * TOKEN_TILE
    n_sub = FFN1_SUB_TILES
    assert (bsz * n_all) % n_sub == 0
    per_tile = []
    for i in range(n_sub):
        tile = lambda s, i=i: s * n_sub + i
        per_tile += [
            pl.BlockSpec((1, TOKEN_TILE, D_MODEL),
                         lambda s, t=tile: (t(s) // n_all, jnp.minimum(t(s) % n_all, n_lat - 1), 0)),
            pl.BlockSpec((1, TOKEN_TILE, D_MODEL), lambda s, t=tile: (t(s) // n_all, 0, 0)),
            pl.BlockSpec((1, N_MOD, D_MODEL),
                         lambda s, t=tile: (jnp.where(t(s) % n_all == n_lat, 0, t(s) // n_all + 1), 0, 0))]
    widths = (D_MODEL, 3 * GDN_WIDTH, GDN_WIDTH, LRU_WIDTH, LRU_WIDTH, BA_PAD)
    outs = pl.pallas_call(
        functools.partial(_ffn1_in_kernel, n_lat, n_all),
        grid=(bsz * n_all // n_sub,),
        in_specs=per_tile + [
            _resident((3, D_MODEL)),
            _resident((None, D_MODEL, D_FF), (0, 0, 0)), _resident((None, D_MODEL, D_FF), (0, 0, 0)),
            _resident((None, D_FF, D_MODEL), (0, 0, 0)),
            _resident((D_MODEL, P_HEAD_COLS)), _resident((D_MODEL, P_TAIL_COLS))],
        out_specs=[pl.BlockSpec((n_sub * TOKEN_TILE, w), lambda s: (s, 0)) for w in widths],
        out_shape=[jax.ShapeDtypeStruct((bsz * t_all, w), F32) for w in widths],
        compiler_params=pltpu.CompilerParams(dimension_semantics=("arbitrary",),
                                             vmem_limit_bytes=VMEM_LIMIT),
        name="ffn1_in",
    )(*([x, ctx, mods] * n_sub), norm_g, w1, w3, w2, w_in_bf, w_tail)
    return [o.reshape(bsz, t_all, o.shape[-1]) for o in outs]


def _pair_block_diag(x, first):
    return jnp.concatenate([jnp.where(first, x, 0.0), jnp.where(first, 0.0, x)], axis=0)


def _run_interleaved(coroutines):
    active = [[gen, next(gen)] for gen in coroutines]
    while active:
        still = []
        for item in active:
            results = [dot() for dot in item[1]]
            try:
                item[1] = item[0].send(results)
                still.append(item)
            except StopIteration:
                pass
        active = still


def _gdn_local_body(seg_start, seg_end, x_ref, prev_ref, next_ref, ba_ref, cw_ref, gp_ref,
                    out_refs, stage_in, stage_out):
    c = GDN_CHUNK
    pairs = GDN_HEADS // 2
    row = lax.broadcasted_iota(jnp.int32, (c, 2 * c), 0)
    lane2 = lax.broadcasted_iota(jnp.int32, (c, 2 * c), 1)
    col = lane2 & (c - 1)
    first = lane2 < c
    first_row = first[0:1]
    eye2 = (row == col).astype(F32)
    zeros_k = jnp.zeros((c, GDN_DK), BF16)
    unit_index = lambda ci, d, p: (ci * 2 + d) * pairs + p

    def coupling(k):
        return jnp.logical_and((row >> k) != (col >> k), (row >> (k + 1)) == (col >> (k + 1)))

    lhs_in, kr_in, rhs_in, dec_in, kdt_in, qe_in, eg_in = stage_in

    def pair_unit(ci, d, p):
        u_ref, wq_ref, ak_ref, eg_ref = out_refs[d]
        un = unit_index(ci, d, p)
        strict = (row < col) if d else (row > col)
        k_rows = kr_in[ci * pairs + p]
        k_bd = jnp.concatenate([jnp.concatenate([k_rows[:c], zeros_k], axis=1),
                                jnp.concatenate([zeros_k, k_rows[c:]], axis=1)], axis=0)
        lhs = lhs_in[un]
        (prods,) = yield [lambda: lax.dot_general(lhs, k_bd, NT_DIMS, preferred_element_type=F32)]
        decay = dec_in[un]
        a = jnp.where(strict, prods[:c] * decay, 0.0)
        ps = slice(p * 2 * c, (p + 1) * 2 * c)
        ak_ref[ci, 0, 0:c, ps] = (prods[c:] * decay).astype(BF16)
        ak_ref[ci, 0, c:3 * c, ps] = kdt_in[un]
        t = eye2 - jnp.where(coupling(0), a, 0.0)
        k = 1
        while 2 ** k < c:
            l = jnp.where(coupling(k), a, 0.0)
            t_bd = _pair_block_diag(t, first)
            (lt,) = yield [lambda: _mm(l, t_bd)]
            lt_bd = _pair_block_diag(lt, first)
            (tlt,) = yield [lambda: _mm(t, lt_bd)]
            t = t - tlt
            k += 1
        t_bd = _pair_block_diag(t, first).astype(BF16)
        rhs = rhs_in[un]
        (sol,) = yield [lambda: jnp.dot(t_bd, rhs, preferred_element_type=F32)]
        for i in range(2):
            h = 2 * p + i
            hs = slice(h * GDN_DK, (h + 1) * GDN_DK)
            u_ref[ci, 0, :, hs] = sol[i * c:(i + 1) * c, :GDN_DK]
            wq_ref[ci, 0, 0:c, hs] = sol[i * c:(i + 1) * c, GDN_DK:].astype(BF16)
            wq_ref[ci, 0, c:2 * c, hs] = qe_in[un, i * c:(i + 1) * c]
            eg_ref[ci, 0, h:h + 1, :] = eg_in[un, i:i + 1]

    lhs_out, kr_out, rhs_out, dec_out, kdt_out, qe_out, eg_out = stage_out
    half = _dwconv(x_ref[0], jnp.where(seg_start, 0.0, prev_ref[0]), jnp.where(seg_end, 0.0, next_ref[0]),
                   0.5 * cw_ref[...])
    y = half * jnp.tanh(half) + half
    ba = ba_ref[0]
    lane = lax.broadcasted_iota(jnp.int32, (1, BA_PAD), 1)
    is_g = jnp.logical_and(lane >= 2 * GDN_HEADS, lane < 4 * GDN_HEADS)
    beta_all = _sigmoid(ba)
    g_all = jnp.where(is_g, -jnp.exp(gp_ref[0:1]) * _softplus(ba + gp_ref[1:2]), 0.0)

    def chunk(ci):
        rs = slice(ci * c, (ci + 1) * c)
        heads = []
        for h in range(GDN_HEADS):
            lo = h * GDN_DK
            q = y[rs, lo:lo + GDN_DK]
            k = y[rs, GDN_WIDTH + lo:GDN_WIDTH + lo + GDN_DK]
            v = y[rs, 2 * GDN_WIDTH + lo:2 * GDN_WIDTH + lo + GDN_DK]
            q = q * (lax.rsqrt(jnp.sum(q * q, axis=-1, keepdims=True) + EPS) * (GDN_DK ** -0.5))
            k = k * lax.rsqrt(jnp.sum(k * k, axis=-1, keepdims=True) + EPS)
            heads.append((q, k, v))
        g = g_all[rs]
        k_rows = [jnp.concatenate([heads[2 * p][1], heads[2 * p + 1][1]], axis=0) for p in range(pairs)]
        k_ts = [kr.T for kr in k_rows]
        grow = lax.broadcasted_iota(jnp.int32, g.shape, 0)
        gcum_f = g
        step = 1
        while step < c:
            gcum_f = gcum_f + jnp.where(grow >= step, pltpu.roll(gcum_f, step, 0), 0.0)
            step *= 2
        gcum_b = gcum_f[c - 1:c] - gcum_f + g
        gcum_t = jnp.concatenate([gcum_f, gcum_b], axis=0).T
        beta_c = beta_all[rs]
        for p in range(pairs):
            kr_out[ci * pairs + p] = k_rows[p].astype(BF16)
        for d in range(2):
            gcum = gcum_b if d else gcum_f
            incl = (row <= col) if d else (row >= col)
            last = 0 if d else c - 1
            for p in range(pairs):
                un = unit_index(ci, d, p)
                (q0, k0, v0), (q1, k1, v1) = heads[2 * p], heads[2 * p + 1]
                cb = d * GDN_HEADS + 2 * p
                cg = 2 * GDN_HEADS + cb
                beta0, beta1 = beta_c[:, cb:cb + 1], beta_c[:, cb + 1:cb + 2]
                gc0, gc1 = gcum[:, cg:cg + 1], gcum[:, cg + 1:cg + 2]
                gt0, gt1 = gcum_t[cg:cg + 1], gcum_t[cg + 1:cg + 2]
                gr = (jnp.where(first_row, pltpu.roll(gt0, c, 1), gt1) if d
                      else jnp.where(first_row, gt0, pltpu.roll(gt1, c, 1)))
                gc = jnp.where(first, gc0, gc1)
                e0, e1 = jnp.exp(gc0), jnp.exp(gc1)
                kb0, kb1 = k0 * beta0, k1 * beta1
                dec_out[un] = jnp.exp(jnp.where(incl, gc - gr, NEG_BIG))
                lhs_out[un] = jnp.concatenate([jnp.concatenate([kb0, kb1], axis=1),
                                               jnp.concatenate([q0, q1], axis=1)], axis=0).astype(BF16)
                rhs_out[un] = jnp.concatenate([jnp.concatenate([v0 * beta0, kb0 * e0], axis=1),
                                               jnp.concatenate([v1 * beta1, kb1 * e1], axis=1)],
                                              axis=0).astype(BF16)
                kdt_out[un] = (k_ts[p] * jnp.exp(gc[last:last + 1] - gr)).astype(BF16)
                qe_out[un] = jnp.concatenate([q0 * e0, q1 * e1], axis=0).astype(BF16)
                eg_out[un] = jnp.concatenate(
                    [jnp.broadcast_to(jnp.exp(gc0[last:last + 1]), (1, BA_PAD)),
                     jnp.broadcast_to(jnp.exp(gc1[last:last + 1]), (1, BA_PAD)),
                     jnp.zeros((HALO - 2, BA_PAD), F32)], axis=0)

    for refs in out_refs:
        refs[3][...] = jnp.zeros_like(refs[3])
    _run_interleaved([pair_unit(ci, d, p) for ci in range(GDN_TILE_CHUNKS) for d in range(2)
                      for p in range(pairs)])
    for ci in range(GDN_TILE_CHUNKS):
        chunk(ci)


def _gdn_local_kernel(n_lat_tiles, n_all_tiles, n_tiles, x_ref, prev_ref, next_ref, ba_ref, cw_ref, gp_ref,
                      uf_ref, wqf_ref, akf_ref, egf_ref, ub_ref, wqb_ref, akb_ref, egb_ref, *stage_refs):
    s = pl.program_id(0)
    seg_start, seg_end = _segment_edges(jnp.minimum(s, n_tiles - 1) % n_all_tiles, n_lat_tiles, n_all_tiles)
    out_refs = ((uf_ref, wqf_ref, akf_ref, egf_ref), (ub_ref, wqb_ref, akb_ref, egb_ref))
    half = len(stage_refs) // 2
    sets = (stage_refs[:half], stage_refs[half:])

    @pl.when(s == 0)
    def _():
        for r in sets[1]:
            r[...] = jnp.zeros_like(r)

    for parity in range(2):
        @pl.when(s % 2 == parity)
        def _():
            _gdn_local_body(seg_start, seg_end, x_ref, prev_ref, next_ref, ba_ref, cw_ref, gp_ref,
                            out_refs, sets[1 - parity], sets[parity])


def _gdn_local_call(qkv, ba, conv_w, gate_params, n_lat_tok):
    bsz, t_all, width = qkv.shape
    c = GDN_CHUNK
    tile = GDN_TILE_CHUNKS * c
    n_all = t_all // tile
    n_lat = n_lat_tok // tile
    per_tile = tile // HALO
    last_halo = t_all // HALO - 1
    n_chunks = t_all // c
    n_tiles = bsz * n_all
    pairs = GDN_HEADS // 2
    units = GDN_TILE_CHUNKS * 2 * pairs
    t_in = lambda s: jnp.minimum(s, n_tiles - 1)
    t_out = lambda s: jnp.maximum(s - 1, 0)
    in_tok = lambda w: pl.BlockSpec((1, tile, w), lambda s: (t_in(s) // n_all, t_in(s) % n_all, 0))
    cblk = lambda rows, w: pl.BlockSpec((GDN_TILE_CHUNKS, 1, rows, w),
                                        lambda s: (t_out(s) % n_all, t_out(s) // n_all, 0, 0))
    vec = pl.BlockSpec((2, BA_PAD), lambda s: (0, 0))
    outs = [(jax.ShapeDtypeStruct((n_chunks, bsz, c, GDN_WIDTH), F32), cblk(c, GDN_WIDTH)),
            (jax.ShapeDtypeStruct((n_chunks, bsz, 2 * c, GDN_WIDTH), BF16), cblk(2 * c, GDN_WIDTH)),
            (jax.ShapeDtypeStruct((n_chunks, bsz, 3 * c, GDN_HEADS * c), BF16), cblk(3 * c, GDN_HEADS * c)),
            (jax.ShapeDtypeStruct((n_chunks, bsz, HALO, BA_PAD), F32), cblk(HALO, BA_PAD))] * 2
    stage = [pltpu.VMEM((units, 2 * c, 2 * GDN_DK), BF16),
             pltpu.VMEM((GDN_TILE_CHUNKS * pairs, 2 * c, GDN_DK), BF16),
             pltpu.VMEM((units, 2 * c, 2 * GDN_DK), BF16),
             pltpu.VMEM((units, c, 2 * c), F32),
             pltpu.VMEM((units, GDN_DK, 2 * c), BF16),
             pltpu.VMEM((units, 2 * c, GDN_DK), BF16),
             pltpu.VMEM((units, HALO, BA_PAD), F32)]
    return pl.pallas_call(
        functools.partial(_gdn_local_kernel, n_lat, n_all, n_tiles),
        grid=(n_tiles + 1,),
        in_specs=[in_tok(width),
                  pl.BlockSpec((1, HALO, width),
                               lambda s: (t_in(s) // n_all, jnp.maximum((t_in(s) % n_all) * per_tile - 1, 0), 0)),
                  pl.BlockSpec((1, HALO, width),
                               lambda s: (t_in(s) // n_all,
                                          jnp.minimum((t_in(s) % n_all + 1) * per_tile, last_halo), 0)),
                  in_tok(BA_PAD), pl.BlockSpec((CONV_W, width), lambda s: (0, 0)), vec],
        out_specs=[o[1] for o in outs],
        out_shape=[o[0] for o in outs],
        scratch_shapes=stage + stage,
        compiler_params=pltpu.CompilerParams(dimension_semantics=("arbitrary",),
                                             vmem_limit_bytes=VMEM_LIMIT),
        name="gdn_local",
    )(qkv, qkv, qkv, ba, conv_w, gate_params)


def _gdn_scan_steps(uf_ref, wqf_ref, akf_ref, egf_ref, ub_ref, wqb_ref, akb_ref, egb_ref,
                    of_ref, ob_ref, sf_ref, sb_ref):
    c = GDN_CHUNK
    bsz = uf_ref.shape[1]
    hs = lambda h: slice(h * GDN_DK, (h + 1) * GDN_DK)
    for sub in range(GDN_SCAN_CHUNKS):
        chains = [(refs, ci, b, h)
                  for refs, ci in (((uf_ref, wqf_ref, akf_ref, egf_ref, of_ref, sf_ref), sub),
                                   ((ub_ref, wqb_ref, akb_ref, egb_ref, ob_ref, sb_ref),
                                    GDN_SCAN_CHUNKS - 1 - sub))
                  for b in range(bsz) for h in range(GDN_HEADS)]
        states = [refs[5][b * GDN_HEADS + h] for refs, ci, b, h in chains]
        with_s = yield [lambda refs=refs, ci=ci, b=b, h=h, s=s:
                        jnp.dot(refs[1][ci, b, :, hs(h)], s.astype(BF16), preferred_element_type=F32)
                        for (refs, ci, b, h), s in zip(chains, states)]
        v_new = [refs[0][ci, b, :, hs(h)] - r[:c] for (refs, ci, b, h), r in zip(chains, with_s)]
        with_v = yield [lambda refs=refs, ci=ci, b=b, h=h, v=v:
                        jnp.dot(refs[2][ci, b, :, h * c:(h + 1) * c], v.astype(BF16), preferred_element_type=F32)
                        for (refs, ci, b, h), v in zip(chains, v_new)]
        for (refs, ci, b, h), s, r, a in zip(chains, states, with_s, with_v):
            refs[4][b, ci * c:(ci + 1) * c, hs(h)] = r[c:] + a[:c]
            refs[5][b * GDN_HEADS + h] = s * refs[3][ci, b, h:h + 1, :] + a[c:]


def _lin_scan(a, b, reverse):
    r = a.shape[0]
    row = lax.broadcasted_iota(jnp.int32, a.shape, 0)
    s = 1
    while s < r:
        ok = (row < r - s) if reverse else (row >= s)
        shift = r - s if reverse else s
        a_sh = jnp.where(ok, pltpu.roll(a, shift, 0), 1.0)
        b_sh = jnp.where(ok, pltpu.roll(b, shift, 0), 0.0)
        b = a * b_sh + b
        a = a * a_sh
        s *= 2
    return a, b


def _lru_ctx_direction(reverse, n_slabs, slab, x_ref, prev_ref, next_ref, cw, cb, w_ref, br, bi, lam, carry_ref):
    bsz, r, width = x_ref.shape
    step = pl.program_id(0)
    has_prev = slab > 0
    has_next = slab < n_slabs - 1
    xs = []
    for b in range(bsz):
        prev = jnp.where(has_prev, prev_ref[b], 0.0)
        nxt = jnp.where(has_next, next_ref[b], 0.0)
        xs.append(_dwconv(x_ref[b], prev, nxt, cw) + cb)
    x = jnp.concatenate(xs, axis=0)
    xb = x.astype(BF16)
    d = 1 if reverse else 0
    gate = lambda g: jnp.concatenate(
        [jnp.dot(xb[:, t * LRU_LANES:(t + 1) * LRU_LANES], w_ref[2 * d + g, t], preferred_element_type=F32)
         for t in range(width // LRU_LANES)], axis=1)
    a, mult, gated_x = _lru_a_b(gate(0), gate(1), br, bi, lam, 0.5 * x)
    row = lax.broadcasted_iota(jnp.int32, (r, width), 0)
    is_first = jnp.logical_and(step == 0, row == ((r - 1) if reverse else 0))
    edge = 0 if reverse else r - 1
    for b in range(bsz):
        sl = slice(b * r, (b + 1) * r)
        m = jnp.where(is_first, 1.0, mult[sl])
        a_cum, h = _lin_scan(a[sl], m * gated_x[sl], reverse)
        h = h + a_cum * carry_ref[b:b + 1]
        carry_ref[b:b + 1] = h[edge:edge + 1]


def _lru_ctx_kernel(n_slabs, xf_ref, pf_ref, nf_ref, xb_ref, pb_ref, nb_ref, cw_ref, cb_ref,
                    w_ref, bg_ref, lam_ref, fin_ref, cf_ref, cbk_ref):
    step = pl.program_id(0)

    @pl.when(step == 0)
    def _():
        cf_ref[...] = jnp.zeros_like(cf_ref)
        cbk_ref[...] = jnp.zeros_like(cbk_ref)

    cw = cw_ref[...]
    cb = cb_ref[...]
    _lru_ctx_direction(False, n_slabs, step, xf_ref, pf_ref, nf_ref, cw, cb, w_ref,
                       bg_ref[0:1], bg_ref[1:2], lam_ref[0:1], cf_ref)
    _lru_ctx_direction(True, n_slabs, n_slabs - 1 - step, xb_ref, pb_ref, nb_ref, cw, cb, w_ref,
                       bg_ref[2:3], bg_ref[3:4], lam_ref[1:2], cbk_ref)
    fin_ref[0] = cf_ref[...]
    fin_ref[1] = cbk_ref[...]


def _lru_ctx_call(rx, n_lat_tok, conv_w, conv_b, w_tiles, b_gate, lam):
    bsz, t_all, width = rx.shape
    r = GRID_W
    per = r // HALO
    n_slabs = (t_all - n_lat_tok) // r
    first = n_lat_tok // r
    main = lambda f: pl.BlockSpec((bsz, r, width), lambda s: (0, first + f(s), 0))
    prev = lambda f: pl.BlockSpec((bsz, HALO, width), lambda s: (0, (first + f(s)) * per - 1, 0))
    nxt = lambda f: pl.BlockSpec(
        (bsz, HALO, width), lambda s: (0, jnp.minimum((first + f(s) + 1) * per, t_all // HALO - 1), 0))
    fwd = lambda s: s
    bwd = lambda s: n_slabs - 1 - s
    full = lambda shape: pl.BlockSpec(shape, lambda s: (0,) * len(shape))
    carry = pltpu.VMEM((bsz, width), F32)
    return pl.pallas_call(
        functools.partial(_lru_ctx_kernel, n_slabs),
        grid=(n_slabs,),
        in_specs=[main(fwd), prev(fwd), nxt(fwd), main(bwd), prev(bwd), nxt(bwd),
                  full((CONV_W, width)), full((1, width)), full(w_tiles.shape),
                  full((4, width)), full((2, width))],
        out_specs=full((2, bsz, width)),
        out_shape=jax.ShapeDtypeStruct((2, bsz, width), F32),
        scratch_shapes=[carry, carry],
        compiler_params=pltpu.CompilerParams(dimension_semantics=("arbitrary",),
                                             vmem_limit_bytes=VMEM_LIMIT),
        name="lru_ctx",
    )(rx, rx, rx, rx, rx, rx, conv_w, conv_b, w_tiles, b_gate, lam)


def _lru_grid_gates(x_ref, cw_ref, cb_ref, w_ref, bg_ref, lam_ref, af_ref, bf_ref, ab_ref, bb_ref):
    n = GRID_W
    rp = LRU_ROWS_PER_PASS
    lanes = x_ref.shape[-1]
    crow = lax.broadcasted_iota(jnp.int32, (n, lanes), 0)
    cw = cw_ref[...]
    cb = cb_ref[...]

    def grid_rows(lo, hi):
        parts = []
        for r in range(lo, min(hi, 0)):
            parts.append(jnp.where(crow == 0, 0.0, pltpu.roll(x_ref[0, (n + r) * n:(n + r + 1) * n, :], 1, 0)))
        if max(lo, 0) < min(hi, n):
            parts.append(x_ref[0, max(lo, 0) * n:min(hi, n) * n, :])
        for r in range(max(lo, n), hi):
            parts.append(jnp.where(crow == n - 1, 0.0,
                                   pltpu.roll(x_ref[0, (r - n) * n:(r - n + 1) * n, :], n - 1, 0)))
        return parts[0] if len(parts) == 1 else jnp.concatenate(parts, axis=0)

    for r0 in range(0, n, rp):
        sl = slice(r0 * n, (r0 + rp) * n)
        xc = (grid_rows(r0 - 2, r0 + rp - 2) * cw[0:1] + grid_rows(r0 - 1, r0 + rp - 1) * cw[1:2]
              + grid_rows(r0, r0 + rp) * cw[2:3] + grid_rows(r0 + 1, r0 + rp + 1) * cw[3:4] + cb)
        xb = xc.astype(BF16)
        gates = yield [lambda i=i: jnp.dot(xb, w_ref[i, 0], preferred_element_type=F32) for i in range(4)]
        half_x = 0.5 * xc
        for d, (a_ref, b_ref) in enumerate(((af_ref, bf_ref), (ab_ref, bb_ref))):
            a, gain, gated_x = _lru_a_b(gates[2 * d], gates[2 * d + 1], bg_ref[2 * d:2 * d + 1],
                                        bg_ref[2 * d + 1:2 * d + 2], lam_ref[d:d + 1], half_x)
            a_ref[sl] = a
            b_ref[sl] = gain * gated_x


def _lru_grid_finish(h0_ref, out_ref, af_ref, bf_ref, ab_ref, bb_ref):
    n = GRID_W
    lanes = out_ref.shape[-1]
    crow = lax.broadcasted_iota(jnp.int32, (n, lanes), 0)
    hf = jnp.zeros((n, lanes), F32)
    hb = jnp.zeros((n, lanes), F32)
    pf = jnp.ones((n, lanes), F32)
    pb = jnp.ones((n, lanes), F32)
    for i in range(n):
        sf = slice(i * n, (i + 1) * n)
        sb = slice((n - 1 - i) * n, (n - i) * n)
        a = af_ref[sf]
        hf = a * hf + bf_ref[sf]
        pf = a * pf
        bf_ref[sf] = hf
        af_ref[sf] = pf
        a = ab_ref[sb]
        hb = a * hb + bb_ref[sb]
        pb = a * pb
        bb_ref[sb] = hb
        ab_ref[sb] = pb

    h0f = h0_ref[0, 0]
    h0b = h0_ref[1, 0]
    acc_a, acc_h = _lin_scan(pf, hf, False)
    in_f = jnp.where(crow == 0, h0f, pltpu.roll(acc_h + acc_a * h0f, 1, 0))
    acc_a, acc_h = _lin_scan(pb, hb, True)
    in_b = jnp.where(crow == n - 1, h0b, pltpu.roll(acc_h + acc_a * h0b, n - 1, 0))

    for r in range(n):
        sl = slice(r * n, (r + 1) * n)
        out_ref[0, sl, :] = (bf_ref[sl] + af_ref[sl] * in_f) + (bb_ref[sl] + ab_ref[sl] * in_b)


def _mixers_kernel(*refs):
    scan_in, lru_in, outs, scratch = refs[:8], refs[8:15], refs[15:18], refs[18:]
    of_ref, ob_ref, hs_ref = outs
    sf_ref, sb_ref = scratch[:2]

    @pl.when(pl.program_id(0) == 0)
    def _():
        sf_ref[...] = jnp.zeros_like(sf_ref)
        sb_ref[...] = jnp.zeros_like(sb_ref)

    x_ref, h0_ref = lru_in[:2]
    _run_interleaved([_gdn_scan_steps(*scan_in, of_ref, ob_ref, sf_ref, sb_ref),
                      _lru_grid_gates(x_ref, *lru_in[2:], *scratch[2:])])
    _lru_grid_finish(h0_ref, hs_ref, *scratch[2:])


def _mixers_call(local, rx, n_lat_tok, h0, conv_w, conv_b, w_tiles, b_gate, lam):
    n_chunks, bsz = local[0].shape[:2]
    width = rx.shape[2]
    c = GDN_CHUNK
    g = GDN_SCAN_CHUNKS
    n_all = n_chunks // g
    n_lat = n_lat_tok // (c * g)
    n_ctx = n_all - n_lat
    fwd = lambda s: jnp.where(s < n_ctx, n_lat + s, s - n_ctx)
    bwd = lambda s: n_all - 1 - s
    fwd_out = lambda s: jnp.maximum(s - n_ctx, 0)
    bwd_out = lambda s: jnp.minimum(n_all - 1 - s, n_lat - 1)
    cblk = lambda rows, w, f: pl.BlockSpec((g, bsz, rows, w), lambda s: (f(s), 0, 0, 0))
    ins = lambda f: [cblk(c, GDN_WIDTH, f), cblk(2 * c, GDN_WIDTH, f), cblk(3 * c, GDN_HEADS * c, f),
                     cblk(HALO, BA_PAD, f)]
    tok = lambda f: pl.BlockSpec((bsz, g * c, GDN_WIDTH), lambda s: (0, f(s), 0))
    state = pltpu.VMEM((bsz * GDN_HEADS, GDN_DK, GDN_DK), F32)

    nt = width // LRU_LANES
    unit = lambda s: jnp.minimum(s, bsz * nt - 1)
    assert n_all >= bsz * nt
    lane = lambda rows: pl.BlockSpec((rows, LRU_LANES), lambda s: (0, unit(s) % nt))
    grid_tok = pl.BlockSpec((1, n_lat_tok, LRU_LANES), lambda s: (unit(s) // nt, 0, unit(s) % nt))
    lru_scratch = pltpu.VMEM((n_lat_tok, LRU_LANES), F32)
    return pl.pallas_call(
        _mixers_kernel,
        grid=(n_all,),
        in_specs=ins(fwd) + ins(bwd) + [
            grid_tok,
            pl.BlockSpec((2, 1, 1, LRU_LANES), lambda s: (0, unit(s) // nt, 0, unit(s) % nt)),
            lane(CONV_W), lane(1),
            pl.BlockSpec((4, 1, LRU_LANES, LRU_LANES), lambda s: (0, unit(s) % nt, 0, 0)),
            lane(4), lane(2)],
        out_specs=[tok(fwd_out), tok(bwd_out), grid_tok],
        out_shape=[jax.ShapeDtypeStruct((bsz, n_lat_tok, GDN_WIDTH), F32)] * 2
                  + [jax.ShapeDtypeStruct((bsz, n_lat_tok, width), F32)],
        scratch_shapes=[state, state] + [lru_scratch] * 4,
        compiler_params=pltpu.CompilerParams(dimension_semantics=("arbitrary",),
                                             vmem_limit_bytes=VMEM_LIMIT),
        name="mixers",
    )(*local, rx, h0.reshape(2, bsz, 1, width), conv_w, conv_b, w_tiles, b_gate, lam)


def _out_ffn2_kernel(h1_ref, of_ref, ob_ref, z_ref, hs_ref, rg_ref, mod_ref, ng_ref, gnw_ref,
                     wout_ref, w1_ref, w3_ref, w2_ref, fg_ref, out_ref):
    mod = mod_ref[0]

    def sub_tile(rows):
        o = of_ref[0, rows] + ob_ref[0, rows]
        z = z_ref[0, rows]
        parts = []
        for h in range(GDN_HEADS):
            sl = slice(h * GDN_DK, (h + 1) * GDN_DK)
            oh = o[:, sl]
            parts.append(oh * lax.rsqrt(jnp.mean(oh * oh, axis=-1, keepdims=True) + EPS)
                         * gnw_ref[...] * _silu(z[:, sl]))
        parts.append(hs_ref[0, rows] * _gelu_tanh(rg_ref[0, rows]))
        mixed = jnp.concatenate(parts, axis=1).astype(BF16)
        (y,) = yield [lambda: jnp.dot(mixed, wout_ref[...], preferred_element_type=F32)]
        h2 = h1_ref[0, rows] + mod[5:6] * y
        ub = _rms_mod(h2, ng_ref[2:3], mod[6:7], mod[7:8]).astype(BF16)
        a, b = yield [lambda: jnp.dot(ub, w1_ref[...], preferred_element_type=F32),
                      lambda: jnp.dot(ub, w3_ref[...], preferred_element_type=F32)]
        g = (_silu(a) * b).astype(BF16)
        (f,) = yield [lambda: jnp.dot(g, w2_ref[...], preferred_element_type=F32)]
        h3 = h2 + FFN_RESIDUAL * mod[8:9] * f
        out_ref[0, rows] = h3 * lax.rsqrt(jnp.mean(h3 * h3, axis=-1, keepdims=True) + EPS) * fg_ref[...]

    _run_interleaved([sub_tile(slice(i * TOKEN_TILE, (i + 1) * TOKEN_TILE))
                      for i in range(out_ref.shape[1] // TOKEN_TILE)])


def _out_ffn2_call(h1, o_f, o_b, z, h_sum, rg, mods, norm_g, gdn_norm_w, w_out, w1, w3, w2, final_g):
    bsz, seq, _ = o_f.shape
    rows = OUT_SUB_TILES * TOKEN_TILE
    tok = lambda w: pl.BlockSpec((1, rows, w), lambda b, j: (b, j, 0))
    return pl.pallas_call(
        _out_ffn2_kernel,
        grid=(bsz, seq // rows),
        in_specs=[tok(D_MODEL), tok(GDN_WIDTH), tok(GDN_WIDTH), tok(GDN_WIDTH),
                  tok(LRU_WIDTH), tok(LRU_WIDTH),
                  pl.BlockSpec((1, N_MOD, D_MODEL), lambda b, j: (b + 1, 0, 0)),
                  _resident((3, D_MODEL)), _resident((1, GDN_DK)),
                  _resident((D_MODEL, D_MODEL)),
                  _resident((None, D_MODEL, D_FF), (1, 0, 0)), _resident((None, D_MODEL, D_FF), (1, 0, 0)),
                  _resident((None, D_FF, D_MODEL), (1, 0, 0)),
                  _resident((1, D_MODEL))],
        out_specs=tok(D_MODEL),
        out_shape=jax.ShapeDtypeStruct((bsz, seq, D_MODEL), F32),
        compiler_params=pltpu.CompilerParams(dimension_semantics=("arbitrary", "arbitrary"),
                                             vmem_limit_bytes=VMEM_LIMIT),
        name="out_ffn2",
    )(h1, o_f, o_b, z, h_sum, rg, mods, norm_g, gdn_norm_w, w_out, w1, w3, w2, final_g)


def kernel(x, c, ctx, c_ctx, w_ada, b_ada, norm_g, ffn_w1, ffn_w3, ffn_w2, w_in, w_out, gdn_conv_w, gdn_a_log,
           gdn_dt_bias, gdn_norm_w, lru_conv_w, lru_conv_b, lru_w_gate, lru_b_gate, lru_lambda, final_norm_g):
    bsz, seq, _ = x.shape
    assert w_ada.shape[0] == 1 and seq == GRID_W * GRID_W and ctx.shape[1] == TOKEN_TILE

    rows = -(-(bsz + 1) // 8) * 8
    cvec = jnp.concatenate([c_ctx[None, :], c, jnp.zeros((rows - bsz - 1, D_MODEL), F32)], axis=0)
    mods = _ada_call(cvec, w_ada[0], b_ada).reshape(rows, N_MOD, D_MODEL)

    w_in_bf = w_in[0].astype(BF16)
    o_ba = P_HEAD_COLS
    o_rx = o_ba + 4 * GDN_HEADS
    w_tail = jnp.concatenate([w_in_bf[:, o_rx:], w_in_bf[:, o_ba:o_rx],
                              jnp.zeros((D_MODEL, BA_PAD - 4 * GDN_HEADS), BF16)], axis=1)
    w1 = ffn_w1[0].astype(BF16)
    w3 = ffn_w3[0].astype(BF16)
    w2 = ffn_w2[0].astype(BF16)

    h1, qkv, z, rx, rg, ba = _ffn1_in_call(x, ctx, mods, norm_g[0], w1, w3, w2, w_in_bf, w_tail)

    gate_params = jnp.pad(jnp.stack([gdn_a_log[0].reshape(-1), gdn_dt_bias[0].reshape(-1)]),
                          ((0, 0), (2 * GDN_HEADS, BA_PAD - 4 * GDN_HEADS)))
    local = _gdn_local_call(qkv, ba, gdn_conv_w[0], gate_params, seq)

    nt = LRU_WIDTH // LRU_LANES
    per_tile = LRU_LANES // LRU_BW
    w_tiles = jnp.einsum('gticd,ij->gticjd', lru_w_gate[0].reshape(4, nt, per_tile, LRU_BW, LRU_BW),
                         0.5 * jnp.eye(per_tile, dtype=F32)).reshape(4, nt, LRU_LANES, LRU_LANES).astype(BF16)
    b_gate = lru_b_gate[0].reshape(4, LRU_WIDTH)
    lam = lru_lambda[0]
    cw, cb = lru_conv_w[0], lru_conv_b
    h_ctx = _lru_ctx_call(rx, seq, cw, cb, w_tiles, b_gate, lam)
    o_f, o_b, h_sum = _mixers_call(local, rx, seq, h_ctx, cw, cb, w_tiles, b_gate, lam)

    return _out_ffn2_call(h1, o_f, o_b, z, h_sum, rg, mods, norm_g[0], gdn_norm_w,
                          w_out[0].astype(BF16), w1, w3, w2, final_norm_g[None, :])
```

```python
import functools

import jax
import jax.numpy as jnp
from jax import lax
from jax.experimental import pallas as pl
from jax.experimental.pallas import tpu as pltpu

D_MODEL = 1024
D_FF = 2816
N_MOD = 9
EPS = 1e-6
FFN_RESIDUAL = 0.5

GDN_WIDTH = 512
GDN_HEADS = 4
GDN_DK = 128
GDN_CHUNK = 64
CONV_W = 4
LRU_WIDTH = 512
LRU_BW = 64
LRU_C = 8.0
GRID_W = 64

ADA_MODS_PER_STEP = 3
TOKEN_TILE = 256
GDN_TILE_CHUNKS = TOKEN_TILE // GDN_CHUNK
GDN_SCAN_CHUNKS = 4
OUT_SUB_TILES = 2
FFN1_SUB_TILES = 2
HALO = 8
BA_PAD = 128
P_HEAD_COLS = 4 * GDN_WIDTH
P_TAIL_COLS = 2 * LRU_WIDTH + BA_PAD
LRU_LANES = 128
LRU_ROWS_PER_PASS = 8
VMEM_LIMIT = 56 * 1024 * 1024

BF16 = jnp.bfloat16
F32 = jnp.float32
NEG_BIG = -1e30
SQRT_GUARD = 1e-30
LOG2_E = 1.4426950408889634
NT_DIMS = (((1,), (1,)), ((), ()))


def _mm(a, b):
    return jnp.dot(a.astype(BF16), b.astype(BF16), preferred_element_type=F32)


def _sigmoid(x):
    return 1.0 / (1.0 + jnp.exp(-x))


def _lru_a_b(half_r, half_i, bias_r, bias_i, lam, half_x):
    t_r = jnp.tanh(half_r + 0.5 * bias_r)
    t_i = jnp.tanh(half_i + 0.5 * bias_i)
    k = (-0.5 * LRU_C * LOG2_E) * _softplus(-lam)
    a = jnp.exp2(k * t_r + k)
    y = 1.0 - a * a
    return a, y * lax.rsqrt(y + SQRT_GUARD), t_i * half_x + half_x


def _silu(x):
    half = 0.5 * x
    return half * jnp.tanh(half) + half


def _softplus(x):
    return jnp.maximum(x, 0.0) + jnp.log(1.0 + jnp.exp(-jnp.abs(x)))


def _gelu_tanh(x):
    return 0.5 * x * (1.0 + jnp.tanh(0.7978845608028654 * (x + 0.044715 * x * x * x)))


def _rms_mod(h, g, shift, scale):
    y = h * lax.rsqrt(jnp.mean(h * h, axis=-1, keepdims=True) + EPS) * g
    return y * (1.0 + scale) + shift


def _swiglu(u, w1_ref, w3_ref, w2_ref):
    ub = u.astype(BF16)
    a = jnp.dot(ub, w1_ref[...], preferred_element_type=F32)
    b = jnp.dot(ub, w3_ref[...], preferred_element_type=F32)
    g = (_silu(a) * b).astype(BF16)
    return jnp.dot(g, w2_ref[...], preferred_element_type=F32)


def _dwconv(x, prev, nxt, w):
    t = x.shape[0]
    row = lax.broadcasted_iota(jnp.int32, x.shape, 0)
    xm1 = jnp.where(row == 0, prev[HALO - 1:HALO], pltpu.roll(x, 1, 0))
    xm2 = jnp.where(row == 0, prev[HALO - 2:HALO - 1],
                    jnp.where(row == 1, prev[HALO - 1:HALO], pltpu.roll(x, 2, 0)))
    xp1 = jnp.where(row == t - 1, nxt[0:1], pltpu.roll(x, t - 1, 0))
    return xm2 * w[0:1] + xm1 * w[1:2] + x * w[2:3] + xp1 * w[3:4]


def _resident(shape, index=None):
    index = index or (0,) * len(shape)
    return pl.BlockSpec(shape, lambda *_: index, pipeline_mode=pl.Buffered(1))


def _ada_kernel(c_ref, w_ref, b_ref, o_ref):
    o_ref[...] = _mm(_silu(c_ref[...]), w_ref[...]) + b_ref[...]


def _ada_call(cvec, w_ada, b_ada):
    rows = cvec.shape[0]
    cols = ADA_MODS_PER_STEP * D_MODEL
    return pl.pallas_call(
        _ada_kernel,
        grid=(N_MOD // ADA_MODS_PER_STEP,),
        in_specs=[pl.BlockSpec((rows, D_MODEL), lambda k: (0, 0)),
                  pl.BlockSpec((D_MODEL, cols), lambda k: (0, k)),
                  pl.BlockSpec((1, cols), lambda k: (0, k))],
        out_specs=pl.BlockSpec((rows, cols), lambda k: (0, k)),
        out_shape=jax.ShapeDtypeStruct((rows, N_MOD * D_MODEL), F32),
        compiler_params=pltpu.CompilerParams(dimension_semantics=("arbitrary",),
                                             vmem_limit_bytes=VMEM_LIMIT),
        name="ada",
    )(cvec, w_ada, b_ada)


def _segment_edges(j, n_lat_tiles, n_all_tiles):
    return (jnp.logical_or(j == 0, j == n_lat_tiles),
            jnp.logical_or(j == n_lat_tiles - 1, j == n_all_tiles - 1))


def _ffn1_in_kernel(n_lat_tiles, n_all_tiles, *refs):
    n_sub = FFN1_SUB_TILES
    ins, refs = refs[:3 * n_sub], refs[3 * n_sub:]
    ng_ref, w1_ref, w3_ref, w2_ref, win_ref, wtail_ref, h1_ref, qkv_ref, z_ref, rx_ref, rg_ref, ba_ref = refs
    s = pl.program_id(0)

    def sub_tile(i):
        x_ref, ctx_ref, mod_ref = ins[3 * i:3 * i + 3]
        rows = slice(i * TOKEN_TILE, (i + 1) * TOKEN_TILE)
        h = jnp.where((s * n_sub + i) % n_all_tiles == n_lat_tiles, ctx_ref[0], x_ref[0])
        mod = mod_ref[0]
        ub = _rms_mod(h, ng_ref[0:1], mod[0:1], mod[1:2]).astype(BF16)
        a, b = yield [lambda: jnp.dot(ub, w1_ref[...], preferred_element_type=F32),
                      lambda: jnp.dot(ub, w3_ref[...], preferred_element_type=F32)]
        g = (_silu(a) * b).astype(BF16)
        (f,) = yield [lambda: jnp.dot(g, w2_ref[...], preferred_element_type=F32)]
        h1 = h + FFN_RESIDUAL * mod[2:3] * f
        h1_ref[rows] = h1
        ub2 = _rms_mod(h1, ng_ref[1:2], mod[3:4], mod[4:5]).astype(BF16)
        head, tail = yield [lambda: jnp.dot(ub2, win_ref[...], preferred_element_type=F32),
                            lambda: jnp.dot(ub2, wtail_ref[...], preferred_element_type=F32)]
        qkv_ref[rows] = head[:, :3 * GDN_WIDTH]
        z_ref[rows] = head[:, 3 * GDN_WIDTH:]
        rx_ref[rows] = tail[:, :LRU_WIDTH]
        rg_ref[rows] = tail[:, LRU_WIDTH:2 * LRU_WIDTH]
        ba_ref[rows] = tail[:, 2 * LRU_WIDTH:]

    _run_interleaved([sub_tile(i) for i in range(n_sub)])


def _ffn1_in_call(x, ctx, mods, norm_g, w1, w3, w2, w_in_bf, w_tail):
    bsz, seq, _ = x.shape
    n_lat = seq // TOKEN_TILE
    n_all = n_lat + 1
    t_all = n_all * TOKEN_TILE
    n_sub = FFN1_SUB_TILES
    assert (bsz * n_all) % n_sub == 0
    per_tile = []
    for i in range(n_sub):
        tile = lambda s, i=i: s * n_sub + i
        per_tile += [
            pl.BlockSpec((1, TOKEN_TILE, D_MODEL),
                         lambda s, t=tile: (t(s) // n_all, jnp.minimum(t(s) % n_all, n_lat - 1), 0)),
            pl.BlockSpec((1, TOKEN_TILE, D_MODEL), lambda s, t=tile: (t(s) // n_all, 0, 0)),
            pl.BlockSpec((1, N_MOD, D_MODEL),
                         lambda s, t=tile: (jnp.where(t(s) % n_all == n_lat, 0, t(s) // n_all + 1), 0, 0))]
    widths = (D_MODEL, 3 * GDN_WIDTH, GDN_WIDTH, LRU_WIDTH, LRU_WIDTH, BA_PAD)
    outs = pl.pallas_call(
        functools.partial(_ffn1_in_kernel, n_lat, n_all),
        grid=(bsz * n_all // n_sub,),
        in_specs=per_tile + [
            _resident((3, D_MODEL)),
            _resident((None, D_MODEL, D_FF), (0, 0, 0)), _resident((None, D_MODEL, D_FF), (0, 0, 0)),
            _resident((None, D_FF, D_MODEL), (0, 0, 0)),
            _resident((D_MODEL, P_HEAD_COLS)), _resident((D_MODEL, P_TAIL_COLS))],
        out_specs=[pl.BlockSpec((n_sub * TOKEN_TILE, w), lambda s: (s, 0)) for w in widths],
        out_shape=[jax.ShapeDtypeStruct((bsz * t_all, w), F32) for w in widths],
        compiler_params=pltpu.CompilerParams(dimension_semantics=("arbitrary",),
                                             vmem_limit_bytes=VMEM_LIMIT),
        name="ffn1_in",
    )(*([x, ctx, mods] * n_sub), norm_g, w1, w3, w2, w_in_bf, w_tail)
    return [o.reshape(bsz, t_all, o.shape[-1]) for o in outs]


def _pair_block_diag(x, first):
    return jnp.concatenate([jnp.where(first, x, 0.0), jnp.where(first, 0.0, x)], axis=0)


def _run_interleaved(coroutines):
    active = [[gen, next(gen)] for gen in coroutines]
    while active:
        still = []
        for item in active:
            results = [dot() for dot in item[1]]
            try:
                item[1] = item[0].send(results)
                still.append(item)
            except StopIteration:
                pass
        active = still


def _gdn_local_body(seg_start, seg_end, x_ref, prev_ref, next_ref, ba_ref, cw_ref, gp_ref,
                    out_refs, stage_in, stage_out):
    c = GDN_CHUNK
    pairs = GDN_HEADS // 2
    row = lax.broadcasted_iota(jnp.int32, (c, 2 * c), 0)
    lane2 = lax.broadcasted_iota(jnp.int32, (c, 2 * c), 1)
    col = lane2 & (c - 1)
    first = lane2 < c
    first_row = first[0:1]
    eye2 = (row == col).astype(F32)
    zeros_k = jnp.zeros((c, GDN_DK), BF16)
    unit_index = lambda ci, d, p: (ci * 2 + d) * pairs + p

    def coupling(k):
        return jnp.logical_and((row >> k) != (col >> k), (row >> (k + 1)) == (col >> (k + 1)))

    lhs_in, kr_in, rhs_in, dec_in, kdt_in, qe_in, eg_in = stage_in

    def pair_unit(ci, d, p):
        u_ref, wq_ref, ak_ref, eg_ref = out_refs[d]
        un = unit_index(ci, d, p)
        strict = (row < col) if d else (row > col)
        k_rows = kr_in[ci * pairs + p]
        k_bd = jnp.concatenate([jnp.concatenate([k_rows[:c], zeros_k], axis=1),
                                jnp.concatenate([zeros_k, k_rows[c:]], axis=1)], axis=0)
        lhs = lhs_in[un]
        (prods,) = yield [lambda: lax.dot_general(lhs, k_bd, NT_DIMS, preferred_element_type=F32)]
        decay = dec_in[un]
        a = jnp.where(strict, prods[:c] * decay, 0.0)
        ps = slice(p * 2 * c, (p + 1) * 2 * c)
        ak_ref[ci, 0, 0:c, ps] = (prods[c:] * decay).astype(BF16)
        ak_ref[ci, 0, c:3 * c, ps] = kdt_in[un]
        t = eye2 - jnp.where(coupling(0), a, 0.0)
        k = 1
        while 2 ** k < c:
            l = jnp.where(coupling(k), a, 0.0)
            t_bd = _pair_block_diag(t, first)
            (lt,) = yield [lambda: _mm(l, t_bd)]
            lt_bd = _pair_block_diag(lt, first)
            (tlt,) = yield [lambda: _mm(t, lt_bd)]
            t = t - tlt
            k += 1
        t_bd = _pair_block_diag(t, first).astype(BF16)
        rhs = rhs_in[un]
        (sol,) = yield [lambda: jnp.dot(t_bd, rhs, preferred_element_type=F32)]
        for i in range(2):
            h = 2 * p + i
            hs = slice(h * GDN_DK, (h + 1) * GDN_DK)
            u_ref[ci, 0, :, hs] = sol[i * c:(i + 1) * c, :GDN_DK]
            wq_ref[ci, 0, 0:c, hs] = sol[i * c:(i + 1) * c, GDN_DK:].astype(BF16)
            wq_ref[ci, 0, c:2 * c, hs] = qe_in[un, i * c:(i + 1) * c]
            eg_ref[ci, 0, h:h + 1, :] = eg_in[un, i:i + 1]

    lhs_out, kr_out, rhs_out, dec_out, kdt_out, qe_out, eg_out = stage_out
    half = _dwconv(x_ref[0], jnp.where(seg_start, 0.0, prev_ref[0]), jnp.where(seg_end, 0.0, next_ref[0]),
                   0.5 * cw_ref[...])
    y = half * jnp.tanh(half) + half
    ba = ba_ref[0]
    lane = lax.broadcasted_iota(jnp.int32, (1, BA_PAD), 1)
    is_g = jnp.logical_and(lane >= 2 * GDN_HEADS, lane < 4 * GDN_HEADS)
    beta_all = _sigmoid(ba)
    g_all = jnp.where(is_g, -jnp.exp(gp_ref[0:1]) * _softplus(ba + gp_ref[1:2]), 0.0)

    def chunk(ci):
        rs = slice(ci * c, (ci + 1) * c)
        heads = []
        for h in range(GDN_HEADS):
            lo = h * GDN_DK
            q = y[rs, lo:lo + GDN_DK]
            k = y[rs, GDN_WIDTH + lo:GDN_WIDTH + lo + GDN_DK]
            v = y[rs, 2 * GDN_WIDTH + lo:2 * GDN_WIDTH + lo + GDN_DK]
            q = q * (lax.rsqrt(jnp.sum(q * q, axis=-1, keepdims=True) + EPS) * (GDN_DK ** -0.5))
            k = k * lax.rsqrt(jnp.sum(k * k, axis=-1, keepdims=True) + EPS)
            heads.append((q, k, v))
        g = g_all[rs]
        k_rows = [jnp.concatenate([heads[2 * p][1], heads[2 * p + 1][1]], axis=0) for p in range(pairs)]
        k_ts = [kr.T for kr in k_rows]
        grow = lax.broadcasted_iota(jnp.int32, g.shape, 0)
        gcum_f = g
        step = 1
        while step < c:
            gcum_f = gcum_f + jnp.where(grow >= step, pltpu.roll(gcum_f, step, 0), 0.0)
            step *= 2
        gcum_b = gcum_f[c - 1:c] - gcum_f + g
        gcum_t = jnp.concatenate([gcum_f, gcum_b], axis=0).T
        beta_c = beta_all[rs]
        for p in range(pairs):
            kr_out[ci * pairs + p] = k_rows[p].astype(BF16)
        for d in range(2):
            gcum = gcum_b if d else gcum_f
            incl = (row <= col) if d else (row >= col)
            last = 0 if d else c - 1
            for p in range(pairs):
                un = unit_index(ci, d, p)
                (q0, k0, v0), (q1, k1, v1) = heads[2 * p], heads[2 * p + 1]
                cb = d * GDN_HEADS + 2 * p
                cg = 2 * GDN_HEADS + cb
                beta0, beta1 = beta_c[:, cb:cb + 1], beta_c[:, cb + 1:cb + 2]
                gc0, gc1 = gcum[:, cg:cg + 1], gcum[:, cg + 1:cg + 2]
                gt0, gt1 = gcum_t[cg:cg + 1], gcum_t[cg + 1:cg + 2]
                gr = (jnp.where(first_row, pltpu.roll(gt0, c, 1), gt1) if d
                      else jnp.where(first_row, gt0, pltpu.roll(gt1, c, 1)))
                gc = jnp.where(first, gc0, gc1)
                e0, e1 = jnp.exp(gc0), jnp.exp(gc1)
                kb0, kb1 = k0 * beta0, k1 * beta1
                dec_out[un] = jnp.exp(jnp.where(incl, gc - gr, NEG_BIG))
                lhs_out[un] = jnp.concatenate([jnp.concatenate([kb0, kb1], axis=1),
                                               jnp.concatenate([q0, q1], axis=1)], axis=0).astype(BF16)
                rhs_out[un] = jnp.concatenate([jnp.concatenate([v0 * beta0, kb0 * e0], axis=1),
                                               jnp.concatenate([v1 * beta1, kb1 * e1], axis=1)],
                                              axis=0).astype(BF16)
                kdt_out[un] = (k_ts[p] * jnp.exp(gc[last:last + 1] - gr)).astype(BF16)
                qe_out[un] = jnp.concatenate([q0 * e0, q1 * e1], axis=0).astype(BF16)
                eg_out[un] = jnp.concatenate(
                    [jnp.broadcast_to(jnp.exp(gc0[last:last + 1]), (1, BA_PAD)),
                     jnp.broadcast_to(jnp.exp(gc1[last:last + 1]), (1, BA_PAD)),
                     jnp.zeros((HALO - 2, BA_PAD), F32)], axis=0)

    for refs in out_refs:
        refs[3][...] = jnp.zeros_like(refs[3])
    _run_interleaved([pair_unit(ci, d, p) for ci in range(GDN_TILE_CHUNKS) for d in range(2)
                      for p in range(pairs)])
    for ci in range(GDN_TILE_CHUNKS):
        chunk(ci)


def _gdn_local_kernel(n_lat_tiles, n_all_tiles, n_tiles, x_ref, prev_ref, next_ref, ba_ref, cw_ref, gp_ref,
                      uf_ref, wqf_ref, akf_ref, egf_ref, ub_ref, wqb_ref, akb_ref, egb_ref, *stage_refs):
    s = pl.program_id(0)
    seg_start, seg_end = _segment_edges(jnp.minimum(s, n_tiles - 1) % n_all_tiles, n_lat_tiles, n_all_tiles)
    out_refs = ((uf_ref, wqf_ref, akf_ref, egf_ref), (ub_ref, wqb_ref, akb_ref, egb_ref))
    half = len(stage_refs) // 2
    sets = (stage_refs[:half], stage_refs[half:])

    @pl.when(s == 0)
    def _():
        for r in sets[1]:
            r[...] = jnp.zeros_like(r)

    for parity in range(2):
        @pl.when(s % 2 == parity)
        def _():
            _gdn_local_body(seg_start, seg_end, x_ref, prev_ref, next_ref, ba_ref, cw_ref, gp_ref,
                            out_refs, sets[1 - parity], sets[parity])


def _gdn_local_call(qkv, ba, conv_w, gate_params, n_lat_tok):
    bsz, t_all, width = qkv.shape
    c = GDN_CHUNK
    tile = GDN_TILE_CHUNKS * c
    n_all = t_all // tile
    n_lat = n_lat_tok // tile
    per_tile = tile // HALO
    last_halo = t_all // HALO - 1
    n_chunks = t_all // c
    n_tiles = bsz * n_all
    pairs = GDN_HEADS // 2
    units = GDN_TILE_CHUNKS * 2 * pairs
    t_in = lambda s: jnp.minimum(s, n_tiles - 1)
    t_out = lambda s: jnp.maximum(s - 1, 0)
    in_tok = lambda w: pl.BlockSpec((1, tile, w), lambda s: (t_in(s) // n_all, t_in(s) % n_all, 0))
    cblk = lambda rows, w: pl.BlockSpec((GDN_TILE_CHUNKS, 1, rows, w),
                                        lambda s: (t_out(s) % n_all, t_out(s) // n_all, 0, 0))
    vec = pl.BlockSpec((2, BA_PAD), lambda s: (0, 0))
    outs = [(jax.ShapeDtypeStruct((n_chunks, bsz, c, GDN_WIDTH), F32), cblk(c, GDN_WIDTH)),
            (jax.ShapeDtypeStruct((n_chunks, bsz, 2 * c, GDN_WIDTH), BF16), cblk(2 * c, GDN_WIDTH)),
            (jax.ShapeDtypeStruct((n_chunks, bsz, 3 * c, GDN_HEADS * c), BF16), cblk(3 * c, GDN_HEADS * c)),
            (jax.ShapeDtypeStruct((n_chunks, bsz, HALO, BA_PAD), F32), cblk(HALO, BA_PAD))] * 2
    stage = [pltpu.VMEM((units, 2 * c, 2 * GDN_DK), BF16),
             pltpu.VMEM((GDN_TILE_CHUNKS * pairs, 2 * c, GDN_DK), BF16),
             pltpu.VMEM((units, 2 * c, 2 * GDN_DK), BF16),
             pltpu.VMEM((units, c, 2 * c), F32),
             pltpu.VMEM((units, GDN_DK, 2 * c), BF16),
             pltpu.VMEM((units, 2 * c, GDN_DK), BF16),
             pltpu.VMEM((units, HALO, BA_PAD), F32)]
    return pl.pallas_call(
        functools.partial(_gdn_local_kernel, n_lat, n_all, n_tiles),
        grid=(n_tiles + 1,),
        in_specs=[in_tok(width),
                  pl.BlockSpec((1, HALO, width),
                               lambda s: (t_in(s) // n_all, jnp.maximum((t_in(s) % n_all) * per_tile - 1, 0), 0)),
                  pl.BlockSpec((1, HALO, width),
                               lambda s: (t_in(s) // n_all,
                                          jnp.minimum((t_in(s) % n_all + 1) * per_tile, last_halo), 0)),
                  in_tok(BA_PAD), pl.BlockSpec((CONV_W, width), lambda s: (0, 0)), vec],
        out_specs=[o[1] for o in outs],
        out_shape=[o[0] for o in outs],
        scratch_shapes=stage + stage,
        compiler_params=pltpu.CompilerParams(dimension_semantics=("arbitrary",),
                                             vmem_limit_bytes=VMEM_LIMIT),
        name="gdn_local",
    )(qkv, qkv, qkv, ba, conv_w, gate_params)


def _gdn_scan_steps(uf_ref, wqf_ref, akf_ref, egf_ref, ub_ref, wqb_ref, akb_ref, egb_ref,
                    of_ref, ob_ref, sf_ref, sb_ref):
    c = GDN_CHUNK
    bsz = uf_ref.shape[1]
    hs = lambda h: slice(h * GDN_DK, (h + 1) * GDN_DK)
    for sub in range(GDN_SCAN_CHUNKS):
        chains = [(refs, ci, b, h)
                  for refs, ci in (((uf_ref, wqf_ref, akf_ref, egf_ref, of_ref, sf_ref), sub),
                                   ((ub_ref, wqb_ref, akb_ref, egb_ref, ob_ref, sb_ref),
                                    GDN_SCAN_CHUNKS - 1 - sub))
                  for b in range(bsz) for h in range(GDN_HEADS)]
        states = [refs[5][b * GDN_HEADS + h] for refs, ci, b, h in chains]
        with_s = yield [lambda refs=refs, ci=ci, b=b, h=h, s=s:
                        jnp.dot(refs[1][ci, b, :, hs(h)], s.astype(BF16), preferred_element_type=F32)
                        for (refs, ci, b, h), s in zip(chains, states)]
        v_new = [refs[0][ci, b, :, hs(h)] - r[:c] for (refs, ci, b, h), r in zip(chains, with_s)]
        with_v = yield [lambda refs=refs, ci=ci, b=b, h=h, v=v:
                        jnp.dot(refs[2][ci, b, :, h * c:(h + 1) * c], v.astype(BF16), preferred_element_type=F32)
                        for (refs, ci, b, h), v in zip(chains, v_new)]
        for (refs, ci, b, h), s, r, a in zip(chains, states, with_s, with_v):
            refs[4][b, ci * c:(ci + 1) * c, hs(h)] = r[c:] + a[:c]
            refs[5][b * GDN_HEADS + h] = s * refs[3][ci, b, h:h + 1, :] + a[c:]


def _lin_scan(a, b, reverse):
    r = a.shape[0]
    row = lax.broadcasted_iota(jnp.int32, a.shape, 0)
    s = 1
    while s < r:
        ok = (row < r - s) if reverse else (row >= s)
        shift = r - s if reverse else s
        a_sh = jnp.where(ok, pltpu.roll(a, shift, 0), 1.0)
        b_sh = jnp.where(ok, pltpu.roll(b, shift, 0), 0.0)
        b = a * b_sh + b
        a = a * a_sh
        s *= 2
    return a, b


def _lru_ctx_direction(reverse, n_slabs, slab, x_ref, prev_ref, next_ref, cw, cb, w_ref, br, bi, lam, carry_ref):
    bsz, r, width = x_ref.shape
    step = pl.program_id(0)
    has_prev = slab > 0
    has_next = slab < n_slabs - 1
    xs = []
    for b in range(bsz):
        prev = jnp.where(has_prev, prev_ref[b], 0.0)
        nxt = jnp.where(has_next, next_ref[b], 0.0)
        xs.append(_dwconv(x_ref[b], prev, nxt, cw) + cb)
    x = jnp.concatenate(xs, axis=0)
    xb = x.astype(BF16)
    d = 1 if reverse else 0
    gate = lambda g: jnp.concatenate(
        [jnp.dot(xb[:, t * LRU_LANES:(t + 1) * LRU_LANES], w_ref[2 * d + g, t], preferred_element_type=F32)
         for t in range(width // LRU_LANES)], axis=1)
    a, mult, gated_x = _lru_a_b(gate(0), gate(1), br, bi, lam, 0.5 * x)
    row = lax.broadcasted_iota(jnp.int32, (r, width), 0)
    is_first = jnp.logical_and(step == 0, row == ((r - 1) if reverse else 0))
    edge = 0 if reverse else r - 1
    for b in range(bsz):
        sl = slice(b * r, (b + 1) * r)
        m = jnp.where(is_first, 1.0, mult[sl])
        a_cum, h = _lin_scan(a[sl], m * gated_x[sl], reverse)
        h = h + a_cum * carry_ref[b:b + 1]
        carry_ref[b:b + 1] = h[edge:edge + 1]


def _lru_ctx_kernel(n_slabs, xf_ref, pf_ref, nf_ref, xb_ref, pb_ref, nb_ref, cw_ref, cb_ref,
                    w_ref, bg_ref, lam_ref, fin_ref, cf_ref, cbk_ref):
    step = pl.program_id(0)

    @pl.when(step == 0)
    def _():
        cf_ref[...] = jnp.zeros_like(cf_ref)
        cbk_ref[...] = jnp.zeros_like(cbk_ref)

    cw = cw_ref[...]
    cb = cb_ref[...]
    _lru_ctx_direction(False, n_slabs, step, xf_ref, pf_ref, nf_ref, cw, cb, w_ref,
                       bg_ref[0:1], bg_ref[1:2], lam_ref[0:1], cf_ref)
    _lru_ctx_direction(True, n_slabs, n_slabs - 1 - step, xb_ref, pb_ref, nb_ref, cw, cb, w_ref,
                       bg_ref[2:3], bg_ref[3:4], lam_ref[1:2], cbk_ref)
    fin_ref[0] = cf_ref[...]
    fin_ref[1] = cbk_ref[...]


def _lru_ctx_call(rx, n_lat_tok, conv_w, conv_b, w_tiles, b_gate, lam):
    bsz, t_all, width = rx.shape
    r = GRID_W
    per = r // HALO
    n_slabs = (t_all - n_lat_tok) // r
    first = n_lat_tok // r
    main = lambda f: pl.BlockSpec((bsz, r, width), lambda s: (0, first + f(s), 0))
    prev = lambda f: pl.BlockSpec((bsz, HALO, width), lambda s: (0, (first + f(s)) * per - 1, 0))
    nxt = lambda f: pl.BlockSpec(
        (bsz, HALO, width), lambda s: (0, jnp.minimum((first + f(s) + 1) * per, t_all // HALO - 1), 0))
    fwd = lambda s: s
    bwd = lambda s: n_slabs - 1 - s
    full = lambda shape: pl.BlockSpec(shape, lambda s: (0,) * len(shape))
    carry = pltpu.VMEM((bsz, width), F32)
    return pl.pallas_call(
        functools.partial(_lru_ctx_kernel, n_slabs),
        grid=(n_slabs,),
        in_specs=[main(fwd), prev(fwd), nxt(fwd), main(bwd), prev(bwd), nxt(bwd),
                  full((CONV_W, width)), full((1, width)), full(w_tiles.shape),
                  full((4, width)), full((2, width))],
        out_specs=full((2, bsz, width)),
        out_shape=jax.ShapeDtypeStruct((2, bsz, width), F32),
        scratch_shapes=[carry, carry],
        compiler_params=pltpu.CompilerParams(dimension_semantics=("arbitrary",),
                                             vmem_limit_bytes=VMEM_LIMIT),
        name="lru_ctx",
    )(rx, rx, rx, rx, rx, rx, conv_w, conv_b, w_tiles, b_gate, lam)


def _lru_grid_gates(x_ref, cw_ref, cb_ref, w_ref, bg_ref, lam_ref, af_ref, bf_ref, ab_ref, bb_ref):
    n = GRID_W
    rp = LRU_ROWS_PER_PASS
    lanes = x_ref.shape[-1]
    crow = lax.broadcasted_iota(jnp.int32, (n, lanes), 0)
    cw = cw_ref[...]
    cb = cb_ref[...]

    def grid_rows(lo, hi):
        parts = []
        for r in range(lo, min(hi, 0)):
            parts.append(jnp.where(crow == 0, 0.0, pltpu.roll(x_ref[0, (n + r) * n:(n + r + 1) * n, :], 1, 0)))
        if max(lo, 0) < min(hi, n):
            parts.append(x_ref[0, max(lo, 0) * n:min(hi, n) * n, :])
        for r in range(max(lo, n), hi):
            parts.append(jnp.where(crow == n - 1, 0.0,
                                   pltpu.roll(x_ref[0, (r - n) * n:(r - n + 1) * n, :], n - 1, 0)))
        return parts[0] if len(parts) == 1 else jnp.concatenate(parts, axis=0)

    for r0 in range(0, n, rp):
        sl = slice(r0 * n, (r0 + rp) * n)
        xc = (grid_rows(r0 - 2, r0 + rp - 2) * cw[0:1] + grid_rows(r0 - 1, r0 + rp - 1) * cw[1:2]
              + grid_rows(r0, r0 + rp) * cw[2:3] + grid_rows(r0 + 1, r0 + rp + 1) * cw[3:4] + cb)
        xb = xc.astype(BF16)
        gates = yield [lambda i=i: jnp.dot(xb, w_ref[i, 0], preferred_element_type=F32) for i in range(4)]
        half_x = 0.5 * xc
        for d, (a_ref, b_ref) in enumerate(((af_ref, bf_ref), (ab_ref, bb_ref))):
            a, gain, gated_x = _lru_a_b(gates[2 * d], gates[2 * d + 1], bg_ref[2 * d:2 * d + 1],
                                        bg_ref[2 * d + 1:2 * d + 2], lam_ref[d:d + 1], half_x)
            a_ref[sl] = a
            b_ref[sl] = gain * gated_x


def _lru_grid_finish(h0_ref, out_ref, af_ref, bf_ref, ab_ref, bb_ref):
    n = GRID_W
    lanes = out_ref.shape[-1]
    crow = lax.broadcasted_iota(jnp.int32, (n, lanes), 0)
    hf = jnp.zeros((n, lanes), F32)
    hb = jnp.zeros((n, lanes), F32)
    pf = jnp.ones((n, lanes), F32)
    pb = jnp.ones((n, lanes), F32)
    for i in range(n):
        sf = slice(i * n, (i + 1) * n)
        sb = slice((n - 1 - i) * n, (n - i) * n)
        a = af_ref[sf]
        hf = a * hf + bf_ref[sf]
        pf = a * pf
        bf_ref[sf] = hf
        af_ref[sf] = pf
        a = ab_ref[sb]
        hb = a * hb + bb_ref[sb]
        pb = a * pb
        bb_ref[sb] = hb
        ab_ref[sb] = pb

    h0f = h0_ref[0, 0]
    h0b = h0_ref[1, 0]
    acc_a, acc_h = _lin_scan(pf, hf, False)
    in_f = jnp.where(crow == 0, h0f, pltpu.roll(acc_h + acc_a * h0f, 1, 0))
    acc_a, acc_h = _lin_scan(pb, hb, True)
    in_b = jnp.where(crow == n - 1, h0b, pltpu.roll(acc_h + acc_a * h0b, n - 1, 0))

    for r in range(n):
        sl = slice(r * n, (r + 1) * n)
        out_ref[0, sl, :] = (bf_ref[sl] + af_ref[sl] * in_f) + (bb_ref[sl] + ab_ref[sl] * in_b)


def _mixers_kernel(*refs):
    scan_in, lru_in, outs, scratch = refs[:8], refs[8:15], refs[15:18], refs[18:]
    of_ref, ob_ref, hs_ref = outs
    sf_ref, sb_ref = scratch[:2]

    @pl.when(pl.program_id(0) == 0)
    def _():
        sf_ref[...] = jnp.zeros_like(sf_ref)
        sb_ref[...] = jnp.zeros_like(sb_ref)

    x_ref, h0_ref = lru_in[:2]
    _run_interleaved([_gdn_scan_steps(*scan_in, of_ref, ob_ref, sf_ref, sb_ref),
                      _lru_grid_gates(x_ref, *lru_in[2:], *scratch[2:])])
    _lru_grid_finish(h0_ref, hs_ref, *scratch[2:])


def _mixers_call(local, rx, n_lat_tok, h0, conv_w, conv_b, w_tiles, b_gate, lam):
    n_chunks, bsz = local[0].shape[:2]
    width = rx.shape[2]
    c = GDN_CHUNK
    g = GDN_SCAN_CHUNKS
    n_all = n_chunks // g
    n_lat = n_lat_tok // (c * g)
    n_ctx = n_all - n_lat
    fwd = lambda s: jnp.where(s < n_ctx, n_lat + s, s - n_ctx)
    bwd = lambda s: n_all - 1 - s
    fwd_out = lambda s: jnp.maximum(s - n_ctx, 0)
    bwd_out = lambda s: jnp.minimum(n_all - 1 - s, n_lat - 1)
    cblk = lambda rows, w, f: pl.BlockSpec((g, bsz, rows, w), lambda s: (f(s), 0, 0, 0))
    ins = lambda f: [cblk(c, GDN_WIDTH, f), cblk(2 * c, GDN_WIDTH, f), cblk(3 * c, GDN_HEADS * c, f),
                     cblk(HALO, BA_PAD, f)]
    tok = lambda f: pl.BlockSpec((bsz, g * c, GDN_WIDTH), lambda s: (0, f(s), 0))
    state = pltpu.VMEM((bsz * GDN_HEADS, GDN_DK, GDN_DK), F32)

    nt = width // LRU_LANES
    unit = lambda s: jnp.minimum(s, bsz * nt - 1)
    assert n_all >= bsz * nt
    lane = lambda rows: pl.BlockSpec((rows, LRU_LANES), lambda s: (0, unit(s) % nt))
    grid_tok = pl.BlockSpec((1, n_lat_tok, LRU_LANES), lambda s: (unit(s) // nt, 0, unit(s) % nt))
    lru_scratch = pltpu.VMEM((n_lat_tok, LRU_LANES), F32)
    return pl.pallas_call(
        _mixers_kernel,
        grid=(n_all,),
        in_specs=ins(fwd) + ins(bwd) + [
            grid_tok,
            pl.BlockSpec((2, 1, 1, LRU_LANES), lambda s: (0, unit(s) // nt, 0, unit(s) % nt)),
            lane(CONV_W), lane(1),
            pl.BlockSpec((4, 1, LRU_LANES, LRU_LANES), lambda s: (0, unit(s) % nt, 0, 0)),
            lane(4), lane(2)],
        out_specs=[tok(fwd_out), tok(bwd_out), grid_tok],
        out_shape=[jax.ShapeDtypeStruct((bsz, n_lat_tok, GDN_WIDTH), F32)] * 2
                  + [jax.ShapeDtypeStruct((bsz, n_lat_tok, width), F32)],
        scratch_shapes=[state, state] + [lru_scratch] * 4,
        compiler_params=pltpu.CompilerParams(dimension_semantics=("arbitrary",),
                                             vmem_limit_bytes=VMEM_LIMIT),
        name="mixers",
    )(*local, rx, h0.reshape(2, bsz, 1, width), conv_w, conv_b, w_tiles, b_gate, lam)


def _out_ffn2_kernel(h1_ref, of_ref, ob_ref, z_ref, hs_ref, rg_ref, mod_ref, ng_ref, gnw_ref,
                     wout_ref, w1_ref, w3_ref, w2_ref, fg_ref, out_ref):
    mod = mod_ref[0]

    def sub_tile(rows):
        o = of_ref[0, rows] + ob_ref[0, rows]
        z = z_ref[0, rows]
        parts = []
        for h in range(GDN_HEADS):
            sl = slice(h * GDN_DK, (h + 1) * GDN_DK)
            oh = o[:, sl]
            parts.append(oh * lax.rsqrt(jnp.mean(oh * oh, axis=-1, keepdims=True) + EPS)
                         * gnw_ref[...] * _silu(z[:, sl]))
        parts.append(hs_ref[0, rows] * _gelu_tanh(rg_ref[0, rows]))
        mixed = jnp.concatenate(parts, axis=1).astype(BF16)
        (y,) = yield [lambda: jnp.dot(mixed, wout_ref[...], preferred_element_type=F32)]
        h2 = h1_ref[0, rows] + mod[5:6] * y
        ub = _rms_mod(h2, ng_ref[2:3], mod[6:7], mod[7:8]).astype(BF16)
        a, b = yield [lambda: jnp.dot(ub, w1_ref[...], preferred_element_type=F32),
                      lambda: jnp.dot(ub, w3_ref[...], preferred_element_type=F32)]
        g = (_silu(a) * b).astype(BF16)
        (f,) = yield [lambda: jnp.dot(g, w2_ref[...], preferred_element_type=F32)]
        h3 = h2 + FFN_RESIDUAL * mod[8:9] * f
        out_ref[0, rows] = h3 * lax.rsqrt(jnp.mean(h3 * h3, axis=-1, keepdims=True) + EPS) * fg_ref[...]

    _run_interleaved([sub_tile(slice(i * TOKEN_TILE, (i + 1) * TOKEN_TILE))
                      for i in range(out_ref.shape[1] // TOKEN_TILE)])


def _out_ffn2_call(h1, o_f, o_b, z, h_sum, rg, mods, norm_g, gdn_norm_w, w_out, w1, w3, w2, final_g):
    bsz, seq, _ = o_f.shape
    rows = OUT_SUB_TILES * TOKEN_TILE
    tok = lambda w: pl.BlockSpec((1, rows, w), lambda b, j: (b, j, 0))
    return pl.pallas_call(
        _out_ffn2_kernel,
        grid=(bsz, seq // rows),
        in_specs=[tok(D_MODEL), tok(GDN_WIDTH), tok(GDN_WIDTH), tok(GDN_WIDTH),
                  tok(LRU_WIDTH), tok(LRU_WIDTH),
                  pl.BlockSpec((1, N_MOD, D_MODEL), lambda b, j: (b + 1, 0, 0)),
                  _resident((3, D_MODEL)), _resident((1, GDN_DK)),
                  _resident((D_MODEL, D_MODEL)),
                  _resident((None, D_MODEL, D_FF), (1, 0, 0)), _resident((None, D_MODEL, D_FF), (1, 0, 0)),
                  _resident((None, D_FF, D_MODEL), (1, 0, 0)),
                  _resident((1, D_MODEL))],
        out_specs=tok(D_MODEL),
        out_shape=jax.ShapeDtypeStruct((bsz, seq, D_MODEL), F32),
        compiler_params=pltpu.CompilerParams(dimension_semantics=("arbitrary", "arbitrary"),
                                             vmem_limit_bytes=VMEM_LIMIT),
        name="out_ffn2",
    )(h1, o_f, o_b, z, h_sum, rg, mods, norm_g, gdn_norm_w, w_out, w1, w3, w2, final_g)


def kernel(x, c, ctx, c_ctx, w_ada, b_ada, norm_g, ffn_w1, ffn_w3, ffn_w2, w_in, w_out, gdn_conv_w, gdn_a_log,
           gdn_dt_bias, gdn_norm_w, lru_conv_w, lru_conv_b, lru_w_gate, lru_b_gate, lru_lambda, final_norm_g):
    bsz, seq, _ = x.shape
    assert w_ada.shape[0] == 1 and seq == GRID_W * GRID_W and ctx.shape[1] == TOKEN_TILE

    rows = -(-(bsz + 1) // 8) * 8
    cvec = jnp.concatenate([c_ctx[None, :], c, jnp.zeros((rows - bsz - 1, D_MODEL), F32)], axis=0)
    mods = _ada_call(cvec, w_ada[0], b_ada).reshape(rows, N_MOD, D_MODEL)

    w_in_bf = w_in[0].astype(BF16)
    o_ba = P_HEAD_COLS
    o_rx = o_ba + 4 * GDN_HEADS
    w_tail = jnp.concatenate([w_in_bf[:, o_rx:], w_in_bf[:, o_ba:o_rx],
                              jnp.zeros((D_MODEL, BA_PAD - 4 * GDN_HEADS), BF16)], axis=1)
    w1 = ffn_w1[0].astype(BF16)
    w3 = ffn_w3[0].astype(BF16)
    w2 = ffn_w2[0].astype(BF16)

    h1, qkv, z, rx, rg, ba = _ffn1_in_call(x, ctx, mods, norm_g[0], w1, w3, w2, w_in_bf, w_tail)

    gate_params = jnp.pad(jnp.stack([gdn_a_log[0].reshape(-1), gdn_dt_bias[0].reshape(-1)]),
                          ((0, 0), (2 * GDN_HEADS, BA_PAD - 4 * GDN_HEADS)))
    local = _gdn_local_call(qkv, ba, gdn_conv_w[0], gate_params, seq)

    nt = LRU_WIDTH // LRU_LANES
    per_tile = LRU_LANES // LRU_BW
    w_tiles = jnp.einsum('gticd,ij->gticjd', lru_w_gate[0].reshape(4, nt, per_tile, LRU_BW, LRU_BW),
                         0.5 * jnp.eye(per_tile, dtype=F32)).reshape(4, nt, LRU_LANES, LRU_LANES).astype(BF16)
    b_gate = lru_b_gate[0].reshape(4, LRU_WIDTH)
    lam = lru_lambda[0]
    cw, cb = lru_conv_w[0], lru_conv_b
    h_ctx = _lru_ctx_call(rx, seq, cw, cb, w_tiles, b_gate, lam)
    o_f, o_b, h_sum = _mixers_call(local, rx, seq, h_ctx, cw, cb, w_tiles, b_gate, lam)

    return _out_ffn2_call(h1, o_f, o_b, z, h_sum, rg, mods, norm_g[0], gdn_norm_w,
                          w_out[0].astype(BF16), w1, w3, w2, final_norm_g[None, :])
```

```python
import functools

import jax
import jax.numpy as jnp
from jax import lax
from jax.experimental import pallas as pl
from jax.experimental.pallas import tpu as pltpu

D_MODEL = 1024
D_FF = 2816
N_MOD = 9
EPS = 1e-6
FFN_RESIDUAL = 0.5

GDN_WIDTH = 512
GDN_HEADS = 4
GDN_DK = 128
GDN_CHUNK = 64
CONV_W = 4
LRU_WIDTH = 512
LRU_BW = 64
LRU_C = 8.0
GRID_W = 64

ADA_MODS_PER_STEP = 3
TOKEN_TILE = 256
GDN_TILE_CHUNKS = TOKEN_TILE // GDN_CHUNK
GDN_SCAN_CHUNKS = 4
OUT_SUB_TILES = 2
FFN1_SUB_TILES = 2
HALO = 8
BA_PAD = 128
P_HEAD_COLS = 4 * GDN_WIDTH
P_TAIL_COLS = 2 * LRU_WIDTH + BA_PAD
LRU_LANES = 128
LRU_ROWS_PER_PASS = 8
VMEM_LIMIT = 56 * 1024 * 1024

BF16 = jnp.bfloat16
F32 = jnp.float32
NEG_BIG = -1e30
SQRT_GUARD = 1e-30
LOG2_E = 1.4426950408889634
NT_DIMS = (((1,), (1,)), ((), ()))


def _mm(a, b):
    return jnp.dot(a.astype(BF16), b.astype(BF16), preferred_element_type=F32)


def _sigmoid(x):
    return 1.0 / (1.0 + jnp.exp(-x))


def _lru_a_b(half_r, half_i, bias_r, bias_i, lam, half_x):
    t_r = jnp.tanh(half_r + 0.5 * bias_r)
    t_i = jnp.tanh(half_i + 0.5 * bias_i)
    k = (-0.5 * LRU_C * LOG2_E) * _softplus(-lam)
    a = jnp.exp2(k * t_r + k)
    y = 1.0 - a * a
    return a, y * lax.rsqrt(y + SQRT_GUARD), t_i * half_x + half_x


def _silu(x):
    half = 0.5 * x
    return half * jnp.tanh(half) + half


def _softplus(x):
    return jnp.maximum(x, 0.0) + jnp.log(1.0 + jnp.exp(-jnp.abs(x)))


def _gelu_tanh(x):
    return 0.5 * x * (1.0 + jnp.tanh(0.7978845608028654 * (x + 0.044715 * x * x * x)))


def _rms_mod(h, g, shift, scale):
    y = h * lax.rsqrt(jnp.mean(h * h, axis=-1, keepdims=True) + EPS) * g
    return y * (1.0 + scale) + shift


def _swiglu(u, w1_ref, w3_ref, w2_ref):
    ub = u.astype(BF16)
    a = jnp.dot(ub, w1_ref[...], preferred_element_type=F32)
    b = jnp.dot(ub, w3_ref[...], preferred_element_type=F32)
    g = (_silu(a) * b).astype(BF16)
    return jnp.dot(g, w2_ref[...], preferred_element_type=F32)


def _dwconv(x, prev, nxt, w):
    t = x.shape[0]
    row = lax.broadcasted_iota(jnp.int32, x.shape, 0)
    xm1 = jnp.where(row == 0, prev[HALO - 1:HALO], pltpu.roll(x, 1, 0))
    xm2 = jnp.where(row == 0, prev[HALO - 2:HALO - 1],
                    jnp.where(row == 1, prev[HALO - 1:HALO], pltpu.roll(x, 2, 0)))
    xp1 = jnp.where(row == t - 1, nxt[0:1], pltpu.roll(x, t - 1, 0))
    return xm2 * w[0:1] + xm1 * w[1:2] + x * w[2:3] + xp1 * w[3:4]


def _resident(shape, index=None):
    index = index or (0,) * len(shape)
    return pl.BlockSpec(shape, lambda *_: index, pipeline_mode=pl.Buffered(1))


def _ada_kernel(c_ref, w_ref, b_ref, o_ref):
    o_ref[...] = _mm(_silu(c_ref[...]), w_ref[...]) + b_ref[...]


def _ada_call(cvec, w_ada, b_ada):
    rows = cvec.shape[0]
    cols = ADA_MODS_PER_STEP * D_MODEL
    return pl.pallas_call(
        _ada_kernel,
        grid=(N_MOD // ADA_MODS_PER_STEP,),
        in_specs=[pl.BlockSpec((rows, D_MODEL), lambda k: (0, 0)),
                  pl.BlockSpec((D_MODEL, cols), lambda k: (0, k)),
                  pl.BlockSpec((1, cols), lambda k: (0, k))],
        out_specs=pl.BlockSpec((rows, cols), lambda k: (0, k)),
        out_shape=jax.ShapeDtypeStruct((rows, N_MOD * D_MODEL), F32),
        compiler_params=pltpu.CompilerParams(dimension_semantics=("arbitrary",),
                                             vmem_limit_bytes=VMEM_LIMIT),
        name="ada",
    )(cvec, w_ada, b_ada)


def _segment_edges(j, n_lat_tiles, n_all_tiles):
    return (jnp.logical_or(j == 0, j == n_lat_tiles),
            jnp.logical_or(j == n_lat_tiles - 1, j == n_all_tiles - 1))


def _ffn1_in_kernel(n_lat_tiles, n_all_tiles, *refs):
    n_sub = FFN1_SUB_TILES
    ins, refs = refs[:3 * n_sub], refs[3 * n_sub:]
    ng_ref, w1_ref, w3_ref, w2_ref, win_ref, wtail_ref, h1_ref, qkv_ref, z_ref, rx_ref, rg_ref, ba_ref = refs
    s = pl.program_id(0)

    def sub_tile(i):
        x_ref, ctx_ref, mod_ref = ins[3 * i:3 * i + 3]
        rows = slice(i * TOKEN_TILE, (i + 1) * TOKEN_TILE)
        h = jnp.where((s * n_sub + i) % n_all_tiles == n_lat_tiles, ctx_ref[0], x_ref[0])
        mod = mod_ref[0]
        ub = _rms_mod(h, ng_ref[0:1], mod[0:1], mod[1:2]).astype(BF16)
        a, b = yield [lambda: jnp.dot(ub, w1_ref[...], preferred_element_type=F32),
                      lambda: jnp.dot(ub, w3_ref[...], preferred_element_type=F32)]
        g = ((a * jnp.tanh(a) + a) * b).astype(BF16)
        (f,) = yield [lambda: jnp.dot(g, w2_ref[...], preferred_element_type=F32)]
        h1 = h + FFN_RESIDUAL * mod[2:3] * f
        h1_ref[rows] = h1
        ub2 = _rms_mod(h1, ng_ref[1:2], mod[3:4], mod[4:5]).astype(BF16)
        head, tail = yield [lambda: jnp.dot(ub2, win_ref[...], preferred_element_type=F32),
                            lambda: jnp.dot(ub2, wtail_ref[...], preferred_element_type=F32)]
        qkv_ref[rows] = head[:, :3 * GDN_WIDTH]
        z_ref[rows] = head[:, 3 * GDN_WIDTH:]
        rx_ref[rows] = tail[:, :LRU_WIDTH]
        rg_ref[rows] = tail[:, LRU_WIDTH:2 * LRU_WIDTH]
        ba_ref[rows] = tail[:, 2 * LRU_WIDTH:]

    _run_interleaved([sub_tile(i) for i in range(n_sub)])


def _ffn1_in_call(x, ctx, mods, norm_g, w1, w3, w2, w_in_bf, w_tail):
    bsz, seq, _ = x.shape
    n_lat = seq // TOKEN_TILE
    n_all = n_lat + 1
    t_all = n_all * TOKEN_TILE
    n_sub = FFN1_SUB_TILES
    assert (bsz * n_all) % n_sub == 0
    per_tile = []
    for i in range(n_sub):
        tile = lambda s, i=i: s * n_sub + i
        per_tile += [
            pl.BlockSpec((1, TOKEN_TILE, D_MODEL),
                         lambda s, t=tile: (t(s) // n_all, jnp.minimum(t(s) % n_all, n_lat - 1), 0)),
            pl.BlockSpec((1, TOKEN_TILE, D_MODEL), lambda s, t=tile: (t(s) // n_all, 0, 0)),
            pl.BlockSpec((1, N_MOD, D_MODEL),
                         lambda s, t=tile: (jnp.where(t(s) % n_all == n_lat, 0, t(s) // n_all + 1), 0, 0))]
    widths = (D_MODEL, 3 * GDN_WIDTH, GDN_WIDTH, LRU_WIDTH, LRU_WIDTH, BA_PAD)
    outs = pl.pallas_call(
        functools.partial(_ffn1_in_kernel, n_lat, n_all),
        grid=(bsz * n_all // n_sub,),
        in_specs=per_tile + [
            _resident((3, D_MODEL)),
            _resident((None, D_MODEL, D_FF), (0, 0, 0)), _resident((None, D_MODEL, D_FF), (0, 0, 0)),
            _resident((None, D_FF, D_MODEL), (0, 0, 0)),
            _resident((D_MODEL, P_HEAD_COLS)), _resident((D_MODEL, P_TAIL_COLS))],
        out_specs=[pl.BlockSpec((n_sub * TOKEN_TILE, w), lambda s: (s, 0)) for w in widths],
        out_shape=[jax.ShapeDtypeStruct((bsz * t_all, w), F32) for w in widths],
        compiler_params=pltpu.CompilerParams(dimension_semantics=("arbitrary",),
                                             vmem_limit_bytes=VMEM_LIMIT),
        name="ffn1_in",
    )(*([x, ctx, mods] * n_sub), norm_g, w1, w3, w2, w_in_bf, w_tail)
    return [o.reshape(bsz, t_all, o.shape[-1]) for o in outs]


def _pair_block_diag(x, first):
    return jnp.concatenate([jnp.where(first, x, 0.0), jnp.where(first, 0.0, x)], axis=0)


def _run_interleaved(coroutines):
    active = [[gen, next(gen)] for gen in coroutines]
    while active:
        still = []
        for item in active:
            results = [dot() for dot in item[1]]
            try:
                item[1] = item[0].send(results)
                still.append(item)
            except StopIteration:
                pass
        active = still


def _gdn_local_body(seg_start, seg_end, x_ref, prev_ref, next_ref, ba_ref, cw_ref, gp_ref,
                    out_refs, stage_in, stage_out):
    c = GDN_CHUNK
    pairs = GDN_HEADS // 2
    row = lax.broadcasted_iota(jnp.int32, (c, 2 * c), 0)
    lane2 = lax.broadcasted_iota(jnp.int32, (c, 2 * c), 1)
    col = lane2 & (c - 1)
    first = lane2 < c
    first_row = first[0:1]
    eye2 = (row == col).astype(F32)
    zeros_k = jnp.zeros((c, GDN_DK), BF16)
    unit_index = lambda ci, d, p: (ci * 2 + d) * pairs + p

    def coupling(k):
        return jnp.logical_and((row >> k) != (col >> k), (row >> (k + 1)) == (col >> (k + 1)))

    lhs_in, kr_in, rhs_in, dec_in, kdt_in, qe_in, eg_in = stage_in

    def pair_unit(ci, d, p):
        u_ref, wq_ref, ak_ref, eg_ref = out_refs[d]
        un = unit_index(ci, d, p)
        strict = (row < col) if d else (row > col)
        k_rows = kr_in[ci * pairs + p]
        k_bd = jnp.concatenate([jnp.concatenate([k_rows[:c], zeros_k], axis=1),
                                jnp.concatenate([zeros_k, k_rows[c:]], axis=1)], axis=0)
        lhs = lhs_in[un]
        (prods,) = yield [lambda: lax.dot_general(lhs, k_bd, NT_DIMS, preferred_element_type=F32)]
        decay = dec_in[un]
        a = jnp.where(strict, prods[:c] * decay, 0.0)
        ps = slice(p * 2 * c, (p + 1) * 2 * c)
        ak_ref[ci, 0, 0:c, ps] = (prods[c:] * decay).astype(BF16)
        ak_ref[ci, 0, c:3 * c, ps] = kdt_in[un]
        t = eye2 - jnp.where(coupling(0), a, 0.0)
        k = 1
        while 2 ** k < c:
            l = jnp.where(coupling(k), a, 0.0)
            t_bd = _pair_block_diag(t, first)
            (lt,) = yield [lambda: _mm(l, t_bd)]
            lt_bd = _pair_block_diag(lt, first)
            (tlt,) = yield [lambda: _mm(t, lt_bd)]
            t = t - tlt
            k += 1
        t_bd = _pair_block_diag(t, first).astype(BF16)
        rhs = rhs_in[un]
        (sol,) = yield [lambda: jnp.dot(t_bd, rhs, preferred_element_type=F32)]
        for i in range(2):
            h = 2 * p + i
            hs = slice(h * GDN_DK, (h + 1) * GDN_DK)
            u_ref[ci, 0, :, hs] = sol[i * c:(i + 1) * c, :GDN_DK]
            wq_ref[ci, 0, 0:c, hs] = sol[i * c:(i + 1) * c, GDN_DK:].astype(BF16)
            wq_ref[ci, 0, c:2 * c, hs] = qe_in[un, i * c:(i + 1) * c]
            eg_ref[ci, 0, h:h + 1, :] = eg_in[un, i:i + 1]

    lhs_out, kr_out, rhs_out, dec_out, kdt_out, qe_out, eg_out = stage_out
    half = _dwconv(x_ref[0], jnp.where(seg_start, 0.0, prev_ref[0]), jnp.where(seg_end, 0.0, next_ref[0]),
                   0.5 * cw_ref[...])
    y = half * jnp.tanh(half) + half
    ba = ba_ref[0]
    lane = lax.broadcasted_iota(jnp.int32, (1, BA_PAD), 1)
    is_g = jnp.logical_and(lane >= 2 * GDN_HEADS, lane < 4 * GDN_HEADS)
    beta_all = _sigmoid(ba)
    g_all = jnp.where(is_g, -jnp.exp(gp_ref[0:1]) * _softplus(ba + gp_ref[1:2]), 0.0)

    def chunk(ci):
        rs = slice(ci * c, (ci + 1) * c)
        heads = []
        for h in range(GDN_HEADS):
            lo = h * GDN_DK
            q = y[rs, lo:lo + GDN_DK]
            k = y[rs, GDN_WIDTH + lo:GDN_WIDTH + lo + GDN_DK]
            v = y[rs, 2 * GDN_WIDTH + lo:2 * GDN_WIDTH + lo + GDN_DK]
            q = q * (lax.rsqrt(jnp.sum(q * q, axis=-1, keepdims=True) + EPS) * (GDN_DK ** -0.5))
            k = k * lax.rsqrt(jnp.sum(k * k, axis=-1, keepdims=True) + EPS)
            heads.append((q, k, v))
        g = g_all[rs]
        k_rows = [jnp.concatenate([heads[2 * p][1], heads[2 * p + 1][1]], axis=0) for p in range(pairs)]
        k_ts = [kr.T for kr in k_rows]
        grow = lax.broadcasted_iota(jnp.int32, g.shape, 0)
        gcum_f = g
        step = 1
        while step < c:
            gcum_f = gcum_f + jnp.where(grow >= step, pltpu.roll(gcum_f, step, 0), 0.0)
            step *= 2
        gcum_b = gcum_f[c - 1:c] - gcum_f + g
        gcum_t = jnp.concatenate([gcum_f, gcum_b], axis=0).T
        beta_c = beta_all[rs]
        for p in range(pairs):
            kr_out[ci * pairs + p] = k_rows[p].astype(BF16)
        for d in range(2):
            gcum = gcum_b if d else gcum_f
            incl = (row <= col) if d else (row >= col)
            last = 0 if d else c - 1
            for p in range(pairs):
                un = unit_index(ci, d, p)
                (q0, k0, v0), (q1, k1, v1) = heads[2 * p], heads[2 * p + 1]
                cb = d * GDN_HEADS + 2 * p
                cg = 2 * GDN_HEADS + cb
                beta0, beta1 = beta_c[:, cb:cb + 1], beta_c[:, cb + 1:cb + 2]
                gc0, gc1 = gcum[:, cg:cg + 1], gcum[:, cg + 1:cg + 2]
                gt0, gt1 = gcum_t[cg:cg + 1], gcum_t[cg + 1:cg + 2]
                gr = (jnp.where(first_row, pltpu.roll(gt0, c, 1), gt1) if d
                      else jnp.where(first_row, gt0, pltpu.roll(gt1, c, 1)))
                gc = jnp.where(first, gc0, gc1)
                e0, e1 = jnp.exp(gc0), jnp.exp(gc1)
                kb0, kb1 = k0 * beta0, k1 * beta1
                dec_out[un] = jnp.exp(jnp.where(incl, gc - gr, NEG_BIG))
                lhs_out[un] = jnp.concatenate([jnp.concatenate([kb0, kb1], axis=1),
                                               jnp.concatenate([q0, q1], axis=1)], axis=0).astype(BF16)
                rhs_out[un] = jnp.concatenate([jnp.concatenate([v0 * beta0, kb0 * e0], axis=1),
                                               jnp.concatenate([v1 * beta1, kb1 * e1], axis=1)],
                                              axis=0).astype(BF16)
                kdt_out[un] = (k_ts[p] * jnp.exp(gc[last:last + 1] - gr)).astype(BF16)
                qe_out[un] = jnp.concatenate([q0 * e0, q1 * e1], axis=0).astype(BF16)
                eg_out[un] = jnp.concatenate(
                    [jnp.broadcast_to(jnp.exp(gc0[last:last + 1]), (1, BA_PAD)),
                     jnp.broadcast_to(jnp.exp(gc1[last:last + 1]), (1, BA_PAD)),
                     jnp.zeros((HALO - 2, BA_PAD), F32)], axis=0)

    for refs in out_refs:
        refs[3][...] = jnp.zeros_like(refs[3])
    _run_interleaved([pair_unit(ci, d, p) for ci in range(GDN_TILE_CHUNKS) for d in range(2)
                      for p in range(pairs)])
    for ci in range(GDN_TILE_CHUNKS):
        chunk(ci)


def _gdn_local_kernel(n_lat_tiles, n_all_tiles, n_tiles, x_ref, prev_ref, next_ref, ba_ref, cw_ref, gp_ref,
                      uf_ref, wqf_ref, akf_ref, egf_ref, ub_ref, wqb_ref, akb_ref, egb_ref, *stage_refs):
    s = pl.program_id(0)
    seg_start, seg_end = _segment_edges(jnp.minimum(s, n_tiles - 1) % n_all_tiles, n_lat_tiles, n_all_tiles)
    out_refs = ((uf_ref, wqf_ref, akf_ref, egf_ref), (ub_ref, wqb_ref, akb_ref, egb_ref))
    half = len(stage_refs) // 2
    sets = (stage_refs[:half], stage_refs[half:])

    @pl.when(s == 0)
    def _():
        for r in sets[1]:
            r[...] = jnp.zeros_like(r)

    for parity in range(2):
        @pl.when(s % 2 == parity)
        def _():
            _gdn_local_body(seg_start, seg_end, x_ref, prev_ref, next_ref, ba_ref, cw_ref, gp_ref,
                            out_refs, sets[1 - parity], sets[parity])


def _gdn_local_call(qkv, ba, conv_w, gate_params, n_lat_tok):
    bsz, t_all, width = qkv.shape
    c = GDN_CHUNK
    tile = GDN_TILE_CHUNKS * c
    n_all = t_all // tile
    n_lat = n_lat_tok // tile
    per_tile = tile // HALO
    last_halo = t_all // HALO - 1
    n_chunks = t_all // c
    n_tiles = bsz * n_all
    pairs = GDN_HEADS // 2
    units = GDN_TILE_CHUNKS * 2 * pairs
    t_in = lambda s: jnp.minimum(s, n_tiles - 1)
    t_out = lambda s: jnp.maximum(s - 1, 0)
    in_tok = lambda w: pl.BlockSpec((1, tile, w), lambda s: (t_in(s) // n_all, t_in(s) % n_all, 0))
    cblk = lambda rows, w: pl.BlockSpec((GDN_TILE_CHUNKS, 1, rows, w),
                                        lambda s: (t_out(s) % n_all, t_out(s) // n_all, 0, 0))
    vec = pl.BlockSpec((2, BA_PAD), lambda s: (0, 0))
    outs = [(jax.ShapeDtypeStruct((n_chunks, bsz, c, GDN_WIDTH), F32), cblk(c, GDN_WIDTH)),
            (jax.ShapeDtypeStruct((n_chunks, bsz, 2 * c, GDN_WIDTH), BF16), cblk(2 * c, GDN_WIDTH)),
            (jax.ShapeDtypeStruct((n_chunks, bsz, 3 * c, GDN_HEADS * c), BF16), cblk(3 * c, GDN_HEADS * c)),
            (jax.ShapeDtypeStruct((n_chunks, bsz, HALO, BA_PAD), F32), cblk(HALO, BA_PAD))] * 2
    stage = [pltpu.VMEM((units, 2 * c, 2 * GDN_DK), BF16),
             pltpu.VMEM((GDN_TILE_CHUNKS * pairs, 2 * c, GDN_DK), BF16),
             pltpu.VMEM((units, 2 * c, 2 * GDN_DK), BF16),
             pltpu.VMEM((units, c, 2 * c), F32),
             pltpu.VMEM((units, GDN_DK, 2 * c), BF16),
             pltpu.VMEM((units, 2 * c, GDN_DK), BF16),
             pltpu.VMEM((units, HALO, BA_PAD), F32)]
    return pl.pallas_call(
        functools.partial(_gdn_local_kernel, n_lat, n_all, n_tiles),
        grid=(n_tiles + 1,),
        in_specs=[in_tok(width),
                  pl.BlockSpec((1, HALO, width),
                               lambda s: (t_in(s) // n_all, jnp.maximum((t_in(s) % n_all) * per_tile - 1, 0), 0)),
                  pl.BlockSpec((1, HALO, width),
                               lambda s: (t_in(s) // n_all,
                                          jnp.minimum((t_in(s) % n_all + 1) * per_tile, last_halo), 0)),
                  in_tok(BA_PAD), pl.BlockSpec((CONV_W, width), lambda s: (0, 0)), vec],
        out_specs=[o[1] for o in outs],
        out_shape=[o[0] for o in outs],
        scratch_shapes=stage + stage,
        compiler_params=pltpu.CompilerParams(dimension_semantics=("arbitrary",),
                                             vmem_limit_bytes=VMEM_LIMIT),
        name="gdn_local",
    )(qkv, qkv, qkv, ba, conv_w, gate_params)


def _gdn_scan_steps(uf_ref, wqf_ref, akf_ref, egf_ref, ub_ref, wqb_ref, akb_ref, egb_ref,
                    of_ref, ob_ref, sf_ref, sb_ref):
    c = GDN_CHUNK
    bsz = uf_ref.shape[1]
    hs = lambda h: slice(h * GDN_DK, (h + 1) * GDN_DK)
    for sub in range(GDN_SCAN_CHUNKS):
        chains = [(refs, ci, b, h)
                  for refs, ci in (((uf_ref, wqf_ref, akf_ref, egf_ref, of_ref, sf_ref), sub),
                                   ((ub_ref, wqb_ref, akb_ref, egb_ref, ob_ref, sb_ref),
                                    GDN_SCAN_CHUNKS - 1 - sub))
                  for b in range(bsz) for h in range(GDN_HEADS)]
        states = [refs[5][b * GDN_HEADS + h] for refs, ci, b, h in chains]
        with_s = yield [lambda refs=refs, ci=ci, b=b, h=h, s=s:
                        jnp.dot(refs[1][ci, b, :, hs(h)], s.astype(BF16), preferred_element_type=F32)
                        for (refs, ci, b, h), s in zip(chains, states)]
        v_new = [refs[0][ci, b, :, hs(h)] - r[:c] for (refs, ci, b, h), r in zip(chains, with_s)]
        with_v = yield [lambda refs=refs, ci=ci, b=b, h=h, v=v:
                        jnp.dot(refs[2][ci, b, :, h * c:(h + 1) * c], v.astype(BF16), preferred_element_type=F32)
                        for (refs, ci, b, h), v in zip(chains, v_new)]
        for (refs, ci, b, h), s, r, a in zip(chains, states, with_s, with_v):
            refs[4][b, ci * c:(ci + 1) * c, hs(h)] = r[c:] + a[:c]
            refs[5][b * GDN_HEADS + h] = s * refs[3][ci, b, h:h + 1, :] + a[c:]


def _lin_scan(a, b, reverse):
    r = a.shape[0]
    row = lax.broadcasted_iota(jnp.int32, a.shape, 0)
    s = 1
    while s < r:
        ok = (row < r - s) if reverse else (row >= s)
        shift = r - s if reverse else s
        a_sh = jnp.where(ok, pltpu.roll(a, shift, 0), 1.0)
        b_sh = jnp.where(ok, pltpu.roll(b, shift, 0), 0.0)
        b = a * b_sh + b
        a = a * a_sh
        s *= 2
    return a, b


def _lru_ctx_direction(reverse, n_slabs, slab, x_ref, prev_ref, next_ref, cw, cb, w_ref, br, bi, lam, carry_ref):
    bsz, r, width = x_ref.shape
    step = pl.program_id(0)
    has_prev = slab > 0
    has_next = slab < n_slabs - 1
    xs = []
    for b in range(bsz):
        prev = jnp.where(has_prev, prev_ref[b], 0.0)
        nxt = jnp.where(has_next, next_ref[b], 0.0)
        xs.append(_dwconv(x_ref[b], prev, nxt, cw) + cb)
    x = jnp.concatenate(xs, axis=0)
    xb = x.astype(BF16)
    d = 1 if reverse else 0
    gate = lambda g: jnp.concatenate(
        [jnp.dot(xb[:, t * LRU_LANES:(t + 1) * LRU_LANES], w_ref[2 * d + g, t], preferred_element_type=F32)
         for t in range(width // LRU_LANES)], axis=1)
    a, mult, gated_x = _lru_a_b(gate(0), gate(1), br, bi, lam, 0.5 * x)
    row = lax.broadcasted_iota(jnp.int32, (r, width), 0)
    is_first = jnp.logical_and(step == 0, row == ((r - 1) if reverse else 0))
    edge = 0 if reverse else r - 1
    for b in range(bsz):
        sl = slice(b * r, (b + 1) * r)
        m = jnp.where(is_first, 1.0, mult[sl])
        a_cum, h = _lin_scan(a[sl], m * gated_x[sl], reverse)
        h = h + a_cum * carry_ref[b:b + 1]
        carry_ref[b:b + 1] = h[edge:edge + 1]


def _lru_ctx_kernel(n_slabs, xf_ref, pf_ref, nf_ref, xb_ref, pb_ref, nb_ref, cw_ref, cb_ref,
                    w_ref, bg_ref, lam_ref, fin_ref, cf_ref, cbk_ref):
    step = pl.program_id(0)

    @pl.when(step == 0)
    def _():
        cf_ref[...] = jnp.zeros_like(cf_ref)
        cbk_ref[...] = jnp.zeros_like(cbk_ref)

    cw = cw_ref[...]
    cb = cb_ref[...]
    _lru_ctx_direction(False, n_slabs, step, xf_ref, pf_ref, nf_ref, cw, cb, w_ref,
                       bg_ref[0:1], bg_ref[1:2], lam_ref[0:1], cf_ref)
    _lru_ctx_direction(True, n_slabs, n_slabs - 1 - step, xb_ref, pb_ref, nb_ref, cw, cb, w_ref,
                       bg_ref[2:3], bg_ref[3:4], lam_ref[1:2], cbk_ref)
    fin_ref[0] = cf_ref[...]
    fin_ref[1] = cbk_ref[...]


def _lru_ctx_call(rx, n_lat_tok, conv_w, conv_b, w_tiles, b_gate, lam):
    bsz, t_all, width = rx.shape
    r = GRID_W
    per = r // HALO
    n_slabs = (t_all - n_lat_tok) // r
    first = n_lat_tok // r
    main = lambda f: pl.BlockSpec((bsz, r, width), lambda s: (0, first + f(s), 0))
    prev = lambda f: pl.BlockSpec((bsz, HALO, width), lambda s: (0, (first + f(s)) * per - 1, 0))
    nxt = lambda f: pl.BlockSpec(
        (bsz, HALO, width), lambda s: (0, jnp.minimum((first + f(s) + 1) * per, t_all // HALO - 1), 0))
    fwd = lambda s: s
    bwd = lambda s: n_slabs - 1 - s
    full = lambda shape: pl.BlockSpec(shape, lambda s: (0,) * len(shape))
    carry = pltpu.VMEM((bsz, width), F32)
    return pl.pallas_call(
        functools.partial(_lru_ctx_kernel, n_slabs),
        grid=(n_slabs,),
        in_specs=[main(fwd), prev(fwd), nxt(fwd), main(bwd), prev(bwd), nxt(bwd),
                  full((CONV_W, width)), full((1, width)), full(w_tiles.shape),
                  full((4, width)), full((2, width))],
        out_specs=full((2, bsz, width)),
        out_shape=jax.ShapeDtypeStruct((2, bsz, width), F32),
        scratch_shapes=[carry, carry],
        compiler_params=pltpu.CompilerParams(dimension_semantics=("arbitrary",),
                                             vmem_limit_bytes=VMEM_LIMIT),
        name="lru_ctx",
    )(rx, rx, rx, rx, rx, rx, conv_w, conv_b, w_tiles, b_gate, lam)


def _lru_grid_gates(x_ref, cw_ref, cb_ref, w_ref, bg_ref, lam_ref, af_ref, bf_ref, ab_ref, bb_ref):
    n = GRID_W
    rp = LRU_ROWS_PER_PASS
    lanes = x_ref.shape[-1]
    crow = lax.broadcasted_iota(jnp.int32, (n, lanes), 0)
    cw = cw_ref[...]
    cb = cb_ref[...]

    def grid_rows(lo, hi):
        parts = []
        for r in range(lo, min(hi, 0)):
            parts.append(jnp.where(crow == 0, 0.0, pltpu.roll(x_ref[0, (n + r) * n:(n + r + 1) * n, :], 1, 0)))
        if max(lo, 0) < min(hi, n):
            parts.append(x_ref[0, max(lo, 0) * n:min(hi, n) * n, :])
        for r in range(max(lo, n), hi):
            parts.append(jnp.where(crow == n - 1, 0.0,
                                   pltpu.roll(x_ref[0, (r - n) * n:(r - n + 1) * n, :], n - 1, 0)))
        return parts[0] if len(parts) == 1 else jnp.concatenate(parts, axis=0)

    for r0 in range(0, n, rp):
        sl = slice(r0 * n, (r0 + rp) * n)
        xc = (grid_rows(r0 - 2, r0 + rp - 2) * cw[0:1] + grid_rows(r0 - 1, r0 + rp - 1) * cw[1:2]
              + grid_rows(r0, r0 + rp) * cw[2:3] + grid_rows(r0 + 1, r0 + rp + 1) * cw[3:4] + cb)
        xb = xc.astype(BF16)
        gates = yield [lambda i=i: jnp.dot(xb, w_ref[i, 0], preferred_element_type=F32) for i in range(4)]
        half_x = 0.5 * xc
        for d, (a_ref, b_ref) in enumerate(((af_ref, bf_ref), (ab_ref, bb_ref))):
            a, gain, gated_x = _lru_a_b(gates[2 * d], gates[2 * d + 1], bg_ref[2 * d:2 * d + 1],
                                        bg_ref[2 * d + 1:2 * d + 2], lam_ref[d:d + 1], half_x)
            a_ref[sl] = a
            b_ref[sl] = gain * gated_x


def _lru_grid_finish(h0_ref, out_ref, af_ref, bf_ref, ab_ref, bb_ref):
    n = GRID_W
    lanes = out_ref.shape[-1]
    crow = lax.broadcasted_iota(jnp.int32, (n, lanes), 0)
    hf = jnp.zeros((n, lanes), F32)
    hb = jnp.zeros((n, lanes), F32)
    pf = jnp.ones((n, lanes), F32)
    pb = jnp.ones((n, lanes), F32)
    for i in range(n):
        sf = slice(i * n, (i + 1) * n)
        sb = slice((n - 1 - i) * n, (n - i) * n)
        a = af_ref[sf]
        hf = a * hf + bf_ref[sf]
        pf = a * pf
        bf_ref[sf] = hf
        af_ref[sf] = pf
        a = ab_ref[sb]
        hb = a * hb + bb_ref[sb]
        pb = a * pb
        bb_ref[sb] = hb
        ab_ref[sb] = pb

    h0f = h0_ref[0, 0]
    h0b = h0_ref[1, 0]
    acc_a, acc_h = _lin_scan(pf, hf, False)
    in_f = jnp.where(crow == 0, h0f, pltpu.roll(acc_h + acc_a * h0f, 1, 0))
    acc_a, acc_h = _lin_scan(pb, hb, True)
    in_b = jnp.where(crow == n - 1, h0b, pltpu.roll(acc_h + acc_a * h0b, n - 1, 0))

    for r in range(n):
        sl = slice(r * n, (r + 1) * n)
        out_ref[0, sl, :] = (bf_ref[sl] + af_ref[sl] * in_f) + (bb_ref[sl] + ab_ref[sl] * in_b)


def _mixers_kernel(*refs):
    scan_in, lru_in, outs, scratch = refs[:8], refs[8:15], refs[15:18], refs[18:]
    of_ref, ob_ref, hs_ref = outs
    sf_ref, sb_ref = scratch[:2]

    @pl.when(pl.program_id(0) == 0)
    def _():
        sf_ref[...] = jnp.zeros_like(sf_ref)
        sb_ref[...] = jnp.zeros_like(sb_ref)

    x_ref, h0_ref = lru_in[:2]
    _run_interleaved([_gdn_scan_steps(*scan_in, of_ref, ob_ref, sf_ref, sb_ref),
                      _lru_grid_gates(x_ref, *lru_in[2:], *scratch[2:])])
    _lru_grid_finish(h0_ref, hs_ref, *scratch[2:])


def _mixers_call(local, rx, n_lat_tok, h0, conv_w, conv_b, w_tiles, b_gate, lam):
    n_chunks, bsz = local[0].shape[:2]
    width = rx.shape[2]
    c = GDN_CHUNK
    g = GDN_SCAN_CHUNKS
    n_all = n_chunks // g
    n_lat = n_lat_tok // (c * g)
    n_ctx = n_all - n_lat
    fwd = lambda s: jnp.where(s < n_ctx, n_lat + s, s - n_ctx)
    bwd = lambda s: n_all - 1 - s
    fwd_out = lambda s: jnp.maximum(s - n_ctx, 0)
    bwd_out = lambda s: jnp.minimum(n_all - 1 - s, n_lat - 1)
    cblk = lambda rows, w, f: pl.BlockSpec((g, bsz, rows, w), lambda s: (f(s), 0, 0, 0))
    ins = lambda f: [cblk(c, GDN_WIDTH, f), cblk(2 * c, GDN_WIDTH, f), cblk(3 * c, GDN_HEADS * c, f),
                     cblk(HALO, BA_PAD, f)]
    tok = lambda f: pl.BlockSpec((bsz, g * c, GDN_WIDTH), lambda s: (0, f(s), 0))
    state = pltpu.VMEM((bsz * GDN_HEADS, GDN_DK, GDN_DK), F32)

    nt = width // LRU_LANES
    unit = lambda s: jnp.minimum(s, bsz * nt - 1)
    assert n_all >= bsz * nt
    lane = lambda rows: pl.BlockSpec((rows, LRU_LANES), lambda s: (0, unit(s) % nt))
    grid_tok = pl.BlockSpec((1, n_lat_tok, LRU_LANES), lambda s: (unit(s) // nt, 0, unit(s) % nt))
    lru_scratch = pltpu.VMEM((n_lat_tok, LRU_LANES), F32)
    return pl.pallas_call(
        _mixers_kernel,
        grid=(n_all,),
        in_specs=ins(fwd) + ins(bwd) + [
            grid_tok,
            pl.BlockSpec((2, 1, 1, LRU_LANES), lambda s: (0, unit(s) // nt, 0, unit(s) % nt)),
            lane(CONV_W), lane(1),
            pl.BlockSpec((4, 1, LRU_LANES, LRU_LANES), lambda s: (0, unit(s) % nt, 0, 0)),
            lane(4), lane(2)],
        out_specs=[tok(fwd_out), tok(bwd_out), grid_tok],
        out_shape=[jax.ShapeDtypeStruct((bsz, n_lat_tok, GDN_WIDTH), F32)] * 2
                  + [jax.ShapeDtypeStruct((bsz, n_lat_tok, width), F32)],
        scratch_shapes=[state, state] + [lru_scratch] * 4,
        compiler_params=pltpu.CompilerParams(dimension_semantics=("arbitrary",),
                                             vmem_limit_bytes=VMEM_LIMIT),
        name="mixers",
    )(*local, rx, h0.reshape(2, bsz, 1, width), conv_w, conv_b, w_tiles, b_gate, lam)


def _out_ffn2_kernel(h1_ref, of_ref, ob_ref, z_ref, hs_ref, rg_ref, mod_ref, ng_ref, gnw_ref,
                     wout_ref, w1_ref, w3_ref, w2_ref, fg_ref, out_ref):
    mod = mod_ref[0]

    def sub_tile(rows):
        o = of_ref[0, rows] + ob_ref[0, rows]
        z = z_ref[0, rows]
        parts = []
        for h in range(GDN_HEADS):
            sl = slice(h * GDN_DK, (h + 1) * GDN_DK)
            oh = o[:, sl]
            parts.append(oh * lax.rsqrt(jnp.mean(oh * oh, axis=-1, keepdims=True) + EPS)
                         * gnw_ref[...] * _silu(z[:, sl]))
        parts.append(hs_ref[0, rows] * _gelu_tanh(rg_ref[0, rows]))
        mixed = jnp.concatenate(parts, axis=1).astype(BF16)
        (y,) = yield [lambda: jnp.dot(mixed, wout_ref[...], preferred_element_type=F32)]
        h2 = h1_ref[0, rows] + mod[5:6] * y
        ub = _rms_mod(h2, ng_ref[2:3], mod[6:7], mod[7:8]).astype(BF16)
        a, b = yield [lambda: jnp.dot(ub, w1_ref[...], preferred_element_type=F32),
                      lambda: jnp.dot(ub, w3_ref[...], preferred_element_type=F32)]
        g = ((a * jnp.tanh(a) + a) * b).astype(BF16)
        (f,) = yield [lambda: jnp.dot(g, w2_ref[...], preferred_element_type=F32)]
        h3 = h2 + FFN_RESIDUAL * mod[8:9] * f
        out_ref[0, rows] = h3 * lax.rsqrt(jnp.mean(h3 * h3, axis=-1, keepdims=True) + EPS) * fg_ref[...]

    _run_interleaved([sub_tile(slice(i * TOKEN_TILE, (i + 1) * TOKEN_TILE))
                      for i in range(out_ref.shape[1] // TOKEN_TILE)])


def _out_ffn2_call(h1, o_f, o_b, z, h_sum, rg, mods, norm_g, gdn_norm_w, w_out, w1, w3, w2, final_g):
    bsz, seq, _ = o_f.shape
    rows = OUT_SUB_TILES * TOKEN_TILE
    tok = lambda w: pl.BlockSpec((1, rows, w), lambda b, j: (b, j, 0))
    return pl.pallas_call(
        _out_ffn2_kernel,
        grid=(bsz, seq // rows),
        in_specs=[tok(D_MODEL), tok(GDN_WIDTH), tok(GDN_WIDTH), tok(GDN_WIDTH),
                  tok(LRU_WIDTH), tok(LRU_WIDTH),
                  pl.BlockSpec((1, N_MOD, D_MODEL), lambda b, j: (b + 1, 0, 0)),
                  _resident((3, D_MODEL)), _resident((1, GDN_DK)),
                  _resident((D_MODEL, D_MODEL)),
                  _resident((None, D_MODEL, D_FF), (1, 0, 0)), _resident((None, D_MODEL, D_FF), (1, 0, 0)),
                  _resident((None, D_FF, D_MODEL), (1, 0, 0)),
                  _resident((1, D_MODEL))],
        out_specs=tok(D_MODEL),
        out_shape=jax.ShapeDtypeStruct((bsz, seq, D_MODEL), F32),
        compiler_params=pltpu.CompilerParams(dimension_semantics=("arbitrary", "arbitrary"),
                                             vmem_limit_bytes=VMEM_LIMIT),
        name="out_ffn2",
    )(h1, o_f, o_b, z, h_sum, rg, mods, norm_g, gdn_norm_w, w_out, w1, w3, w2, final_g)


def kernel(x, c, ctx, c_ctx, w_ada, b_ada, norm_g, ffn_w1, ffn_w3, ffn_w2, w_in, w_out, gdn_conv_w, gdn_a_log,
           gdn_dt_bias, gdn_norm_w, lru_conv_w, lru_conv_b, lru_w_gate, lru_b_gate, lru_lambda, final_norm_g):
    bsz, seq, _ = x.shape
    assert w_ada.shape[0] == 1 and seq == GRID_W * GRID_W and ctx.shape[1] == TOKEN_TILE

    rows = -(-(bsz + 1) // 8) * 8
    cvec = jnp.concatenate([c_ctx[None, :], c, jnp.zeros((rows - bsz - 1, D_MODEL), F32)], axis=0)
    mods = _ada_call(cvec, w_ada[0], b_ada).reshape(rows, N_MOD, D_MODEL)

    w_in_bf = w_in[0].astype(BF16)
    o_ba = P_HEAD_COLS
    o_rx = o_ba + 4 * GDN_HEADS
    w_tail = jnp.concatenate([w_in_bf[:, o_rx:], w_in_bf[:, o_ba:o_rx],
                              jnp.zeros((D_MODEL, BA_PAD - 4 * GDN_HEADS), BF16)], axis=1)
    w1 = (0.5 * ffn_w1[0]).astype(BF16)
    w3 = ffn_w3[0].astype(BF16)
    w2 = ffn_w2[0].astype(BF16)

    h1, qkv, z, rx, rg, ba = _ffn1_in_call(x, ctx, mods, norm_g[0], w1, w3, w2, w_in_bf, w_tail)

    gate_params = jnp.pad(jnp.stack([gdn_a_log[0].reshape(-1), gdn_dt_bias[0].reshape(-1)]),
                          ((0, 0), (2 * GDN_HEADS, BA_PAD - 4 * GDN_HEADS)))
    local = _gdn_local_call(qkv, ba, gdn_conv_w[0], gate_params, seq)

    nt = LRU_WIDTH // LRU_LANES
    per_tile = LRU_LANES // LRU_BW
    w_tiles = jnp.einsum('gticd,ij->gticjd', lru_w_gate[0].reshape(4, nt, per_tile, LRU_BW, LRU_BW),
                         0.5 * jnp.eye(per_tile, dtype=F32)).reshape(4, nt, LRU_LANES, LRU_LANES).astype(BF16)
    b_gate = lru_b_gate[0].reshape(4, LRU_WIDTH)
    lam = lru_lambda[0]
    cw, cb = lru_conv_w[0], lru_conv_b
    h_ctx = _lru_ctx_call(rx, seq, cw, cb, w_tiles, b_gate, lam)
    o_f, o_b, h_sum = _mixers_call(local, rx, seq, h_ctx, cw, cb, w_tiles, b_gate, lam)

    return _out_ffn2_call(h1, o_f, o_b, z, h_sum, rg, mods, norm_g[0], gdn_norm_w,
                          w_out[0].astype(BF16), w1, w3, w2, final_norm_g[None, :])
```
